```python
import math
import jax, jax.numpy as jnp
from jax import lax
import numpy as np

D_MODEL = 1024
BATCH = 8
SEQ = 2048
DEPTH = 1

D_MIX = D_MODEL
GLA_WIDTH = D_MIX // 2
S5_WIDTH = D_MIX - GLA_WIDTH
GLA_HEADS = 4
GLA_DV = GLA_WIDTH // GLA_HEADS
GLA_DK = GLA_DV // 2
GLA_LOWRANK = 16
GLA_TAU = 16.0
GLA_CHUNK = 64
S5_CH = 16
S5_GROUPS = S5_WIDTH // S5_CH
S5_STATE = 64
DT_MIN = 1e-3
DT_MAX = 1e-1
N_EXPERTS = 32
TOP_K = 4
D_FF = D_MODEL
SWIGLU_ALPHA = 1.702
SWIGLU_LIMIT = 7.0
MOE_BLOCK = 128
EPS = 1e-6
Q_COLS = GLA_HEADS * GLA_DK
K_COLS = GLA_HEADS * GLA_DK
V_COLS = GLA_WIDTH
G_COLS = GLA_WIDTH
A_COLS = GLA_LOWRANK
U_COLS = S5_WIDTH
N_IN = Q_COLS + K_COLS + V_COLS + G_COLS + A_COLS + U_COLS
SPLITS = [Q_COLS, Q_COLS + K_COLS, Q_COLS + K_COLS + V_COLS, Q_COLS + K_COLS + V_COLS + G_COLS,
          Q_COLS + K_COLS + V_COLS + G_COLS + A_COLS]

kernel_name = "hymba_gla_s5_moe_block"


def rmsnorm(x, g):
    xf = x.astype(jnp.float32)
    r = lax.rsqrt(jnp.mean(xf * xf, axis=-1, keepdims=True) + EPS)
    return (xf * r * g.astype(jnp.float32)).astype(x.dtype)


def gla_chunked(q, k, v, log_a):
    f32 = jnp.float32
    B_, H_, T_, DK_ = q.shape
    DV_ = v.shape[-1]
    C = GLA_CHUNK
    nc = T_ // C
    q = q.astype(f32).reshape(B_, H_, nc, C, DK_) * (DK_ ** -0.5)
    k = k.astype(f32).reshape(B_, H_, nc, C, DK_)
    v = v.astype(f32).reshape(B_, H_, nc, C, DV_)
    b = jnp.cumsum(log_a.astype(f32).reshape(B_, H_, nc, C, DK_), axis=3)
    b_last = b[:, :, :, -1:, :]
    q_dec = q * jnp.exp(b)
    k_inv = k * jnp.exp(-b)
    k_to_end = k * jnp.exp(b_last - b)
    causal = jnp.tril(jnp.ones((C, C), dtype=bool))
    scores = jnp.where(causal, jnp.einsum('bhnik,bhnjk->bhnij', q_dec, k_inv), 0.0)
    o_intra = jnp.einsum('bhnij,bhnjv->bhniv', scores, v)
    delta = jnp.einsum('bhnjk,bhnjv->bhnkv', k_to_end, v)
    decay = jnp.exp(b_last[:, :, :, 0, :])

    def step(S, inp):
        d, dS = inp
        return d[..., None] * S + dS, S

    S0 = jnp.zeros((B_, H_, DK_, DV_), f32)
    _, S_prev = lax.scan(step, S0, (jnp.moveaxis(decay, 2, 0), jnp.moveaxis(delta, 2, 0)))
    S_prev = jnp.moveaxis(S_prev, 0, 2)
    o_inter = jnp.einsum('bhnik,bhnkv->bhniv', q_dec, S_prev)
    return (o_intra + o_inter).reshape(B_, H_, T_, DV_)


def s5_layer(u, lam_re, lam_im, log_dt, b_re, b_im, c_re, c_im, d, glu_w, glu_b):
    f32 = jnp.float32
    B_, T_, _ = u.shape
    uf = u.astype(f32).reshape(B_, T_, S5_GROUPS, S5_CH)
    lr = lam_re.astype(f32)
    li = lam_im.astype(f32)
    dt = jnp.exp(log_dt.astype(f32))[:, None]
    mag = jnp.exp(lr * dt)
    ab_re = mag * jnp.cos(li * dt)
    ab_im = mag * jnp.sin(li * dt)
    den = lr * lr + li * li
    f_re = ((ab_re - 1.0) * lr + ab_im * li) / den
    f_im = (ab_im * lr - (ab_re - 1.0) * li) / den
    br = b_re.astype(f32)
    bi = b_im.astype(f32)
    bb_re = f_re[..., None] * br - f_im[..., None] * bi
    bb_im = f_re[..., None] * bi + f_im[..., None] * br
    bu_re = jnp.einsum('btgh,gph->btgp', uf, bb_re)
    bu_im = jnp.einsum('btgh,gph->btgp', uf, bb_im)
    a_re = jnp.broadcast_to(ab_re, (1, T_, S5_GROUPS, S5_STATE))
    a_im = jnp.broadcast_to(ab_im, (1, T_, S5_GROUPS, S5_STATE))

    def combine(e1, e2):
        a1r, a1i, b1r, b1i = e1
        a2r, a2i, b2r, b2i = e2
        return (a2r * a1r - a2i * a1i,
                a2r * a1i + a2i * a1r,
                a2r * b1r - a2i * b1i + b2r,
                a2r * b1i + a2i * b1r + b2i)

    _, _, xr, xi = lax.associative_scan(combine, (a_re, a_im, bu_re, bu_im), axis=1)
    y = (jnp.einsum('btgp,ghp->btgh', xr, c_re.astype(f32))
         - jnp.einsum('btgp,ghp->btgh', xi, c_im.astype(f32))
         + d.astype(f32) * uf)
    y = jax.nn.gelu(y.reshape(B_, T_, S5_WIDTH))
    return y * jax.nn.sigmoid(y @ glu_w.astype(f32) + glu_b.astype(f32))


def moe_ffn(h, router_w, router_b, w_gu, b_gu, w_down, b_down):
    f32 = jnp.float32
    B_, T_, D_ = h.shape
    N = B_ * T_
    NK = N * TOP_K
    xt = h.reshape(N, D_)
    logits = (xt @ router_w + router_b).astype(f32)
    top_val, top_idx = lax.top_k(logits, TOP_K)
    gate = jax.nn.softmax(top_val, axis=-1)
    flat_e = top_idx.reshape(NK).astype(jnp.int32)
    flat_tok = jnp.arange(NK, dtype=jnp.int32) // TOP_K
    flat_w = gate.reshape(NK)
    order = jnp.argsort(flat_e)
    se, stok, sw = flat_e[order], flat_tok[order], flat_w[order]
    counts = jnp.zeros((N_EXPERTS,), jnp.int32).at[flat_e].add(1)
    starts = jnp.cumsum(counts) - counts
    padded = (counts + MOE_BLOCK - 1) // MOE_BLOCK * MOE_BLOCK
    pends = jnp.cumsum(padded)
    pstarts = pends - padded
    dest = pstarts[se] + jnp.arange(NK, dtype=jnp.int32) - starts[se]
    n_blocks = -(-NK // MOE_BLOCK) + N_EXPERTS
    cap = n_blocks * MOE_BLOCK
    tok_buf = jnp.full((cap,), N, jnp.int32).at[dest].set(stok)
    w_buf = jnp.zeros((cap,), f32).at[dest].set(sw)
    block_e = jnp.minimum(
        jnp.searchsorted(pends, jnp.arange(n_blocks, dtype=jnp.int32) * MOE_BLOCK, side='right'),
        N_EXPERTS - 1).astype(jnp.int32)
    x_pad = jnp.concatenate([xt, jnp.zeros((1, D_), xt.dtype)], axis=0)
    xb = x_pad[tok_buf].reshape(n_blocks, MOE_BLOCK, D_)

    def expert_block(args):
        xblk, e = args
        gu = xblk @ w_gu[e] + b_gu[e]
        g, up = gu[:, :D_FF], gu[:, D_FF:]
        g = jnp.minimum(g, SWIGLU_LIMIT)
        up = jnp.clip(up, -SWIGLU_LIMIT, SWIGLU_LIMIT)
        act = (up + 1.0) * (g * jax.nn.sigmoid(SWIGLU_ALPHA * g))
        return act @ w_down[e] + b_down[e]

    yb = lax.map(expert_block, (xb, block_e)).reshape(cap, D_)
    out = jnp.zeros((N + 1, D_), f32).at[tok_buf].add(yb.astype(f32) * w_buf[:, None])[:N]
    return out.reshape(B_, T_, D_).astype(h.dtype)


def setup_inputs(seed: int = 0) -> dict:
    key = jax.random.key(seed)
    ks = jax.random.split(key, 32)
    f32 = jnp.float32
    L, D = DEPTH, D_MODEL

    def nrm(k, shape, scale):
        return jax.random.normal(k, shape, f32) * scale

    n_idx = jnp.arange(S5_STATE, dtype=f32)
    return {
        "x": nrm(ks[0], (BATCH, SEQ, D), 1.0),
        "c": nrm(ks[1], (BATCH, D), 1.0),
        "ada_w": nrm(ks[2], (L, D, 6 * D), 0.02),
        "ada_b": nrm(ks[3], (L, 6 * D), 0.02),
        "mix_pre_g": 1.0 + nrm(ks[4], (L, D), 0.05),
        "mix_post_g": 1.0 + nrm(ks[5], (L, D), 0.05),
        "ffn_pre_g": 1.0 + nrm(ks[6], (L, D), 0.05),
        "ffn_post_g": 1.0 + nrm(ks[7], (L, D), 0.05),
        "w_in": nrm(ks[8], (L, D, N_IN), D ** -0.5),
        "w_alpha": nrm(ks[9], (L, GLA_LOWRANK, GLA_HEADS * GLA_DK), GLA_LOWRANK ** -0.5),
        "b_alpha": nrm(ks[10], (L, GLA_HEADS * GLA_DK), 0.1),
        "gla_norm_g": 1.0 + nrm(ks[11], (L, GLA_DV), 0.05),
        "s5_lambda_re": -0.5 + nrm(ks[12], (L, S5_GROUPS, S5_STATE), 0.01),
        "s5_lambda_im": math.pi * n_idx + nrm(ks[13], (L, S5_GROUPS, S5_STATE), 0.01),
        "s5_log_dt": jax.random.uniform(ks[14], (L, S5_GROUPS), f32, math.log(DT_MIN), math.log(DT_MAX)),
        "s5_b_re": nrm(ks[15], (L, S5_GROUPS, S5_STATE, S5_CH), (2 * S5_CH) ** -0.5),
        "s5_b_im": nrm(ks[16], (L, S5_GROUPS, S5_STATE, S5_CH), (2 * S5_CH) ** -0.5),
        "s5_c_re": nrm(ks[17], (L, S5_GROUPS, S5_CH, S5_STATE), S5_STATE ** -0.5),
        "s5_c_im": nrm(ks[18], (L, S5_GROUPS, S5_CH, S5_STATE), S5_STATE ** -0.5),
        "s5_d": nrm(ks[19], (L, S5_GROUPS, S5_CH), 1.0),
        "s5_glu_w": nrm(ks[20], (L, S5_WIDTH, S5_WIDTH), S5_WIDTH ** -0.5),
        "s5_glu_b": nrm(ks[21], (L, S5_WIDTH), 0.01),
        "w_out": nrm(ks[22], (L, D_MIX, D), D_MIX ** -0.5),
        "router_w": nrm(ks[23], (L, D, N_EXPERTS), D ** -0.5),
        "router_b": nrm(ks[24], (L, N_EXPERTS), 0.01),
        "exp_w_gu": nrm(ks[25], (L, N_EXPERTS, D, 2 * D_FF), D ** -0.5),
        "exp_b_gu": nrm(ks[26], (L, N_EXPERTS, 2 * D_FF), 0.01),
        "exp_w_down": nrm(ks[27], (L, N_EXPERTS, D_FF, D), D_FF ** -0.5),
        "exp_b_down": nrm(ks[28], (L, N_EXPERTS, D), 0.01),
    }


def reference(x, c, ada_w, ada_b, mix_pre_g, mix_post_g, ffn_pre_g, ffn_post_g, w_in, w_alpha, b_alpha,
              gla_norm_g, s5_lambda_re, s5_lambda_im, s5_log_dt, s5_b_re, s5_b_im, s5_c_re, s5_c_im, s5_d,
              s5_glu_w, s5_glu_b, w_out, router_w, router_b, exp_w_gu, exp_b_gu, exp_w_down, exp_b_down):
    B_, T_, D_ = x.shape
    h = x
    for l in range(DEPTH):
        mod = jax.nn.silu(c) @ ada_w[l] + ada_b[l]
        sh1, sc1, g1, sh2, sc2, g2 = jnp.split(mod, 6, axis=-1)

        hn = rmsnorm(h, mix_pre_g[l]) * (1.0 + sc1[:, None, :]) + sh1[:, None, :]
        proj = hn @ w_in[l]
        q, k, v, og, a_lr, u = jnp.split(proj, SPLITS, axis=-1)

        def heads(t, dh):
            return t.reshape(B_, T_, GLA_HEADS, dh).transpose(0, 2, 1, 3)

        log_a = jax.nn.log_sigmoid((a_lr @ w_alpha[l] + b_alpha[l]).astype(jnp.float32)) / GLA_TAU
        o = gla_chunked(heads(q, GLA_DK), heads(k, GLA_DK), heads(v, GLA_DV), heads(log_a, GLA_DK))
        o = rmsnorm(o, gla_norm_g[l]).transpose(0, 2, 1, 3)
        gla_out = (o * jax.nn.silu(og.astype(jnp.float32)).reshape(B_, T_, GLA_HEADS, GLA_DV))
        gla_out = gla_out.reshape(B_, T_, GLA_WIDTH).astype(hn.dtype)

        s5_out = s5_layer(u, s5_lambda_re[l], s5_lambda_im[l], s5_log_dt[l], s5_b_re[l], s5_b_im[l],
                          s5_c_re[l], s5_c_im[l], s5_d[l], s5_glu_w[l], s5_glu_b[l]).astype(hn.dtype)

        mix = jnp.concatenate([gla_out, s5_out], axis=-1) @ w_out[l]
        h = h + g1[:, None, :] * rmsnorm(mix, mix_post_g[l])

        hn2 = rmsnorm(h, ffn_pre_g[l]) * (1.0 + sc2[:, None, :]) + sh2[:, None, :]
        ff = moe_ffn(hn2, router_w[l], router_b[l], exp_w_gu[l], exp_b_gu[l], exp_w_down[l], exp_b_down[l])
        h = h + g2[:, None, :] * rmsnorm(ff, ffn_post_g[l])
    return h
```

```python
import functools
import math

import jax
import jax.numpy as jnp
from jax import lax
from jax.experimental import pallas as pl
from jax.experimental.pallas import tpu as pltpu

f32 = jnp.float32
bf16 = jnp.bfloat16
i32 = jnp.int32

D = 1024
B = 8
T = 2048
N = B * T
GLA_H = 4
GLA_DV = 128
GLA_DK = 64
GLA_TAU = 16.0
GLA_CHUNK = 64
LOWRANK = 16
S5_W = 512
S5_CH = 16
S5_G = 32
S5_P = 64
NE = 32
TOPK = 4
DFF = 1024
ALPHA = 1.702
LIMIT = 7.0
EPS = 1e-6

LANES = 128
SUBLANES = 8
VMEM_LIMIT = 56 * 1024 * 1024

TT = 512
S5_L = 16
S5_M = S5_W // LANES
S5_GPT = LANES // S5_CH
S5_ST = S5_GPT * S5_P
S5_ROWS = 256
MB = 128
NB = (N * TOPK) // MB + NE
CAP = NB * MB
ET = 256


def _dot(a, b):
    return jnp.dot(a, b, preferred_element_type=f32)


def _dot_t(a, b, ca, cb):
    return lax.dot_general(a, b, (((ca,), (cb,)), ((), ())), preferred_element_type=f32)


def _rms(x):
    return lax.rsqrt(jnp.mean(x * x, axis=-1, keepdims=True) + EPS)


def _ada_kernel(c_ref, w_ref, b_ref, o_ref):
    c = c_ref[...]
    s = (c * jax.nn.sigmoid(c)).astype(bf16)
    o_ref[...] = _dot(s, w_ref[...].astype(bf16)) + b_ref[...]


def _ada(c, w, b):
    return pl.pallas_call(
        _ada_kernel,
        grid=(6,),
        in_specs=[
            pl.BlockSpec((B, D), lambda j: (0, 0)),
            pl.BlockSpec((D, D), lambda j: (0, j)),
            pl.BlockSpec((1, D), lambda j: (0, j)),
        ],
        out_specs=pl.BlockSpec((B, D), lambda j: (0, j)),
        out_shape=jax.ShapeDtypeStruct((B, 6 * D), f32),
        name="ada",
    )(c, w, b.reshape(1, 6 * D))


def _mixer_in_kernel(x_ref, sc_ref, sh_ref, g_ref, wq, wk, wv, wg, wa, wu, walpha, balpha, gng,
                     gla_ref, u0, u1, u2, u3, q_s, k_s, v_s, og_s, la_s, st_s):
    t = pl.program_id(1)

    @pl.when(t == 0)
    def _():
        st_s[...] = jnp.zeros_like(st_s)

    x = x_ref[...]
    hn = (x * _rms(x) * g_ref[...]) * (1.0 + sc_ref[...]) + sh_ref[...]
    hb = hn.astype(bf16)
    q_s[...] = _dot(hb, wq[...]) * (GLA_DK ** -0.5)
    k_s[...] = _dot(hb, wk[...])
    v_s[...] = _dot(hb, wv[...])
    og_s[...] = _dot(hb, wg[...])
    a_lr = _dot(hb, wa[...]).astype(bf16)
    la_s[...] = jax.nn.log_sigmoid(_dot(a_lr, walpha[...]) + balpha[...]) * (1.0 / GLA_TAU)
    u = _dot(hb, wu[...]).astype(bf16)
    for m, u_ref in enumerate((u0, u1, u2, u3)):
        u_ref[...] = u[:, m * LANES:(m + 1) * LANES].reshape(TT // S5_L, S5_L, LANES)

    C = GLA_CHUNK
    row = lax.broadcasted_iota(i32, (C, C), 0)
    col = lax.broadcasted_iota(i32, (C, C), 1)
    causal = row >= col
    tri = causal.astype(bf16)
    gn = gng[...]

    def chunk(c, carry):
        r0 = pl.multiple_of(c * C, C)
        for h in range(GLA_H):
            sl = slice(h * LANES, (h + 1) * LANES)
            la = la_s[pl.ds(r0, C), sl]
            la_hi = la.astype(bf16)
            la_lo = (la - la_hi.astype(f32)).astype(bf16)
            bc = _dot(tri, la_hi) + _dot(tri, la_lo)
            bl = bc[C - 1:C, :]
            q = q_s[pl.ds(r0, C), sl]
            k = k_s[pl.ds(r0, C), sl]
            vb = v_s[pl.ds(r0, C), sl].astype(bf16)
            qd = (q * jnp.exp(bc)).astype(bf16)
            ki = (k * jnp.exp(-bc)).astype(bf16)
            ke = (k * jnp.exp(bl - bc)).astype(bf16)
            sc = _dot_t(qd, ki, 1, 1)
            sc = jnp.where(causal, sc, 0.0).astype(bf16)
            st = st_s[h]
            o = _dot(sc, vb) + _dot_t(qd, st.astype(bf16), 1, 1)
            st_s[h] = st * jnp.exp(bl) + _dot_t(vb, ke, 0, 0)
            on = o * _rms(o) * gn
            og = og_s[pl.ds(r0, C), sl]
            gla_ref[pl.ds(r0, C), sl] = (on * (og * jax.nn.sigmoid(og))).astype(bf16)
        return carry

    lax.fori_loop(0, TT // C, chunk, 0)


def _mixer_in(x, mod3, g, wq, wk, wv, wg, wa, wu, walpha, balpha, gng):
    nt = T // TT
    full = lambda shape: pl.BlockSpec(shape, lambda b, t: (0,) * len(shape))
    u_shape = jax.ShapeDtypeStruct((T // S5_L, B, S5_L, LANES), bf16)
    u_spec = pl.BlockSpec((TT // S5_L, None, S5_L, LANES), lambda b, t: (t, b, 0, 0))
    return pl.pallas_call(
        _mixer_in_kernel,
        grid=(B, nt),
        in_specs=[
            pl.BlockSpec((None, TT, D), lambda b, t: (b, t, 0)),
            pl.BlockSpec((None, 1, D), lambda b, t: (b, 0, 1)),
            pl.BlockSpec((None, 1, D), lambda b, t: (b, 0, 0)),
            full((1, D)),
            full((D, 512)), full((D, 512)), full((D, 512)), full((D, 512)),
            full((D, LANES)), full((D, 512)),
            full((LANES, 512)), full((1, 512)), full((1, LANES)),
        ],
        out_specs=[pl.BlockSpec((None, TT, 512), lambda b, t: (b, t, 0)), u_spec, u_spec, u_spec, u_spec],
        out_shape=[jax.ShapeDtypeStruct((B, T, 512), bf16), u_shape, u_shape, u_shape, u_shape],
        scratch_shapes=[
            pltpu.VMEM((TT, 512), f32), pltpu.VMEM((TT, 512), f32), pltpu.VMEM((TT, 512), f32),
            pltpu.VMEM((TT, 512), f32), pltpu.VMEM((TT, 512), f32),
            pltpu.VMEM((GLA_H, GLA_DV, LANES), f32),
        ],
        compiler_params=pltpu.CompilerParams(
            dimension_semantics=("arbitrary", "arbitrary"), vmem_limit_bytes=VMEM_LIMIT),
        name="mixer_in",
    )(x, mod3, mod3, g, wq, wk, wv, wg, wa, wu, walpha, balpha, gng)


def _cpow(xr, xi, d):
    mag = jnp.exp(xr * d)
    return mag * jnp.cos(xi * d), mag * jnp.sin(xi * d)


def _s5_prep_kernel(lr_r, li_r, ld_r, lr_c, li_c, ld_c, btr, bti, ctr, cti, d_r,
                    wt_ref, mi_ref, mo_ref, a_ref, kt_s):
    L = S5_L
    lr = lr_r[...]
    li = li_r[...]
    dt = jnp.exp(ld_r[...])
    xr, xi = lr * dt, li * dt
    ar, ai = _cpow(xr, xi, 1.0)
    den = lr * lr + li * li
    fr = ((ar - 1.0) * lr + ai * li) / den
    fi = (ai * lr - (ar - 1.0) * li) / den
    br, bi = btr[...], bti[...]
    bbr = fr * br - fi * bi
    bbi = fr * bi + fi * br
    a16r, a16i = _cpow(xr, xi, float(L))
    a_ref[:, :S5_ST] = a16r
    a_ref[:, S5_ST:] = a16i
    cr, ci = ctr[...], cti[...]
    for d in range(L):
        pr, pi = _cpow(xr, xi, float(d))
        xdr = bbr * pr - bbi * pi
        xdi = bbr * pi + bbi * pr
        i = L - 1 - d
        mo_ref[i * LANES:(i + 1) * LANES, :S5_ST] = xdr.astype(bf16)
        mo_ref[i * LANES:(i + 1) * LANES, S5_ST:] = xdi.astype(bf16)
        kt = (jnp.dot(xdr, cr, preferred_element_type=f32, precision=lax.Precision.HIGHEST)
              - jnp.dot(xdi, ci, preferred_element_type=f32, precision=lax.Precision.HIGHEST))
        if d == 0:
            r = lax.broadcasted_iota(i32, (LANES, LANES), 0)
            c = lax.broadcasted_iota(i32, (LANES, LANES), 1)
            kt = kt + jnp.where(r == c, d_r[...], 0.0)
        kt_s[d] = kt.astype(bf16)
    zero = jnp.zeros((LANES, LANES), bf16)
    for i in range(L):
        for j in range(L):
            wt_ref[i * LANES:(i + 1) * LANES, j * LANES:(j + 1) * LANES] = kt_s[j - i] if j >= i else zero
    lrc = lr_c[...]
    lic = li_c[...]
    dtc = jnp.exp(ld_c[...])
    xrc, xic = lrc * dtc, lic * dtc
    for j in range(L):
        pr, pi = _cpow(xrc, xic, float(j + 1))
        mi_ref[:S5_ST, j * LANES:(j + 1) * LANES] = (cr * pr - ci * pi).astype(bf16)
        mi_ref[S5_ST:, j * LANES:(j + 1) * LANES] = (-(cr * pi + ci * pr)).astype(bf16)


def _s5_prep(lr, li, ld, b_re, b_im, c_re, c_im, dvec):
    G, P, H, M, GPT = S5_G, S5_P, S5_CH, S5_M, S5_GPT
    eye = jnp.eye(GPT, dtype=f32)

    def rows(v):
        return v.reshape(M, 1, S5_ST)

    def cols(v):
        return v.reshape(M, S5_ST, 1)

    ldp = jnp.broadcast_to(ld[:, None], (G, P))

    def bt(b):
        b4 = b.reshape(M, GPT, P, H)
        return jnp.einsum('mgph,gk->mkhgp', b4, eye).reshape(M, LANES, S5_ST)

    def ct(c):
        c4 = c.reshape(M, GPT, H, P)
        return jnp.einsum('mghp,gk->mgpkh', c4, eye).reshape(M, S5_ST, LANES)

    L = S5_L
    mspec = lambda shape: pl.BlockSpec((None,) + shape, lambda m: (m,) + (0,) * len(shape))
    return pl.pallas_call(
        _s5_prep_kernel,
        grid=(M,),
        in_specs=[mspec((1, S5_ST))] * 3 + [mspec((S5_ST, 1))] * 3
        + [mspec((LANES, S5_ST))] * 2 + [mspec((S5_ST, LANES))] * 2 + [mspec((1, LANES))],
        out_specs=[mspec((L * LANES, L * LANES)), mspec((2 * S5_ST, L * LANES)),
                   mspec((L * LANES, 2 * S5_ST)), mspec((1, 2 * S5_ST))],
        out_shape=[jax.ShapeDtypeStruct((M, L * LANES, L * LANES), bf16),
                   jax.ShapeDtypeStruct((M, 2 * S5_ST, L * LANES), bf16),
                   jax.ShapeDtypeStruct((M, L * LANES, 2 * S5_ST), bf16),
                   jax.ShapeDtypeStruct((M, 1, 2 * S5_ST), f32)],
        scratch_shapes=[pltpu.VMEM((L, LANES, LANES), bf16)],
        compiler_params=pltpu.CompilerParams(dimension_semantics=("arbitrary",), vmem_limit_bytes=VMEM_LIMIT),
        name="s5_prep",
    )(rows(lr), rows(li), rows(ldp), cols(lr), cols(li), cols(ldp),
      bt(b_re), bt(b_im), ct(c_re), ct(c_im), dvec.reshape(M, 1, LANES))


def _s5_scan_kernel(u_ref, wt, mi, mo, a_ref, y_ref, xs, s_s, x_s):
    r = pl.program_id(1)

    @pl.when(r == 0)
    def _():
        xs[...] = jnp.zeros_like(xs)

    u = u_ref[...]
    s_s[...] = _dot(u, mo[...])
    ar = a_ref[:, :S5_ST]
    ai = a_ref[:, S5_ST:]

    def step(n, carry):
        xr, xi = carry
        r0 = pl.multiple_of(n * B, B)
        x_s[pl.ds(r0, B), :S5_ST] = xr
        x_s[pl.ds(r0, B), S5_ST:] = xi
        sr = s_s[pl.ds(r0, B), :S5_ST]
        si = s_s[pl.ds(r0, B), S5_ST:]
        return ar * xr - ai * xi + sr, ar * xi + ai * xr + si

    xr, xi = lax.fori_loop(0, S5_ROWS // B, step, (xs[:, :S5_ST], xs[:, S5_ST:]))
    xs[:, :S5_ST] = xr
    xs[:, S5_ST:] = xi
    y = _dot(u, wt[...]) + _dot(x_s[...].astype(bf16), mi[...])
    y_ref[...] = jax.nn.gelu(y).astype(bf16)


def _s5_scan(u, wt, mi, mo, a16):
    L = S5_L
    rows = (T // L) * B
    wspec = lambda shape: pl.BlockSpec((None,) + shape, lambda m, r: (m,) + (0,) * len(shape))
    return pl.pallas_call(
        _s5_scan_kernel,
        grid=(S5_M, rows // S5_ROWS),
        in_specs=[pl.BlockSpec((None, S5_ROWS, L * LANES), lambda m, r: (m, r, 0)),
                  wspec((L * LANES, L * LANES)), wspec((2 * S5_ST, L * LANES)),
                  wspec((L * LANES, 2 * S5_ST)), wspec((1, 2 * S5_ST))],
        out_specs=pl.BlockSpec((None, S5_ROWS, L * LANES), lambda m, r: (m, r, 0)),
        out_shape=jax.ShapeDtypeStruct((S5_M, rows, L * LANES), bf16),
        scratch_shapes=[pltpu.VMEM((B, 2 * S5_ST), f32), pltpu.VMEM((S5_ROWS, 2 * S5_ST), f32),
                        pltpu.VMEM((S5_ROWS, 2 * S5_ST), f32)],
        compiler_params=pltpu.CompilerParams(
            dimension_semantics=("arbitrary", "arbitrary"), vmem_limit_bytes=VMEM_LIMIT),
        name="s5_scan",
    )(u, wt, mi, mo, a16)


def _mixer_out_kernel(x_ref, gla_ref, y0, y1, y2, y3, g1_ref, sh2_ref, sc2_ref, gluw, glub, woa, wob,
                      mpg, fpg, rw, rb, h1_ref, hn2_ref, meta_ref, gate_ref, cnt_ref, cnt_s):
    first = jnp.logical_and(pl.program_id(0) == 0, pl.program_id(1) == 0)

    @pl.when(first)
    def _():
        cnt_s[...] = jnp.zeros_like(cnt_s)

    y = jnp.concatenate([r[...].reshape(TT, LANES) for r in (y0, y1, y2, y3)], axis=1)
    z = _dot(y, gluw[...]) + glub[...]
    s5 = y.astype(f32) * jax.nn.sigmoid(z)
    mix = _dot(gla_ref[...], woa[...]) + _dot(s5.astype(bf16), wob[...])
    h1 = x_ref[...] + g1_ref[...] * (mix * _rms(mix) * mpg[...])
    h1_ref[...] = h1
    hn2 = (h1 * _rms(h1) * fpg[...]) * (1.0 + sc2_ref[...]) + sh2_ref[...]
    hn2_ref[...] = hn2
    logits = jnp.dot(hn2, rw[...], preferred_element_type=f32, precision=lax.Precision.HIGHEST) + rb[...]

    lane = lax.broadcasted_iota(i32, (TT, LANES), 1).astype(f32)
    l = logits
    vals, idxs = [], []
    for _ in range(TOPK):
        m = jnp.max(l, axis=-1, keepdims=True)
        ix = jnp.min(jnp.where(l == m, lane, float(LANES)), axis=-1, keepdims=True)
        vals.append(m)
        idxs.append(ix)
        l = jnp.where(lane == ix, -jnp.inf, l)
    es = [jnp.exp(v - vals[0]) for v in vals]
    tot = es[0] + es[1] + es[2] + es[3]
    oh = jnp.zeros((TT, LANES), f32)
    for ix in idxs:
        oh = oh + (lane == ix).astype(f32)
    r = lax.broadcasted_iota(i32, (TT, TT), 0)
    c = lax.broadcasted_iota(i32, (TT, TT), 1)
    cum = _dot((r >= c).astype(bf16), oh.astype(bf16))
    excl = cum - oh + cnt_s[...]
    meta = jnp.zeros((TT, LANES), f32)
    gate = jnp.zeros((TT, LANES), f32)
    for kk in range(TOPK):
        rank = jnp.sum(jnp.where(lane == idxs[kk], excl, 0.0), axis=-1, keepdims=True)
        meta = jnp.where(lane == float(kk), idxs[kk], meta)
        meta = jnp.where(lane == float(TOPK + kk), rank, meta)
        gate = jnp.where(lane == float(kk), es[kk] / tot, gate)
    meta_ref[...] = meta.astype(i32)
    gate_ref[...] = gate
    cnt = cnt_s[...] + cum[TT - 1:TT, :]
    cnt_s[...] = cnt
    cnt_ref[...] = cnt


def _mixer_out(x, gla, y, mod3, gluw, glub, woa, wob, mpg, fpg, rw, rb):
    nt = T // TT
    full = lambda shape: pl.BlockSpec(shape, lambda b, t: (0,) * len(shape))
    tok = lambda w: pl.BlockSpec((None, TT, w), lambda b, t: (b, t, 0))
    y5 = y.reshape(S5_M, T // S5_L, B, S5_L, LANES)
    yspec = lambda m: pl.BlockSpec((None, TT // S5_L, None, S5_L, LANES), lambda b, t: (m, t, b, 0, 0))
    modspec = lambda j: pl.BlockSpec((None, 1, D), lambda b, t: (b, 0, j))
    return pl.pallas_call(
        _mixer_out_kernel,
        grid=(B, nt),
        in_specs=[tok(D), tok(512), yspec(0), yspec(1), yspec(2), yspec(3),
                  modspec(2), modspec(3), modspec(4),
                  full((512, 512)), full((1, 512)), full((512, D)), full((512, D)),
                  full((1, D)), full((1, D)), full((D, LANES)), full((1, LANES))],
        out_specs=[tok(D), tok(D), tok(LANES), tok(LANES), full((1, LANES))],
        out_shape=[jax.ShapeDtypeStruct((B, T, D), f32), jax.ShapeDtypeStruct((B, T, D), f32),
                   jax.ShapeDtypeStruct((B, T, LANES), i32), jax.ShapeDtypeStruct((B, T, LANES), f32),
                   jax.ShapeDtypeStruct((1, LANES), f32)],
        scratch_shapes=[pltpu.VMEM((1, LANES), f32)],
        compiler_params=pltpu.CompilerParams(
            dimension_semantics=("arbitrary", "arbitrary"), vmem_limit_bytes=VMEM_LIMIT),
        name="mixer_out",
    )(x, gla, y5, y5, y5, y5, mod3, mod3, mod3, gluw, glub, woa, wob, mpg, fpg, rw, rb)


ROW = SUBLANES
BLK_ROWS = MB * ROW


def _experts_kernel(be_ref, nu_ref, src0_ref, src1_ref, dst_ref, w_ref, x_hbm, wgu_ref, bgu_ref, wd_ref, bd_ref,
                    o_hbm, xbuf, ybuf, wgu_b, wd_b, gsem, ssem):
    i = pl.program_id(0)
    nused = nu_ref[0]
    slot = lax.rem(i, 2)
    active = i < nused

    def gather_copy(off, r, s):
        return pltpu.make_async_copy(
            x_hbm.at[pl.ds(pl.multiple_of(off, ROW), ROW), :],
            xbuf.at[s, pl.ds(pl.multiple_of(r * ROW, ROW), ROW), :], gsem.at[s])

    def issue_gather(idx_ref, s):
        def body(r, carry):
            gather_copy(idx_ref[0, r], r, s).start()
            return carry
        lax.fori_loop(0, MB, body, 0)

    def wait_gather(s):
        pltpu.make_async_copy(x_hbm.at[pl.ds(0, BLK_ROWS), :], xbuf.at[s], gsem.at[s]).wait()

    def wait_scatter(s):
        pltpu.make_async_copy(ybuf.at[s], o_hbm.at[pl.ds(0, BLK_ROWS), :], ssem.at[s]).wait()

    @pl.when(i == 0)
    def _():
        issue_gather(src0_ref, 0)

    @pl.when(jnp.logical_and(active, i + 1 < nused))
    def _():
        issue_gather(src1_ref, 1 - slot)

    @pl.when(active)
    def _():
        wait_gather(slot)
        changed = jnp.logical_or(i == 0, be_ref[i] != be_ref[jnp.maximum(i - 1, 0)])

        @pl.when(changed)
        def _():
            def cast(j, carry):
                r0 = pl.multiple_of(j * LANES, LANES)
                wgu_b[pl.ds(r0, LANES), :] = wgu_ref[pl.ds(r0, LANES), :].astype(bf16)
                wd_b[pl.ds(r0, LANES), :] = wd_ref[pl.ds(r0, LANES), :].astype(bf16)
                return carry
            lax.fori_loop(0, D // LANES, cast, 0)

        xb = xbuf.at[slot]
        xt = jnp.concatenate([xb[pl.ds(cc, MB, stride=ROW), :] for cc in range(ROW)], axis=1).astype(bf16)
        gu = _dot(xt, wgu_b[...]) + bgu_ref[...]
        g = jnp.minimum(gu[:, :DFF], LIMIT)
        up = jnp.clip(gu[:, DFF:], -LIMIT, LIMIT)
        act = (up + 1.0) * (g * jax.nn.sigmoid(ALPHA * g))
        yv = (_dot(act.astype(bf16), wd_b[...]) + bd_ref[...]) * w_ref[...]

        @pl.when(i >= 2)
        def _():
            wait_scatter(slot)

        yb = ybuf.at[slot]
        for cc in range(ROW):
            yb[pl.ds(cc, MB, stride=ROW), :] = yv[:, cc * LANES:(cc + 1) * LANES]

        def sbody(r, carry):
            pltpu.make_async_copy(
                ybuf.at[slot, pl.ds(pl.multiple_of(r * ROW, ROW), ROW), :],
                o_hbm.at[pl.ds(pl.multiple_of(dst_ref[0, r], ROW), ROW), :], ssem.at[slot]).start()
            return carry
        lax.fori_loop(0, MB, sbody, 0)

    @pl.when(i == nused)
    def _():
        wait_scatter(0)
        wait_scatter(1)


def _experts(block_e, nused, src, dst, wsorted, hn2_rows, w_gu, b_gu, w_down, b_down):
    nxt = lambda i, be, nu: (jnp.minimum(i + 1, NB - 1), 0, 0)
    cur = lambda i, be, nu: (i, 0, 0)
    smem = lambda imap: pl.BlockSpec((None, 1, MB), imap, memory_space=pltpu.SMEM)
    out_rows = (TOPK * N + 2 * MB) * ROW
    grid_spec = pltpu.PrefetchScalarGridSpec(
        num_scalar_prefetch=2,
        grid=(NB,),
        in_specs=[
            smem(cur), smem(nxt), smem(cur),
            pl.BlockSpec((None, MB, 1), cur),
            pl.BlockSpec(memory_space=pl.ANY),
            pl.BlockSpec((None, D, 2 * DFF), lambda i, be, nu: (be[i], 0, 0)),
            pl.BlockSpec((None, 1, 2 * DFF), lambda i, be, nu: (be[i], 0, 0)),
            pl.BlockSpec((None, DFF, D), lambda i, be, nu: (be[i], 0, 0)),
            pl.BlockSpec((None, 1, D), lambda i, be, nu: (be[i], 0, 0)),
        ],
        out_specs=pl.BlockSpec(memory_space=pl.ANY),
        scratch_shapes=[
            pltpu.VMEM((2, BLK_ROWS, LANES), f32), pltpu.VMEM((2, BLK_ROWS, LANES), f32),
            pltpu.VMEM((D, 2 * DFF), bf16), pltpu.VMEM((DFF, D), bf16),
            pltpu.SemaphoreType.DMA((2,)), pltpu.SemaphoreType.DMA((2,)),
        ],
    )
    return pl.pallas_call(
        _experts_kernel,
        grid_spec=grid_spec,
        out_shape=jax.ShapeDtypeStruct((out_rows, LANES), f32),
        compiler_params=pltpu.CompilerParams(dimension_semantics=("arbitrary",), vmem_limit_bytes=VMEM_LIMIT),
        name="experts",
    )(block_e, nused, src, src, dst, wsorted, hn2_rows, w_gu, b_gu.reshape(NE, 1, 2 * DFF),
      w_down, b_down.reshape(NE, 1, D))


def _combine_kernel(h1_ref, e0, e1, e2, e3, g2_ref, pg_ref, o_ref):
    ff = (e0[...] + e1[...]) + (e2[...] + e3[...])
    ff3 = ff.reshape(ET, ROW, LANES)
    ss = jnp.sum(jnp.sum(ff3 * ff3, axis=2, keepdims=True), axis=1, keepdims=True)
    rs = lax.rsqrt(ss * (1.0 / D) + EPS)
    out = h1_ref[...].reshape(ET, ROW, LANES) + g2_ref[...] * (ff3 * rs * pg_ref[...])
    o_ref[...] = out.reshape(ET * ROW, LANES)


def _combine(h1_rows, e_rows, mod4, pg):
    nt = N // ET
    per_b = T // ET
    espec = lambda k: pl.BlockSpec((ET * ROW, LANES), lambda j: (k * nt + j, 0))
    return pl.pallas_call(
        _combine_kernel,
        grid=(nt,),
        in_specs=[pl.BlockSpec((ET * ROW, LANES), lambda j: (j, 0)),
                  espec(0), espec(1), espec(2), espec(3),
                  pl.BlockSpec((None, None, ROW, LANES), lambda j: (j // per_b, 5, 0, 0)),
                  pl.BlockSpec((ROW, LANES), lambda j: (0, 0))],
        out_specs=pl.BlockSpec((ET * ROW, LANES), lambda j: (j, 0)),
        out_shape=jax.ShapeDtypeStruct((N * ROW, LANES), f32),
        compiler_params=pltpu.CompilerParams(dimension_semantics=("arbitrary",), vmem_limit_bytes=VMEM_LIMIT),
        name="combine",
    )(h1_rows, e_rows, e_rows, e_rows, e_rows, mod4, pg)


def _pad_heads(w):
    lead = w.shape[:-1]
    w4 = w.reshape(lead + (GLA_H, GLA_DK))
    w4 = jnp.concatenate([w4, jnp.zeros_like(w4)], axis=-1)
    return w4.reshape(lead + (GLA_H * LANES,))


def kernel(x, c, ada_w, ada_b, mix_pre_g, mix_post_g, ffn_pre_g, ffn_post_g, w_in, w_alpha, b_alpha, gla_norm_g, s5_lambda_re, s5_lambda_im, s5_log_dt, s5_b_re, s5_b_im, s5_c_re, s5_c_im, s5_d, s5_glu_w, s5_glu_b, w_out, router_w, router_b, exp_w_gu, exp_b_gu, exp_w_down, exp_b_down):
    l = 0
    mod = _ada(c, ada_w[l], ada_b[l])
    mod3 = mod.reshape(B, 1, 6 * D)
    mod4 = mod.reshape(B, 6, ROW, LANES)

    w = w_in[l]
    o_q, o_k, o_v, o_g, o_a, o_u = 0, 256, 512, 1024, 1536, 1552
    wq = _pad_heads(w[:, o_q:o_k]).astype(bf16)
    wk = _pad_heads(w[:, o_k:o_v]).astype(bf16)
    wv = w[:, o_v:o_g].astype(bf16)
    wg = w[:, o_g:o_a].astype(bf16)
    wa = jnp.pad(w[:, o_a:o_u], ((0, 0), (0, LANES - LOWRANK))).astype(bf16)
    wu = w[:, o_u:].astype(bf16)
    walpha = jnp.pad(_pad_heads(w_alpha[l]), ((0, LANES - LOWRANK), (0, 0))).astype(bf16)
    balpha = _pad_heads(b_alpha[l]).reshape(1, GLA_H * LANES)

    gla, u0, u1, u2, u3 = _mixer_in(x, mod3, mix_pre_g[l].reshape(1, D), wq, wk, wv, wg, wa, wu,
                                    walpha, balpha, gla_norm_g[l].reshape(1, GLA_DV))
    rows = (T // S5_L) * B
    u = jnp.stack([a.reshape(rows, S5_L * LANES) for a in (u0, u1, u2, u3)], axis=0)

    wt, mi, mo, a16 = _s5_prep(s5_lambda_re[l], s5_lambda_im[l], s5_log_dt[l], s5_b_re[l], s5_b_im[l],
                               s5_c_re[l], s5_c_im[l], s5_d[l])
    y = _s5_scan(u, wt, mi, mo, a16)

    rw = jnp.pad(router_w[l], ((0, 0), (0, LANES - NE)))
    rb = jnp.pad(router_b[l], (0, LANES - NE), constant_values=-1e30).reshape(1, LANES)
    wo = w_out[l].astype(bf16)
    h1, hn2, meta, gates, cnt = _mixer_out(
        x, gla, y, mod3, s5_glu_w[l].astype(bf16), s5_glu_b[l].reshape(1, 512), wo[:512], wo[512:],
        mix_post_g[l].reshape(1, D), ffn_pre_g[l].reshape(1, D), rw, rb)

    meta = meta.reshape(N, LANES)
    idx = meta[:, :TOPK]
    rank = meta[:, TOPK:2 * TOPK]
    gate = gates.reshape(N, LANES)[:, :TOPK]
    counts = cnt[0, :NE].astype(i32)
    padded = (counts + MB - 1) // MB * MB
    pends = jnp.cumsum(padded)
    pstarts = pends - padded
    dest = (pstarts[idx] + rank).reshape(-1)
    tok = jnp.arange(N, dtype=i32)[:, None]
    kk = jnp.arange(TOPK, dtype=i32)[None, :]
    pos = jnp.arange(CAP, dtype=i32)
    dump = TOPK * N + ((pos // MB) % 2) * MB + pos % MB
    src = jnp.zeros((CAP,), i32).at[dest].set(jnp.broadcast_to(tok, (N, TOPK)).reshape(-1) * ROW)
    dst = (dump * ROW).at[dest].set(((kk * N + tok) * ROW).reshape(-1))
    wsorted = jnp.zeros((CAP,), f32).at[dest].set(gate.reshape(-1))
    block_e = jnp.minimum(
        jnp.searchsorted(pends, jnp.arange(NB, dtype=i32) * MB, side='right'), NE - 1).astype(i32)
    nused = (pends[-1] // MB).astype(i32).reshape(1)

    e_rows = _experts(block_e, nused, src.reshape(NB, 1, MB), dst.reshape(NB, 1, MB),
                      wsorted.reshape(NB, MB, 1), hn2.reshape(N * ROW, LANES),
                      exp_w_gu[l], exp_b_gu[l], exp_w_down[l], exp_b_down[l])

    out = _combine(h1.reshape(N * ROW, LANES), e_rows, mod4, ffn_post_g[l].reshape(ROW, LANES))
    return out.reshape(B, T, D)
```

```python
import functools
import math

import jax
import jax.numpy as jnp
from jax import lax
from jax.experimental import pallas as pl
from jax.experimental.pallas import tpu as pltpu

f32 = jnp.float32
bf16 = jnp.bfloat16
i32 = jnp.int32

D = 1024
B = 8
T = 2048
N = B * T
GLA_H = 4
GLA_DV = 128
GLA_DK = 64
GLA_TAU = 16.0
GLA_CHUNK = 64
LOWRANK = 16
S5_W = 512
S5_CH = 16
S5_G = 32
S5_P = 64
NE = 32
TOPK = 4
DFF = 1024
ALPHA = 1.702
LIMIT = 7.0
EPS = 1e-6

LANES = 128
SUBLANES = 8
VMEM_LIMIT = 56 * 1024 * 1024

TT = 512
S5_L = 16
S5_M = S5_W // LANES
S5_GPT = LANES // S5_CH
S5_ST = S5_GPT * S5_P
S5_ROWS = 256
ROW = SUBLANES
MB = 256
NB = (N * TOPK) // MB + NE
TPE = N // MB
TS = 512
ET = 256
DMA_UNROLL = 8


def _dot(a, b):
    return jnp.dot(a, b, preferred_element_type=f32)


def _dot_t(a, b, ca, cb):
    return lax.dot_general(a, b, (((ca,), (cb,)), ((), ())), preferred_element_type=f32)


def _rms(x):
    return lax.rsqrt(jnp.mean(x * x, axis=-1, keepdims=True) + EPS)


def _ada_kernel(c_ref, w_ref, b_ref, o_ref):
    c = c_ref[...]
    s = (c * jax.nn.sigmoid(c)).astype(bf16)
    o_ref[...] = _dot(s, w_ref[...].astype(bf16)) + b_ref[...]


def _ada(c, w, b):
    return pl.pallas_call(
        _ada_kernel,
        grid=(6,),
        in_specs=[
            pl.BlockSpec((B, D), lambda j: (0, 0)),
            pl.BlockSpec((D, D), lambda j: (0, j)),
            pl.BlockSpec((1, D), lambda j: (0, j)),
        ],
        out_specs=pl.BlockSpec((B, D), lambda j: (0, j)),
        out_shape=jax.ShapeDtypeStruct((B, 6 * D), f32),
        name="ada",
    )(c, w, b.reshape(1, 6 * D))


def _mixer_in_kernel(x_ref, sc_ref, sh_ref, g_ref, wq, wk, wv, wg, wa, wu, walpha, balpha, gng,
                     gla_ref, u0, u1, u2, u3, q_s, k_s, v_s, og_s, la_s, st_s):
    t = pl.program_id(1)

    @pl.when(t == 0)
    def _():
        st_s[...] = jnp.zeros_like(st_s)

    x = x_ref[...]
    hn = (x * _rms(x) * g_ref[...]) * (1.0 + sc_ref[...]) + sh_ref[...]
    hb = hn.astype(bf16)
    q_s[...] = _dot(hb, wq[...]) * (GLA_DK ** -0.5)
    k_s[...] = _dot(hb, wk[...])
    v_s[...] = _dot(hb, wv[...])
    og_s[...] = _dot(hb, wg[...])
    a_lr = _dot(hb, wa[...]).astype(bf16)
    la_s[...] = jax.nn.log_sigmoid(_dot(a_lr, walpha[...]) + balpha[...]) * (1.0 / GLA_TAU)
    u = _dot(hb, wu[...]).astype(bf16)
    for m, u_ref in enumerate((u0, u1, u2, u3)):
        u_ref[...] = u[:, m * LANES:(m + 1) * LANES].reshape(TT // S5_L, S5_L, LANES)

    C = GLA_CHUNK
    row = lax.broadcasted_iota(i32, (C, C), 0)
    col = lax.broadcasted_iota(i32, (C, C), 1)
    causal = row >= col
    tri = causal.astype(bf16)
    gn = gng[...]

    def chunk(c, carry):
        r0 = pl.multiple_of(c * C, C)
        for h in range(GLA_H):
            sl = slice(h * LANES, (h + 1) * LANES)
            la = la_s[pl.ds(r0, C), sl]
            la_hi = la.astype(bf16)
            la_lo = (la - la_hi.astype(f32)).astype(bf16)
            bc = _dot(tri, la_hi) + _dot(tri, la_lo)
            bl = bc[C - 1:C, :]
            q = q_s[pl.ds(r0, C), sl]
            k = k_s[pl.ds(r0, C), sl]
            vb = v_s[pl.ds(r0, C), sl].astype(bf16)
            qd = (q * jnp.exp(bc)).astype(bf16)
            ki = (k * jnp.exp(-bc)).astype(bf16)
            ke = (k * jnp.exp(bl - bc)).astype(bf16)
            sc = _dot_t(qd, ki, 1, 1)
            sc = jnp.where(causal, sc, 0.0).astype(bf16)
            st = st_s[h]
            o = _dot(sc, vb) + _dot_t(qd, st.astype(bf16), 1, 1)
            st_s[h] = st * jnp.exp(bl) + _dot_t(vb, ke, 0, 0)
            on = o * _rms(o) * gn
            og = og_s[pl.ds(r0, C), sl]
            gla_ref[pl.ds(r0, C), sl] = (on * (og * jax.nn.sigmoid(og))).astype(bf16)
        return carry

    lax.fori_loop(0, TT // C, chunk, 0)


def _mixer_in(x, mod3, g, wq, wk, wv, wg, wa, wu, walpha, balpha, gng):
    nt = T // TT
    full = lambda shape: pl.BlockSpec(shape, lambda b, t: (0,) * len(shape))
    u_shape = jax.ShapeDtypeStruct((T // S5_L, B, S5_L, LANES), bf16)
    u_spec = pl.BlockSpec((TT // S5_L, None, S5_L, LANES), lambda b, t: (t, b, 0, 0))
    return pl.pallas_call(
        _mixer_in_kernel,
        grid=(B, nt),
        in_specs=[
            pl.BlockSpec((None, TT, D), lambda b, t: (b, t, 0)),
            pl.BlockSpec((None, 1, D), lambda b, t: (b, 0, 1)),
            pl.BlockSpec((None, 1, D), lambda b, t: (b, 0, 0)),
            full((1, D)),
            full((D, 512)), full((D, 512)), full((D, 512)), full((D, 512)),
            full((D, LANES)), full((D, 512)),
            full((LANES, 512)), full((1, 512)), full((1, LANES)),
        ],
        out_specs=[pl.BlockSpec((None, TT, 512), lambda b, t: (b, t, 0)), u_spec, u_spec, u_spec, u_spec],
        out_shape=[jax.ShapeDtypeStruct((B, T, 512), bf16), u_shape, u_shape, u_shape, u_shape],
        scratch_shapes=[
            pltpu.VMEM((TT, 512), f32), pltpu.VMEM((TT, 512), f32), pltpu.VMEM((TT, 512), f32),
            pltpu.VMEM((TT, 512), f32), pltpu.VMEM((TT, 512), f32),
            pltpu.VMEM((GLA_H, GLA_DV, LANES), f32),
        ],
        compiler_params=pltpu.CompilerParams(
            dimension_semantics=("arbitrary", "arbitrary"), vmem_limit_bytes=VMEM_LIMIT),
        name="mixer_in",
    )(x, mod3, mod3, g, wq, wk, wv, wg, wa, wu, walpha, balpha, gng)


def _cpow(xr, xi, d):
    mag = jnp.exp(xr * d)
    return mag * jnp.cos(xi * d), mag * jnp.sin(xi * d)


def _s5_prep_kernel(lr_r, li_r, ld_r, lr_c, li_c, ld_c, btr, bti, ctr, cti, d_r,
                    wt_ref, mi_ref, mo_ref, a_ref, kt_s):
    L = S5_L
    lr = lr_r[...]
    li = li_r[...]
    dt = jnp.exp(ld_r[...])
    xr, xi = lr * dt, li * dt
    ar, ai = _cpow(xr, xi, 1.0)
    den = lr * lr + li * li
    fr = ((ar - 1.0) * lr + ai * li) / den
    fi = (ai * lr - (ar - 1.0) * li) / den
    br, bi = btr[...], bti[...]
    bbr = fr * br - fi * bi
    bbi = fr * bi + fi * br
    a16r, a16i = _cpow(xr, xi, float(L))
    a_ref[:, :S5_ST] = a16r
    a_ref[:, S5_ST:] = a16i
    cr, ci = ctr[...], cti[...]
    for d in range(L):
        pr, pi = _cpow(xr, xi, float(d))
        xdr = bbr * pr - bbi * pi
        xdi = bbr * pi + bbi * pr
        i = L - 1 - d
        mo_ref[i * LANES:(i + 1) * LANES, :S5_ST] = xdr.astype(bf16)
        mo_ref[i * LANES:(i + 1) * LANES, S5_ST:] = xdi.astype(bf16)
        kt = (jnp.dot(xdr, cr, preferred_element_type=f32, precision=lax.Precision.HIGHEST)
              - jnp.dot(xdi, ci, preferred_element_type=f32, precision=lax.Precision.HIGHEST))
        if d == 0:
            r = lax.broadcasted_iota(i32, (LANES, LANES), 0)
            c = lax.broadcasted_iota(i32, (LANES, LANES), 1)
            kt = kt + jnp.where(r == c, d_r[...], 0.0)
        kt_s[d] = kt.astype(bf16)
    zero = jnp.zeros((LANES, LANES), bf16)
    for i in range(L):
        for j in range(L):
            wt_ref[i * LANES:(i + 1) * LANES, j * LANES:(j + 1) * LANES] = kt_s[j - i] if j >= i else zero
    lrc = lr_c[...]
    lic = li_c[...]
    dtc = jnp.exp(ld_c[...])
    xrc, xic = lrc * dtc, lic * dtc
    for j in range(L):
        pr, pi = _cpow(xrc, xic, float(j + 1))
        mi_ref[:S5_ST, j * LANES:(j + 1) * LANES] = (cr * pr - ci * pi).astype(bf16)
        mi_ref[S5_ST:, j * LANES:(j + 1) * LANES] = (-(cr * pi + ci * pr)).astype(bf16)


def _s5_prep(lr, li, ld, b_re, b_im, c_re, c_im, dvec):
    G, P, H, M, GPT = S5_G, S5_P, S5_CH, S5_M, S5_GPT
    eye = jnp.eye(GPT, dtype=f32)

    def rows(v):
        return v.reshape(M, 1, S5_ST)

    def cols(v):
        return v.reshape(M, S5_ST, 1)

    ldp = jnp.broadcast_to(ld[:, None], (G, P))

    def bt(b):
        b4 = b.reshape(M, GPT, P, H)
        return jnp.einsum('mgph,gk->mkhgp', b4, eye).reshape(M, LANES, S5_ST)

    def ct(c):
        c4 = c.reshape(M, GPT, H, P)
        return jnp.einsum('mghp,gk->mgpkh', c4, eye).reshape(M, S5_ST, LANES)

    L = S5_L
    mspec = lambda shape: pl.BlockSpec((None,) + shape, lambda m: (m,) + (0,) * len(shape))
    return pl.pallas_call(
        _s5_prep_kernel,
        grid=(M,),
        in_specs=[mspec((1, S5_ST))] * 3 + [mspec((S5_ST, 1))] * 3
        + [mspec((LANES, S5_ST))] * 2 + [mspec((S5_ST, LANES))] * 2 + [mspec((1, LANES))],
        out_specs=[mspec((L * LANES, L * LANES)), mspec((2 * S5_ST, L * LANES)),
                   mspec((L * LANES, 2 * S5_ST)), mspec((1, 2 * S5_ST))],
        out_shape=[jax.ShapeDtypeStruct((M, L * LANES, L * LANES), bf16),
                   jax.ShapeDtypeStruct((M, 2 * S5_ST, L * LANES), bf16),
                   jax.ShapeDtypeStruct((M, L * LANES, 2 * S5_ST), bf16),
                   jax.ShapeDtypeStruct((M, 1, 2 * S5_ST), f32)],
        scratch_shapes=[pltpu.VMEM((L, LANES, LANES), bf16)],
        compiler_params=pltpu.CompilerParams(dimension_semantics=("arbitrary",), vmem_limit_bytes=VMEM_LIMIT),
        name="s5_prep",
    )(rows(lr), rows(li), rows(ldp), cols(lr), cols(li), cols(ldp),
      bt(b_re), bt(b_im), ct(c_re), ct(c_im), dvec.reshape(M, 1, LANES))


def _s5_scan_kernel(u_ref, wt, mi, mo, a_ref, y_ref, xs, s_s, x_s):
    r = pl.program_id(1)

    @pl.when(r == 0)
    def _():
        xs[...] = jnp.zeros_like(xs)

    u = u_ref[...]
    s_s[...] = _dot(u, mo[...])
    ar = a_ref[:, :S5_ST]
    ai = a_ref[:, S5_ST:]

    def step(n, carry):
        xr, xi = carry
        r0 = pl.multiple_of(n * B, B)
        x_s[pl.ds(r0, B), :S5_ST] = xr
        x_s[pl.ds(r0, B), S5_ST:] = xi
        sr = s_s[pl.ds(r0, B), :S5_ST]
        si = s_s[pl.ds(r0, B), S5_ST:]
        return ar * xr - ai * xi + sr, ar * xi + ai * xr + si

    xr, xi = lax.fori_loop(0, S5_ROWS // B, step, (xs[:, :S5_ST], xs[:, S5_ST:]))
    xs[:, :S5_ST] = xr
    xs[:, S5_ST:] = xi
    y = _dot(u, wt[...]) + _dot(x_s[...].astype(bf16), mi[...])
    y_ref[...] = jax.nn.gelu(y).astype(bf16)


def _s5_scan(u, wt, mi, mo, a16):
    L = S5_L
    rows = (T // L) * B
    wspec = lambda shape: pl.BlockSpec((None,) + shape, lambda m, r: (m,) + (0,) * len(shape))
    return pl.pallas_call(
        _s5_scan_kernel,
        grid=(S5_M, rows // S5_ROWS),
        in_specs=[pl.BlockSpec((None, S5_ROWS, L * LANES), lambda m, r: (m, r, 0)),
                  wspec((L * LANES, L * LANES)), wspec((2 * S5_ST, L * LANES)),
                  wspec((L * LANES, 2 * S5_ST)), wspec((1, 2 * S5_ST))],
        out_specs=pl.BlockSpec((None, S5_ROWS, L * LANES), lambda m, r: (m, r, 0)),
        out_shape=jax.ShapeDtypeStruct((S5_M, rows, L * LANES), bf16),
        scratch_shapes=[pltpu.VMEM((B, 2 * S5_ST), f32), pltpu.VMEM((S5_ROWS, 2 * S5_ST), f32),
                        pltpu.VMEM((S5_ROWS, 2 * S5_ST), f32)],
        compiler_params=pltpu.CompilerParams(
            dimension_semantics=("arbitrary", "arbitrary"), vmem_limit_bytes=VMEM_LIMIT),
        name="s5_scan",
    )(u, wt, mi, mo, a16)


def _mixer_out_kernel(x_ref, gla_ref, y0, y1, y2, y3, g1_ref, sh2_ref, sc2_ref, gluw, glub, woa, wob,
                      mpg, fpg, rw_hi, rw_lo, rb, h1_ref, hn2_ref, meta_ref, gate_ref, cnt_ref, cnt_s):
    first = jnp.logical_and(pl.program_id(0) == 0, pl.program_id(1) == 0)

    @pl.when(first)
    def _():
        cnt_s[...] = jnp.zeros_like(cnt_s)

    y = jnp.concatenate([r[...].reshape(TT, LANES) for r in (y0, y1, y2, y3)], axis=1)
    z = _dot(y, gluw[...]) + glub[...]
    s5 = y.astype(f32) * jax.nn.sigmoid(z)
    mix = _dot(gla_ref[...], woa[...]) + _dot(s5.astype(bf16), wob[...])
    h1 = x_ref[...] + g1_ref[...] * (mix * _rms(mix) * mpg[...])
    hn2 = (h1 * _rms(h1) * fpg[...]) * (1.0 + sc2_ref[...]) + sh2_ref[...]
    for cc in range(ROW):
        h1_ref[pl.ds(cc, TT, stride=ROW), :] = h1[:, cc * LANES:(cc + 1) * LANES]
        hn2_ref[pl.ds(cc, TT, stride=ROW), :] = hn2[:, cc * LANES:(cc + 1) * LANES]
    x_hi = hn2.astype(bf16)
    x_lo = (hn2 - x_hi.astype(f32)).astype(bf16)
    logits = _dot(x_hi, rw_hi[...]) + (_dot(x_lo, rw_hi[...]) + _dot(x_hi, rw_lo[...])) + rb[...]

    lane = lax.broadcasted_iota(i32, (TT, LANES), 1).astype(f32)
    l = logits
    vals, idxs = [], []
    for _ in range(TOPK):
        m = jnp.max(l, axis=-1, keepdims=True)
        ix = jnp.min(jnp.where(l == m, lane, float(LANES)), axis=-1, keepdims=True)
        vals.append(m)
        idxs.append(ix)
        l = jnp.where(lane == ix, -jnp.inf, l)
    es = [jnp.exp(v - vals[0]) for v in vals]
    tot = es[0] + es[1] + es[2] + es[3]
    oh = jnp.zeros((TT, LANES), f32)
    for ix in idxs:
        oh = oh + (lane == ix).astype(f32)
    r = lax.broadcasted_iota(i32, (TT, TT), 0)
    c = lax.broadcasted_iota(i32, (TT, TT), 1)
    cum = _dot((r >= c).astype(bf16), oh.astype(bf16))
    excl = cum - oh + cnt_s[...]
    meta = jnp.zeros((TT, LANES), f32)
    gate = jnp.zeros((TT, LANES), f32)
    for kk in range(TOPK):
        rank = jnp.sum(jnp.where(lane == idxs[kk], excl, 0.0), axis=-1, keepdims=True)
        meta = jnp.where(lane == float(kk), idxs[kk], meta)
        meta = jnp.where(lane == float(TOPK + kk), rank, meta)
        gate = jnp.where(lane == float(kk), es[kk] / tot, gate)
    meta_ref[...] = meta.astype(i32)
    gate_ref[...] = gate
    cnt = cnt_s[...] + cum[TT - 1:TT, :]
    cnt_s[...] = cnt
    cnt_ref[...] = cnt


def _mixer_out(x, gla, y, mod3, gluw, glub, woa, wob, mpg, fpg, rw_hi, rw_lo, rb):
    nt = T // TT
    full = lambda shape: pl.BlockSpec(shape, lambda b, t: (0,) * len(shape))
    tok = lambda w: pl.BlockSpec((None, TT, w), lambda b, t: (b, t, 0))
    slab = pl.BlockSpec((TT * ROW, LANES), lambda b, t: (b * nt + t, 0))
    slab_shape = jax.ShapeDtypeStruct((N * ROW, LANES), f32)
    y5 = y.reshape(S5_M, T // S5_L, B, S5_L, LANES)
    yspec = lambda m: pl.BlockSpec((None, TT // S5_L, None, S5_L, LANES), lambda b, t: (m, t, b, 0, 0))
    modspec = lambda j: pl.BlockSpec((None, 1, D), lambda b, t: (b, 0, j))
    return pl.pallas_call(
        _mixer_out_kernel,
        grid=(B, nt),
        in_specs=[tok(D), tok(512), yspec(0), yspec(1), yspec(2), yspec(3),
                  modspec(2), modspec(3), modspec(4),
                  full((512, 512)), full((1, 512)), full((512, D)), full((512, D)),
                  full((1, D)), full((1, D)), full((D, LANES)), full((D, LANES)), full((1, LANES))],
        out_specs=[slab, slab, tok(LANES), tok(LANES), full((1, LANES))],
        out_shape=[slab_shape, slab_shape,
                   jax.ShapeDtypeStruct((B, T, LANES), i32), jax.ShapeDtypeStruct((B, T, LANES), f32),
                   jax.ShapeDtypeStruct((1, LANES), f32)],
        scratch_shapes=[pltpu.VMEM((1, LANES), f32)],
        compiler_params=pltpu.CompilerParams(
            dimension_semantics=("arbitrary", "arbitrary"), vmem_limit_bytes=VMEM_LIMIT),
        name="mixer_out",
    )(x, gla, y5, y5, y5, y5, mod3, mod3, mod3, gluw, glub, woa, wob, mpg, fpg, rw_hi, rw_lo, rb)


def _slab(ref, off):
    return ref.at[pl.ds(pl.multiple_of(off, ROW), ROW), :]


def _dispatch_kernel(off_ref, x_ref, xs_hbm, sem):
    def group(g, carry):
        for j in range(DMA_UNROLL):
            t = g * DMA_UNROLL + j
            for kk in range(TOPK):
                pltpu.make_async_copy(_slab(x_ref, t * ROW), _slab(xs_hbm, off_ref[0, t * TOPK + kk]),
                                      sem).start(priority=kk % 2)
        return carry

    lax.fori_loop(0, TS // DMA_UNROLL, group, 0)
    for kk in range(TOPK):
        pltpu.make_async_copy(x_ref, xs_hbm.at[pl.ds(0, TS * ROW), :], sem).wait()


def _dispatch(offs, hn2_rows):
    return pl.pallas_call(
        _dispatch_kernel,
        grid=(N // TS,),
        in_specs=[pl.BlockSpec((None, 1, TS * TOPK), lambda i: (i, 0, 0), memory_space=pltpu.SMEM),
                  pl.BlockSpec((TS * ROW, LANES), lambda i: (i, 0))],
        out_specs=pl.BlockSpec(memory_space=pl.ANY),
        out_shape=jax.ShapeDtypeStruct((NE * N * ROW, LANES), f32),
        scratch_shapes=[pltpu.SemaphoreType.DMA(())],
        compiler_params=pltpu.CompilerParams(dimension_semantics=("arbitrary",), vmem_limit_bytes=VMEM_LIMIT),
        name="dispatch",
    )(offs.reshape(N // TS, 1, TS * TOPK), hn2_rows)


def _experts_kernel(tb_ref, te_ref, tv_ref, nu_ref, x_ref, wgu_ref, bgu_ref, wd_ref, bd_ref, y_ref, wgu_b, wd_b):
    i = pl.program_id(0)

    @pl.when(i < nu_ref[0])
    def _():
        changed = jnp.logical_or(i == 0, te_ref[i] != te_ref[jnp.maximum(i - 1, 0)])

        @pl.when(changed)
        def _():
            def cast(j, carry):
                r0 = pl.multiple_of(j * LANES, LANES)
                wgu_b[pl.ds(r0, LANES), :] = wgu_ref[pl.ds(r0, LANES), :].astype(bf16)
                wd_b[pl.ds(r0, LANES), :] = wd_ref[pl.ds(r0, LANES), :].astype(bf16)
                return carry
            lax.fori_loop(0, D // LANES, cast, 0)

        xt = jnp.concatenate([x_ref[pl.ds(cc, MB, stride=ROW), :] for cc in range(ROW)], axis=1)
        valid = lax.broadcasted_iota(i32, (MB, 1), 0) < tv_ref[i]
        xt = jnp.where(valid, xt, 0.0).astype(bf16)
        gu = _dot(xt, wgu_b[...]) + bgu_ref[...]
        g = jnp.minimum(gu[:, :DFF], LIMIT)
        up = jnp.clip(gu[:, DFF:], -LIMIT, LIMIT)
        act = (up + 1.0) * (g * jax.nn.sigmoid(ALPHA * g))
        yv = _dot(act.astype(bf16), wd_b[...]) + bd_ref[...]
        for cc in range(ROW):
            y_ref[pl.ds(cc, MB, stride=ROW), :] = yv[:, cc * LANES:(cc + 1) * LANES]


def _experts(tile_blk, tile_e, tile_nv, nused, xs, w_gu, b_gu, w_down, b_down):
    blk = lambda i, tb, te, tv, nu: (tb[i], 0)
    wsel = lambda i, tb, te, tv, nu: (te[i], 0, 0)
    grid_spec = pltpu.PrefetchScalarGridSpec(
        num_scalar_prefetch=4,
        grid=(NB,),
        in_specs=[
            pl.BlockSpec((MB * ROW, LANES), blk),
            pl.BlockSpec((None, D, 2 * DFF), wsel),
            pl.BlockSpec((None, 1, 2 * DFF), wsel),
            pl.BlockSpec((None, DFF, D), wsel),
            pl.BlockSpec((None, 1, D), wsel),
        ],
        out_specs=pl.BlockSpec((MB * ROW, LANES), blk),
        scratch_shapes=[pltpu.VMEM((D, 2 * DFF), bf16), pltpu.VMEM((DFF, D), bf16)],
    )
    return pl.pallas_call(
        _experts_kernel,
        grid_spec=grid_spec,
        out_shape=jax.ShapeDtypeStruct((NE * N * ROW, LANES), f32),
        compiler_params=pltpu.CompilerParams(dimension_semantics=("arbitrary",), vmem_limit_bytes=VMEM_LIMIT),
        name="experts",
    )(tile_blk, tile_e, tile_nv, nused, xs, w_gu, b_gu.reshape(NE, 1, 2 * DFF), w_down, b_down.reshape(NE, 1, D))


def _combine_kernel(off_ref, gate_ref, h1_ref, g2_ref, pg_ref, ys_hbm, o_ref, gbuf, wbuf, rbuf, sem):
    def group(g, carry):
        for j in range(DMA_UNROLL):
            t = g * DMA_UNROLL + j
            for kk in range(TOPK):
                pltpu.make_async_copy(_slab(ys_hbm, off_ref[0, t * TOPK + kk]),
                                      gbuf.at[kk, pl.ds(pl.multiple_of(t * ROW, ROW), ROW), :],
                                      sem).start(priority=kk % 2)
        return carry

    lax.fori_loop(0, ET // DMA_UNROLL, group, 0)
    gate = gate_ref[...]
    for s in range(ROW):
        wbuf[pl.ds(s, ET, stride=ROW), :] = gate
    for kk in range(TOPK):
        pltpu.make_async_copy(ys_hbm.at[pl.ds(0, ET * ROW), :], gbuf.at[kk], sem).wait()
    ff = gbuf[0] * wbuf[:, 0:1]
    for kk in range(1, TOPK):
        ff = ff + gbuf[kk] * wbuf[:, kk:kk + 1]
    ff3 = ff.reshape(ET, ROW, LANES)
    ss = jnp.sum(jnp.sum(ff3 * ff3, axis=2, keepdims=True), axis=1, keepdims=True)
    rs = lax.rsqrt(ss * (1.0 / D) + EPS)
    out = h1_ref[...].reshape(ET, ROW, LANES) + g2_ref[...] * (ff3 * rs * pg_ref[...])
    rbuf[...] = out.reshape(ET * ROW, LANES)
    for cc in range(ROW):
        o_ref[:, cc * LANES:(cc + 1) * LANES] = rbuf[pl.ds(cc, ET, stride=ROW), :]


def _combine(offs, gates, h1_rows, mod4, pg, ys):
    nt = N // ET
    per_b = T // ET
    return pl.pallas_call(
        _combine_kernel,
        grid=(nt,),
        in_specs=[pl.BlockSpec((None, 1, ET * TOPK), lambda j: (j, 0, 0), memory_space=pltpu.SMEM),
                  pl.BlockSpec((ET, LANES), lambda j: (j, 0)),
                  pl.BlockSpec((ET * ROW, LANES), lambda j: (j, 0)),
                  pl.BlockSpec((None, None, ROW, LANES), lambda j: (j // per_b, 5, 0, 0)),
                  pl.BlockSpec((ROW, LANES), lambda j: (0, 0)),
                  pl.BlockSpec(memory_space=pl.ANY)],
        out_specs=pl.BlockSpec((ET, D), lambda j: (j, 0)),
        out_shape=jax.ShapeDtypeStruct((N, D), f32),
        scratch_shapes=[pltpu.VMEM((TOPK, ET * ROW, LANES), f32), pltpu.VMEM((ET * ROW, LANES), f32),
                        pltpu.VMEM((ET * ROW, LANES), f32), pltpu.SemaphoreType.DMA(())],
        compiler_params=pltpu.CompilerParams(dimension_semantics=("arbitrary",), vmem_limit_bytes=VMEM_LIMIT),
        name="combine",
    )(offs.reshape(nt, 1, ET * TOPK), gates, h1_rows, mod4, pg, ys)


def _pad_heads(w):
    lead = w.shape[:-1]
    w4 = w.reshape(lead + (GLA_H, GLA_DK))
    w4 = jnp.concatenate([w4, jnp.zeros_like(w4)], axis=-1)
    return w4.reshape(lead + (GLA_H * LANES,))


def kernel(x, c, ada_w, ada_b, mix_pre_g, mix_post_g, ffn_pre_g, ffn_post_g, w_in, w_alpha, b_alpha, gla_norm_g, s5_lambda_re, s5_lambda_im, s5_log_dt, s5_b_re, s5_b_im, s5_c_re, s5_c_im, s5_d, s5_glu_w, s5_glu_b, w_out, router_w, router_b, exp_w_gu, exp_b_gu, exp_w_down, exp_b_down):
    l = 0
    mod = _ada(c, ada_w[l], ada_b[l])
    mod3 = mod.reshape(B, 1, 6 * D)
    mod4 = mod.reshape(B, 6, ROW, LANES)

    w = w_in[l]
    o_q, o_k, o_v, o_g, o_a, o_u = 0, 256, 512, 1024, 1536, 1552
    wq = _pad_heads(w[:, o_q:o_k]).astype(bf16)
    wk = _pad_heads(w[:, o_k:o_v]).astype(bf16)
    wv = w[:, o_v:o_g].astype(bf16)
    wg = w[:, o_g:o_a].astype(bf16)
    wa = jnp.pad(w[:, o_a:o_u], ((0, 0), (0, LANES - LOWRANK))).astype(bf16)
    wu = w[:, o_u:].astype(bf16)
    walpha = jnp.pad(_pad_heads(w_alpha[l]), ((0, LANES - LOWRANK), (0, 0))).astype(bf16)
    balpha = _pad_heads(b_alpha[l]).reshape(1, GLA_H * LANES)

    gla, u0, u1, u2, u3 = _mixer_in(x, mod3, mix_pre_g[l].reshape(1, D), wq, wk, wv, wg, wa, wu,
                                    walpha, balpha, gla_norm_g[l].reshape(1, GLA_DV))
    rows = (T // S5_L) * B
    u = jnp.stack([a.reshape(rows, S5_L * LANES) for a in (u0, u1, u2, u3)], axis=0)

    wt, mi, mo, a16 = _s5_prep(s5_lambda_re[l], s5_lambda_im[l], s5_log_dt[l], s5_b_re[l], s5_b_im[l],
                               s5_c_re[l], s5_c_im[l], s5_d[l])
    y = _s5_scan(u, wt, mi, mo, a16)

    rw = jnp.pad(router_w[l], ((0, 0), (0, LANES - NE)))
    rw_hi = rw.astype(bf16)
    rw_lo = (rw - rw_hi.astype(f32)).astype(bf16)
    rb = jnp.pad(router_b[l], (0, LANES - NE), constant_values=-1e30).reshape(1, LANES)
    wo = w_out[l].astype(bf16)
    h1_rows, hn2_rows, meta, gates, cnt = _mixer_out(
        x, gla, y, mod3, s5_glu_w[l].astype(bf16), s5_glu_b[l].reshape(1, 512), wo[:512], wo[512:],
        mix_post_g[l].reshape(1, D), ffn_pre_g[l].reshape(1, D), rw_hi, rw_lo, rb)

    meta = meta.reshape(N, LANES)
    idx = meta[:, :TOPK]
    rank = meta[:, TOPK:2 * TOPK]
    offs = ((idx * N + rank) * ROW).reshape(-1)
    counts = cnt[0, :NE].astype(i32)
    ntile = (counts + MB - 1) // MB
    tends = jnp.cumsum(ntile)
    nused = tends[-1]
    gi = jnp.minimum(jnp.arange(NB, dtype=i32), nused - 1)
    tile_e = jnp.sum((tends[None, :] <= gi[:, None]).astype(i32), axis=1)
    tile_j = gi - (tends - ntile)[tile_e]
    tile_blk = tile_e * TPE + tile_j
    tile_nv = jnp.clip(counts[tile_e] - tile_j * MB, 0, MB)

    xs = _dispatch(offs, hn2_rows)
    ys = _experts(tile_blk, tile_e, tile_nv, nused.reshape(1), xs,
                  exp_w_gu[l], exp_b_gu[l], exp_w_down[l], exp_b_down[l])
    out = _combine(offs, gates.reshape(N, LANES), h1_rows, mod4, ffn_post_g[l].reshape(ROW, LANES), ys)
    return out.reshape(B, T, D)
```

```python
import functools
import math

import jax
import jax.numpy as jnp
from jax import lax
from jax.experimental import pallas as pl
from jax.experimental.pallas import tpu as pltpu

f32 = jnp.float32
bf16 = jnp.bfloat16
i32 = jnp.int32

D = 1024
B = 8
T = 2048
N = B * T
GLA_H = 4
GLA_DV = 128
GLA_DK = 64
GLA_TAU = 16.0
GLA_CHUNK = 64
LOWRANK = 16
S5_W = 512
S5_CH = 16
S5_G = 32
S5_P = 64
NE = 32
TOPK = 4
DFF = 1024
ALPHA = 1.702
LIMIT = 7.0
EPS = 1e-6

LANES = 128
SUBLANES = 8
VMEM_LIMIT = 56 * 1024 * 1024

TT = 512
S5_L = 16
S5_M = S5_W // LANES
S5_GPT = LANES // S5_CH
S5_ST = S5_GPT * S5_P
S5_ROWS = 256
ROW = SUBLANES
MB = 256
EB = 2
NB = (N * TOPK) // (EB * MB) + NE
TPE = N // (EB * MB)
TS = 512
ET = 256
DMA_UNROLL = 8


def _dot(a, b):
    return jnp.dot(a, b, preferred_element_type=f32)


def _dot_t(a, b, ca, cb):
    return lax.dot_general(a, b, (((ca,), (cb,)), ((), ())), preferred_element_type=f32)


def _rms(x):
    return lax.rsqrt(jnp.mean(x * x, axis=-1, keepdims=True) + EPS)


def _ada_kernel(c_ref, w_ref, b_ref, o_ref):
    c = c_ref[...]
    s = (c * jax.nn.sigmoid(c)).astype(bf16)
    o_ref[...] = _dot(s, w_ref[...].astype(bf16)) + b_ref[...]


def _ada(c, w, b):
    return pl.pallas_call(
        _ada_kernel,
        grid=(6,),
        in_specs=[
            pl.BlockSpec((B, D), lambda j: (0, 0)),
            pl.BlockSpec((D, D), lambda j: (0, j)),
            pl.BlockSpec((1, D), lambda j: (0, j)),
        ],
        out_specs=pl.BlockSpec((B, D), lambda j: (0, j)),
        out_shape=jax.ShapeDtypeStruct((B, 6 * D), f32),
        name="ada",
    )(c, w, b.reshape(1, 6 * D))


def _mixer_in_kernel(x_ref, sc_ref, sh_ref, g_ref, wq, wk, wv, wg, wa, wu, walpha, balpha, gng,
                     gla_ref, u0, u1, u2, u3, st_s):
    t = pl.program_id(1)

    @pl.when(t == 0)
    def _():
        st_s[...] = jnp.zeros_like(st_s)

    x = x_ref[...]
    hn = (x * _rms(x) * g_ref[...]) * (1.0 + sc_ref[...]) + sh_ref[...]
    hb = hn.astype(bf16)
    u = _dot(hb, wu[...]).astype(bf16)
    for m, u_ref in enumerate((u0, u1, u2, u3)):
        u_ref[...] = u[:, m * LANES:(m + 1) * LANES].reshape(TT // S5_L, S5_L, LANES)

    C = GLA_CHUNK
    NC = TT // C
    row = lax.broadcasted_iota(i32, (TT, TT), 0)
    col = lax.broadcasted_iota(i32, (TT, TT), 1)
    causal = jnp.logical_and(row >= col, (row // C) == (col // C))
    tri = causal.astype(bf16)
    a_lr = _dot(hb, wa[...]).astype(bf16)
    la = jax.nn.log_sigmoid(_dot(a_lr, walpha[...]) + balpha[...]) * (1.0 / GLA_TAU)
    la_hi = la.astype(bf16)
    la_lo = (la - la_hi.astype(f32)).astype(bf16)
    bc = _dot(tri, la_hi) + _dot(tri, la_lo)
    bl = jnp.broadcast_to(bc.reshape(NC, C, GLA_H * LANES)[:, C - 1:C, :],
                          (NC, C, GLA_H * LANES)).reshape(TT, GLA_H * LANES)
    q = _dot(hb, wq[...]) * (GLA_DK ** -0.5)
    k = _dot(hb, wk[...])
    qd = (q * jnp.exp(bc)).astype(bf16)
    ki = (k * jnp.exp(-bc)).astype(bf16)
    ke = (k * jnp.exp(bl - bc)).astype(bf16)
    dec = jnp.exp(bl)
    vb = _dot(hb, wv[...]).astype(bf16)
    og = _dot(hb, wg[...])
    gn = gng[...]
    for h in range(GLA_H):
        sl = slice(h * LANES, (h + 1) * LANES)
        qh, kih, keh, vh = qd[:, sl], ki[:, sl], ke[:, sl], vb[:, sl]
        sc = jnp.where(causal, _dot_t(qh, kih, 1, 1), 0.0).astype(bf16)
        o_intra = _dot(sc, vh)
        st = st_s[h]
        o_parts = []
        for c in range(NC):
            rs = slice(c * C, (c + 1) * C)
            o_parts.append(_dot_t(qh[rs], st.astype(bf16), 1, 1))
            st = st * dec[c * C:c * C + 1, sl] + _dot_t(vh[rs], keh[rs], 0, 0)
        st_s[h] = st
        o = o_intra + jnp.concatenate(o_parts, axis=0)
        on = o * _rms(o) * gn
        ogh = og[:, sl]
        gla_ref[:, sl] = (on * (ogh * jax.nn.sigmoid(ogh))).astype(bf16)


def _mixer_in(x, mod3, g, wq, wk, wv, wg, wa, wu, walpha, balpha, gng):
    nt = T // TT
    full = lambda shape: pl.BlockSpec(shape, lambda b, t: (0,) * len(shape))
    u_shape = jax.ShapeDtypeStruct((T // S5_L, B, S5_L, LANES), bf16)
    u_spec = pl.BlockSpec((TT // S5_L, None, S5_L, LANES), lambda b, t: (t, b, 0, 0))
    return pl.pallas_call(
        _mixer_in_kernel,
        grid=(B, nt),
        in_specs=[
            pl.BlockSpec((None, TT, D), lambda b, t: (b, t, 0)),
            pl.BlockSpec((None, 1, D), lambda b, t: (b, 0, 1)),
            pl.BlockSpec((None, 1, D), lambda b, t: (b, 0, 0)),
            full((1, D)),
            full((D, 512)), full((D, 512)), full((D, 512)), full((D, 512)),
            full((D, LANES)), full((D, 512)),
            full((LANES, 512)), full((1, 512)), full((1, LANES)),
        ],
        out_specs=[pl.BlockSpec((None, TT, 512), lambda b, t: (b, t, 0)), u_spec, u_spec, u_spec, u_spec],
        out_shape=[jax.ShapeDtypeStruct((B, T, 512), bf16), u_shape, u_shape, u_shape, u_shape],
        scratch_shapes=[pltpu.VMEM((GLA_H, GLA_DV, LANES), f32)],
        compiler_params=pltpu.CompilerParams(
            dimension_semantics=("arbitrary", "arbitrary"), vmem_limit_bytes=VMEM_LIMIT),
        name="mixer_in",
    )(x, mod3, mod3, g, wq, wk, wv, wg, wa, wu, walpha, balpha, gng)


def _cpow(xr, xi, d):
    mag = jnp.exp(xr * d)
    return mag * jnp.cos(xi * d), mag * jnp.sin(xi * d)


def _s5_prep_kernel(lr_r, li_r, ld_r, lr_c, li_c, ld_c, btr, bti, ctr, cti, d_r,
                    wt_ref, mi_ref, mo_ref, a_ref, kt_s):
    L = S5_L
    lr = lr_r[...]
    li = li_r[...]
    dt = jnp.exp(ld_r[...])
    xr, xi = lr * dt, li * dt
    ar, ai = _cpow(xr, xi, 1.0)
    den = lr * lr + li * li
    fr = ((ar - 1.0) * lr + ai * li) / den
    fi = (ai * lr - (ar - 1.0) * li) / den
    br, bi = btr[...], bti[...]
    bbr = fr * br - fi * bi
    bbi = fr * bi + fi * br
    cr, ci = ctr[...], cti[...]
    pr, pi = jnp.ones_like(ar), jnp.zeros_like(ai)
    for d in range(L):
        xdr = bbr * pr - bbi * pi
        xdi = bbr * pi + bbi * pr
        i = L - 1 - d
        mo_ref[i * LANES:(i + 1) * LANES, :S5_ST] = xdr.astype(bf16)
        mo_ref[i * LANES:(i + 1) * LANES, S5_ST:] = xdi.astype(bf16)
        kt = (jnp.dot(xdr, cr, preferred_element_type=f32, precision=lax.Precision.HIGHEST)
              - jnp.dot(xdi, ci, preferred_element_type=f32, precision=lax.Precision.HIGHEST))
        if d == 0:
            r = lax.broadcasted_iota(i32, (LANES, LANES), 0)
            c = lax.broadcasted_iota(i32, (LANES, LANES), 1)
            kt = kt + jnp.where(r == c, d_r[...], 0.0)
        kt_s[d] = kt.astype(bf16)
        pr, pi = pr * ar - pi * ai, pr * ai + pi * ar
    a_ref[:, :S5_ST] = pr
    a_ref[:, S5_ST:] = pi
    zero = jnp.zeros((LANES, LANES), bf16)
    for i in range(L):
        for j in range(L):
            wt_ref[i * LANES:(i + 1) * LANES, j * LANES:(j + 1) * LANES] = kt_s[j - i] if j >= i else zero
    lrc = lr_c[...]
    lic = li_c[...]
    dtc = jnp.exp(ld_c[...])
    acr, aci = _cpow(lrc * dtc, lic * dtc, 1.0)
    pr, pi = acr, aci
    for j in range(L):
        mi_ref[:S5_ST, j * LANES:(j + 1) * LANES] = (cr * pr - ci * pi).astype(bf16)
        mi_ref[S5_ST:, j * LANES:(j + 1) * LANES] = (-(cr * pi + ci * pr)).astype(bf16)
        pr, pi = pr * acr - pi * aci, pr * aci + pi * acr


def _s5_prep(lr, li, ld, b_re, b_im, c_re, c_im, dvec):
    G, P, H, M, GPT = S5_G, S5_P, S5_CH, S5_M, S5_GPT
    eye = jnp.eye(GPT, dtype=f32)

    def rows(v):
        return v.reshape(M, 1, S5_ST)

    def cols(v):
        return v.reshape(M, S5_ST, 1)

    ldp = jnp.broadcast_to(ld[:, None], (G, P))

    def bt(b):
        b4 = b.reshape(M, GPT, P, H)
        return jnp.einsum('mgph,gk->mkhgp', b4, eye).reshape(M, LANES, S5_ST)

    def ct(c):
        c4 = c.reshape(M, GPT, H, P)
        return jnp.einsum('mghp,gk->mgpkh', c4, eye).reshape(M, S5_ST, LANES)

    L = S5_L
    mspec = lambda shape: pl.BlockSpec((None,) + shape, lambda m: (m,) + (0,) * len(shape))
    return pl.pallas_call(
        _s5_prep_kernel,
        grid=(M,),
        in_specs=[mspec((1, S5_ST))] * 3 + [mspec((S5_ST, 1))] * 3
        + [mspec((LANES, S5_ST))] * 2 + [mspec((S5_ST, LANES))] * 2 + [mspec((1, LANES))],
        out_specs=[mspec((L * LANES, L * LANES)), mspec((2 * S5_ST, L * LANES)),
                   mspec((L * LANES, 2 * S5_ST)), mspec((1, 2 * S5_ST))],
        out_shape=[jax.ShapeDtypeStruct((M, L * LANES, L * LANES), bf16),
                   jax.ShapeDtypeStruct((M, 2 * S5_ST, L * LANES), bf16),
                   jax.ShapeDtypeStruct((M, L * LANES, 2 * S5_ST), bf16),
                   jax.ShapeDtypeStruct((M, 1, 2 * S5_ST), f32)],
        scratch_shapes=[pltpu.VMEM((L, LANES, LANES), bf16)],
        compiler_params=pltpu.CompilerParams(dimension_semantics=("arbitrary",), vmem_limit_bytes=VMEM_LIMIT),
        name="s5_prep",
    )(rows(lr), rows(li), rows(ldp), cols(lr), cols(li), cols(ldp),
      bt(b_re), bt(b_im), ct(c_re), ct(c_im), dvec.reshape(M, 1, LANES))


def _s5_scan_kernel(u_ref, wt, mi, mo, a_ref, y_ref, xs, s_s, x_s):
    r = pl.program_id(1)

    @pl.when(r == 0)
    def _():
        xs[...] = jnp.zeros_like(xs)

    u = u_ref[...]
    s_s[...] = _dot(u, mo[...])
    ar = a_ref[:, :S5_ST]
    ai = a_ref[:, S5_ST:]

    def step(n, carry):
        xr, xi = carry
        r0 = pl.multiple_of(n * B, B)
        x_s[pl.ds(r0, B), :S5_ST] = xr
        x_s[pl.ds(r0, B), S5_ST:] = xi
        sr = s_s[pl.ds(r0, B), :S5_ST]
        si = s_s[pl.ds(r0, B), S5_ST:]
        return ar * xr - ai * xi + sr, ar * xi + ai * xr + si

    xr, xi = lax.fori_loop(0, S5_ROWS // B, step, (xs[:, :S5_ST], xs[:, S5_ST:]))
    xs[:, :S5_ST] = xr
    xs[:, S5_ST:] = xi
    y = _dot(u, wt[...]) + _dot(x_s[...].astype(bf16), mi[...])
    y_ref[...] = jax.nn.gelu(y).astype(bf16)


def _s5_scan(u, wt, mi, mo, a16):
    L = S5_L
    rows = (T // L) * B
    wspec = lambda shape: pl.BlockSpec((None,) + shape, lambda m, r: (m,) + (0,) * len(shape))
    return pl.pallas_call(
        _s5_scan_kernel,
        grid=(S5_M, rows // S5_ROWS),
        in_specs=[pl.BlockSpec((None, S5_ROWS, L * LANES), lambda m, r: (m, r, 0)),
                  wspec((L * LANES, L * LANES)), wspec((2 * S5_ST, L * LANES)),
                  wspec((L * LANES, 2 * S5_ST)), wspec((1, 2 * S5_ST))],
        out_specs=pl.BlockSpec((None, S5_ROWS, L * LANES), lambda m, r: (m, r, 0)),
        out_shape=jax.ShapeDtypeStruct((S5_M, rows, L * LANES), bf16),
        scratch_shapes=[pltpu.VMEM((B, 2 * S5_ST), f32), pltpu.VMEM((S5_ROWS, 2 * S5_ST), f32),
                        pltpu.VMEM((S5_ROWS, 2 * S5_ST), f32)],
        compiler_params=pltpu.CompilerParams(
            dimension_semantics=("arbitrary", "arbitrary"), vmem_limit_bytes=VMEM_LIMIT),
        name="s5_scan",
    )(u, wt, mi, mo, a16)


def _mixer_out_kernel(x_ref, gla_ref, y0, y1, y2, y3, g1_ref, sh2_ref, sc2_ref, gluw, glub, woa, wob,
                      mpg, fpg, rw_hi, rw_lo, rb, h1_ref, hn2_ref, meta_ref, gate_ref, cnt_ref, cnt_s):
    first = jnp.logical_and(pl.program_id(0) == 0, pl.program_id(1) == 0)

    @pl.when(first)
    def _():
        cnt_s[...] = jnp.zeros_like(cnt_s)

    y = jnp.concatenate([r[...].reshape(TT, LANES) for r in (y0, y1, y2, y3)], axis=1)
    z = _dot(y, gluw[...]) + glub[...]
    s5 = y.astype(f32) * jax.nn.sigmoid(z)
    mix = _dot(gla_ref[...], woa[...]) + _dot(s5.astype(bf16), wob[...])
    h1 = x_ref[...] + g1_ref[...] * (mix * _rms(mix) * mpg[...])
    hn2 = (h1 * _rms(h1) * fpg[...]) * (1.0 + sc2_ref[...]) + sh2_ref[...]
    for cc in range(ROW):
        h1_ref[pl.ds(cc, TT, stride=ROW), :] = h1[:, cc * LANES:(cc + 1) * LANES]
        hn2_ref[pl.ds(cc, TT, stride=ROW), :] = hn2[:, cc * LANES:(cc + 1) * LANES]
    x_hi = hn2.astype(bf16)
    x_lo = (hn2 - x_hi.astype(f32)).astype(bf16)
    logits = _dot(x_hi, rw_hi[...]) + (_dot(x_lo, rw_hi[...]) + _dot(x_hi, rw_lo[...])) + rb[...]

    lane = lax.broadcasted_iota(i32, (TT, LANES), 1).astype(f32)
    l = logits
    vals, idxs = [], []
    for _ in range(TOPK):
        m = jnp.max(l, axis=-1, keepdims=True)
        ix = jnp.min(jnp.where(l == m, lane, float(LANES)), axis=-1, keepdims=True)
        vals.append(m)
        idxs.append(ix)
        l = jnp.where(lane == ix, -jnp.inf, l)
    es = [jnp.exp(v - vals[0]) for v in vals]
    tot = es[0] + es[1] + es[2] + es[3]
    oh = jnp.zeros((TT, LANES), f32)
    for ix in idxs:
        oh = oh + (lane == ix).astype(f32)
    r = lax.broadcasted_iota(i32, (TT, TT), 0)
    c = lax.broadcasted_iota(i32, (TT, TT), 1)
    cum = _dot((r >= c).astype(bf16), oh.astype(bf16))
    excl = cum - oh + cnt_s[...]
    meta = jnp.zeros((TT, LANES), f32)
    gate = jnp.zeros((TT, LANES), f32)
    for kk in range(TOPK):
        rank = jnp.sum(jnp.where(lane == idxs[kk], excl, 0.0), axis=-1, keepdims=True)
        meta = jnp.where(lane == float(kk), idxs[kk], meta)
        meta = jnp.where(lane == float(TOPK + kk), rank, meta)
        gate = jnp.where(lane == float(kk), es[kk] / tot, gate)
    meta_ref[...] = meta.astype(i32)
    gate_ref[...] = gate
    cnt = cnt_s[...] + cum[TT - 1:TT, :]
    cnt_s[...] = cnt
    cnt_ref[...] = cnt


def _mixer_out(x, gla, y, mod3, gluw, glub, woa, wob, mpg, fpg, rw_hi, rw_lo, rb):
    nt = T // TT
    full = lambda shape: pl.BlockSpec(shape, lambda b, t: (0,) * len(shape))
    tok = lambda w: pl.BlockSpec((None, TT, w), lambda b, t: (b, t, 0))
    slab = pl.BlockSpec((TT * ROW, LANES), lambda b, t: (b * nt + t, 0))
    slab_shape = jax.ShapeDtypeStruct((N * ROW, LANES), f32)
    y5 = y.reshape(S5_M, T // S5_L, B, S5_L, LANES)
    yspec = lambda m: pl.BlockSpec((None, TT // S5_L, None, S5_L, LANES), lambda b, t: (m, t, b, 0, 0))
    modspec = lambda j: pl.BlockSpec((None, 1, D), lambda b, t: (b, 0, j))
    return pl.pallas_call(
        _mixer_out_kernel,
        grid=(B, nt),
        in_specs=[tok(D), tok(512), yspec(0), yspec(1), yspec(2), yspec(3),
                  modspec(2), modspec(3), modspec(4),
                  full((512, 512)), full((1, 512)), full((512, D)), full((512, D)),
                  full((1, D)), full((1, D)), full((D, LANES)), full((D, LANES)), full((1, LANES))],
        out_specs=[slab, slab, tok(LANES), tok(LANES), full((1, LANES))],
        out_shape=[slab_shape, slab_shape,
                   jax.ShapeDtypeStruct((B, T, LANES), i32), jax.ShapeDtypeStruct((B, T, LANES), f32),
                   jax.ShapeDtypeStruct((1, LANES), f32)],
        scratch_shapes=[pltpu.VMEM((1, LANES), f32)],
        compiler_params=pltpu.CompilerParams(
            dimension_semantics=("arbitrary", "arbitrary"), vmem_limit_bytes=VMEM_LIMIT),
        name="mixer_out",
    )(x, gla, y5, y5, y5, y5, mod3, mod3, mod3, gluw, glub, woa, wob, mpg, fpg, rw_hi, rw_lo, rb)


def _slab(ref, off):
    return ref.at[pl.ds(pl.multiple_of(off, ROW), ROW), :]


def _dispatch_kernel(off_ref, x_ref, xs_hbm, sem):
    def group(g, carry):
        for j in range(DMA_UNROLL):
            t = g * DMA_UNROLL + j
            for kk in range(TOPK):
                pltpu.make_async_copy(_slab(x_ref, t * ROW), _slab(xs_hbm, off_ref[0, t * TOPK + kk]),
                                      sem).start(priority=kk % 2)
        return carry

    lax.fori_loop(0, TS // DMA_UNROLL, group, 0)
    for kk in range(TOPK):
        pltpu.make_async_copy(x_ref, xs_hbm.at[pl.ds(0, TS * ROW), :], sem).wait()


def _dispatch(offs, hn2_rows):
    return pl.pallas_call(
        _dispatch_kernel,
        grid=(N // TS,),
        in_specs=[pl.BlockSpec((None, 1, TS * TOPK), lambda i: (i, 0, 0), memory_space=pltpu.SMEM),
                  pl.BlockSpec((TS * ROW, LANES), lambda i: (i, 0))],
        out_specs=pl.BlockSpec(memory_space=pl.ANY),
        out_shape=jax.ShapeDtypeStruct((NE * N * ROW, LANES), f32),
        scratch_shapes=[pltpu.SemaphoreType.DMA(())],
        compiler_params=pltpu.CompilerParams(dimension_semantics=("arbitrary",), vmem_limit_bytes=VMEM_LIMIT),
        name="dispatch",
    )(offs.reshape(N // TS, 1, TS * TOPK), hn2_rows)


def _experts_kernel(tb_ref, te_ref, tv_ref, nu_ref, x_ref, wgu_ref, bgu_ref, wd_ref, bd_ref, y_ref, wgu_b, wd_b):
    i = pl.program_id(0)

    @pl.when(i < nu_ref[0])
    def _():
        changed = jnp.logical_or(i == 0, te_ref[i] != te_ref[jnp.maximum(i - 1, 0)])

        @pl.when(changed)
        def _():
            def cast(j, carry):
                r0 = pl.multiple_of(j * LANES, LANES)
                wgu_b[pl.ds(r0, LANES), :] = wgu_ref[pl.ds(r0, LANES), :].astype(bf16)
                wd_b[pl.ds(r0, LANES), :] = wd_ref[pl.ds(r0, LANES), :].astype(bf16)
                return carry
            lax.fori_loop(0, D // LANES, cast, 0)

        def tile(h):
            base = h * MB * ROW
            xt = jnp.concatenate([x_ref[pl.ds(base + cc, MB, stride=ROW), :] for cc in range(ROW)], axis=1)
            valid = lax.broadcasted_iota(i32, (MB, 1), 0) < tv_ref[i] - h * MB
            xt = jnp.where(valid, xt, 0.0).astype(bf16)
            gu = _dot(xt, wgu_b[...]) + bgu_ref[...]
            g = jnp.minimum(gu[:, :DFF], LIMIT)
            up = jnp.clip(gu[:, DFF:], -LIMIT, LIMIT)
            act = (up + 1.0) * (g * jax.nn.sigmoid(ALPHA * g))
            yv = _dot(act.astype(bf16), wd_b[...]) + bd_ref[...]
            for cc in range(ROW):
                y_ref[pl.ds(base + cc, MB, stride=ROW), :] = yv[:, cc * LANES:(cc + 1) * LANES]

        tile(0)
        for h in range(1, EB):
            pl.when(tv_ref[i] > h * MB)(functools.partial(tile, h))


def _experts(tile_blk, tile_e, tile_nv, nused, xs, w_gu, b_gu, w_down, b_down):
    blk = lambda i, tb, te, tv, nu: (tb[i], 0)
    wsel = lambda i, tb, te, tv, nu: (te[i], 0, 0)
    grid_spec = pltpu.PrefetchScalarGridSpec(
        num_scalar_prefetch=4,
        grid=(NB,),
        in_specs=[
            pl.BlockSpec((EB * MB * ROW, LANES), blk),
            pl.BlockSpec((None, D, 2 * DFF), wsel),
            pl.BlockSpec((None, 1, 2 * DFF), wsel),
            pl.BlockSpec((None, DFF, D), wsel),
            pl.BlockSpec((None, 1, D), wsel),
        ],
        out_specs=pl.BlockSpec((EB * MB * ROW, LANES), blk),
        scratch_shapes=[pltpu.VMEM((D, 2 * DFF), bf16), pltpu.VMEM((DFF, D), bf16)],
    )
    return pl.pallas_call(
        _experts_kernel,
        grid_spec=grid_spec,
        out_shape=jax.ShapeDtypeStruct((NE * N * ROW, LANES), f32),
        compiler_params=pltpu.CompilerParams(dimension_semantics=("arbitrary",), vmem_limit_bytes=VMEM_LIMIT),
        name="experts",
    )(tile_blk, tile_e, tile_nv, nused, xs, w_gu, b_gu.reshape(NE, 1, 2 * DFF), w_down, b_down.reshape(NE, 1, D))


def _combine_kernel(off_ref, offn_ref, gate_ref, h1_ref, g2_ref, pg_ref, ys_hbm, o_ref, gbuf, wbuf, rbuf, sem):
    j = pl.program_id(0)
    slot = lax.rem(j, 2)

    def issue(idx_ref, s):
        def group(g, carry):
            for jj in range(DMA_UNROLL):
                t = g * DMA_UNROLL + jj
                for kk in range(TOPK):
                    pltpu.make_async_copy(_slab(ys_hbm, idx_ref[0, t * TOPK + kk]),
                                          gbuf.at[s, kk, pl.ds(pl.multiple_of(t * ROW, ROW), ROW), :],
                                          sem.at[s]).start(priority=kk % 2)
            return carry
        lax.fori_loop(0, ET // DMA_UNROLL, group, 0)

    @pl.when(j == 0)
    def _():
        issue(off_ref, 0)

    @pl.when(j + 1 < pl.num_programs(0))
    def _():
        issue(offn_ref, 1 - slot)

    for kk in range(TOPK):
        gk = jnp.broadcast_to(gate_ref[:, kk:kk + 1], (ET, LANES))
        for s in range(ROW):
            wbuf[kk, pl.ds(s, ET, stride=ROW), :] = gk
    for kk in range(TOPK):
        pltpu.make_async_copy(ys_hbm.at[pl.ds(0, ET * ROW), :], gbuf.at[slot, kk], sem.at[slot]).wait()
    ff = gbuf[slot, 0] * wbuf[0]
    for kk in range(1, TOPK):
        ff = ff + gbuf[slot, kk] * wbuf[kk]
    ff3 = ff.reshape(ET, ROW, LANES)
    ss = jnp.sum(jnp.sum(ff3 * ff3, axis=2, keepdims=True), axis=1, keepdims=True)
    rs = lax.rsqrt(ss * (1.0 / D) + EPS)
    out = h1_ref[...].reshape(ET, ROW, LANES) + g2_ref[...] * (ff3 * rs * pg_ref[...])
    rbuf[...] = out.reshape(ET * ROW, LANES)
    for cc in range(ROW):
        o_ref[:, cc * LANES:(cc + 1) * LANES] = rbuf[pl.ds(cc, ET, stride=ROW), :]


def _combine(offs, gates, h1_rows, mod4, pg, ys):
    nt = N // ET
    per_b = T // ET
    offs3 = offs.reshape(nt, 1, ET * TOPK)
    return pl.pallas_call(
        _combine_kernel,
        grid=(nt,),
        in_specs=[pl.BlockSpec((None, 1, ET * TOPK), lambda j: (j, 0, 0), memory_space=pltpu.SMEM),
                  pl.BlockSpec((None, 1, ET * TOPK), lambda j: (jnp.minimum(j + 1, nt - 1), 0, 0),
                               memory_space=pltpu.SMEM),
                  pl.BlockSpec((ET, LANES), lambda j: (j, 0)),
                  pl.BlockSpec((ET * ROW, LANES), lambda j: (j, 0)),
                  pl.BlockSpec((None, None, ROW, LANES), lambda j: (j // per_b, 5, 0, 0)),
                  pl.BlockSpec((ROW, LANES), lambda j: (0, 0)),
                  pl.BlockSpec(memory_space=pl.ANY)],
        out_specs=pl.BlockSpec((ET, D), lambda j: (j, 0)),
        out_shape=jax.ShapeDtypeStruct((N, D), f32),
        scratch_shapes=[pltpu.VMEM((2, TOPK, ET * ROW, LANES), f32), pltpu.VMEM((TOPK, ET * ROW, LANES), f32),
                        pltpu.VMEM((ET * ROW, LANES), f32), pltpu.SemaphoreType.DMA((2,))],
        compiler_params=pltpu.CompilerParams(dimension_semantics=("arbitrary",), vmem_limit_bytes=VMEM_LIMIT),
        name="combine",
    )(offs3, offs3, gates, h1_rows, mod4, pg, ys)


def _pad_heads(w):
    lead = w.shape[:-1]
    w4 = w.reshape(lead + (GLA_H, GLA_DK))
    w4 = jnp.concatenate([w4, jnp.zeros_like(w4)], axis=-1)
    return w4.reshape(lead + (GLA_H * LANES,))


def kernel(x, c, ada_w, ada_b, mix_pre_g, mix_post_g, ffn_pre_g, ffn_post_g, w_in, w_alpha, b_alpha, gla_norm_g, s5_lambda_re, s5_lambda_im, s5_log_dt, s5_b_re, s5_b_im, s5_c_re, s5_c_im, s5_d, s5_glu_w, s5_glu_b, w_out, router_w, router_b, exp_w_gu, exp_b_gu, exp_w_down, exp_b_down):
    l = 0
    mod = _ada(c, ada_w[l], ada_b[l])
    mod3 = mod.reshape(B, 1, 6 * D)
    mod4 = mod.reshape(B, 6, ROW, LANES)

    w = w_in[l]
    o_q, o_k, o_v, o_g, o_a, o_u = 0, 256, 512, 1024, 1536, 1552
    wq = _pad_heads(w[:, o_q:o_k]).astype(bf16)
    wk = _pad_heads(w[:, o_k:o_v]).astype(bf16)
    wv = w[:, o_v:o_g].astype(bf16)
    wg = w[:, o_g:o_a].astype(bf16)
    wa = jnp.pad(w[:, o_a:o_u], ((0, 0), (0, LANES - LOWRANK))).astype(bf16)
    wu = w[:, o_u:].astype(bf16)
    walpha = jnp.pad(_pad_heads(w_alpha[l]), ((0, LANES - LOWRANK), (0, 0))).astype(bf16)
    balpha = _pad_heads(b_alpha[l]).reshape(1, GLA_H * LANES)

    gla, u0, u1, u2, u3 = _mixer_in(x, mod3, mix_pre_g[l].reshape(1, D), wq, wk, wv, wg, wa, wu,
                                    walpha, balpha, gla_norm_g[l].reshape(1, GLA_DV))
    rows = (T // S5_L) * B
    u = jnp.stack([a.reshape(rows, S5_L * LANES) for a in (u0, u1, u2, u3)], axis=0)

    wt, mi, mo, a16 = _s5_prep(s5_lambda_re[l], s5_lambda_im[l], s5_log_dt[l], s5_b_re[l], s5_b_im[l],
                               s5_c_re[l], s5_c_im[l], s5_d[l])
    y = _s5_scan(u, wt, mi, mo, a16)

    rw = jnp.pad(router_w[l], ((0, 0), (0, LANES - NE)))
    rw_hi = rw.astype(bf16)
    rw_lo = (rw - rw_hi.astype(f32)).astype(bf16)
    rb = jnp.pad(router_b[l], (0, LANES - NE), constant_values=-1e30).reshape(1, LANES)
    wo = w_out[l].astype(bf16)
    h1_rows, hn2_rows, meta, gates, cnt = _mixer_out(
        x, gla, y, mod3, s5_glu_w[l].astype(bf16), s5_glu_b[l].reshape(1, 512), wo[:512], wo[512:],
        mix_post_g[l].reshape(1, D), ffn_pre_g[l].reshape(1, D), rw_hi, rw_lo, rb)

    meta = meta.reshape(N, LANES)
    idx = meta[:, :TOPK]
    rank = meta[:, TOPK:2 * TOPK]
    offs = ((idx * N + rank) * ROW).reshape(-1)
    counts = cnt[0, :NE].astype(i32)
    step_rows = EB * MB
    ntile = (counts + step_rows - 1) // step_rows
    tends = jnp.cumsum(ntile)
    nused = tends[-1]
    gi = jnp.minimum(jnp.arange(NB, dtype=i32), nused - 1)
    tile_e = jnp.sum((tends[None, :] <= gi[:, None]).astype(i32), axis=1)
    tile_j = gi - (tends - ntile)[tile_e]
    tile_blk = tile_e * TPE + tile_j
    tile_nv = jnp.clip(counts[tile_e] - tile_j * step_rows, 0, step_rows)

    xs = _dispatch(offs, hn2_rows)
    ys = _experts(tile_blk, tile_e, tile_nv, nused.reshape(1), xs,
                  exp_w_gu[l], exp_b_gu[l], exp_w_down[l], exp_b_down[l])
    out = _combine(offs, gates.reshape(N, LANES), h1_rows, mod4, ffn_post_g[l].reshape(ROW, LANES), ys)
    return out.reshape(B, T, D)
```

```python
import functools
import math

import jax
import jax.numpy as jnp
from jax import lax
from jax.experimental import pallas as pl
from jax.experimental.pallas import tpu as pltpu

f32 = jnp.float32
bf16 = jnp.bfloat16
i32 = jnp.int32

D = 1024
B = 8
T = 2048
N = B * T
GLA_H = 4
GLA_DV = 128
GLA_DK = 64
GLA_TAU = 16.0
GLA_CHUNK = 64
LOWRANK = 16
S5_W = 512
S5_CH = 16
S5_G = 32
S5_P = 64
NE = 32
TOPK = 4
DFF = 1024
ALPHA = 1.702
LIMIT = 7.0
EPS = 1e-6

LANES = 128
SUBLANES = 8
VMEM_LIMIT = 56 * 1024 * 1024

TT = 512
S5_L = 16
S5_M = S5_W // LANES
S5_GPT = LANES // S5_CH
S5_ST = S5_GPT * S5_P
S5_NCH = 32
S5_ROWS = B * S5_NCH
ROW = SUBLANES
MB = 256
EB = 2
NB = (N * TOPK) // (EB * MB) + NE
TPE = N // (EB * MB)
TS = 512
ET = 256
DMA_UNROLL = 8


def _dot(a, b):
    return jnp.dot(a, b, preferred_element_type=f32)


def _dot_t(a, b, ca, cb):
    return lax.dot_general(a, b, (((ca,), (cb,)), ((), ())), preferred_element_type=f32)


def _rms(x):
    return lax.rsqrt(jnp.mean(x * x, axis=-1, keepdims=True) + EPS)


def _ada_kernel(c_ref, w_ref, b_ref, o_ref):
    c = c_ref[...]
    s = (c * jax.nn.sigmoid(c)).astype(bf16)
    o_ref[...] = _dot(s, w_ref[...].astype(bf16)) + b_ref[...]


def _ada(c, w, b):
    return pl.pallas_call(
        _ada_kernel,
        grid=(6,),
        in_specs=[
            pl.BlockSpec((B, D), lambda j: (0, 0)),
            pl.BlockSpec((D, D), lambda j: (0, j)),
            pl.BlockSpec((1, D), lambda j: (0, j)),
        ],
        out_specs=pl.BlockSpec((B, D), lambda j: (0, j)),
        out_shape=jax.ShapeDtypeStruct((B, 6 * D), f32),
        name="ada",
    )(c, w, b.reshape(1, 6 * D))


def _mixer_in_kernel(x_ref, sc_ref, sh_ref, g_ref, wq, wk, wv, wg, wa, wu, walpha, balpha, gng,
                     gla_ref, u_ref, st_s, u_s):
    t = pl.program_id(1)

    @pl.when(t == 0)
    def _():
        st_s[...] = jnp.zeros_like(st_s)

    x = x_ref[...]
    hn = (x * _rms(x) * g_ref[...]) * (1.0 + sc_ref[...]) + sh_ref[...]
    hb = hn.astype(bf16)
    u = _dot(hb, wu[...])
    for m in range(S5_M):
        u_s[m] = u[:, m * LANES:(m + 1) * LANES]
        for i in range(S5_L):
            u_ref[m, :, i * LANES:(i + 1) * LANES] = u_s[m, pl.ds(i, TT // S5_L, stride=S5_L), :].astype(bf16)

    C = GLA_CHUNK
    NC = TT // C
    row = lax.broadcasted_iota(i32, (TT, TT), 0)
    col = lax.broadcasted_iota(i32, (TT, TT), 1)
    causal = jnp.logical_and(row >= col, (row // C) == (col // C))
    tri = causal.astype(bf16)
    a_lr = _dot(hb, wa[...]).astype(bf16)
    la = jax.nn.log_sigmoid(_dot(a_lr, walpha[...]) + balpha[...]) * (1.0 / GLA_TAU)
    la_hi = la.astype(bf16)
    la_lo = (la - la_hi.astype(f32)).astype(bf16)
    bc = _dot(tri, la_hi) + _dot(tri, la_lo)
    bl = jnp.broadcast_to(bc.reshape(NC, C, GLA_H * LANES)[:, C - 1:C, :],
                          (NC, C, GLA_H * LANES)).reshape(TT, GLA_H * LANES)
    q = _dot(hb, wq[...]) * (GLA_DK ** -0.5)
    k = _dot(hb, wk[...])
    qd = (q * jnp.exp(bc)).astype(bf16)
    ki = (k * jnp.exp(-bc)).astype(bf16)
    ke = (k * jnp.exp(bl - bc)).astype(bf16)
    dec = jnp.exp(bl)
    vb = _dot(hb, wv[...]).astype(bf16)
    og = _dot(hb, wg[...])
    gn = gng[...]
    for h in range(GLA_H):
        sl = slice(h * LANES, (h + 1) * LANES)
        qh, kih, keh, vh = qd[:, sl], ki[:, sl], ke[:, sl], vb[:, sl]
        sc = jnp.where(causal, _dot_t(qh, kih, 1, 1), 0.0).astype(bf16)
        o_intra = _dot(sc, vh)
        st = st_s[h]
        o_parts = []
        for c in range(NC):
            rs = slice(c * C, (c + 1) * C)
            o_parts.append(_dot_t(qh[rs], st.astype(bf16), 1, 1))
            st = st * dec[c * C:c * C + 1, sl] + _dot_t(vh[rs], keh[rs], 0, 0)
        st_s[h] = st
        o = o_intra + jnp.concatenate(o_parts, axis=0)
        on = o * _rms(o) * gn
        ogh = og[:, sl]
        gla_ref[:, sl] = (on * (ogh * jax.nn.sigmoid(ogh))).astype(bf16)


def _mixer_in(x, mod3, g, wq, wk, wv, wg, wa, wu, walpha, balpha, gng):
    nt = T // TT
    full = lambda shape: pl.BlockSpec(shape, lambda b, t: (0,) * len(shape))
    u_shape = jax.ShapeDtypeStruct((S5_M, B, T // S5_L, S5_L * LANES), bf16)
    u_spec = pl.BlockSpec((S5_M, None, TT // S5_L, S5_L * LANES), lambda b, t: (0, b, t, 0))
    return pl.pallas_call(
        _mixer_in_kernel,
        grid=(B, nt),
        in_specs=[
            pl.BlockSpec((None, TT, D), lambda b, t: (b, t, 0)),
            pl.BlockSpec((None, 1, D), lambda b, t: (b, 0, 1)),
            pl.BlockSpec((None, 1, D), lambda b, t: (b, 0, 0)),
            full((1, D)),
            full((D, 512)), full((D, 512)), full((D, 512)), full((D, 512)),
            full((D, LANES)), full((D, 512)),
            full((LANES, 512)), full((1, 512)), full((1, LANES)),
        ],
        out_specs=[pl.BlockSpec((None, TT, 512), lambda b, t: (b, t, 0)), u_spec],
        out_shape=[jax.ShapeDtypeStruct((B, T, 512), bf16), u_shape],
        scratch_shapes=[pltpu.VMEM((GLA_H, GLA_DV, LANES), f32), pltpu.VMEM((S5_M, TT, LANES), f32)],
        compiler_params=pltpu.CompilerParams(
            dimension_semantics=("arbitrary", "arbitrary"), vmem_limit_bytes=VMEM_LIMIT),
        name="mixer_in",
    )(x, mod3, mod3, g, wq, wk, wv, wg, wa, wu, walpha, balpha, gng)


def _cpow(xr, xi, d):
    mag = jnp.exp(xr * d)
    return mag * jnp.cos(xi * d), mag * jnp.sin(xi * d)


def _s5_prep_kernel(lr_r, li_r, ld_r, lr_c, li_c, ld_c, btr, bti, ctr, cti, d_r,
                    wt_ref, mi_ref, mo_ref, a_ref, kt_s):
    L = S5_L
    lr = lr_r[...]
    li = li_r[...]
    dt = jnp.exp(ld_r[...])
    xr, xi = lr * dt, li * dt
    ar, ai = _cpow(xr, xi, 1.0)
    den = lr * lr + li * li
    fr = ((ar - 1.0) * lr + ai * li) / den
    fi = (ai * lr - (ar - 1.0) * li) / den
    br, bi = btr[...], bti[...]
    bbr = fr * br - fi * bi
    bbi = fr * bi + fi * br
    cr, ci = ctr[...], cti[...]
    pr, pi = jnp.ones_like(ar), jnp.zeros_like(ai)
    for d in range(L):
        xdr = bbr * pr - bbi * pi
        xdi = bbr * pi + bbi * pr
        i = L - 1 - d
        mo_ref[i * LANES:(i + 1) * LANES, :S5_ST] = xdr.astype(bf16)
        mo_ref[i * LANES:(i + 1) * LANES, S5_ST:] = xdi.astype(bf16)
        kt = (jnp.dot(xdr, cr, preferred_element_type=f32, precision=lax.Precision.HIGHEST)
              - jnp.dot(xdi, ci, preferred_element_type=f32, precision=lax.Precision.HIGHEST))
        if d == 0:
            r = lax.broadcasted_iota(i32, (LANES, LANES), 0)
            c = lax.broadcasted_iota(i32, (LANES, LANES), 1)
            kt = kt + jnp.where(r == c, d_r[...], 0.0)
        kt_s[d] = kt.astype(bf16)
        pr, pi = pr * ar - pi * ai, pr * ai + pi * ar
    a_ref[:, :S5_ST] = pr
    a_ref[:, S5_ST:] = pi
    zero = jnp.zeros((LANES, LANES), bf16)
    for i in range(L):
        for j in range(L):
            wt_ref[i * LANES:(i + 1) * LANES, j * LANES:(j + 1) * LANES] = kt_s[j - i] if j >= i else zero
    lrc = lr_c[...]
    lic = li_c[...]
    dtc = jnp.exp(ld_c[...])
    acr, aci = _cpow(lrc * dtc, lic * dtc, 1.0)
    pr, pi = acr, aci
    for j in range(L):
        mi_ref[:S5_ST, j * LANES:(j + 1) * LANES] = (cr * pr - ci * pi).astype(bf16)
        mi_ref[S5_ST:, j * LANES:(j + 1) * LANES] = (-(cr * pi + ci * pr)).astype(bf16)
        pr, pi = pr * acr - pi * aci, pr * aci + pi * acr


def _s5_prep(lr, li, ld, b_re, b_im, c_re, c_im, dvec):
    G, P, H, M, GPT = S5_G, S5_P, S5_CH, S5_M, S5_GPT
    eye = jnp.eye(GPT, dtype=f32)

    def rows(v):
        return v.reshape(M, 1, S5_ST)

    def cols(v):
        return v.reshape(M, S5_ST, 1)

    ldp = jnp.broadcast_to(ld[:, None], (G, P))

    def bt(b):
        b4 = b.reshape(M, GPT, P, H)
        return jnp.einsum('mgph,gk->mkhgp', b4, eye).reshape(M, LANES, S5_ST)

    def ct(c):
        c4 = c.reshape(M, GPT, H, P)
        return jnp.einsum('mghp,gk->mgpkh', c4, eye).reshape(M, S5_ST, LANES)

    L = S5_L
    mspec = lambda shape: pl.BlockSpec((None,) + shape, lambda m: (m,) + (0,) * len(shape))
    return pl.pallas_call(
        _s5_prep_kernel,
        grid=(M,),
        in_specs=[mspec((1, S5_ST))] * 3 + [mspec((S5_ST, 1))] * 3
        + [mspec((LANES, S5_ST))] * 2 + [mspec((S5_ST, LANES))] * 2 + [mspec((1, LANES))],
        out_specs=[mspec((L * LANES, L * LANES)), mspec((2 * S5_ST, L * LANES)),
                   mspec((L * LANES, 2 * S5_ST)), mspec((1, 2 * S5_ST))],
        out_shape=[jax.ShapeDtypeStruct((M, L * LANES, L * LANES), bf16),
                   jax.ShapeDtypeStruct((M, 2 * S5_ST, L * LANES), bf16),
                   jax.ShapeDtypeStruct((M, L * LANES, 2 * S5_ST), bf16),
                   jax.ShapeDtypeStruct((M, 1, 2 * S5_ST), f32)],
        scratch_shapes=[pltpu.VMEM((L, LANES, LANES), bf16)],
        compiler_params=pltpu.CompilerParams(dimension_semantics=("arbitrary",), vmem_limit_bytes=VMEM_LIMIT),
        name="s5_prep",
    )(rows(lr), rows(li), rows(ldp), cols(lr), cols(li), cols(ldp),
      bt(b_re), bt(b_im), ct(c_re), ct(c_im), dvec.reshape(M, 1, LANES))


def _s5_scan_kernel(u_ref, wt, mi, mo, a_ref, y_ref, xs, s_s, x_s):
    r = pl.program_id(1)

    @pl.when(r == 0)
    def _():
        xs[...] = jnp.zeros_like(xs)

    u = u_ref[...].reshape(S5_ROWS, S5_L * LANES)
    s = _dot(u, mo[...])
    NT = 2 * S5_ST // LANES
    for c in range(NT):
        s_s[c] = s[:, c * LANES:(c + 1) * LANES]
    a = [a_ref[:, c * LANES:(c + 1) * LANES] for c in range(NT)]

    def step(n, x):
        rows = pl.ds(n, B, stride=S5_NCH)
        new = []
        for c in range(NT):
            x_s[c, rows, :] = x[c]
        for c in range(NT // 2):
            ar, ai, xr, xi = a[c], a[NT // 2 + c], x[c], x[NT // 2 + c]
            new.append((ar * xr - ai * xi + s_s[c, rows, :], ar * xi + ai * xr + s_s[NT // 2 + c, rows, :]))
        return tuple(p[0] for p in new) + tuple(p[1] for p in new)

    x = lax.fori_loop(0, S5_NCH, step, tuple(xs[:, c * LANES:(c + 1) * LANES] for c in range(NT)))
    for c in range(NT):
        xs[:, c * LANES:(c + 1) * LANES] = x[c]
    x_in = jnp.concatenate([x_s[c] for c in range(NT)], axis=1)
    y = _dot(u, wt[...]) + _dot(x_in.astype(bf16), mi[...])
    y_ref[...] = jax.nn.gelu(y).astype(bf16).reshape(B, S5_NCH, S5_L * LANES)


def _s5_scan(u, wt, mi, mo, a16):
    L = S5_L
    wspec = lambda shape: pl.BlockSpec((None,) + shape, lambda m, r: (m,) + (0,) * len(shape))
    uspec = pl.BlockSpec((None, B, S5_NCH, L * LANES), lambda m, r: (m, 0, r, 0))
    return pl.pallas_call(
        _s5_scan_kernel,
        grid=(S5_M, T // L // S5_NCH),
        in_specs=[uspec,
                  wspec((L * LANES, L * LANES)), wspec((2 * S5_ST, L * LANES)),
                  wspec((L * LANES, 2 * S5_ST)), wspec((1, 2 * S5_ST))],
        out_specs=uspec,
        out_shape=jax.ShapeDtypeStruct((S5_M, B, T // L, L * LANES), bf16),
        scratch_shapes=[pltpu.VMEM((B, 2 * S5_ST), f32), pltpu.VMEM((2 * S5_ST // LANES, S5_ROWS, LANES), f32),
                        pltpu.VMEM((2 * S5_ST // LANES, S5_ROWS, LANES), f32)],
        compiler_params=pltpu.CompilerParams(
            dimension_semantics=("arbitrary", "arbitrary"), vmem_limit_bytes=VMEM_LIMIT),
        name="s5_scan",
    )(u, wt, mi, mo, a16)


def _mixer_out_kernel(x_ref, gla_ref, y_ref, g1_ref, sh2_ref, sc2_ref, gluw, glub, woa, wob,
                      mpg, fpg, rw_hi, rw_lo, rb, h1_ref, hn2_ref, off_ref, gate_ref, cnt_ref, cnt_s, y_s):
    first = jnp.logical_and(pl.program_id(0) == 0, pl.program_id(1) == 0)

    @pl.when(first)
    def _():
        cnt_s[...] = jnp.zeros_like(cnt_s)

    for m in range(S5_M):
        for jj in range(S5_L):
            y_s[m, pl.ds(jj, TT // S5_L, stride=S5_L), :] = y_ref[m, :, jj * LANES:(jj + 1) * LANES].astype(f32)
    y = jnp.concatenate([y_s[m] for m in range(S5_M)], axis=1)
    z = _dot(y.astype(bf16), gluw[...]) + glub[...]
    s5 = y * jax.nn.sigmoid(z)
    mix = _dot(gla_ref[...], woa[...]) + _dot(s5.astype(bf16), wob[...])
    h1 = x_ref[...] + g1_ref[...] * (mix * _rms(mix) * mpg[...])
    hn2 = (h1 * _rms(h1) * fpg[...]) * (1.0 + sc2_ref[...]) + sh2_ref[...]
    for cc in range(ROW):
        h1_ref[pl.ds(cc, TT, stride=ROW), :] = h1[:, cc * LANES:(cc + 1) * LANES]
        hn2_ref[pl.ds(cc, TT, stride=ROW), :] = hn2[:, cc * LANES:(cc + 1) * LANES]
    x_hi = hn2.astype(bf16)
    x_lo = (hn2 - x_hi.astype(f32)).astype(bf16)
    logits = _dot(x_hi, rw_hi[...]) + (_dot(x_lo, rw_hi[...]) + _dot(x_hi, rw_lo[...])) + rb[...]

    lane = lax.broadcasted_iota(i32, (TT, LANES), 1).astype(f32)
    l = logits
    vals, idxs = [], []
    for _ in range(TOPK):
        m = jnp.max(l, axis=-1, keepdims=True)
        ix = jnp.min(jnp.where(l == m, lane, float(LANES)), axis=-1, keepdims=True)
        vals.append(m)
        idxs.append(ix)
        l = jnp.where(lane == ix, -jnp.inf, l)
    es = [jnp.exp(v - vals[0]) for v in vals]
    tot = es[0] + es[1] + es[2] + es[3]
    oh = jnp.zeros((TT, LANES), f32)
    for ix in idxs:
        oh = oh + (lane == ix).astype(f32)
    r = lax.broadcasted_iota(i32, (TT, TT), 0)
    c = lax.broadcasted_iota(i32, (TT, TT), 1)
    cum = _dot((r >= c).astype(bf16), oh.astype(bf16))
    excl = cum - oh + cnt_s[...]
    offs = jnp.zeros((TT, LANES), f32)
    gate = jnp.zeros((TT, LANES), f32)
    for kk in range(TOPK):
        rank = jnp.sum(jnp.where(lane == idxs[kk], excl, 0.0), axis=-1, keepdims=True)
        offs = jnp.where(lane == float(kk), (idxs[kk] * float(N) + rank) * float(ROW), offs)
        gate = jnp.where(lane == float(kk), es[kk] / tot, gate)
    off_ref[...] = jnp.transpose(offs)[:ROW, :].astype(i32)
    gate_ref[...] = gate
    cnt = cnt_s[...] + cum[TT - 1:TT, :]
    cnt_s[...] = cnt
    cnt_ref[...] = cnt


def _mixer_out(x, gla, y, mod3, gluw, glub, woa, wob, mpg, fpg, rw_hi, rw_lo, rb):
    nt = T // TT
    full = lambda shape: pl.BlockSpec(shape, lambda b, t: (0,) * len(shape))
    tok = lambda w: pl.BlockSpec((None, TT, w), lambda b, t: (b, t, 0))
    slab = pl.BlockSpec((TT * ROW, LANES), lambda b, t: (b * nt + t, 0))
    slab_shape = jax.ShapeDtypeStruct((N * ROW, LANES), f32)
    yspec = pl.BlockSpec((S5_M, None, TT // S5_L, S5_L * LANES), lambda b, t: (0, b, t, 0))
    modspec = lambda j: pl.BlockSpec((None, 1, D), lambda b, t: (b, 0, j))
    return pl.pallas_call(
        _mixer_out_kernel,
        grid=(B, nt),
        in_specs=[tok(D), tok(512), yspec,
                  modspec(2), modspec(3), modspec(4),
                  full((512, 512)), full((1, 512)), full((512, D)), full((512, D)),
                  full((1, D)), full((1, D)), full((D, LANES)), full((D, LANES)), full((1, LANES))],
        out_specs=[slab, slab, pl.BlockSpec((ROW, TT), lambda b, t: (0, b * nt + t)), tok(LANES), full((1, LANES))],
        out_shape=[slab_shape, slab_shape,
                   jax.ShapeDtypeStruct((ROW, N), i32), jax.ShapeDtypeStruct((B, T, LANES), f32),
                   jax.ShapeDtypeStruct((1, LANES), f32)],
        scratch_shapes=[pltpu.VMEM((1, LANES), f32), pltpu.VMEM((S5_M, TT, LANES), f32)],
        compiler_params=pltpu.CompilerParams(
            dimension_semantics=("arbitrary", "arbitrary"), vmem_limit_bytes=VMEM_LIMIT),
        name="mixer_out",
    )(x, gla, y, mod3, mod3, mod3, gluw, glub, woa, wob, mpg, fpg, rw_hi, rw_lo, rb)


def _slab(ref, off):
    return ref.at[pl.ds(pl.multiple_of(off, ROW), ROW), :]


def _dispatch_kernel(off_ref, x_ref, xs_hbm, sem):
    def group(g, carry):
        for j in range(DMA_UNROLL):
            t = g * DMA_UNROLL + j
            for kk in range(TOPK):
                pltpu.make_async_copy(_slab(x_ref, t * ROW), _slab(xs_hbm, off_ref[kk, t]),
                                      sem).start(priority=kk % 2)
        return carry

    lax.fori_loop(0, TS // DMA_UNROLL, group, 0)
    for kk in range(TOPK):
        pltpu.make_async_copy(x_ref, xs_hbm.at[pl.ds(0, TS * ROW), :], sem).wait()


def _dispatch(offs, hn2_rows):
    return pl.pallas_call(
        _dispatch_kernel,
        grid=(N // TS,),
        in_specs=[pl.BlockSpec((ROW, TS), lambda i: (0, i), memory_space=pltpu.SMEM),
                  pl.BlockSpec((TS * ROW, LANES), lambda i: (i, 0))],
        out_specs=pl.BlockSpec(memory_space=pl.ANY),
        out_shape=jax.ShapeDtypeStruct((NE * N * ROW, LANES), f32),
        scratch_shapes=[pltpu.SemaphoreType.DMA(())],
        compiler_params=pltpu.CompilerParams(dimension_semantics=("arbitrary",), vmem_limit_bytes=VMEM_LIMIT),
        name="dispatch",
    )(offs, hn2_rows)


def _experts_kernel(tb_ref, te_ref, tv_ref, nu_ref, x_ref, wgu_ref, bgu_ref, wd_ref, bd_ref, y_ref, wgu_b, wd_b):
    i = pl.program_id(0)

    @pl.when(i < nu_ref[0])
    def _():
        changed = jnp.logical_or(i == 0, te_ref[i] != te_ref[jnp.maximum(i - 1, 0)])

        @pl.when(changed)
        def _():
            def cast(j, carry):
                r0 = pl.multiple_of(j * LANES, LANES)
                wgu_b[pl.ds(r0, LANES), :] = wgu_ref[pl.ds(r0, LANES), :].astype(bf16)
                wd_b[pl.ds(r0, LANES), :] = wd_ref[pl.ds(r0, LANES), :].astype(bf16)
                return carry
            lax.fori_loop(0, D // LANES, cast, 0)

        def tile(h):
            base = h * MB * ROW
            xt = jnp.concatenate([x_ref[pl.ds(base + cc, MB, stride=ROW), :] for cc in range(ROW)], axis=1)
            valid = lax.broadcasted_iota(i32, (MB, 1), 0) < tv_ref[i] - h * MB
            xt = jnp.where(valid, xt, 0.0).astype(bf16)
            gu = _dot(xt, wgu_b[...]) + bgu_ref[...]
            g = jnp.minimum(gu[:, :DFF], LIMIT)
            up = jnp.clip(gu[:, DFF:], -LIMIT, LIMIT)
            act = (up + 1.0) * (g * jax.nn.sigmoid(ALPHA * g))
            yv = _dot(act.astype(bf16), wd_b[...]) + bd_ref[...]
            for cc in range(ROW):
                y_ref[pl.ds(base + cc, MB, stride=ROW), :] = yv[:, cc * LANES:(cc + 1) * LANES]

        tile(0)
        for h in range(1, EB):
            pl.when(tv_ref[i] > h * MB)(functools.partial(tile, h))


def _experts(tile_blk, tile_e, tile_nv, nused, xs, w_gu, b_gu, w_down, b_down):
    blk = lambda i, tb, te, tv, nu: (tb[i], 0)
    wsel = lambda i, tb, te, tv, nu: (te[i], 0, 0)
    grid_spec = pltpu.PrefetchScalarGridSpec(
        num_scalar_prefetch=4,
        grid=(NB,),
        in_specs=[
            pl.BlockSpec((EB * MB * ROW, LANES), blk),
            pl.BlockSpec((None, D, 2 * DFF), wsel),
            pl.BlockSpec((None, 1, 2 * DFF), wsel),
            pl.BlockSpec((None, DFF, D), wsel),
            pl.BlockSpec((None, 1, D), wsel),
        ],
        out_specs=pl.BlockSpec((EB * MB * ROW, LANES), blk),
        scratch_shapes=[pltpu.VMEM((D, 2 * DFF), bf16), pltpu.VMEM((DFF, D), bf16)],
    )
    return pl.pallas_call(
        _experts_kernel,
        grid_spec=grid_spec,
        out_shape=jax.ShapeDtypeStruct((NE * N * ROW, LANES), f32),
        compiler_params=pltpu.CompilerParams(dimension_semantics=("arbitrary",), vmem_limit_bytes=VMEM_LIMIT),
        name="experts",
    )(tile_blk, tile_e, tile_nv, nused, xs, w_gu, b_gu.reshape(NE, 1, 2 * DFF), w_down, b_down.reshape(NE, 1, D))


def _combine_kernel(off_ref, offn_ref, gate_ref, h1_ref, g2_ref, pg_ref, ys_hbm, o_ref, gbuf, wbuf, rbuf, sem):
    j = pl.program_id(0)
    slot = lax.rem(j, 2)

    def issue(idx_ref, s):
        def group(g, carry):
            for jj in range(DMA_UNROLL):
                t = g * DMA_UNROLL + jj
                for kk in range(TOPK):
                    pltpu.make_async_copy(_slab(ys_hbm, idx_ref[kk, t]),
                                          gbuf.at[s, kk, pl.ds(pl.multiple_of(t * ROW, ROW), ROW), :],
                                          sem.at[s]).start(priority=kk % 2)
            return carry
        lax.fori_loop(0, ET // DMA_UNROLL, group, 0)

    @pl.when(j == 0)
    def _():
        issue(off_ref, 0)

    @pl.when(j + 1 < pl.num_programs(0))
    def _():
        issue(offn_ref, 1 - slot)

    for kk in range(TOPK):
        gk = jnp.broadcast_to(gate_ref[:, kk:kk + 1], (ET, LANES))
        for s in range(ROW):
            wbuf[kk, pl.ds(s, ET, stride=ROW), :] = gk
    for kk in range(TOPK):
        pltpu.make_async_copy(ys_hbm.at[pl.ds(0, ET * ROW), :], gbuf.at[slot, kk], sem.at[slot]).wait()
    ff = gbuf[slot, 0] * wbuf[0]
    for kk in range(1, TOPK):
        ff = ff + gbuf[slot, kk] * wbuf[kk]
    ff3 = ff.reshape(ET, ROW, LANES)
    ss = jnp.sum(jnp.sum(ff3 * ff3, axis=2, keepdims=True), axis=1, keepdims=True)
    rs = lax.rsqrt(ss * (1.0 / D) + EPS)
    out = h1_ref[...].reshape(ET, ROW, LANES) + g2_ref[...] * (ff3 * rs * pg_ref[...])
    rbuf[...] = out.reshape(ET * ROW, LANES)
    for cc in range(ROW):
        o_ref[:, cc * LANES:(cc + 1) * LANES] = rbuf[pl.ds(cc, ET, stride=ROW), :]


def _combine(offs, gates, h1_rows, mod4, pg, ys):
    nt = N // ET
    per_b = T // ET
    return pl.pallas_call(
        _combine_kernel,
        grid=(nt,),
        in_specs=[pl.BlockSpec((ROW, ET), lambda j: (0, j), memory_space=pltpu.SMEM),
                  pl.BlockSpec((ROW, ET), lambda j: (0, jnp.minimum(j + 1, nt - 1)), memory_space=pltpu.SMEM),
                  pl.BlockSpec((ET, LANES), lambda j: (j, 0)),
                  pl.BlockSpec((ET * ROW, LANES), lambda j: (j, 0)),
                  pl.BlockSpec((None, None, ROW, LANES), lambda j: (j // per_b, 5, 0, 0)),
                  pl.BlockSpec((ROW, LANES), lambda j: (0, 0)),
                  pl.BlockSpec(memory_space=pl.ANY)],
        out_specs=pl.BlockSpec((ET, D), lambda j: (j, 0)),
        out_shape=jax.ShapeDtypeStruct((N, D), f32),
        scratch_shapes=[pltpu.VMEM((2, TOPK, ET * ROW, LANES), f32), pltpu.VMEM((TOPK, ET * ROW, LANES), f32),
                        pltpu.VMEM((ET * ROW, LANES), f32), pltpu.SemaphoreType.DMA((2,))],
        compiler_params=pltpu.CompilerParams(dimension_semantics=("arbitrary",), vmem_limit_bytes=VMEM_LIMIT),
        name="combine",
    )(offs, offs, gates, h1_rows, mod4, pg, ys)


def _pad_heads(w):
    lead = w.shape[:-1]
    w4 = w.reshape(lead + (GLA_H, GLA_DK))
    w4 = jnp.concatenate([w4, jnp.zeros_like(w4)], axis=-1)
    return w4.reshape(lead + (GLA_H * LANES,))


def kernel(x, c, ada_w, ada_b, mix_pre_g, mix_post_g, ffn_pre_g, ffn_post_g, w_in, w_alpha, b_alpha, gla_norm_g, s5_lambda_re, s5_lambda_im, s5_log_dt, s5_b_re, s5_b_im, s5_c_re, s5_c_im, s5_d, s5_glu_w, s5_glu_b, w_out, router_w, router_b, exp_w_gu, exp_b_gu, exp_w_down, exp_b_down):
    l = 0
    mod = _ada(c, ada_w[l], ada_b[l])
    mod3 = mod.reshape(B, 1, 6 * D)
    mod4 = mod.reshape(B, 6, ROW, LANES)

    w = w_in[l]
    o_q, o_k, o_v, o_g, o_a, o_u = 0, 256, 512, 1024, 1536, 1552
    wq = _pad_heads(w[:, o_q:o_k]).astype(bf16)
    wk = _pad_heads(w[:, o_k:o_v]).astype(bf16)
    wv = w[:, o_v:o_g].astype(bf16)
    wg = w[:, o_g:o_a].astype(bf16)
    wa = jnp.pad(w[:, o_a:o_u], ((0, 0), (0, LANES - LOWRANK))).astype(bf16)
    wu = w[:, o_u:].astype(bf16)
    walpha = jnp.pad(_pad_heads(w_alpha[l]), ((0, LANES - LOWRANK), (0, 0))).astype(bf16)
    balpha = _pad_heads(b_alpha[l]).reshape(1, GLA_H * LANES)

    gla, u = _mixer_in(x, mod3, mix_pre_g[l].reshape(1, D), wq, wk, wv, wg, wa, wu,
                       walpha, balpha, gla_norm_g[l].reshape(1, GLA_DV))

    wt, mi, mo, a16 = _s5_prep(s5_lambda_re[l], s5_lambda_im[l], s5_log_dt[l], s5_b_re[l], s5_b_im[l],
                               s5_c_re[l], s5_c_im[l], s5_d[l])
    y = _s5_scan(u, wt, mi, mo, a16)

    rw = jnp.pad(router_w[l], ((0, 0), (0, LANES - NE)))
    rw_hi = rw.astype(bf16)
    rw_lo = (rw - rw_hi.astype(f32)).astype(bf16)
    rb = jnp.pad(router_b[l], (0, LANES - NE), constant_values=-1e30).reshape(1, LANES)
    wo = w_out[l].astype(bf16)
    h1_rows, hn2_rows, offs, gates, cnt = _mixer_out(
        x, gla, y, mod3, s5_glu_w[l].astype(bf16), s5_glu_b[l].reshape(1, 512), wo[:512], wo[512:],
        mix_post_g[l].reshape(1, D), ffn_pre_g[l].reshape(1, D), rw_hi, rw_lo, rb)

    counts = cnt[0, :NE].astype(i32)
    step_rows = EB * MB
    ntile = (counts + step_rows - 1) // step_rows
    tends = jnp.cumsum(ntile)
    nused = tends[-1]
    gi = jnp.minimum(jnp.arange(NB, dtype=i32), nused - 1)
    tile_e = jnp.sum((tends[None, :] <= gi[:, None]).astype(i32), axis=1)
    tile_j = gi - (tends - ntile)[tile_e]
    tile_blk = tile_e * TPE + tile_j
    tile_nv = jnp.clip(counts[tile_e] - tile_j * step_rows, 0, step_rows)

    xs = _dispatch(offs, hn2_rows)
    ys = _experts(tile_blk, tile_e, tile_nv, nused.reshape(1), xs,
                  exp_w_gu[l], exp_b_gu[l], exp_w_down[l], exp_b_down[l])
    out = _combine(offs, gates.reshape(N, LANES), h1_rows, mod4, ffn_post_g[l].reshape(ROW, LANES), ys)
    return out.reshape(B, T, D)
```

```python
import functools
import math

import jax
import jax.numpy as jnp
from jax import lax
from jax.experimental import pallas as pl
from jax.experimental.pallas import tpu as pltpu

f32 = jnp.float32
bf16 = jnp.bfloat16
i32 = jnp.int32

D = 1024
B = 8
T = 2048
N = B * T
GLA_H = 4
GLA_DV = 128
GLA_DK = 64
GLA_TAU = 16.0
GLA_CHUNK = 64
LOWRANK = 16
S5_W = 512
S5_CH = 16
S5_G = 32
S5_P = 64
NE = 32
TOPK = 4
DFF = 1024
ALPHA = 1.702
LIMIT = 7.0
EPS = 1e-6

LANES = 128
SUBLANES = 8
VMEM_LIMIT = 56 * 1024 * 1024

TT = 512
S5_L = 16
S5_M = S5_W // LANES
S5_GPT = LANES // S5_CH
S5_ST = S5_GPT * S5_P
S5_NCH = 32
S5_ROWS = B * S5_NCH
ROW = SUBLANES
MB = 256
EB = 2
NB = (N * TOPK) // (EB * MB) + NE
TPE = N // (EB * MB)
TS = 512
ET = 256
DMA_UNROLL = 8
COMBINE_GROUP = 16


def _dot(a, b):
    return jnp.dot(a, b, preferred_element_type=f32)


def _dot_t(a, b, ca, cb):
    return lax.dot_general(a, b, (((ca,), (cb,)), ((), ())), preferred_element_type=f32)


def _rms(x):
    return lax.rsqrt(jnp.mean(x * x, axis=-1, keepdims=True) + EPS)


def _ada_kernel(c_ref, w_ref, b_ref, o_ref):
    c = c_ref[...]
    s = (c * jax.nn.sigmoid(c)).astype(bf16)
    o_ref[...] = _dot(s, w_ref[...].astype(bf16)) + b_ref[...]


def _ada(c, w, b):
    return pl.pallas_call(
        _ada_kernel,
        grid=(6,),
        in_specs=[
            pl.BlockSpec((B, D), lambda j: (0, 0)),
            pl.BlockSpec((D, D), lambda j: (0, j)),
            pl.BlockSpec((1, D), lambda j: (0, j)),
        ],
        out_specs=pl.BlockSpec((B, D), lambda j: (0, j)),
        out_shape=jax.ShapeDtypeStruct((B, 6 * D), f32),
        name="ada",
    )(c, w, b.reshape(1, 6 * D))


def _mixer_in_kernel(x_ref, sc_ref, sh_ref, g_ref, wq, wk, wv, wg, wa, wu, walpha, balpha, gng,
                     gla_ref, u_ref, st_s, u_s):
    t = pl.program_id(1)

    @pl.when(t == 0)
    def _():
        st_s[...] = jnp.zeros_like(st_s)

    x = x_ref[...]
    hn = (x * _rms(x) * g_ref[...]) * (1.0 + sc_ref[...]) + sh_ref[...]
    hb = hn.astype(bf16)
    u = _dot(hb, wu[...])
    for m in range(S5_M):
        u_s[m] = u[:, m * LANES:(m + 1) * LANES]
        for i in range(S5_L):
            u_ref[m, :, i * LANES:(i + 1) * LANES] = u_s[m, pl.ds(i, TT // S5_L, stride=S5_L), :].astype(bf16)

    C = GLA_CHUNK
    NC = TT // C
    row = lax.broadcasted_iota(i32, (TT, TT), 0)
    col = lax.broadcasted_iota(i32, (TT, TT), 1)
    causal = jnp.logical_and(row >= col, (row // C) == (col // C))
    tri = causal.astype(bf16)
    a_lr = _dot(hb, wa[...]).astype(bf16)
    la = jax.nn.log_sigmoid(_dot(a_lr, walpha[...]) + balpha[...]) * (1.0 / GLA_TAU)
    la_hi = la.astype(bf16)
    la_lo = (la - la_hi.astype(f32)).astype(bf16)
    bc = _dot(tri, la_hi) + _dot(tri, la_lo)
    bl = jnp.broadcast_to(bc.reshape(NC, C, GLA_H * LANES)[:, C - 1:C, :],
                          (NC, C, GLA_H * LANES)).reshape(TT, GLA_H * LANES)
    q = _dot(hb, wq[...]) * (GLA_DK ** -0.5)
    k = _dot(hb, wk[...])
    qd = (q * jnp.exp(bc)).astype(bf16)
    ki = (k * jnp.exp(-bc)).astype(bf16)
    ke = (k * jnp.exp(bl - bc)).astype(bf16)
    dec = jnp.exp(bl)
    vb = _dot(hb, wv[...]).astype(bf16)
    og = _dot(hb, wg[...])
    gn = gng[...]
    for h in range(GLA_H):
        sl = slice(h * LANES, (h + 1) * LANES)
        qh, kih, keh, vh = qd[:, sl], ki[:, sl], ke[:, sl], vb[:, sl]
        sc = jnp.where(causal, _dot_t(qh, kih, 1, 1), 0.0).astype(bf16)
        o_intra = _dot(sc, vh)
        st = st_s[h]
        o_parts = []
        for c in range(NC):
            rs = slice(c * C, (c + 1) * C)
            o_parts.append(_dot_t(qh[rs], st.astype(bf16), 1, 1))
            st = st * dec[c * C:c * C + 1, sl] + _dot_t(vh[rs], keh[rs], 0, 0)
        st_s[h] = st
        o = o_intra + jnp.concatenate(o_parts, axis=0)
        on = o * _rms(o) * gn
        ogh = og[:, sl]
        gla_ref[:, sl] = (on * (ogh * jax.nn.sigmoid(ogh))).astype(bf16)


def _mixer_in(x, mod3, g, wq, wk, wv, wg, wa, wu, walpha, balpha, gng):
    nt = T // TT
    full = lambda shape: pl.BlockSpec(shape, lambda b, t: (0,) * len(shape))
    u_shape = jax.ShapeDtypeStruct((S5_M, B, T // S5_L, S5_L * LANES), bf16)
    u_spec = pl.BlockSpec((S5_M, None, TT // S5_L, S5_L * LANES), lambda b, t: (0, b, t, 0))
    return pl.pallas_call(
        _mixer_in_kernel,
        grid=(B, nt),
        in_specs=[
            pl.BlockSpec((None, TT, D), lambda b, t: (b, t, 0)),
            pl.BlockSpec((None, 1, D), lambda b, t: (b, 0, 1)),
            pl.BlockSpec((None, 1, D), lambda b, t: (b, 0, 0)),
            full((1, D)),
            full((D, 512)), full((D, 512)), full((D, 512)), full((D, 512)),
            full((D, LANES)), full((D, 512)),
            full((LANES, 512)), full((1, 512)), full((1, LANES)),
        ],
        out_specs=[pl.BlockSpec((None, TT, 512), lambda b, t: (b, t, 0)), u_spec],
        out_shape=[jax.ShapeDtypeStruct((B, T, 512), bf16), u_shape],
        scratch_shapes=[pltpu.VMEM((GLA_H, GLA_DV, LANES), f32), pltpu.VMEM((S5_M, TT, LANES), f32)],
        compiler_params=pltpu.CompilerParams(
            dimension_semantics=("arbitrary", "arbitrary"), vmem_limit_bytes=VMEM_LIMIT),
        name="mixer_in",
    )(x, mod3, mod3, g, wq, wk, wv, wg, wa, wu, walpha, balpha, gng)


def _cpow(xr, xi, d):
    mag = jnp.exp(xr * d)
    return mag * jnp.cos(xi * d), mag * jnp.sin(xi * d)


def _s5_prep_kernel(lr_r, li_r, ld_r, lr_c, li_c, ld_c, btr, bti, ctr, cti, d_r,
                    wt_ref, mi_ref, mo_ref, a_ref, kt_s):
    L = S5_L
    lr = lr_r[...]
    li = li_r[...]
    dt = jnp.exp(ld_r[...])
    xr, xi = lr * dt, li * dt
    ar, ai = _cpow(xr, xi, 1.0)
    den = lr * lr + li * li
    fr = ((ar - 1.0) * lr + ai * li) / den
    fi = (ai * lr - (ar - 1.0) * li) / den
    br, bi = btr[...], bti[...]
    bbr = fr * br - fi * bi
    bbi = fr * bi + fi * br
    cr, ci = ctr[...], cti[...]
    pr, pi = jnp.ones_like(ar), jnp.zeros_like(ai)
    for d in range(L):
        xdr = bbr * pr - bbi * pi
        xdi = bbr * pi + bbi * pr
        i = L - 1 - d
        mo_ref[i * LANES:(i + 1) * LANES, :S5_ST] = xdr.astype(bf16)
        mo_ref[i * LANES:(i + 1) * LANES, S5_ST:] = xdi.astype(bf16)
        kt = (jnp.dot(xdr, cr, preferred_element_type=f32, precision=lax.Precision.HIGHEST)
              - jnp.dot(xdi, ci, preferred_element_type=f32, precision=lax.Precision.HIGHEST))
        if d == 0:
            r = lax.broadcasted_iota(i32, (LANES, LANES), 0)
            c = lax.broadcasted_iota(i32, (LANES, LANES), 1)
            kt = kt + jnp.where(r == c, d_r[...], 0.0)
        kt_s[d] = kt.astype(bf16)
        pr, pi = pr * ar - pi * ai, pr * ai + pi * ar
    a_ref[:, :S5_ST] = pr
    a_ref[:, S5_ST:] = pi
    zero = jnp.zeros((LANES, LANES), bf16)
    for i in range(L):
        for j in range(L):
            wt_ref[i * LANES:(i + 1) * LANES, j * LANES:(j + 1) * LANES] = kt_s[j - i] if j >= i else zero
    lrc = lr_c[...]
    lic = li_c[...]
    dtc = jnp.exp(ld_c[...])
    acr, aci = _cpow(lrc * dtc, lic * dtc, 1.0)
    pr, pi = acr, aci
    for j in range(L):
        mi_ref[:S5_ST, j * LANES:(j + 1) * LANES] = (cr * pr - ci * pi).astype(bf16)
        mi_ref[S5_ST:, j * LANES:(j + 1) * LANES] = (-(cr * pi + ci * pr)).astype(bf16)
        pr, pi = pr * acr - pi * aci, pr * aci + pi * acr


def _s5_prep(lr, li, ld, b_re, b_im, c_re, c_im, dvec):
    G, P, H, M, GPT = S5_G, S5_P, S5_CH, S5_M, S5_GPT
    eye = jnp.eye(GPT, dtype=f32)

    def rows(v):
        return v.reshape(M, 1, S5_ST)

    def cols(v):
        return v.reshape(M, S5_ST, 1)

    ldp = jnp.broadcast_to(ld[:, None], (G, P))

    def bt(b):
        b4 = b.reshape(M, GPT, P, H)
        return jnp.einsum('mgph,gk->mkhgp', b4, eye).reshape(M, LANES, S5_ST)

    def ct(c):
        c4 = c.reshape(M, GPT, H, P)
        return jnp.einsum('mghp,gk->mgpkh', c4, eye).reshape(M, S5_ST, LANES)

    L = S5_L
    mspec = lambda shape: pl.BlockSpec((None,) + shape, lambda m: (m,) + (0,) * len(shape))
    return pl.pallas_call(
        _s5_prep_kernel,
        grid=(M,),
        in_specs=[mspec((1, S5_ST))] * 3 + [mspec((S5_ST, 1))] * 3
        + [mspec((LANES, S5_ST))] * 2 + [mspec((S5_ST, LANES))] * 2 + [mspec((1, LANES))],
        out_specs=[mspec((L * LANES, L * LANES)), mspec((2 * S5_ST, L * LANES)),
                   mspec((L * LANES, 2 * S5_ST)), mspec((1, 2 * S5_ST))],
        out_shape=[jax.ShapeDtypeStruct((M, L * LANES, L * LANES), bf16),
                   jax.ShapeDtypeStruct((M, 2 * S5_ST, L * LANES), bf16),
                   jax.ShapeDtypeStruct((M, L * LANES, 2 * S5_ST), bf16),
                   jax.ShapeDtypeStruct((M, 1, 2 * S5_ST), f32)],
        scratch_shapes=[pltpu.VMEM((L, LANES, LANES), bf16)],
        compiler_params=pltpu.CompilerParams(dimension_semantics=("arbitrary",), vmem_limit_bytes=VMEM_LIMIT),
        name="s5_prep",
    )(rows(lr), rows(li), rows(ldp), cols(lr), cols(li), cols(ldp),
      bt(b_re), bt(b_im), ct(c_re), ct(c_im), dvec.reshape(M, 1, LANES))


def _s5_scan_kernel(u_ref, wt, mi, mo, a_ref, y_ref, xs, s_s, x_s):
    r = pl.program_id(1)

    @pl.when(r == 0)
    def _():
        xs[...] = jnp.zeros_like(xs)

    u = u_ref[...].reshape(S5_ROWS, S5_L * LANES)
    s = _dot(u, mo[...])
    NT = 2 * S5_ST // LANES
    for c in range(NT):
        s_s[c] = s[:, c * LANES:(c + 1) * LANES]
    a = [a_ref[:, c * LANES:(c + 1) * LANES] for c in range(NT)]

    def step(n, x):
        rows = pl.ds(n, B, stride=S5_NCH)
        new = []
        for c in range(NT):
            x_s[c, rows, :] = x[c]
        for c in range(NT // 2):
            ar, ai, xr, xi = a[c], a[NT // 2 + c], x[c], x[NT // 2 + c]
            new.append((ar * xr - ai * xi + s_s[c, rows, :], ar * xi + ai * xr + s_s[NT // 2 + c, rows, :]))
        return tuple(p[0] for p in new) + tuple(p[1] for p in new)

    x = tuple(xs[:, c * LANES:(c + 1) * LANES] for c in range(NT))
    for n in range(S5_NCH):
        x = step(n, x)
    for c in range(NT):
        xs[:, c * LANES:(c + 1) * LANES] = x[c]
    x_in = jnp.concatenate([x_s[c] for c in range(NT)], axis=1)
    y = _dot(u, wt[...]) + _dot(x_in.astype(bf16), mi[...])
    y_ref[...] = jax.nn.gelu(y).astype(bf16).reshape(B, S5_NCH, S5_L * LANES)


def _s5_scan(u, wt, mi, mo, a16):
    L = S5_L
    wspec = lambda shape: pl.BlockSpec((None,) + shape, lambda m, r: (m,) + (0,) * len(shape))
    uspec = pl.BlockSpec((None, B, S5_NCH, L * LANES), lambda m, r: (m, 0, r, 0))
    return pl.pallas_call(
        _s5_scan_kernel,
        grid=(S5_M, T // L // S5_NCH),
        in_specs=[uspec,
                  wspec((L * LANES, L * LANES)), wspec((2 * S5_ST, L * LANES)),
                  wspec((L * LANES, 2 * S5_ST)), wspec((1, 2 * S5_ST))],
        out_specs=uspec,
        out_shape=jax.ShapeDtypeStruct((S5_M, B, T // L, L * LANES), bf16),
        scratch_shapes=[pltpu.VMEM((B, 2 * S5_ST), f32), pltpu.VMEM((2 * S5_ST // LANES, S5_ROWS, LANES), f32),
                        pltpu.VMEM((2 * S5_ST // LANES, S5_ROWS, LANES), f32)],
        compiler_params=pltpu.CompilerParams(
            dimension_semantics=("arbitrary", "arbitrary"), vmem_limit_bytes=VMEM_LIMIT),
        name="s5_scan",
    )(u, wt, mi, mo, a16)


def _mixer_out_kernel(x_ref, gla_ref, y_ref, g1_ref, sh2_ref, sc2_ref, gluw, glub, woa, wob,
                      mpg, fpg, rw_hi, rw_lo, rb, h1_ref, hn2_ref, off_ref, gate_ref, cnt_ref, cnt_s, y_s):
    first = jnp.logical_and(pl.program_id(0) == 0, pl.program_id(1) == 0)

    @pl.when(first)
    def _():
        cnt_s[...] = jnp.zeros_like(cnt_s)

    for m in range(S5_M):
        for jj in range(S5_L):
            y_s[m, pl.ds(jj, TT // S5_L, stride=S5_L), :] = y_ref[m, :, jj * LANES:(jj + 1) * LANES].astype(f32)
    y = jnp.concatenate([y_s[m] for m in range(S5_M)], axis=1)
    z = _dot(y.astype(bf16), gluw[...]) + glub[...]
    s5 = y * jax.nn.sigmoid(z)
    mix = _dot(gla_ref[...], woa[...]) + _dot(s5.astype(bf16), wob[...])
    h1 = x_ref[...] + g1_ref[...] * (mix * _rms(mix) * mpg[...])
    hn2 = (h1 * _rms(h1) * fpg[...]) * (1.0 + sc2_ref[...]) + sh2_ref[...]
    for cc in range(ROW):
        h1_ref[pl.ds(cc, TT, stride=ROW), :] = h1[:, cc * LANES:(cc + 1) * LANES]
        hn2_ref[pl.ds(cc, TT, stride=ROW), :] = hn2[:, cc * LANES:(cc + 1) * LANES]
    x_hi = hn2.astype(bf16)
    x_lo = (hn2 - x_hi.astype(f32)).astype(bf16)
    logits = _dot(x_hi, rw_hi[...]) + (_dot(x_lo, rw_hi[...]) + _dot(x_hi, rw_lo[...])) + rb[...]

    lane = lax.broadcasted_iota(i32, (TT, LANES), 1).astype(f32)
    l = logits
    vals, idxs = [], []
    for _ in range(TOPK):
        m = jnp.max(l, axis=-1, keepdims=True)
        ix = jnp.min(jnp.where(l == m, lane, float(LANES)), axis=-1, keepdims=True)
        vals.append(m)
        idxs.append(ix)
        l = jnp.where(lane == ix, -jnp.inf, l)
    es = [jnp.exp(v - vals[0]) for v in vals]
    tot = es[0] + es[1] + es[2] + es[3]
    oh = jnp.zeros((TT, LANES), f32)
    for ix in idxs:
        oh = oh + (lane == ix).astype(f32)
    r = lax.broadcasted_iota(i32, (TT, TT), 0)
    c = lax.broadcasted_iota(i32, (TT, TT), 1)
    cum = _dot((r >= c).astype(bf16), oh.astype(bf16))
    excl = cum - oh + cnt_s[...]
    offs = jnp.zeros((TT, LANES), f32)
    gate = jnp.zeros((TT, LANES), f32)
    for kk in range(TOPK):
        rank = jnp.sum(jnp.where(lane == idxs[kk], excl, 0.0), axis=-1, keepdims=True)
        offs = jnp.where(lane == float(kk), (idxs[kk] * float(N) + rank) * float(ROW), offs)
        gate = jnp.where(lane == float(kk), es[kk] / tot, gate)
    off_ref[...] = jnp.transpose(offs)[:ROW, :].astype(i32)
    gate_ref[...] = gate
    cnt = cnt_s[...] + cum[TT - 1:TT, :]
    cnt_s[...] = cnt
    cnt_ref[...] = cnt


def _mixer_out(x, gla, y, mod3, gluw, glub, woa, wob, mpg, fpg, rw_hi, rw_lo, rb):
    nt = T // TT
    full = lambda shape: pl.BlockSpec(shape, lambda b, t: (0,) * len(shape))
    tok = lambda w: pl.BlockSpec((None, TT, w), lambda b, t: (b, t, 0))
    slab = pl.BlockSpec((TT * ROW, LANES), lambda b, t: (b * nt + t, 0))
    slab_shape = jax.ShapeDtypeStruct((N * ROW, LANES), f32)
    yspec = pl.BlockSpec((S5_M, None, TT // S5_L, S5_L * LANES), lambda b, t: (0, b, t, 0))
    modspec = lambda j: pl.BlockSpec((None, 1, D), lambda b, t: (b, 0, j))
    return pl.pallas_call(
        _mixer_out_kernel,
        grid=(B, nt),
        in_specs=[tok(D), tok(512), yspec,
                  modspec(2), modspec(3), modspec(4),
                  full((512, 512)), full((1, 512)), full((512, D)), full((512, D)),
                  full((1, D)), full((1, D)), full((D, LANES)), full((D, LANES)), full((1, LANES))],
        out_specs=[slab, slab, pl.BlockSpec((ROW, TT), lambda b, t: (0, b * nt + t)), tok(LANES), full((1, LANES))],
        out_shape=[slab_shape, slab_shape,
                   jax.ShapeDtypeStruct((ROW, N), i32), jax.ShapeDtypeStruct((B, T, LANES), f32),
                   jax.ShapeDtypeStruct((1, LANES), f32)],
        scratch_shapes=[pltpu.VMEM((1, LANES), f32), pltpu.VMEM((S5_M, TT, LANES), f32)],
        compiler_params=pltpu.CompilerParams(
            dimension_semantics=("arbitrary", "arbitrary"), vmem_limit_bytes=VMEM_LIMIT),
        name="mixer_out",
    )(x, gla, y, mod3, mod3, mod3, gluw, glub, woa, wob, mpg, fpg, rw_hi, rw_lo, rb)


def _slab(ref, off):
    return ref.at[pl.ds(pl.multiple_of(off, ROW), ROW), :]


def _dispatch_kernel(off_ref, x_ref, xs_hbm, sem):
    def group(g, carry):
        for j in range(DMA_UNROLL):
            t = g * DMA_UNROLL + j
            for kk in range(TOPK):
                pltpu.make_async_copy(_slab(x_ref, t * ROW), _slab(xs_hbm, off_ref[kk, t]),
                                      sem).start(priority=kk % 2)
        return carry

    lax.fori_loop(0, TS // DMA_UNROLL, group, 0)
    for kk in range(TOPK):
        pltpu.make_async_copy(x_ref, xs_hbm.at[pl.ds(0, TS * ROW), :], sem).wait()


def _dispatch(offs, hn2_rows):
    return pl.pallas_call(
        _dispatch_kernel,
        grid=(N // TS,),
        in_specs=[pl.BlockSpec((ROW, TS), lambda i: (0, i), memory_space=pltpu.SMEM),
                  pl.BlockSpec((TS * ROW, LANES), lambda i: (i, 0))],
        out_specs=pl.BlockSpec(memory_space=pl.ANY),
        out_shape=jax.ShapeDtypeStruct((NE * N * ROW, LANES), f32),
        scratch_shapes=[pltpu.SemaphoreType.DMA(())],
        compiler_params=pltpu.CompilerParams(dimension_semantics=("arbitrary",), vmem_limit_bytes=VMEM_LIMIT),
        name="dispatch",
    )(offs, hn2_rows)


def _experts_kernel(tb_ref, te_ref, tv_ref, nu_ref, x_ref, wgu_ref, bgu_ref, wd_ref, bd_ref, y_ref, wgu_b, wd_b):
    i = pl.program_id(0)

    @pl.when(i < nu_ref[0])
    def _():
        changed = jnp.logical_or(i == 0, te_ref[i] != te_ref[jnp.maximum(i - 1, 0)])

        @pl.when(changed)
        def _():
            def cast(j, carry):
                r0 = pl.multiple_of(j * LANES, LANES)
                wgu_b[pl.ds(r0, LANES), :] = wgu_ref[pl.ds(r0, LANES), :].astype(bf16)
                wd_b[pl.ds(r0, LANES), :] = wd_ref[pl.ds(r0, LANES), :].astype(bf16)
                return carry
            lax.fori_loop(0, D // LANES, cast, 0)

        def tile(h):
            base = h * MB * ROW
            xt = jnp.concatenate([x_ref[pl.ds(base + cc, MB, stride=ROW), :] for cc in range(ROW)], axis=1)
            valid = lax.broadcasted_iota(i32, (MB, 1), 0) < tv_ref[i] - h * MB
            xt = jnp.where(valid, xt, 0.0).astype(bf16)
            gu = _dot(xt, wgu_b[...]) + bgu_ref[...]
            g = jnp.minimum(gu[:, :DFF], LIMIT)
            up = jnp.clip(gu[:, DFF:], -LIMIT, LIMIT)
            act = (up + 1.0) * (g * jax.nn.sigmoid(ALPHA * g))
            yv = _dot(act.astype(bf16), wd_b[...]) + bd_ref[...]
            for cc in range(ROW):
                y_ref[pl.ds(base + cc, MB, stride=ROW), :] = yv[:, cc * LANES:(cc + 1) * LANES]

        tile(0)
        for h in range(1, EB):
            pl.when(tv_ref[i] > h * MB)(functools.partial(tile, h))


def _experts(tile_blk, tile_e, tile_nv, nused, xs, w_gu, b_gu, w_down, b_down):
    blk = lambda i, tb, te, tv, nu: (tb[i], 0)
    wsel = lambda i, tb, te, tv, nu: (te[i], 0, 0)
    grid_spec = pltpu.PrefetchScalarGridSpec(
        num_scalar_prefetch=4,
        grid=(NB,),
        in_specs=[
            pl.BlockSpec((EB * MB * ROW, LANES), blk),
            pl.BlockSpec((None, D, 2 * DFF), wsel),
            pl.BlockSpec((None, 1, 2 * DFF), wsel),
            pl.BlockSpec((None, DFF, D), wsel),
            pl.BlockSpec((None, 1, D), wsel),
        ],
        out_specs=pl.BlockSpec((EB * MB * ROW, LANES), blk),
        scratch_shapes=[pltpu.VMEM((D, 2 * DFF), bf16), pltpu.VMEM((DFF, D), bf16)],
    )
    return pl.pallas_call(
        _experts_kernel,
        grid_spec=grid_spec,
        out_shape=jax.ShapeDtypeStruct((NE * N * ROW, LANES), f32),
        compiler_params=pltpu.CompilerParams(dimension_semantics=("arbitrary",), vmem_limit_bytes=VMEM_LIMIT),
        name="experts",
    )(tile_blk, tile_e, tile_nv, nused, xs, w_gu, b_gu.reshape(NE, 1, 2 * DFF), w_down, b_down.reshape(NE, 1, D))


def _combine_kernel(off_ref, offn_ref, gate_ref, h1_ref, g2_ref, pg_ref, ys_hbm, o_ref, gbuf, wbuf, rbuf, sem):
    j = pl.program_id(0)
    slot = lax.rem(j, 2)

    GT = COMBINE_GROUP
    GR = GT * ROW

    def issue_group(idx_ref, s, g):
        for jj in range(GT):
            t = g * GT + jj
            for kk in range(TOPK):
                pltpu.make_async_copy(_slab(ys_hbm, idx_ref[kk, t]),
                                      gbuf.at[s, kk, pl.ds(pl.multiple_of(t * ROW, ROW), ROW), :],
                                      sem.at[s]).start(priority=kk % 2)

    @pl.when(j == 0)
    def _():
        def group(g, carry):
            issue_group(off_ref, 0, g)
            return carry
        lax.fori_loop(0, ET // GT, group, 0)

    for kk in range(TOPK):
        wbuf[kk] = jnp.broadcast_to(gate_ref[:, kk:kk + 1], (ET, LANES))
    for kk in range(TOPK):
        pltpu.make_async_copy(ys_hbm.at[pl.ds(0, ET * ROW), :], gbuf.at[slot, kk], sem.at[slot]).wait()
    g2 = g2_ref[...]
    pg = pg_ref[...]

    def group(g, carry):
        rows = pl.ds(pl.multiple_of(g * GR, GR), GR)
        parts = []
        for jj in range(GT):
            t = g * GT + jj
            slab_rows = pl.ds(pl.multiple_of(t * ROW, ROW), ROW)
            acc = None
            for kk in range(TOPK):
                w = jnp.broadcast_to(wbuf[kk, pl.ds(t, 1), :], (ROW, LANES))
                term = gbuf[slot, kk, slab_rows, :] * w
                acc = term if acc is None else acc + term
            parts.append(acc)
        ff = jnp.concatenate(parts, axis=0)
        h1g = h1_ref[rows, :]
        issue_group(offn_ref, 1 - slot, g)
        ff3 = ff.reshape(GT, ROW, LANES)
        ss = jnp.sum(jnp.sum(ff3 * ff3, axis=2, keepdims=True), axis=1, keepdims=True)
        rs = lax.rsqrt(ss * (1.0 / D) + EPS)
        out = h1g.reshape(GT, ROW, LANES) + g2 * (ff3 * rs * pg)
        rbuf[rows, :] = out.reshape(GR, LANES)
        return carry

    lax.fori_loop(0, ET // GT, group, 0)
    for cc in range(ROW):
        o_ref[:, cc * LANES:(cc + 1) * LANES] = rbuf[pl.ds(cc, ET, stride=ROW), :]

    @pl.when(j + 1 == pl.num_programs(0))
    def _():
        for kk in range(TOPK):
            pltpu.make_async_copy(ys_hbm.at[pl.ds(0, ET * ROW), :], gbuf.at[1 - slot, kk], sem.at[1 - slot]).wait()


def _combine(offs, gates, h1_rows, mod4, pg, ys):
    nt = N // ET
    per_b = T // ET
    return pl.pallas_call(
        _combine_kernel,
        grid=(nt,),
        in_specs=[pl.BlockSpec((ROW, ET), lambda j: (0, j), memory_space=pltpu.SMEM),
                  pl.BlockSpec((ROW, ET), lambda j: (0, jnp.minimum(j + 1, nt - 1)), memory_space=pltpu.SMEM),
                  pl.BlockSpec((ET, LANES), lambda j: (j, 0)),
                  pl.BlockSpec((ET * ROW, LANES), lambda j: (j, 0)),
                  pl.BlockSpec((None, None, ROW, LANES), lambda j: (j // per_b, 5, 0, 0)),
                  pl.BlockSpec((ROW, LANES), lambda j: (0, 0)),
                  pl.BlockSpec(memory_space=pl.ANY)],
        out_specs=pl.BlockSpec((ET, D), lambda j: (j, 0)),
        out_shape=jax.ShapeDtypeStruct((N, D), f32),
        scratch_shapes=[pltpu.VMEM((2, TOPK, ET * ROW, LANES), f32), pltpu.VMEM((TOPK, ET, LANES), f32),
                        pltpu.VMEM((ET * ROW, LANES), f32), pltpu.SemaphoreType.DMA((2,))],
        compiler_params=pltpu.CompilerParams(dimension_semantics=("arbitrary",), vmem_limit_bytes=VMEM_LIMIT),
        name="combine",
    )(offs, offs, gates, h1_rows, mod4, pg, ys)


def _pad_heads(w):
    lead = w.shape[:-1]
    w4 = w.reshape(lead + (GLA_H, GLA_DK))
    w4 = jnp.concatenate([w4, jnp.zeros_like(w4)], axis=-1)
    return w4.reshape(lead + (GLA_H * LANES,))


def kernel(x, c, ada_w, ada_b, mix_pre_g, mix_post_g, ffn_pre_g, ffn_post_g, w_in, w_alpha, b_alpha, gla_norm_g, s5_lambda_re, s5_lambda_im, s5_log_dt, s5_b_re, s5_b_im, s5_c_re, s5_c_im, s5_d, s5_glu_w, s5_glu_b, w_out, router_w, router_b, exp_w_gu, exp_b_gu, exp_w_down, exp_b_down):
    l = 0
    mod = _ada(c, ada_w[l], ada_b[l])
    mod3 = mod.reshape(B, 1, 6 * D)
    mod4 = mod.reshape(B, 6, ROW, LANES)

    w = w_in[l]
    o_q, o_k, o_v, o_g, o_a, o_u = 0, 256, 512, 1024, 1536, 1552
    wq = _pad_heads(w[:, o_q:o_k]).astype(bf16)
    wk = _pad_heads(w[:, o_k:o_v]).astype(bf16)
    wv = w[:, o_v:o_g].astype(bf16)
    wg = w[:, o_g:o_a].astype(bf16)
    wa = jnp.pad(w[:, o_a:o_u], ((0, 0), (0, LANES - LOWRANK))).astype(bf16)
    wu = w[:, o_u:].astype(bf16)
    walpha = jnp.pad(_pad_heads(w_alpha[l]), ((0, LANES - LOWRANK), (0, 0))).astype(bf16)
    balpha = _pad_heads(b_alpha[l]).reshape(1, GLA_H * LANES)

    gla, u = _mixer_in(x, mod3, mix_pre_g[l].reshape(1, D), wq, wk, wv, wg, wa, wu,
                       walpha, balpha, gla_norm_g[l].reshape(1, GLA_DV))

    wt, mi, mo, a16 = _s5_prep(s5_lambda_re[l], s5_lambda_im[l], s5_log_dt[l], s5_b_re[l], s5_b_im[l],
                               s5_c_re[l], s5_c_im[l], s5_d[l])
    y = _s5_scan(u, wt, mi, mo, a16)

    rw = jnp.pad(router_w[l], ((0, 0), (0, LANES - NE)))
    rw_hi = rw.astype(bf16)
    rw_lo = (rw - rw_hi.astype(f32)).astype(bf16)
    rb = jnp.pad(router_b[l], (0, LANES - NE), constant_values=-1e30).reshape(1, LANES)
    wo = w_out[l].astype(bf16)
    h1_rows, hn2_rows, offs, gates, cnt = _mixer_out(
        x, gla, y, mod3, s5_glu_w[l].astype(bf16), s5_glu_b[l].reshape(1, 512), wo[:512], wo[512:],
        mix_post_g[l].reshape(1, D), ffn_pre_g[l].reshape(1, D), rw_hi, rw_lo, rb)

    counts = cnt[0, :NE].astype(i32)
    step_rows = EB * MB
    ntile = (counts + step_rows - 1) // step_rows
    tends = jnp.cumsum(ntile)
    nused = tends[-1]
    gi = jnp.minimum(jnp.arange(NB, dtype=i32), nused - 1)
    tile_e = jnp.sum((tends[None, :] <= gi[:, None]).astype(i32), axis=1)
    sel = tile_e[:, None] == jnp.arange(NE, dtype=i32)[None, :]
    pick = lambda v: jnp.sum(jnp.where(sel, v[None, :], 0), axis=1)
    tile_j = gi - pick(tends - ntile)
    tile_blk = tile_e * TPE + tile_j
    tile_nv = jnp.clip(pick(counts) - tile_j * step_rows, 0, step_rows)

    xs = _dispatch(offs, hn2_rows)
    ys = _experts(tile_blk, tile_e, tile_nv, nused.reshape(1), xs,
                  exp_w_gu[l], exp_b_gu[l], exp_w_down[l], exp_b_down[l])
    out = _combine(offs, gates.reshape(N, LANES), h1_rows, mod4, ffn_post_g[l].reshape(ROW, LANES), ys)
    return out.reshape(B, T, D)
```

```python
import functools
import math

import jax
import jax.numpy as jnp
from jax import lax
from jax.experimental import pallas as pl
from jax.experimental.pallas import tpu as pltpu

f32 = jnp.float32
bf16 = jnp.bfloat16
i32 = jnp.int32

D = 1024
B = 8
T = 2048
N = B * T
GLA_H = 4
GLA_DV = 128
GLA_DK = 64
GLA_TAU = 16.0
GLA_CHUNK = 64
LOWRANK = 16
S5_W = 512
S5_CH = 16
S5_G = 32
S5_P = 64
NE = 32
TOPK = 4
DFF = 1024
ALPHA = 1.702
LIMIT = 7.0
EPS = 1e-6

LANES = 128
SUBLANES = 8
VMEM_LIMIT = 56 * 1024 * 1024

TT = 512
S5_L = 16
S5_M = S5_W // LANES
S5_GPT = LANES // S5_CH
S5_ST = S5_GPT * S5_P
S5_NCH = 32
S5_ROWS = B * S5_NCH
ROW = SUBLANES
MB = 256
EB = 2
NB = (N * TOPK) // (EB * MB) + NE
TPE = N // (EB * MB)
DMA_UNROLL = 8
COMBINE_GROUP = 16


def _dot(a, b):
    return jnp.dot(a, b, preferred_element_type=f32)


def _dot_t(a, b, ca, cb):
    return lax.dot_general(a, b, (((ca,), (cb,)), ((), ())), preferred_element_type=f32)


def _rms(x):
    return lax.rsqrt(jnp.mean(x * x, axis=-1, keepdims=True) + EPS)


def _ada_kernel(c_ref, w_ref, b_ref, o_ref):
    c = c_ref[...]
    s = (c * jax.nn.sigmoid(c)).astype(bf16)
    o_ref[...] = _dot(s, w_ref[...].astype(bf16)) + b_ref[...]


def _ada(c, w, b):
    return pl.pallas_call(
        _ada_kernel,
        grid=(6,),
        in_specs=[
            pl.BlockSpec((B, D), lambda j: (0, 0)),
            pl.BlockSpec((D, D), lambda j: (0, j)),
            pl.BlockSpec((1, D), lambda j: (0, j)),
        ],
        out_specs=pl.BlockSpec((B, D), lambda j: (0, j)),
        out_shape=jax.ShapeDtypeStruct((B, 6 * D), f32),
        name="ada",
    )(c, w, b.reshape(1, 6 * D))


def _mixer_in_kernel(x_ref, sc_ref, sh_ref, g_ref, wq, wk, wv, wg, wa, wu, walpha, balpha, gng,
                     gla_ref, u_ref, st_s, u_s):
    t = pl.program_id(1)

    @pl.when(t == 0)
    def _():
        st_s[...] = jnp.zeros_like(st_s)

    x = x_ref[...]
    hn = (x * _rms(x) * g_ref[...]) * (1.0 + sc_ref[...]) + sh_ref[...]
    hb = hn.astype(bf16)
    u = _dot(hb, wu[...])
    for m in range(S5_M):
        u_s[m] = u[:, m * LANES:(m + 1) * LANES]
        for i in range(S5_L):
            u_ref[m, :, i * LANES:(i + 1) * LANES] = u_s[m, pl.ds(i, TT // S5_L, stride=S5_L), :].astype(bf16)

    C = GLA_CHUNK
    NC = TT // C
    row = lax.broadcasted_iota(i32, (TT, TT), 0)
    col = lax.broadcasted_iota(i32, (TT, TT), 1)
    causal = jnp.logical_and(row >= col, (row // C) == (col // C))
    tri = causal.astype(bf16)
    a_lr = _dot(hb, wa[...]).astype(bf16)
    la = jax.nn.log_sigmoid(_dot(a_lr, walpha[...]) + balpha[...]) * (1.0 / GLA_TAU)
    la_hi = la.astype(bf16)
    la_lo = (la - la_hi.astype(f32)).astype(bf16)
    bc = _dot(tri, la_hi) + _dot(tri, la_lo)
    bl = jnp.broadcast_to(bc.reshape(NC, C, GLA_H * LANES)[:, C - 1:C, :],
                          (NC, C, GLA_H * LANES)).reshape(TT, GLA_H * LANES)
    q = _dot(hb, wq[...]) * (GLA_DK ** -0.5)
    k = _dot(hb, wk[...])
    qd = (q * jnp.exp(bc)).astype(bf16)
    ki = (k * jnp.exp(-bc)).astype(bf16)
    ke = (k * jnp.exp(bl - bc)).astype(bf16)
    dec = jnp.exp(bl)
    vb = _dot(hb, wv[...]).astype(bf16)
    og = _dot(hb, wg[...])
    gn = gng[...]
    for h in range(GLA_H):
        sl = slice(h * LANES, (h + 1) * LANES)
        qh, kih, keh, vh = qd[:, sl], ki[:, sl], ke[:, sl], vb[:, sl]
        sc = jnp.where(causal, _dot_t(qh, kih, 1, 1), 0.0).astype(bf16)
        o_intra = _dot(sc, vh)
        st = st_s[h]
        o_parts = []
        for c in range(NC):
            rs = slice(c * C, (c + 1) * C)
            o_parts.append(_dot_t(qh[rs], st.astype(bf16), 1, 1))
            st = st * dec[c * C:c * C + 1, sl] + _dot_t(vh[rs], keh[rs], 0, 0)
        st_s[h] = st
        o = o_intra + jnp.concatenate(o_parts, axis=0)
        on = o * _rms(o) * gn
        ogh = og[:, sl]
        gla_ref[:, sl] = (on * (ogh * jax.nn.sigmoid(ogh))).astype(bf16)


def _mixer_in(x, mod3, g, wq, wk, wv, wg, wa, wu, walpha, balpha, gng):
    nt = T // TT
    full = lambda shape: pl.BlockSpec(shape, lambda b, t: (0,) * len(shape))
    u_shape = jax.ShapeDtypeStruct((S5_M, B, T // S5_L, S5_L * LANES), bf16)
    u_spec = pl.BlockSpec((S5_M, None, TT // S5_L, S5_L * LANES), lambda b, t: (0, b, t, 0))
    return pl.pallas_call(
        _mixer_in_kernel,
        grid=(B, nt),
        in_specs=[
            pl.BlockSpec((None, TT, D), lambda b, t: (b, t, 0)),
            pl.BlockSpec((None, 1, D), lambda b, t: (b, 0, 1)),
            pl.BlockSpec((None, 1, D), lambda b, t: (b, 0, 0)),
            full((1, D)),
            full((D, 512)), full((D, 512)), full((D, 512)), full((D, 512)),
            full((D, LANES)), full((D, 512)),
            full((LANES, 512)), full((1, 512)), full((1, LANES)),
        ],
        out_specs=[pl.BlockSpec((None, TT, 512), lambda b, t: (b, t, 0)), u_spec],
        out_shape=[jax.ShapeDtypeStruct((B, T, 512), bf16), u_shape],
        scratch_shapes=[pltpu.VMEM((GLA_H, GLA_DV, LANES), f32), pltpu.VMEM((S5_M, TT, LANES), f32)],
        compiler_params=pltpu.CompilerParams(
            dimension_semantics=("arbitrary", "arbitrary"), vmem_limit_bytes=VMEM_LIMIT),
        name="mixer_in",
    )(x, mod3, mod3, g, wq, wk, wv, wg, wa, wu, walpha, balpha, gng)


def _cpow(xr, xi, d):
    mag = jnp.exp(xr * d)
    return mag * jnp.cos(xi * d), mag * jnp.sin(xi * d)


def _s5_prep_kernel(lr_r, li_r, ld_r, lr_c, li_c, ld_c, btr, bti, ctr, cti, d_r,
                    wt_ref, mi_ref, mo_ref, a_ref, kt_s):
    L = S5_L
    lr = lr_r[...]
    li = li_r[...]
    dt = jnp.exp(ld_r[...])
    xr, xi = lr * dt, li * dt
    ar, ai = _cpow(xr, xi, 1.0)
    den = lr * lr + li * li
    fr = ((ar - 1.0) * lr + ai * li) / den
    fi = (ai * lr - (ar - 1.0) * li) / den
    br, bi = btr[...], bti[...]
    bbr = fr * br - fi * bi
    bbi = fr * bi + fi * br
    cr, ci = ctr[...], cti[...]
    pr, pi = jnp.ones_like(ar), jnp.zeros_like(ai)
    for d in range(L):
        xdr = bbr * pr - bbi * pi
        xdi = bbr * pi + bbi * pr
        i = L - 1 - d
        mo_ref[i * LANES:(i + 1) * LANES, :S5_ST] = xdr.astype(bf16)
        mo_ref[i * LANES:(i + 1) * LANES, S5_ST:] = xdi.astype(bf16)
        kt = (jnp.dot(xdr, cr, preferred_element_type=f32, precision=lax.Precision.HIGHEST)
              - jnp.dot(xdi, ci, preferred_element_type=f32, precision=lax.Precision.HIGHEST))
        if d == 0:
            r = lax.broadcasted_iota(i32, (LANES, LANES), 0)
            c = lax.broadcasted_iota(i32, (LANES, LANES), 1)
            kt = kt + jnp.where(r == c, d_r[...], 0.0)
        kt_s[d] = kt.astype(bf16)
        pr, pi = pr * ar - pi * ai, pr * ai + pi * ar
    a_ref[:, :S5_ST] = pr
    a_ref[:, S5_ST:] = pi
    zero = jnp.zeros((LANES, LANES), bf16)
    for i in range(L):
        for j in range(L):
            wt_ref[i * LANES:(i + 1) * LANES, j * LANES:(j + 1) * LANES] = kt_s[j - i] if j >= i else zero
    lrc = lr_c[...]
    lic = li_c[...]
    dtc = jnp.exp(ld_c[...])
    acr, aci = _cpow(lrc * dtc, lic * dtc, 1.0)
    pr, pi = acr, aci
    for j in range(L):
        mi_ref[:S5_ST, j * LANES:(j + 1) * LANES] = (cr * pr - ci * pi).astype(bf16)
        mi_ref[S5_ST:, j * LANES:(j + 1) * LANES] = (-(cr * pi + ci * pr)).astype(bf16)
        pr, pi = pr * acr - pi * aci, pr * aci + pi * acr


def _s5_prep(lr, li, ld, b_re, b_im, c_re, c_im, dvec):
    G, P, H, M, GPT = S5_G, S5_P, S5_CH, S5_M, S5_GPT
    eye = jnp.eye(GPT, dtype=f32)

    def rows(v):
        return v.reshape(M, 1, S5_ST)

    def cols(v):
        return v.reshape(M, S5_ST, 1)

    ldp = jnp.broadcast_to(ld[:, None], (G, P))

    def bt(b):
        b4 = b.reshape(M, GPT, P, H)
        return jnp.einsum('mgph,gk->mkhgp', b4, eye).reshape(M, LANES, S5_ST)

    def ct(c):
        c4 = c.reshape(M, GPT, H, P)
        return jnp.einsum('mghp,gk->mgpkh', c4, eye).reshape(M, S5_ST, LANES)

    L = S5_L
    mspec = lambda shape: pl.BlockSpec((None,) + shape, lambda m: (m,) + (0,) * len(shape))
    return pl.pallas_call(
        _s5_prep_kernel,
        grid=(M,),
        in_specs=[mspec((1, S5_ST))] * 3 + [mspec((S5_ST, 1))] * 3
        + [mspec((LANES, S5_ST))] * 2 + [mspec((S5_ST, LANES))] * 2 + [mspec((1, LANES))],
        out_specs=[mspec((L * LANES, L * LANES)), mspec((2 * S5_ST, L * LANES)),
                   mspec((L * LANES, 2 * S5_ST)), mspec((1, 2 * S5_ST))],
        out_shape=[jax.ShapeDtypeStruct((M, L * LANES, L * LANES), bf16),
                   jax.ShapeDtypeStruct((M, 2 * S5_ST, L * LANES), bf16),
                   jax.ShapeDtypeStruct((M, L * LANES, 2 * S5_ST), bf16),
                   jax.ShapeDtypeStruct((M, 1, 2 * S5_ST), f32)],
        scratch_shapes=[pltpu.VMEM((L, LANES, LANES), bf16)],
        compiler_params=pltpu.CompilerParams(dimension_semantics=("arbitrary",), vmem_limit_bytes=VMEM_LIMIT),
        name="s5_prep",
    )(rows(lr), rows(li), rows(ldp), cols(lr), cols(li), cols(ldp),
      bt(b_re), bt(b_im), ct(c_re), ct(c_im), dvec.reshape(M, 1, LANES))


def _s5_scan_kernel(u_ref, wt, mi, mo, a_ref, y_ref, xs, s_s, x_s):
    r = pl.program_id(1)

    @pl.when(r == 0)
    def _():
        xs[...] = jnp.zeros_like(xs)

    u = u_ref[...].reshape(S5_ROWS, S5_L * LANES)
    s = _dot(u, mo[...])
    NT = 2 * S5_ST // LANES
    for c in range(NT):
        s_s[c] = s[:, c * LANES:(c + 1) * LANES]
    a = [a_ref[:, c * LANES:(c + 1) * LANES] for c in range(NT)]

    def step(n, x):
        rows = pl.ds(n, B, stride=S5_NCH)
        new = []
        for c in range(NT):
            x_s[c, rows, :] = x[c]
        for c in range(NT // 2):
            ar, ai, xr, xi = a[c], a[NT // 2 + c], x[c], x[NT // 2 + c]
            new.append((ar * xr - ai * xi + s_s[c, rows, :], ar * xi + ai * xr + s_s[NT // 2 + c, rows, :]))
        return tuple(p[0] for p in new) + tuple(p[1] for p in new)

    x = tuple(xs[:, c * LANES:(c + 1) * LANES] for c in range(NT))
    for n in range(S5_NCH):
        x = step(n, x)
    for c in range(NT):
        xs[:, c * LANES:(c + 1) * LANES] = x[c]
    x_in = jnp.concatenate([x_s[c] for c in range(NT)], axis=1)
    y = _dot(u, wt[...]) + _dot(x_in.astype(bf16), mi[...])
    y_ref[...] = jax.nn.gelu(y).astype(bf16).reshape(B, S5_NCH, S5_L * LANES)


def _s5_scan(u, wt, mi, mo, a16):
    L = S5_L
    wspec = lambda shape: pl.BlockSpec((None,) + shape, lambda m, r: (m,) + (0,) * len(shape))
    uspec = pl.BlockSpec((None, B, S5_NCH, L * LANES), lambda m, r: (m, 0, r, 0))
    return pl.pallas_call(
        _s5_scan_kernel,
        grid=(S5_M, T // L // S5_NCH),
        in_specs=[uspec,
                  wspec((L * LANES, L * LANES)), wspec((2 * S5_ST, L * LANES)),
                  wspec((L * LANES, 2 * S5_ST)), wspec((1, 2 * S5_ST))],
        out_specs=uspec,
        out_shape=jax.ShapeDtypeStruct((S5_M, B, T // L, L * LANES), bf16),
        scratch_shapes=[pltpu.VMEM((B, 2 * S5_ST), f32), pltpu.VMEM((2 * S5_ST // LANES, S5_ROWS, LANES), f32),
                        pltpu.VMEM((2 * S5_ST // LANES, S5_ROWS, LANES), f32)],
        compiler_params=pltpu.CompilerParams(
            dimension_semantics=("arbitrary", "arbitrary"), vmem_limit_bytes=VMEM_LIMIT),
        name="s5_scan",
    )(u, wt, mi, mo, a16)


def _mixer_out_kernel(x_ref, gla_ref, y_ref, g1_ref, sh2_ref, sc2_ref, gluw, glub, woa, wob,
                      mpg, fpg, rw_hi, rw_lo, rb, h1_ref, hn2_ref, pos_ref, tab_ref, gate_ref, cnt_ref, cnt_s, y_s):
    first = jnp.logical_and(pl.program_id(0) == 0, pl.program_id(1) == 0)

    @pl.when(first)
    def _():
        cnt_s[...] = jnp.zeros_like(cnt_s)

    for m in range(S5_M):
        for jj in range(S5_L):
            y_s[m, pl.ds(jj, TT // S5_L, stride=S5_L), :] = y_ref[m, :, jj * LANES:(jj + 1) * LANES].astype(f32)
    y = jnp.concatenate([y_s[m] for m in range(S5_M)], axis=1)
    z = _dot(y.astype(bf16), gluw[...]) + glub[...]
    s5 = y * jax.nn.sigmoid(z)
    mix = _dot(gla_ref[...], woa[...]) + _dot(s5.astype(bf16), wob[...])
    h1 = x_ref[...] + g1_ref[...] * (mix * _rms(mix) * mpg[...])
    hn2 = (h1 * _rms(h1) * fpg[...]) * (1.0 + sc2_ref[...]) + sh2_ref[...]
    for cc in range(ROW):
        h1_ref[pl.ds(cc, TT, stride=ROW), :] = h1[:, cc * LANES:(cc + 1) * LANES]
        hn2_ref[pl.ds(cc, TT, stride=ROW), :] = hn2[:, cc * LANES:(cc + 1) * LANES]
    x_hi = hn2.astype(bf16)
    x_lo = (hn2 - x_hi.astype(f32)).astype(bf16)
    logits = _dot(x_hi, rw_hi[...]) + (_dot(x_lo, rw_hi[...]) + _dot(x_hi, rw_lo[...])) + rb[...]

    lane = lax.broadcasted_iota(i32, (TT, LANES), 1).astype(f32)
    l = logits
    vals, idxs = [], []
    for _ in range(TOPK):
        m = jnp.max(l, axis=-1, keepdims=True)
        ix = jnp.min(jnp.where(l == m, lane, float(LANES)), axis=-1, keepdims=True)
        vals.append(m)
        idxs.append(ix)
        l = jnp.where(lane == ix, -jnp.inf, l)
    es = [jnp.exp(v - vals[0]) for v in vals]
    tot = es[0] + es[1] + es[2] + es[3]
    oh = jnp.zeros((TT, LANES), f32)
    for ix in idxs:
        oh = oh + (lane == ix).astype(f32)
    r = lax.broadcasted_iota(i32, (TT, TT), 0)
    c = lax.broadcasted_iota(i32, (TT, TT), 1)
    cum = _dot((r >= c).astype(bf16), oh.astype(bf16))
    lcnt = cum[TT - 1:TT, :]
    gbase = cnt_s[...]
    hi = jnp.floor(lcnt * (1.0 / 64.0))
    lo = lcnt - hi * 64.0
    upper = (lax.broadcasted_iota(i32, (LANES, LANES), 0) < lax.broadcasted_iota(i32, (LANES, LANES), 1)).astype(bf16)
    hi8 = jnp.broadcast_to(hi, (ROW, LANES)).astype(bf16)
    lo8 = jnp.broadcast_to(lo, (ROW, LANES)).astype(bf16)
    lbase = (64.0 * _dot(hi8, upper) + _dot(lo8, upper))[0:1, :]
    loc = lbase + (cum - oh)
    posv = jnp.zeros((TT, LANES), f32)
    gate = jnp.zeros((TT, LANES), f32)
    for kk in range(TOPK):
        p = jnp.sum(jnp.where(lane == idxs[kk], loc, 0.0), axis=-1, keepdims=True)
        posv = jnp.where(lane == float(kk), p, posv)
        gate = jnp.where(lane == float(kk), es[kk] / tot, gate)
    pos_ref[...] = jnp.transpose(posv)[:ROW, :].astype(i32)
    sub = lax.broadcasted_iota(i32, (ROW, LANES), 0)
    tab = jnp.where(sub == 0, lcnt, jnp.where(sub == 1, lbase, jnp.where(sub == 2, gbase, 0.0)))
    tab_ref[...] = tab.astype(i32)
    gate_ref[...] = gate
    cnt = cnt_s[...] + cum[TT - 1:TT, :]
    cnt_s[...] = cnt
    cnt_ref[...] = cnt


def _mixer_out(x, gla, y, mod3, gluw, glub, woa, wob, mpg, fpg, rw_hi, rw_lo, rb):
    nt = T // TT
    full = lambda shape: pl.BlockSpec(shape, lambda b, t: (0,) * len(shape))
    tok = lambda w: pl.BlockSpec((None, TT, w), lambda b, t: (b, t, 0))
    slab = pl.BlockSpec((TT * ROW, LANES), lambda b, t: (b * nt + t, 0))
    slab_shape = jax.ShapeDtypeStruct((N * ROW, LANES), f32)
    yspec = pl.BlockSpec((S5_M, None, TT // S5_L, S5_L * LANES), lambda b, t: (0, b, t, 0))
    modspec = lambda j: pl.BlockSpec((None, 1, D), lambda b, t: (b, 0, j))
    return pl.pallas_call(
        _mixer_out_kernel,
        grid=(B, nt),
        in_specs=[tok(D), tok(512), yspec,
                  modspec(2), modspec(3), modspec(4),
                  full((512, 512)), full((1, 512)), full((512, D)), full((512, D)),
                  full((1, D)), full((1, D)), full((D, LANES)), full((D, LANES)), full((1, LANES))],
        out_specs=[slab, slab, pl.BlockSpec((ROW, TT), lambda b, t: (0, b * nt + t)),
                   pl.BlockSpec((None, ROW, LANES), lambda b, t: (b * nt + t, 0, 0)), tok(LANES), full((1, LANES))],
        out_shape=[slab_shape, slab_shape,
                   jax.ShapeDtypeStruct((ROW, N), i32), jax.ShapeDtypeStruct((N // TT, ROW, LANES), i32),
                   jax.ShapeDtypeStruct((B, T, LANES), f32),
                   jax.ShapeDtypeStruct((1, LANES), f32)],
        scratch_shapes=[pltpu.VMEM((1, LANES), f32), pltpu.VMEM((S5_M, TT, LANES), f32)],
        compiler_params=pltpu.CompilerParams(
            dimension_semantics=("arbitrary", "arbitrary"), vmem_limit_bytes=VMEM_LIMIT),
        name="mixer_out",
    )(x, gla, y, mod3, mod3, mod3, gluw, glub, woa, wob, mpg, fpg, rw_hi, rw_lo, rb)


STAGE_ROWS = TT * TOPK * ROW


def _strip_copies(tab_ref, stage, region_hbm, sem, to_region):
    def expert(e, carry):
        n = tab_ref[0, e] * ROW

        @pl.when(n > 0)
        def _():
            local = stage.at[pl.ds(pl.multiple_of(tab_ref[1, e] * ROW, ROW), n), :]
            remote = region_hbm.at[pl.ds(pl.multiple_of((e * N + tab_ref[2, e]) * ROW, ROW), n), :]
            src, dst = (local, remote) if to_region else (remote, local)
            pltpu.make_async_copy(src, dst, sem).start()
        return carry
    lax.fori_loop(0, NE, expert, 0)


def _strip_wait(stage, region_hbm, sem):
    pltpu.make_async_copy(stage, region_hbm.at[pl.ds(0, STAGE_ROWS), :], sem).wait()


def _dispatch_kernel(pos_ref, tab_ref, x_ref, xs_hbm, stage, sem):
    i = pl.program_id(0)
    slot = lax.rem(i, 2)

    @pl.when(i >= 2)
    def _():
        _strip_wait(stage.at[slot], xs_hbm, sem.at[slot])

    def group(g, carry):
        for jj in range(DMA_UNROLL):
            t = g * DMA_UNROLL + jj
            v = x_ref[pl.ds(pl.multiple_of(t * ROW, ROW), ROW), :]
            for kk in range(TOPK):
                stage[slot, pl.ds(pl.multiple_of(pos_ref[kk, t] * ROW, ROW), ROW), :] = v
        return carry

    lax.fori_loop(0, TT // DMA_UNROLL, group, 0)
    _strip_copies(tab_ref, stage.at[slot], xs_hbm, sem.at[slot], to_region=True)

    @pl.when(i == pl.num_programs(0) - 1)
    def _():
        _strip_wait(stage.at[1 - slot], xs_hbm, sem.at[1 - slot])
        _strip_wait(stage.at[slot], xs_hbm, sem.at[slot])


def _dispatch(pos, tab, hn2_rows):
    return pl.pallas_call(
        _dispatch_kernel,
        grid=(N // TT,),
        in_specs=[pl.BlockSpec((ROW, TT), lambda i: (0, i), memory_space=pltpu.SMEM),
                  pl.BlockSpec((None, ROW, LANES), lambda i: (i, 0, 0), memory_space=pltpu.SMEM),
                  pl.BlockSpec((TT * ROW, LANES), lambda i: (i, 0))],
        out_specs=pl.BlockSpec(memory_space=pl.ANY),
        out_shape=jax.ShapeDtypeStruct((NE * N * ROW, LANES), f32),
        scratch_shapes=[pltpu.VMEM((2, STAGE_ROWS, LANES), f32), pltpu.SemaphoreType.DMA((2,))],
        compiler_params=pltpu.CompilerParams(dimension_semantics=("arbitrary",), vmem_limit_bytes=VMEM_LIMIT),
        name="dispatch",
    )(pos, tab, hn2_rows)


def _experts_kernel(tb_ref, te_ref, tv_ref, nu_ref, x_ref, wgu_ref, bgu_ref, wd_ref, bd_ref, y_ref, wgu_b, wd_b):
    i = pl.program_id(0)

    @pl.when(i < nu_ref[0])
    def _():
        changed = jnp.logical_or(i == 0, te_ref[i] != te_ref[jnp.maximum(i - 1, 0)])

        @pl.when(changed)
        def _():
            def cast(j, carry):
                r0 = pl.multiple_of(j * LANES, LANES)
                wgu_b[pl.ds(r0, LANES), :] = wgu_ref[pl.ds(r0, LANES), :].astype(bf16)
                wd_b[pl.ds(r0, LANES), :] = wd_ref[pl.ds(r0, LANES), :].astype(bf16)
                return carry
            lax.fori_loop(0, D // LANES, cast, 0)

        def tile(h):
            base = h * MB * ROW
            xt = jnp.concatenate([x_ref[pl.ds(base + cc, MB, stride=ROW), :] for cc in range(ROW)], axis=1)
            valid = lax.broadcasted_iota(i32, (MB, 1), 0) < tv_ref[i] - h * MB
            xt = jnp.where(valid, xt, 0.0).astype(bf16)
            gu = _dot(xt, wgu_b[...]) + bgu_ref[...]
            g = jnp.minimum(gu[:, :DFF], LIMIT)
            up = jnp.clip(gu[:, DFF:], -LIMIT, LIMIT)
            act = (up + 1.0) * (g * jax.nn.sigmoid(ALPHA * g))
            yv = _dot(act.astype(bf16), wd_b[...]) + bd_ref[...]
            for cc in range(ROW):
                y_ref[pl.ds(base + cc, MB, stride=ROW), :] = yv[:, cc * LANES:(cc + 1) * LANES]

        tile(0)
        for h in range(1, EB):
            pl.when(tv_ref[i] > h * MB)(functools.partial(tile, h))


def _experts(tile_blk, tile_e, tile_nv, nused, xs, w_gu, b_gu, w_down, b_down):
    blk = lambda i, tb, te, tv, nu: (tb[i], 0)
    wsel = lambda i, tb, te, tv, nu: (te[i], 0, 0)
    grid_spec = pltpu.PrefetchScalarGridSpec(
        num_scalar_prefetch=4,
        grid=(NB,),
        in_specs=[
            pl.BlockSpec((EB * MB * ROW, LANES), blk),
            pl.BlockSpec((None, D, 2 * DFF), wsel),
            pl.BlockSpec((None, 1, 2 * DFF), wsel),
            pl.BlockSpec((None, DFF, D), wsel),
            pl.BlockSpec((None, 1, D), wsel),
        ],
        out_specs=pl.BlockSpec((EB * MB * ROW, LANES), blk),
        scratch_shapes=[pltpu.VMEM((D, 2 * DFF), bf16), pltpu.VMEM((DFF, D), bf16)],
    )
    return pl.pallas_call(
        _experts_kernel,
        grid_spec=grid_spec,
        out_shape=jax.ShapeDtypeStruct((NE * N * ROW, LANES), f32),
        compiler_params=pltpu.CompilerParams(dimension_semantics=("arbitrary",), vmem_limit_bytes=VMEM_LIMIT),
        name="experts",
    )(tile_blk, tile_e, tile_nv, nused, xs, w_gu, b_gu.reshape(NE, 1, 2 * DFF), w_down, b_down.reshape(NE, 1, D))


def _combine_kernel(pos_ref, tab_ref, tabn_ref, gate_ref, h1_ref, g2_ref, pg_ref, ys_hbm, o_ref,
                    stage, wbuf, rbuf, sem):
    j = pl.program_id(0)
    slot = lax.rem(j, 2)
    GT = COMBINE_GROUP
    GR = GT * ROW

    @pl.when(j == 0)
    def _():
        _strip_copies(tab_ref, stage.at[0], ys_hbm, sem.at[0], to_region=False)

    @pl.when(j + 1 < pl.num_programs(0))
    def _():
        _strip_copies(tabn_ref, stage.at[1 - slot], ys_hbm, sem.at[1 - slot], to_region=False)

    for kk in range(TOPK):
        wbuf[kk] = jnp.broadcast_to(gate_ref[:, kk:kk + 1], (TT, LANES))
    _strip_wait(stage.at[slot], ys_hbm, sem.at[slot])
    g2 = g2_ref[...]
    pg = pg_ref[...]

    def group(g, carry):
        rows = pl.ds(pl.multiple_of(g * GR, GR), GR)
        parts = []
        for jj in range(GT):
            t = g * GT + jj
            acc = None
            for kk in range(TOPK):
                w = jnp.broadcast_to(wbuf[kk, pl.ds(t, 1), :], (ROW, LANES))
                term = stage[slot, pl.ds(pl.multiple_of(pos_ref[kk, t] * ROW, ROW), ROW), :] * w
                acc = term if acc is None else acc + term
            parts.append(acc)
        ff3 = jnp.concatenate(parts, axis=0).reshape(GT, ROW, LANES)
        ss = jnp.sum(jnp.sum(ff3 * ff3, axis=2, keepdims=True), axis=1, keepdims=True)
        rs = lax.rsqrt(ss * (1.0 / D) + EPS)
        out = h1_ref[rows, :].reshape(GT, ROW, LANES) + g2 * (ff3 * rs * pg)
        rbuf[rows, :] = out.reshape(GR, LANES)
        return carry

    lax.fori_loop(0, TT // GT, group, 0)
    for cc in range(ROW):
        o_ref[:, cc * LANES:(cc + 1) * LANES] = rbuf[pl.ds(cc, TT, stride=ROW), :]


def _combine(pos, tab, gates, h1_rows, mod4, pg, ys):
    nt = N // TT
    per_b = T // TT
    tabspec = lambda imap: pl.BlockSpec((None, ROW, LANES), imap, memory_space=pltpu.SMEM)
    return pl.pallas_call(
        _combine_kernel,
        grid=(nt,),
        in_specs=[pl.BlockSpec((ROW, TT), lambda j: (0, j), memory_space=pltpu.SMEM),
                  tabspec(lambda j: (j, 0, 0)), tabspec(lambda j: (jnp.minimum(j + 1, nt - 1), 0, 0)),
                  pl.BlockSpec((TT, LANES), lambda j: (j, 0)),
                  pl.BlockSpec((TT * ROW, LANES), lambda j: (j, 0)),
                  pl.BlockSpec((None, None, ROW, LANES), lambda j: (j // per_b, 5, 0, 0)),
                  pl.BlockSpec((ROW, LANES), lambda j: (0, 0)),
                  pl.BlockSpec(memory_space=pl.ANY)],
        out_specs=pl.BlockSpec((TT, D), lambda j: (j, 0)),
        out_shape=jax.ShapeDtypeStruct((N, D), f32),
        scratch_shapes=[pltpu.VMEM((2, STAGE_ROWS, LANES), f32), pltpu.VMEM((TOPK, TT, LANES), f32),
                        pltpu.VMEM((TT * ROW, LANES), f32), pltpu.SemaphoreType.DMA((2,))],
        compiler_params=pltpu.CompilerParams(dimension_semantics=("arbitrary",), vmem_limit_bytes=VMEM_LIMIT),
        name="combine",
    )(pos, tab, tab, gates, h1_rows, mod4, pg, ys)


def _pad_heads(w):
    lead = w.shape[:-1]
    w4 = w.reshape(lead + (GLA_H, GLA_DK))
    w4 = jnp.concatenate([w4, jnp.zeros_like(w4)], axis=-1)
    return w4.reshape(lead + (GLA_H * LANES,))


def kernel(x, c, ada_w, ada_b, mix_pre_g, mix_post_g, ffn_pre_g, ffn_post_g, w_in, w_alpha, b_alpha, gla_norm_g, s5_lambda_re, s5_lambda_im, s5_log_dt, s5_b_re, s5_b_im, s5_c_re, s5_c_im, s5_d, s5_glu_w, s5_glu_b, w_out, router_w, router_b, exp_w_gu, exp_b_gu, exp_w_down, exp_b_down):
    l = 0
    mod = _ada(c, ada_w[l], ada_b[l])
    mod3 = mod.reshape(B, 1, 6 * D)
    mod4 = mod.reshape(B, 6, ROW, LANES)

    w = w_in[l]
    o_q, o_k, o_v, o_g, o_a, o_u = 0, 256, 512, 1024, 1536, 1552
    wq = _pad_heads(w[:, o_q:o_k]).astype(bf16)
    wk = _pad_heads(w[:, o_k:o_v]).astype(bf16)
    wv = w[:, o_v:o_g].astype(bf16)
    wg = w[:, o_g:o_a].astype(bf16)
    wa = jnp.pad(w[:, o_a:o_u], ((0, 0), (0, LANES - LOWRANK))).astype(bf16)
    wu = w[:, o_u:].astype(bf16)
    walpha = jnp.pad(_pad_heads(w_alpha[l]), ((0, LANES - LOWRANK), (0, 0))).astype(bf16)
    balpha = _pad_heads(b_alpha[l]).reshape(1, GLA_H * LANES)

    gla, u = _mixer_in(x, mod3, mix_pre_g[l].reshape(1, D), wq, wk, wv, wg, wa, wu,
                       walpha, balpha, gla_norm_g[l].reshape(1, GLA_DV))

    wt, mi, mo, a16 = _s5_prep(s5_lambda_re[l], s5_lambda_im[l], s5_log_dt[l], s5_b_re[l], s5_b_im[l],
                               s5_c_re[l], s5_c_im[l], s5_d[l])
    y = _s5_scan(u, wt, mi, mo, a16)

    rw = jnp.pad(router_w[l], ((0, 0), (0, LANES - NE)))
    rw_hi = rw.astype(bf16)
    rw_lo = (rw - rw_hi.astype(f32)).astype(bf16)
    rb = jnp.pad(router_b[l], (0, LANES - NE), constant_values=-1e30).reshape(1, LANES)
    wo = w_out[l].astype(bf16)
    h1_rows, hn2_rows, pos, tab, gates, cnt = _mixer_out(
        x, gla, y, mod3, s5_glu_w[l].astype(bf16), s5_glu_b[l].reshape(1, 512), wo[:512], wo[512:],
        mix_post_g[l].reshape(1, D), ffn_pre_g[l].reshape(1, D), rw_hi, rw_lo, rb)

    counts = cnt[0, :NE].astype(i32)
    step_rows = EB * MB
    ntile = (counts + step_rows - 1) // step_rows
    tends = jnp.cumsum(ntile)
    nused = tends[-1]
    gi = jnp.minimum(jnp.arange(NB, dtype=i32), nused - 1)
    tile_e = jnp.sum((tends[None, :] <= gi[:, None]).astype(i32), axis=1)
    sel = tile_e[:, None] == jnp.arange(NE, dtype=i32)[None, :]
    pick = lambda v: jnp.sum(jnp.where(sel, v[None, :], 0), axis=1)
    tile_j = gi - pick(tends - ntile)
    tile_blk = tile_e * TPE + tile_j
    tile_nv = jnp.clip(pick(counts) - tile_j * step_rows, 0, step_rows)

    xs = _dispatch(pos, tab, hn2_rows)
    ys = _experts(tile_blk, tile_e, tile_nv, nused.reshape(1), xs,
                  exp_w_gu[l], exp_b_gu[l], exp_w_down[l], exp_b_down[l])
    out = _combine(pos, tab, gates.reshape(N, LANES), h1_rows, mod4, ffn_post_g[l].reshape(ROW, LANES), ys)
    return out.reshape(B, T, D)
```

```python
import functools
import math

import jax
import jax.numpy as jnp
from jax import lax
from jax.experimental import pallas as pl
from jax.experimental.pallas import tpu as pltpu

f32 = jnp.float32
bf16 = jnp.bfloat16
i32 = jnp.int32

D = 1024
B = 8
T = 2048
N = B * T
GLA_H = 4
GLA_DV = 128
GLA_DK = 64
GLA_TAU = 16.0
GLA_CHUNK = 64
LOWRANK = 16
S5_W = 512
S5_CH = 16
S5_G = 32
S5_P = 64
NE = 32
TOPK = 4
DFF = 1024
ALPHA = 1.702
LIMIT = 7.0
EPS = 1e-6

LANES = 128
SUBLANES = 8
VMEM_LIMIT = 56 * 1024 * 1024

TT = 512
S5_L = 16
S5_M = S5_W // LANES
S5_GPT = LANES // S5_CH
S5_ST = S5_GPT * S5_P
S5_NCH = 32
S5_ROWS = B * S5_NCH
ROW = SUBLANES
MB = 256
EB = 2
NB = (N * TOPK) // (EB * MB) + NE
TPE = N // (EB * MB)
PG = 16


def _dot(a, b):
    return jnp.dot(a, b, preferred_element_type=f32)


def _dot_t(a, b, ca, cb):
    return lax.dot_general(a, b, (((ca,), (cb,)), ((), ())), preferred_element_type=f32)


def _rms(x):
    return lax.rsqrt(jnp.mean(x * x, axis=-1, keepdims=True) + EPS)


def _ada_kernel(c_ref, w_ref, b_ref, o_ref):
    c = c_ref[...]
    s = (c * jax.nn.sigmoid(c)).astype(bf16)
    o_ref[...] = _dot(s, w_ref[...].astype(bf16)) + b_ref[...]


def _ada(c, w, b):
    return pl.pallas_call(
        _ada_kernel,
        grid=(6,),
        in_specs=[
            pl.BlockSpec((B, D), lambda j: (0, 0)),
            pl.BlockSpec((D, D), lambda j: (0, j)),
            pl.BlockSpec((1, D), lambda j: (0, j)),
        ],
        out_specs=pl.BlockSpec((B, D), lambda j: (0, j)),
        out_shape=jax.ShapeDtypeStruct((B, 6 * D), f32),
        name="ada",
    )(c, w, b.reshape(1, 6 * D))


def _mixer_in_kernel(x_ref, sc_ref, sh_ref, g_ref, wq, wk, wv, wg, wa, wu, walpha, balpha, gng,
                     gla_ref, u_ref, st_s, u_s):
    t = pl.program_id(1)

    @pl.when(t == 0)
    def _():
        st_s[...] = jnp.zeros_like(st_s)

    x = x_ref[...]
    hn = (x * _rms(x) * g_ref[...]) * (1.0 + sc_ref[...]) + sh_ref[...]
    hb = hn.astype(bf16)
    u = _dot(hb, wu[...])
    for m in range(S5_M):
        u_s[m] = u[:, m * LANES:(m + 1) * LANES]
        for i in range(S5_L):
            u_ref[m, :, i * LANES:(i + 1) * LANES] = u_s[m, pl.ds(i, TT // S5_L, stride=S5_L), :].astype(bf16)

    C = GLA_CHUNK
    NC = TT // C
    row = lax.broadcasted_iota(i32, (TT, TT), 0)
    col = lax.broadcasted_iota(i32, (TT, TT), 1)
    causal = jnp.logical_and(row >= col, (row // C) == (col // C))
    tri = causal.astype(bf16)
    a_lr = _dot(hb, wa[...]).astype(bf16)
    la = jax.nn.log_sigmoid(_dot(a_lr, walpha[...]) + balpha[...]) * (1.0 / GLA_TAU)
    la_hi = la.astype(bf16)
    la_lo = (la - la_hi.astype(f32)).astype(bf16)
    bc = _dot(tri, la_hi) + _dot(tri, la_lo)
    bl = jnp.broadcast_to(bc.reshape(NC, C, GLA_H * LANES)[:, C - 1:C, :],
                          (NC, C, GLA_H * LANES)).reshape(TT, GLA_H * LANES)
    q = _dot(hb, wq[...]) * (GLA_DK ** -0.5)
    k = _dot(hb, wk[...])
    qd = (q * jnp.exp(bc)).astype(bf16)
    ki = (k * jnp.exp(-bc)).astype(bf16)
    ke = (k * jnp.exp(bl - bc)).astype(bf16)
    dec = jnp.exp(bl)
    vb = _dot(hb, wv[...]).astype(bf16)
    og = _dot(hb, wg[...])
    gn = gng[...]
    for h in range(GLA_H):
        sl = slice(h * LANES, (h + 1) * LANES)
        qh, kih, keh, vh = qd[:, sl], ki[:, sl], ke[:, sl], vb[:, sl]
        sc = jnp.where(causal, _dot_t(qh, kih, 1, 1), 0.0).astype(bf16)
        o_intra = _dot(sc, vh)
        st = st_s[h]
        o_parts = []
        for c in range(NC):
            rs = slice(c * C, (c + 1) * C)
            o_parts.append(_dot_t(qh[rs], st.astype(bf16), 1, 1))
            st = st * dec[c * C:c * C + 1, sl] + _dot_t(vh[rs], keh[rs], 0, 0)
        st_s[h] = st
        o = o_intra + jnp.concatenate(o_parts, axis=0)
        on = o * _rms(o) * gn
        ogh = og[:, sl]
        gla_ref[:, sl] = (on * (ogh * jax.nn.sigmoid(ogh))).astype(bf16)


def _mixer_in(x, mod3, g, wq, wk, wv, wg, wa, wu, walpha, balpha, gng):
    nt = T // TT
    full = lambda shape: pl.BlockSpec(shape, lambda b, t: (0,) * len(shape))
    u_shape = jax.ShapeDtypeStruct((S5_M, B, T // S5_L, S5_L * LANES), bf16)
    u_spec = pl.BlockSpec((S5_M, None, TT // S5_L, S5_L * LANES), lambda b, t: (0, b, t, 0))
    return pl.pallas_call(
        _mixer_in_kernel,
        grid=(B, nt),
        in_specs=[
            pl.BlockSpec((None, TT, D), lambda b, t: (b, t, 0)),
            pl.BlockSpec((None, 1, D), lambda b, t: (b, 0, 1)),
            pl.BlockSpec((None, 1, D), lambda b, t: (b, 0, 0)),
            full((1, D)),
            full((D, 512)), full((D, 512)), full((D, 512)), full((D, 512)),
            full((D, LANES)), full((D, 512)),
            full((LANES, 512)), full((1, 512)), full((1, LANES)),
        ],
        out_specs=[pl.BlockSpec((None, TT, 512), lambda b, t: (b, t, 0)), u_spec],
        out_shape=[jax.ShapeDtypeStruct((B, T, 512), bf16), u_shape],
        scratch_shapes=[pltpu.VMEM((GLA_H, GLA_DV, LANES), f32), pltpu.VMEM((S5_M, TT, LANES), f32)],
        compiler_params=pltpu.CompilerParams(
            dimension_semantics=("arbitrary", "arbitrary"), vmem_limit_bytes=VMEM_LIMIT),
        name="mixer_in",
    )(x, mod3, mod3, g, wq, wk, wv, wg, wa, wu, walpha, balpha, gng)


def _cpow(xr, xi, d):
    mag = jnp.exp(xr * d)
    return mag * jnp.cos(xi * d), mag * jnp.sin(xi * d)


def _s5_prep_kernel(lr_r, li_r, ld_r, lr_c, li_c, ld_c, btr, bti, ctr, cti, d_r,
                    wt_ref, mi_ref, mo_ref, a_ref, kt_s):
    L = S5_L
    lr = lr_r[...]
    li = li_r[...]
    dt = jnp.exp(ld_r[...])
    xr, xi = lr * dt, li * dt
    ar, ai = _cpow(xr, xi, 1.0)
    den = lr * lr + li * li
    fr = ((ar - 1.0) * lr + ai * li) / den
    fi = (ai * lr - (ar - 1.0) * li) / den
    br, bi = btr[...], bti[...]
    bbr = fr * br - fi * bi
    bbi = fr * bi + fi * br
    cr, ci = ctr[...], cti[...]
    pr, pi = jnp.ones_like(ar), jnp.zeros_like(ai)
    for d in range(L):
        xdr = bbr * pr - bbi * pi
        xdi = bbr * pi + bbi * pr
        i = L - 1 - d
        mo_ref[i * LANES:(i + 1) * LANES, :S5_ST] = xdr.astype(bf16)
        mo_ref[i * LANES:(i + 1) * LANES, S5_ST:] = xdi.astype(bf16)
        kt = (jnp.dot(xdr, cr, preferred_element_type=f32, precision=lax.Precision.HIGHEST)
              - jnp.dot(xdi, ci, preferred_element_type=f32, precision=lax.Precision.HIGHEST))
        if d == 0:
            r = lax.broadcasted_iota(i32, (LANES, LANES), 0)
            c = lax.broadcasted_iota(i32, (LANES, LANES), 1)
            kt = kt + jnp.where(r == c, d_r[...], 0.0)
        kt_s[d] = kt.astype(bf16)
        pr, pi = pr * ar - pi * ai, pr * ai + pi * ar
    a_ref[:, :S5_ST] = pr
    a_ref[:, S5_ST:] = pi
    zero = jnp.zeros((LANES, LANES), bf16)
    for i in range(L):
        for j in range(L):
            wt_ref[i * LANES:(i + 1) * LANES, j * LANES:(j + 1) * LANES] = kt_s[j - i] if j >= i else zero
    lrc = lr_c[...]
    lic = li_c[...]
    dtc = jnp.exp(ld_c[...])
    acr, aci = _cpow(lrc * dtc, lic * dtc, 1.0)
    pr, pi = acr, aci
    for j in range(L):
        mi_ref[:S5_ST, j * LANES:(j + 1) * LANES] = (cr * pr - ci * pi).astype(bf16)
        mi_ref[S5_ST:, j * LANES:(j + 1) * LANES] = (-(cr * pi + ci * pr)).astype(bf16)
        pr, pi = pr * acr - pi * aci, pr * aci + pi * acr


def _s5_prep(lr, li, ld, b_re, b_im, c_re, c_im, dvec):
    G, P, H, M, GPT = S5_G, S5_P, S5_CH, S5_M, S5_GPT
    eye = jnp.eye(GPT, dtype=f32)

    def rows(v):
        return v.reshape(M, 1, S5_ST)

    def cols(v):
        return v.reshape(M, S5_ST, 1)

    ldp = jnp.broadcast_to(ld[:, None], (G, P))

    def bt(b):
        b4 = b.reshape(M, GPT, P, H)
        return jnp.einsum('mgph,gk->mkhgp', b4, eye).reshape(M, LANES, S5_ST)

    def ct(c):
        c4 = c.reshape(M, GPT, H, P)
        return jnp.einsum('mghp,gk->mgpkh', c4, eye).reshape(M, S5_ST, LANES)

    L = S5_L
    mspec = lambda shape: pl.BlockSpec((None,) + shape, lambda m: (m,) + (0,) * len(shape))
    return pl.pallas_call(
        _s5_prep_kernel,
        grid=(M,),
        in_specs=[mspec((1, S5_ST))] * 3 + [mspec((S5_ST, 1))] * 3
        + [mspec((LANES, S5_ST))] * 2 + [mspec((S5_ST, LANES))] * 2 + [mspec((1, LANES))],
        out_specs=[mspec((L * LANES, L * LANES)), mspec((2 * S5_ST, L * LANES)),
                   mspec((L * LANES, 2 * S5_ST)), mspec((1, 2 * S5_ST))],
        out_shape=[jax.ShapeDtypeStruct((M, L * LANES, L * LANES), bf16),
                   jax.ShapeDtypeStruct((M, 2 * S5_ST, L * LANES), bf16),
                   jax.ShapeDtypeStruct((M, L * LANES, 2 * S5_ST), bf16),
                   jax.ShapeDtypeStruct((M, 1, 2 * S5_ST), f32)],
        scratch_shapes=[pltpu.VMEM((L, LANES, LANES), bf16)],
        compiler_params=pltpu.CompilerParams(dimension_semantics=("arbitrary",), vmem_limit_bytes=VMEM_LIMIT),
        name="s5_prep",
    )(rows(lr), rows(li), rows(ldp), cols(lr), cols(li), cols(ldp),
      bt(b_re), bt(b_im), ct(c_re), ct(c_im), dvec.reshape(M, 1, LANES))


def _s5_scan_kernel(u_ref, wt, mi, mo, a_ref, y_ref, xs, s_s, x_s):
    r = pl.program_id(1)

    @pl.when(r == 0)
    def _():
        xs[...] = jnp.zeros_like(xs)

    u = u_ref[...].reshape(S5_ROWS, S5_L * LANES)
    s = _dot(u, mo[...])
    NT = 2 * S5_ST // LANES
    for c in range(NT):
        s_s[c] = s[:, c * LANES:(c + 1) * LANES]
    a = [a_ref[:, c * LANES:(c + 1) * LANES] for c in range(NT)]

    def step(n, x):
        rows = pl.ds(n, B, stride=S5_NCH)
        new = []
        for c in range(NT):
            x_s[c, rows, :] = x[c]
        for c in range(NT // 2):
            ar, ai, xr, xi = a[c], a[NT // 2 + c], x[c], x[NT // 2 + c]
            new.append((ar * xr - ai * xi + s_s[c, rows, :], ar * xi + ai * xr + s_s[NT // 2 + c, rows, :]))
        return tuple(p[0] for p in new) + tuple(p[1] for p in new)

    x = tuple(xs[:, c * LANES:(c + 1) * LANES] for c in range(NT))
    for n in range(S5_NCH):
        x = step(n, x)
    for c in range(NT):
        xs[:, c * LANES:(c + 1) * LANES] = x[c]
    x_in = jnp.concatenate([x_s[c] for c in range(NT)], axis=1)
    y = _dot(u, wt[...]) + _dot(x_in.astype(bf16), mi[...])
    y_ref[...] = jax.nn.gelu(y).astype(bf16).reshape(B, S5_NCH, S5_L * LANES)


def _s5_scan(u, wt, mi, mo, a16):
    L = S5_L
    wspec = lambda shape: pl.BlockSpec((None,) + shape, lambda m, r: (m,) + (0,) * len(shape))
    uspec = pl.BlockSpec((None, B, S5_NCH, L * LANES), lambda m, r: (m, 0, r, 0))
    return pl.pallas_call(
        _s5_scan_kernel,
        grid=(S5_M, T // L // S5_NCH),
        in_specs=[uspec,
                  wspec((L * LANES, L * LANES)), wspec((2 * S5_ST, L * LANES)),
                  wspec((L * LANES, 2 * S5_ST)), wspec((1, 2 * S5_ST))],
        out_specs=uspec,
        out_shape=jax.ShapeDtypeStruct((S5_M, B, T // L, L * LANES), bf16),
        scratch_shapes=[pltpu.VMEM((B, 2 * S5_ST), f32), pltpu.VMEM((2 * S5_ST // LANES, S5_ROWS, LANES), f32),
                        pltpu.VMEM((2 * S5_ST // LANES, S5_ROWS, LANES), f32)],
        compiler_params=pltpu.CompilerParams(
            dimension_semantics=("arbitrary", "arbitrary"), vmem_limit_bytes=VMEM_LIMIT),
        name="s5_scan",
    )(u, wt, mi, mo, a16)


def _mixer_out_kernel(x_ref, gla_ref, y_ref, g1_ref, sh2_ref, sc2_ref, gluw, glub, woa, wob,
                      mpg, fpg, rw_hi, rw_lo, rb, h1_ref, hn2_ref, pos_ref, tab_ref, gate_ref, cnt_ref, cnt_s, y_s):
    first = jnp.logical_and(pl.program_id(0) == 0, pl.program_id(1) == 0)

    @pl.when(first)
    def _():
        cnt_s[...] = jnp.zeros_like(cnt_s)

    for m in range(S5_M):
        for jj in range(S5_L):
            y_s[m, pl.ds(jj, TT // S5_L, stride=S5_L), :] = y_ref[m, :, jj * LANES:(jj + 1) * LANES].astype(f32)
    y = jnp.concatenate([y_s[m] for m in range(S5_M)], axis=1)
    z = _dot(y.astype(bf16), gluw[...]) + glub[...]
    s5 = y * jax.nn.sigmoid(z)
    mix = _dot(gla_ref[...], woa[...]) + _dot(s5.astype(bf16), wob[...])
    h1 = x_ref[...] + g1_ref[...] * (mix * _rms(mix) * mpg[...])
    hn2 = (h1 * _rms(h1) * fpg[...]) * (1.0 + sc2_ref[...]) + sh2_ref[...]
    for cc in range(ROW):
        h1_ref[pl.ds(cc, TT, stride=ROW), :] = h1[:, cc * LANES:(cc + 1) * LANES]
        hn2_ref[pl.ds(cc, TT, stride=ROW), :] = hn2[:, cc * LANES:(cc + 1) * LANES]
    x_hi = hn2.astype(bf16)
    x_lo = (hn2 - x_hi.astype(f32)).astype(bf16)
    logits = _dot(x_hi, rw_hi[...]) + (_dot(x_lo, rw_hi[...]) + _dot(x_hi, rw_lo[...])) + rb[...]

    lane = lax.broadcasted_iota(i32, (TT, LANES), 1).astype(f32)
    l = logits
    vals, idxs = [], []
    for _ in range(TOPK):
        m = jnp.max(l, axis=-1, keepdims=True)
        ix = jnp.min(jnp.where(l == m, lane, float(LANES)), axis=-1, keepdims=True)
        vals.append(m)
        idxs.append(ix)
        l = jnp.where(lane == ix, -jnp.inf, l)
    es = [jnp.exp(v - vals[0]) for v in vals]
    tot = es[0] + es[1] + es[2] + es[3]
    oh = jnp.zeros((TT, LANES), f32)
    for ix in idxs:
        oh = oh + (lane == ix).astype(f32)
    r = lax.broadcasted_iota(i32, (TT, TT), 0)
    c = lax.broadcasted_iota(i32, (TT, TT), 1)
    cum = _dot((r >= c).astype(bf16), oh.astype(bf16))
    lcnt = cum[TT - 1:TT, :]
    gbase = cnt_s[...]
    hi = jnp.floor(lcnt * (1.0 / 64.0))
    lo = lcnt - hi * 64.0
    upper = (lax.broadcasted_iota(i32, (LANES, LANES), 0) < lax.broadcasted_iota(i32, (LANES, LANES), 1)).astype(bf16)
    hi8 = jnp.broadcast_to(hi, (ROW, LANES)).astype(bf16)
    lo8 = jnp.broadcast_to(lo, (ROW, LANES)).astype(bf16)
    lbase = (64.0 * _dot(hi8, upper) + _dot(lo8, upper))[0:1, :]
    loc = lbase + (cum - oh)
    tmod = (lax.broadcasted_iota(i32, (TT, LANES), 0) % PG).astype(f32) * float(TOPK)
    spread = jnp.zeros((TT, LANES), f32)
    gate = jnp.zeros((TT, LANES), f32)
    for kk in range(TOPK):
        p = jnp.sum(jnp.where(lane == idxs[kk], loc, 0.0), axis=-1, keepdims=True)
        spread = jnp.where(lane == tmod + float(kk), p, spread)
        gate = jnp.where(lane == float(kk), es[kk] / tot, gate)
    s_hi = jnp.floor(spread * (1.0 / 64.0))
    s_lo = spread - s_hi * 64.0
    grp = (lax.broadcasted_iota(i32, (TT // PG, TT), 1) // PG
           == lax.broadcasted_iota(i32, (TT // PG, TT), 0)).astype(bf16)
    folded = 64.0 * _dot(grp, s_hi.astype(bf16)) + _dot(grp, s_lo.astype(bf16))
    pos_ref[...] = (folded * float(ROW)).astype(i32)
    sub = lax.broadcasted_iota(i32, (ROW, LANES), 0)
    tab = jnp.where(sub == 0, lcnt, jnp.where(sub == 1, lbase, jnp.where(sub == 2, gbase, 0.0)))
    tab_ref[...] = tab.astype(i32)
    gate_ref[...] = gate
    cnt = cnt_s[...] + cum[TT - 1:TT, :]
    cnt_s[...] = cnt
    cnt_ref[...] = cnt


def _mixer_out(x, gla, y, mod3, gluw, glub, woa, wob, mpg, fpg, rw_hi, rw_lo, rb):
    nt = T // TT
    full = lambda shape: pl.BlockSpec(shape, lambda b, t: (0,) * len(shape))
    tok = lambda w: pl.BlockSpec((None, TT, w), lambda b, t: (b, t, 0))
    slab = pl.BlockSpec((TT * ROW, LANES), lambda b, t: (b * nt + t, 0))
    slab_shape = jax.ShapeDtypeStruct((N * ROW, LANES), f32)
    yspec = pl.BlockSpec((S5_M, None, TT // S5_L, S5_L * LANES), lambda b, t: (0, b, t, 0))
    modspec = lambda j: pl.BlockSpec((None, 1, D), lambda b, t: (b, 0, j))
    return pl.pallas_call(
        _mixer_out_kernel,
        grid=(B, nt),
        in_specs=[tok(D), tok(512), yspec,
                  modspec(2), modspec(3), modspec(4),
                  full((512, 512)), full((1, 512)), full((512, D)), full((512, D)),
                  full((1, D)), full((1, D)), full((D, LANES)), full((D, LANES)), full((1, LANES))],
        out_specs=[slab, slab, pl.BlockSpec((TT // PG, LANES), lambda b, t: (b * nt + t, 0)),
                   pl.BlockSpec((None, ROW, LANES), lambda b, t: (b * nt + t, 0, 0)), tok(LANES), full((1, LANES))],
        out_shape=[slab_shape, slab_shape,
                   jax.ShapeDtypeStruct((N // PG, LANES), i32), jax.ShapeDtypeStruct((N // TT, ROW, LANES), i32),
                   jax.ShapeDtypeStruct((B, T, LANES), f32),
                   jax.ShapeDtypeStruct((1, LANES), f32)],
        scratch_shapes=[pltpu.VMEM((1, LANES), f32), pltpu.VMEM((S5_M, TT, LANES), f32)],
        compiler_params=pltpu.CompilerParams(
            dimension_semantics=("arbitrary", "arbitrary"), vmem_limit_bytes=VMEM_LIMIT),
        name="mixer_out",
    )(x, gla, y, mod3, mod3, mod3, gluw, glub, woa, wob, mpg, fpg, rw_hi, rw_lo, rb)


STAGE_ROWS = TT * TOPK * ROW


def _strip_copies(tab_ref, stage, region_hbm, sem, to_region):
    def expert(e, carry):
        n = tab_ref[0, e] * ROW

        @pl.when(n > 0)
        def _():
            local = stage.at[pl.ds(pl.multiple_of(tab_ref[1, e] * ROW, ROW), n), :]
            remote = region_hbm.at[pl.ds(pl.multiple_of((e * N + tab_ref[2, e]) * ROW, ROW), n), :]
            src, dst = (local, remote) if to_region else (remote, local)
            pltpu.make_async_copy(src, dst, sem).start()
        return carry
    lax.fori_loop(0, NE, expert, 0)


def _strip_wait(stage, region_hbm, sem):
    pltpu.make_async_copy(stage, region_hbm.at[pl.ds(0, STAGE_ROWS), :], sem).wait()


def _dispatch_kernel(pos_ref, tab_ref, x_ref, xs_hbm, stage, sem):
    i = pl.program_id(0)
    slot = lax.rem(i, 2)

    @pl.when(i >= 2)
    def _():
        _strip_wait(stage.at[slot], xs_hbm, sem.at[slot])

    def scatter(s):
        def group(g, carry):
            for jj in range(PG):
                v = x_ref[pl.ds(pl.multiple_of((g * PG + jj) * ROW, ROW), ROW), :]
                for kk in range(TOPK):
                    stage[s, pl.ds(pl.multiple_of(pos_ref[g, jj * TOPK + kk], ROW), ROW), :] = v
            return carry
        lax.fori_loop(0, TT // PG, group, 0)

    for s in range(2):
        pl.when(slot == s)(functools.partial(scatter, s))
    _strip_copies(tab_ref, stage.at[slot], xs_hbm, sem.at[slot], to_region=True)

    @pl.when(i == pl.num_programs(0) - 1)
    def _():
        _strip_wait(stage.at[1 - slot], xs_hbm, sem.at[1 - slot])
        _strip_wait(stage.at[slot], xs_hbm, sem.at[slot])


def _dispatch(pos, tab, hn2_rows):
    return pl.pallas_call(
        _dispatch_kernel,
        grid=(N // TT,),
        in_specs=[pl.BlockSpec((TT // PG, LANES), lambda i: (i, 0), memory_space=pltpu.SMEM),
                  pl.BlockSpec((None, ROW, LANES), lambda i: (i, 0, 0), memory_space=pltpu.SMEM),
                  pl.BlockSpec((TT * ROW, LANES), lambda i: (i, 0))],
        out_specs=pl.BlockSpec(memory_space=pl.ANY),
        out_shape=jax.ShapeDtypeStruct((NE * N * ROW, LANES), f32),
        scratch_shapes=[pltpu.VMEM((2, STAGE_ROWS, LANES), f32), pltpu.SemaphoreType.DMA((2,))],
        compiler_params=pltpu.CompilerParams(dimension_semantics=("arbitrary",), vmem_limit_bytes=VMEM_LIMIT),
        name="dispatch",
    )(pos, tab, hn2_rows)


def _experts_kernel(tb_ref, te_ref, tv_ref, nu_ref, x_ref, wgu_ref, bgu_ref, wd_ref, bd_ref, y_ref, wgu_b, wd_b):
    i = pl.program_id(0)

    @pl.when(i < nu_ref[0])
    def _():
        changed = jnp.logical_or(i == 0, te_ref[i] != te_ref[jnp.maximum(i - 1, 0)])

        @pl.when(changed)
        def _():
            def cast(j, carry):
                r0 = pl.multiple_of(j * LANES, LANES)
                wgu_b[pl.ds(r0, LANES), :] = wgu_ref[pl.ds(r0, LANES), :].astype(bf16)
                wd_b[pl.ds(r0, LANES), :] = wd_ref[pl.ds(r0, LANES), :].astype(bf16)
                return carry
            lax.fori_loop(0, D // LANES, cast, 0)

        def tile(h):
            base = h * MB * ROW
            xt = jnp.concatenate([x_ref[pl.ds(base + cc, MB, stride=ROW), :] for cc in range(ROW)], axis=1)
            valid = lax.broadcasted_iota(i32, (MB, 1), 0) < tv_ref[i] - h * MB
            xt = jnp.where(valid, xt, 0.0).astype(bf16)
            gu = _dot(xt, wgu_b[...]) + bgu_ref[...]
            g = jnp.minimum(gu[:, :DFF], LIMIT)
            up = jnp.clip(gu[:, DFF:], -LIMIT, LIMIT)
            act = (up + 1.0) * (g * jax.nn.sigmoid(ALPHA * g))
            yv = _dot(act.astype(bf16), wd_b[...]) + bd_ref[...]
            for cc in range(ROW):
                y_ref[pl.ds(base + cc, MB, stride=ROW), :] = yv[:, cc * LANES:(cc + 1) * LANES]

        tile(0)
        for h in range(1, EB):
            pl.when(tv_ref[i] > h * MB)(functools.partial(tile, h))


def _experts(tile_blk, tile_e, tile_nv, nused, xs, w_gu, b_gu, w_down, b_down):
    blk = lambda i, tb, te, tv, nu: (tb[i], 0)
    wsel = lambda i, tb, te, tv, nu: (te[i], 0, 0)
    grid_spec = pltpu.PrefetchScalarGridSpec(
        num_scalar_prefetch=4,
        grid=(NB,),
        in_specs=[
            pl.BlockSpec((EB * MB * ROW, LANES), blk),
            pl.BlockSpec((None, D, 2 * DFF), wsel),
            pl.BlockSpec((None, 1, 2 * DFF), wsel),
            pl.BlockSpec((None, DFF, D), wsel),
            pl.BlockSpec((None, 1, D), wsel),
        ],
        out_specs=pl.BlockSpec((EB * MB * ROW, LANES), blk),
        scratch_shapes=[pltpu.VMEM((D, 2 * DFF), bf16), pltpu.VMEM((DFF, D), bf16)],
    )
    return pl.pallas_call(
        _experts_kernel,
        grid_spec=grid_spec,
        out_shape=jax.ShapeDtypeStruct((NE * N * ROW, LANES), f32),
        compiler_params=pltpu.CompilerParams(dimension_semantics=("arbitrary",), vmem_limit_bytes=VMEM_LIMIT),
        name="experts",
    )(tile_blk, tile_e, tile_nv, nused, xs, w_gu, b_gu.reshape(NE, 1, 2 * DFF), w_down, b_down.reshape(NE, 1, D))


def _combine_kernel(pos_ref, tab_ref, tabn_ref, gate_ref, h1_ref, g2_ref, pg_ref, ys_hbm, o_ref,
                    stage, wbuf, rbuf, sem):
    j = pl.program_id(0)
    slot = lax.rem(j, 2)
    GT = PG
    GR = GT * ROW

    @pl.when(j == 0)
    def _():
        _strip_copies(tab_ref, stage.at[0], ys_hbm, sem.at[0], to_region=False)

    @pl.when(j + 1 < pl.num_programs(0))
    def _():
        _strip_copies(tabn_ref, stage.at[1 - slot], ys_hbm, sem.at[1 - slot], to_region=False)

    for kk in range(TOPK):
        wbuf[kk] = jnp.broadcast_to(gate_ref[:, kk:kk + 1], (TT, LANES))
    _strip_wait(stage.at[slot], ys_hbm, sem.at[slot])
    g2 = g2_ref[...]
    pg = pg_ref[...]

    def one_group(s, g):
        rows = pl.ds(pl.multiple_of(g * GR, GR), GR)
        parts = []
        for jj in range(GT):
            t = g * GT + jj
            acc = None
            for kk in range(TOPK):
                w = jnp.broadcast_to(wbuf[kk, pl.ds(t, 1), :], (ROW, LANES))
                term = stage[s, pl.ds(pl.multiple_of(pos_ref[g, jj * TOPK + kk], ROW), ROW), :] * w
                acc = term if acc is None else acc + term
            parts.append(acc)
        ff3 = jnp.concatenate(parts, axis=0).reshape(GT, ROW, LANES)
        ss = jnp.sum(jnp.sum(ff3 * ff3, axis=2, keepdims=True), axis=1, keepdims=True)
        rs = lax.rsqrt(ss * (1.0 / D) + EPS)
        out = h1_ref[rows, :].reshape(GT, ROW, LANES) + g2 * (ff3 * rs * pg)
        rbuf[rows, :] = out.reshape(GR, LANES)

    def gather(s):
        def pair(gg, carry):
            one_group(s, 2 * gg)
            one_group(s, 2 * gg + 1)
            return carry
        lax.fori_loop(0, TT // GT // 2, pair, 0)

    for s in range(2):
        pl.when(slot == s)(functools.partial(gather, s))
    for cc in range(ROW):
        o_ref[:, cc * LANES:(cc + 1) * LANES] = rbuf[pl.ds(cc, TT, stride=ROW), :]


def _combine(pos, tab, gates, h1_rows, mod4, pg, ys):
    nt = N // TT
    per_b = T // TT
    tabspec = lambda imap: pl.BlockSpec((None, ROW, LANES), imap, memory_space=pltpu.SMEM)
    return pl.pallas_call(
        _combine_kernel,
        grid=(nt,),
        in_specs=[pl.BlockSpec((TT // PG, LANES), lambda j: (j, 0), memory_space=pltpu.SMEM),
                  tabspec(lambda j: (j, 0, 0)), tabspec(lambda j: (jnp.minimum(j + 1, nt - 1), 0, 0)),
                  pl.BlockSpec((TT, LANES), lambda j: (j, 0)),
                  pl.BlockSpec((TT * ROW, LANES), lambda j: (j, 0)),
                  pl.BlockSpec((None, None, ROW, LANES), lambda j: (j // per_b, 5, 0, 0)),
                  pl.BlockSpec((ROW, LANES), lambda j: (0, 0)),
                  pl.BlockSpec(memory_space=pl.ANY)],
        out_specs=pl.BlockSpec((TT, D), lambda j: (j, 0)),
        out_shape=jax.ShapeDtypeStruct((N, D), f32),
        scratch_shapes=[pltpu.VMEM((2, STAGE_ROWS, LANES), f32), pltpu.VMEM((TOPK, TT, LANES), f32),
                        pltpu.VMEM((TT * ROW, LANES), f32), pltpu.SemaphoreType.DMA((2,))],
        compiler_params=pltpu.CompilerParams(dimension_semantics=("arbitrary",), vmem_limit_bytes=VMEM_LIMIT),
        name="combine",
    )(pos, tab, tab, gates, h1_rows, mod4, pg, ys)


def _pad_heads(w):
    lead = w.shape[:-1]
    w4 = w.reshape(lead + (GLA_H, GLA_DK))
    w4 = jnp.concatenate([w4, jnp.zeros_like(w4)], axis=-1)
    return w4.reshape(lead + (GLA_H * LANES,))


def kernel(x, c, ada_w, ada_b, mix_pre_g, mix_post_g, ffn_pre_g, ffn_post_g, w_in, w_alpha, b_alpha, gla_norm_g, s5_lambda_re, s5_lambda_im, s5_log_dt, s5_b_re, s5_b_im, s5_c_re, s5_c_im, s5_d, s5_glu_w, s5_glu_b, w_out, router_w, router_b, exp_w_gu, exp_b_gu, exp_w_down, exp_b_down):
    l = 0
    mod = _ada(c, ada_w[l], ada_b[l])
    mod3 = mod.reshape(B, 1, 6 * D)
    mod4 = mod.reshape(B, 6, ROW, LANES)

    w = w_in[l]
    o_q, o_k, o_v, o_g, o_a, o_u = 0, 256, 512, 1024, 1536, 1552
    wq = _pad_heads(w[:, o_q:o_k]).astype(bf16)
    wk = _pad_heads(w[:, o_k:o_v]).astype(bf16)
    wv = w[:, o_v:o_g].astype(bf16)
    wg = w[:, o_g:o_a].astype(bf16)
    wa = jnp.pad(w[:, o_a:o_u], ((0, 0), (0, LANES - LOWRANK))).astype(bf16)
    wu = w[:, o_u:].astype(bf16)
    walpha = jnp.pad(_pad_heads(w_alpha[l]), ((0, LANES - LOWRANK), (0, 0))).astype(bf16)
    balpha = _pad_heads(b_alpha[l]).reshape(1, GLA_H * LANES)

    gla, u = _mixer_in(x, mod3, mix_pre_g[l].reshape(1, D), wq, wk, wv, wg, wa, wu,
                       walpha, balpha, gla_norm_g[l].reshape(1, GLA_DV))

    wt, mi, mo, a16 = _s5_prep(s5_lambda_re[l], s5_lambda_im[l], s5_log_dt[l], s5_b_re[l], s5_b_im[l],
                               s5_c_re[l], s5_c_im[l], s5_d[l])
    y = _s5_scan(u, wt, mi, mo, a16)

    rw = jnp.pad(router_w[l], ((0, 0), (0, LANES - NE)))
    rw_hi = rw.astype(bf16)
    rw_lo = (rw - rw_hi.astype(f32)).astype(bf16)
    rb = jnp.pad(router_b[l], (0, LANES - NE), constant_values=-1e30).reshape(1, LANES)
    wo = w_out[l].astype(bf16)
    h1_rows, hn2_rows, pos, tab, gates, cnt = _mixer_out(
        x, gla, y, mod3, s5_glu_w[l].astype(bf16), s5_glu_b[l].reshape(1, 512), wo[:512], wo[512:],
        mix_post_g[l].reshape(1, D), ffn_pre_g[l].reshape(1, D), rw_hi, rw_lo, rb)

    counts = cnt[0, :NE].astype(i32)
    step_rows = EB * MB
    ntile = (counts + step_rows - 1) // step_rows
    tends = jnp.cumsum(ntile)
    nused = tends[-1]
    gi = jnp.minimum(jnp.arange(NB, dtype=i32), nused - 1)
    tile_e = jnp.sum((tends[None, :] <= gi[:, None]).astype(i32), axis=1)
    sel = tile_e[:, None] == jnp.arange(NE, dtype=i32)[None, :]
    pick = lambda v: jnp.sum(jnp.where(sel, v[None, :], 0), axis=1)
    tile_j = gi - pick(tends - ntile)
    tile_blk = tile_e * TPE + tile_j
    tile_nv = jnp.clip(pick(counts) - tile_j * step_rows, 0, step_rows)

    xs = _dispatch(pos, tab, hn2_rows)
    ys = _experts(tile_blk, tile_e, tile_nv, nused.reshape(1), xs,
                  exp_w_gu[l], exp_b_gu[l], exp_w_down[l], exp_b_down[l])
    out = _combine(pos, tab, gates.reshape(N, LANES), h1_rows, mod4, ffn_post_g[l].reshape(ROW, LANES), ys)
    return out.reshape(B, T, D)
```

```python
import functools
import math

import jax
import jax.numpy as jnp
from jax import lax
from jax.experimental import pallas as pl
from jax.experimental.pallas import tpu as pltpu

f32 = jnp.float32
bf16 = jnp.bfloat16
i32 = jnp.int32

D = 1024
B = 8
T = 2048
N = B * T
GLA_H = 4
GLA_DV = 128
GLA_DK = 64
GLA_TAU = 16.0
GLA_CHUNK = 64
LOWRANK = 16
S5_W = 512
S5_CH = 16
S5_G = 32
S5_P = 64
NE = 32
TOPK = 4
DFF = 1024
ALPHA = 1.702
LIMIT = 7.0
EPS = 1e-6

LANES = 128
SUBLANES = 8
VMEM_LIMIT = 56 * 1024 * 1024

TT = 512
S5_L = 16
S5_M = S5_W // LANES
S5_GPT = LANES // S5_CH
S5_ST = S5_GPT * S5_P
S5_NCH = 32
S5_ROWS = B * S5_NCH
ROW = SUBLANES
MB = 256
EB = 4
NB = (N * TOPK) // (EB * MB) + NE
TPE = N // (EB * MB)
PG = 16


def _dot(a, b):
    return jnp.dot(a, b, preferred_element_type=f32)


def _dot_t(a, b, ca, cb):
    return lax.dot_general(a, b, (((ca,), (cb,)), ((), ())), preferred_element_type=f32)


def _rms(x):
    return lax.rsqrt(jnp.mean(x * x, axis=-1, keepdims=True) + EPS)


def _ada_kernel(c_ref, w_ref, b_ref, o_ref):
    c = c_ref[...]
    s = (c * jax.nn.sigmoid(c)).astype(bf16)
    o_ref[...] = _dot(s, w_ref[...].astype(bf16)) + b_ref[...]


def _ada(c, w, b):
    return pl.pallas_call(
        _ada_kernel,
        grid=(6,),
        in_specs=[
            pl.BlockSpec((B, D), lambda j: (0, 0)),
            pl.BlockSpec((D, D), lambda j: (0, j)),
            pl.BlockSpec((1, D), lambda j: (0, j)),
        ],
        out_specs=pl.BlockSpec((B, D), lambda j: (0, j)),
        out_shape=jax.ShapeDtypeStruct((B, 6 * D), f32),
        name="ada",
    )(c, w, b.reshape(1, 6 * D))


def _mixer_in_kernel(x_ref, sc_ref, sh_ref, g_ref, wq, wk, wv, wg, wa, wu, walpha, balpha, gng,
                     gla_ref, u_ref, st_s, u_s):
    t = pl.program_id(1)

    @pl.when(t == 0)
    def _():
        st_s[...] = jnp.zeros_like(st_s)

    x = x_ref[...]
    hn = (x * _rms(x) * g_ref[...]) * (1.0 + sc_ref[...]) + sh_ref[...]
    hb = hn.astype(bf16)
    u = _dot(hb, wu[...])
    for m in range(S5_M):
        u_s[m] = u[:, m * LANES:(m + 1) * LANES]
        for i in range(S5_L):
            u_ref[m, :, i * LANES:(i + 1) * LANES] = u_s[m, pl.ds(i, TT // S5_L, stride=S5_L), :].astype(bf16)

    C = GLA_CHUNK
    NC = TT // C
    row = lax.broadcasted_iota(i32, (TT, TT), 0)
    col = lax.broadcasted_iota(i32, (TT, TT), 1)
    causal = jnp.logical_and(row >= col, (row // C) == (col // C))
    tri = causal.astype(bf16)
    a_lr = _dot(hb, wa[...]).astype(bf16)
    la = jax.nn.log_sigmoid(_dot(a_lr, walpha[...]) + balpha[...]) * (1.0 / GLA_TAU)
    la_hi = la.astype(bf16)
    la_lo = (la - la_hi.astype(f32)).astype(bf16)
    bc = _dot(tri, la_hi) + _dot(tri, la_lo)
    bl = jnp.broadcast_to(bc.reshape(NC, C, GLA_H * LANES)[:, C - 1:C, :],
                          (NC, C, GLA_H * LANES)).reshape(TT, GLA_H * LANES)
    q = _dot(hb, wq[...]) * (GLA_DK ** -0.5)
    k = _dot(hb, wk[...])
    qd = (q * jnp.exp(bc)).astype(bf16)
    ki = (k * jnp.exp(-bc)).astype(bf16)
    ke = (k * jnp.exp(bl - bc)).astype(bf16)
    dec = jnp.exp(bl)
    vb = _dot(hb, wv[...]).astype(bf16)
    og = _dot(hb, wg[...])
    gn = gng[...]
    for h in range(GLA_H):
        sl = slice(h * LANES, (h + 1) * LANES)
        qh, kih, keh, vh = qd[:, sl], ki[:, sl], ke[:, sl], vb[:, sl]
        sc = jnp.where(causal, _dot_t(qh, kih, 1, 1), 0.0).astype(bf16)
        o_intra = _dot(sc, vh)
        st = st_s[h]
        o_parts = []
        for c in range(NC):
            rs = slice(c * C, (c + 1) * C)
            o_parts.append(_dot_t(qh[rs], st.astype(bf16), 1, 1))
            st = st * dec[c * C:c * C + 1, sl] + _dot_t(vh[rs], keh[rs], 0, 0)
        st_s[h] = st
        o = o_intra + jnp.concatenate(o_parts, axis=0)
        on = o * _rms(o) * gn
        ogh = og[:, sl]
        gla_ref[:, sl] = (on * (ogh * jax.nn.sigmoid(ogh))).astype(bf16)


def _mixer_in(x, mod3, g, wq, wk, wv, wg, wa, wu, walpha, balpha, gng):
    nt = T // TT
    full = lambda shape: pl.BlockSpec(shape, lambda b, t: (0,) * len(shape))
    u_shape = jax.ShapeDtypeStruct((S5_M, B, T // S5_L, S5_L * LANES), bf16)
    u_spec = pl.BlockSpec((S5_M, None, TT // S5_L, S5_L * LANES), lambda b, t: (0, b, t, 0))
    return pl.pallas_call(
        _mixer_in_kernel,
        grid=(B, nt),
        in_specs=[
            pl.BlockSpec((None, TT, D), lambda b, t: (b, t, 0)),
            pl.BlockSpec((None, 1, D), lambda b, t: (b, 0, 1)),
            pl.BlockSpec((None, 1, D), lambda b, t: (b, 0, 0)),
            full((1, D)),
            full((D, 512)), full((D, 512)), full((D, 512)), full((D, 512)),
            full((D, LANES)), full((D, 512)),
            full((LANES, 512)), full((1, 512)), full((1, LANES)),
        ],
        out_specs=[pl.BlockSpec((None, TT, 512), lambda b, t: (b, t, 0)), u_spec],
        out_shape=[jax.ShapeDtypeStruct((B, T, 512), bf16), u_shape],
        scratch_shapes=[pltpu.VMEM((GLA_H, GLA_DV, LANES), f32), pltpu.VMEM((S5_M, TT, LANES), f32)],
        compiler_params=pltpu.CompilerParams(
            dimension_semantics=("arbitrary", "arbitrary"), vmem_limit_bytes=VMEM_LIMIT),
        name="mixer_in",
    )(x, mod3, mod3, g, wq, wk, wv, wg, wa, wu, walpha, balpha, gng)


def _cpow(xr, xi, d):
    mag = jnp.exp(xr * d)
    return mag * jnp.cos(xi * d), mag * jnp.sin(xi * d)


def _s5_prep_kernel(lr_r, li_r, ld_r, lr_c, li_c, ld_c, btr, bti, ctr, cti, d_r,
                    wt_ref, mi_ref, mo_ref, a_ref, kt_s):
    L = S5_L
    lr = lr_r[...]
    li = li_r[...]
    dt = jnp.exp(ld_r[...])
    xr, xi = lr * dt, li * dt
    ar, ai = _cpow(xr, xi, 1.0)
    den = lr * lr + li * li
    fr = ((ar - 1.0) * lr + ai * li) / den
    fi = (ai * lr - (ar - 1.0) * li) / den
    br, bi = btr[...], bti[...]
    bbr = fr * br - fi * bi
    bbi = fr * bi + fi * br
    cr, ci = ctr[...], cti[...]
    pr, pi = jnp.ones_like(ar), jnp.zeros_like(ai)
    for d in range(L):
        xdr = bbr * pr - bbi * pi
        xdi = bbr * pi + bbi * pr
        i = L - 1 - d
        mo_ref[i * LANES:(i + 1) * LANES, :S5_ST] = xdr.astype(bf16)
        mo_ref[i * LANES:(i + 1) * LANES, S5_ST:] = xdi.astype(bf16)
        kt = (jnp.dot(xdr, cr, preferred_element_type=f32, precision=lax.Precision.HIGHEST)
              - jnp.dot(xdi, ci, preferred_element_type=f32, precision=lax.Precision.HIGHEST))
        if d == 0:
            r = lax.broadcasted_iota(i32, (LANES, LANES), 0)
            c = lax.broadcasted_iota(i32, (LANES, LANES), 1)
            kt = kt + jnp.where(r == c, d_r[...], 0.0)
        kt_s[d] = kt.astype(bf16)
        pr, pi = pr * ar - pi * ai, pr * ai + pi * ar
    a_ref[:, :S5_ST] = pr
    a_ref[:, S5_ST:] = pi
    zero = jnp.zeros((LANES, LANES), bf16)
    for i in range(L):
        for j in range(L):
            wt_ref[i * LANES:(i + 1) * LANES, j * LANES:(j + 1) * LANES] = kt_s[j - i] if j >= i else zero
    lrc = lr_c[...]
    lic = li_c[...]
    dtc = jnp.exp(ld_c[...])
    acr, aci = _cpow(lrc * dtc, lic * dtc, 1.0)
    pr, pi = acr, aci
    for j in range(L):
        mi_ref[:S5_ST, j * LANES:(j + 1) * LANES] = (cr * pr - ci * pi).astype(bf16)
        mi_ref[S5_ST:, j * LANES:(j + 1) * LANES] = (-(cr * pi + ci * pr)).astype(bf16)
        pr, pi = pr * acr - pi * aci, pr * aci + pi * acr


def _s5_prep(lr, li, ld, b_re, b_im, c_re, c_im, dvec):
    G, P, H, M, GPT = S5_G, S5_P, S5_CH, S5_M, S5_GPT
    eye = jnp.eye(GPT, dtype=f32)

    def rows(v):
        return v.reshape(M, 1, S5_ST)

    def cols(v):
        return v.reshape(M, S5_ST, 1)

    ldp = jnp.broadcast_to(ld[:, None], (G, P))

    def bt(b):
        b4 = b.reshape(M, GPT, P, H)
        return jnp.einsum('mgph,gk->mkhgp', b4, eye).reshape(M, LANES, S5_ST)

    def ct(c):
        c4 = c.reshape(M, GPT, H, P)
        return jnp.einsum('mghp,gk->mgpkh', c4, eye).reshape(M, S5_ST, LANES)

    L = S5_L
    mspec = lambda shape: pl.BlockSpec((None,) + shape, lambda m: (m,) + (0,) * len(shape))
    return pl.pallas_call(
        _s5_prep_kernel,
        grid=(M,),
        in_specs=[mspec((1, S5_ST))] * 3 + [mspec((S5_ST, 1))] * 3
        + [mspec((LANES, S5_ST))] * 2 + [mspec((S5_ST, LANES))] * 2 + [mspec((1, LANES))],
        out_specs=[mspec((L * LANES, L * LANES)), mspec((2 * S5_ST, L * LANES)),
                   mspec((L * LANES, 2 * S5_ST)), mspec((1, 2 * S5_ST))],
        out_shape=[jax.ShapeDtypeStruct((M, L * LANES, L * LANES), bf16),
                   jax.ShapeDtypeStruct((M, 2 * S5_ST, L * LANES), bf16),
                   jax.ShapeDtypeStruct((M, L * LANES, 2 * S5_ST), bf16),
                   jax.ShapeDtypeStruct((M, 1, 2 * S5_ST), f32)],
        scratch_shapes=[pltpu.VMEM((L, LANES, LANES), bf16)],
        compiler_params=pltpu.CompilerParams(dimension_semantics=("arbitrary",), vmem_limit_bytes=VMEM_LIMIT),
        name="s5_prep",
    )(rows(lr), rows(li), rows(ldp), cols(lr), cols(li), cols(ldp),
      bt(b_re), bt(b_im), ct(c_re), ct(c_im), dvec.reshape(M, 1, LANES))


def _s5_scan_kernel(u_ref, wt, mi, mo, a_ref, y_ref, xs, s_s, x_s):
    r = pl.program_id(1)

    @pl.when(r == 0)
    def _():
        xs[...] = jnp.zeros_like(xs)

    u = u_ref[...].reshape(S5_ROWS, S5_L * LANES)
    s = _dot(u, mo[...])
    NT = 2 * S5_ST // LANES
    for c in range(NT):
        s_s[c] = s[:, c * LANES:(c + 1) * LANES]
    a = [a_ref[:, c * LANES:(c + 1) * LANES] for c in range(NT)]

    def step(n, x):
        rows = pl.ds(n, B, stride=S5_NCH)
        new = []
        for c in range(NT):
            x_s[c, rows, :] = x[c]
        for c in range(NT // 2):
            ar, ai, xr, xi = a[c], a[NT // 2 + c], x[c], x[NT // 2 + c]
            new.append((ar * xr - ai * xi + s_s[c, rows, :], ar * xi + ai * xr + s_s[NT // 2 + c, rows, :]))
        return tuple(p[0] for p in new) + tuple(p[1] for p in new)

    x = tuple(xs[:, c * LANES:(c + 1) * LANES] for c in range(NT))
    for n in range(S5_NCH):
        x = step(n, x)
    for c in range(NT):
        xs[:, c * LANES:(c + 1) * LANES] = x[c]
    x_in = jnp.concatenate([x_s[c] for c in range(NT)], axis=1)
    y = _dot(u, wt[...]) + _dot(x_in.astype(bf16), mi[...])
    y_ref[...] = jax.nn.gelu(y).astype(bf16).reshape(B, S5_NCH, S5_L * LANES)


def _s5_scan(u, wt, mi, mo, a16):
    L = S5_L
    wspec = lambda shape: pl.BlockSpec((None,) + shape, lambda m, r: (m,) + (0,) * len(shape))
    uspec = pl.BlockSpec((None, B, S5_NCH, L * LANES), lambda m, r: (m, 0, r, 0))
    return pl.pallas_call(
        _s5_scan_kernel,
        grid=(S5_M, T // L // S5_NCH),
        in_specs=[uspec,
                  wspec((L * LANES, L * LANES)), wspec((2 * S5_ST, L * LANES)),
                  wspec((L * LANES, 2 * S5_ST)), wspec((1, 2 * S5_ST))],
        out_specs=uspec,
        out_shape=jax.ShapeDtypeStruct((S5_M, B, T // L, L * LANES), bf16),
        scratch_shapes=[pltpu.VMEM((B, 2 * S5_ST), f32), pltpu.VMEM((2 * S5_ST // LANES, S5_ROWS, LANES), f32),
                        pltpu.VMEM((2 * S5_ST // LANES, S5_ROWS, LANES), f32)],
        compiler_params=pltpu.CompilerParams(
            dimension_semantics=("arbitrary", "arbitrary"), vmem_limit_bytes=VMEM_LIMIT),
        name="s5_scan",
    )(u, wt, mi, mo, a16)


def _mixer_out_kernel(x_ref, gla_ref, y_ref, g1_ref, sh2_ref, sc2_ref, gluw, glub, woa, wob,
                      mpg, fpg, rw_hi, rw_lo, rb, h1_ref, hn2_ref, pos_ref, tab_ref, gate_ref, cnt_ref, cnt_s, y_s):
    first = jnp.logical_and(pl.program_id(0) == 0, pl.program_id(1) == 0)

    @pl.when(first)
    def _():
        cnt_s[...] = jnp.zeros_like(cnt_s)

    for m in range(S5_M):
        for jj in range(S5_L):
            y_s[m, pl.ds(jj, TT // S5_L, stride=S5_L), :] = y_ref[m, :, jj * LANES:(jj + 1) * LANES].astype(f32)
    y = jnp.concatenate([y_s[m] for m in range(S5_M)], axis=1)
    z = _dot(y.astype(bf16), gluw[...]) + glub[...]
    s5 = y * jax.nn.sigmoid(z)
    mix = _dot(gla_ref[...], woa[...]) + _dot(s5.astype(bf16), wob[...])
    h1 = x_ref[...] + g1_ref[...] * (mix * _rms(mix) * mpg[...])
    hn2 = (h1 * _rms(h1) * fpg[...]) * (1.0 + sc2_ref[...]) + sh2_ref[...]
    for cc in range(ROW):
        h1_ref[pl.ds(cc, TT, stride=ROW), :] = h1[:, cc * LANES:(cc + 1) * LANES]
        hn2_ref[pl.ds(cc, TT, stride=ROW), :] = hn2[:, cc * LANES:(cc + 1) * LANES]
    x_hi = hn2.astype(bf16)
    x_lo = (hn2 - x_hi.astype(f32)).astype(bf16)
    logits = _dot(x_hi, rw_hi[...]) + (_dot(x_lo, rw_hi[...]) + _dot(x_hi, rw_lo[...])) + rb[...]

    lane = lax.broadcasted_iota(i32, (TT, LANES), 1).astype(f32)
    l = logits
    vals, idxs = [], []
    for _ in range(TOPK):
        m = jnp.max(l, axis=-1, keepdims=True)
        ix = jnp.min(jnp.where(l == m, lane, float(LANES)), axis=-1, keepdims=True)
        vals.append(m)
        idxs.append(ix)
        l = jnp.where(lane == ix, -jnp.inf, l)
    es = [jnp.exp(v - vals[0]) for v in vals]
    tot = es[0] + es[1] + es[2] + es[3]
    oh = jnp.zeros((TT, LANES), f32)
    for ix in idxs:
        oh = oh + (lane == ix).astype(f32)
    r = lax.broadcasted_iota(i32, (TT, TT), 0)
    c = lax.broadcasted_iota(i32, (TT, TT), 1)
    cum = _dot((r >= c).astype(bf16), oh.astype(bf16))
    lcnt = cum[TT - 1:TT, :]
    gbase = cnt_s[...]
    hi = jnp.floor(lcnt * (1.0 / 64.0))
    lo = lcnt - hi * 64.0
    upper = (lax.broadcasted_iota(i32, (LANES, LANES), 0) < lax.broadcasted_iota(i32, (LANES, LANES), 1)).astype(bf16)
    hi8 = jnp.broadcast_to(hi, (ROW, LANES)).astype(bf16)
    lo8 = jnp.broadcast_to(lo, (ROW, LANES)).astype(bf16)
    lbase = (64.0 * _dot(hi8, upper) + _dot(lo8, upper))[0:1, :]
    loc = lbase + (cum - oh)
    tmod = (lax.broadcasted_iota(i32, (TT, LANES), 0) % PG).astype(f32) * float(TOPK)
    spread = jnp.zeros((TT, LANES), f32)
    gate = jnp.zeros((TT, LANES), f32)
    for kk in range(TOPK):
        p = jnp.sum(jnp.where(lane == idxs[kk], loc, 0.0), axis=-1, keepdims=True)
        spread = jnp.where(lane == tmod + float(kk), p, spread)
        gate = jnp.where(lane == float(kk), es[kk] / tot, gate)
    s_hi = jnp.floor(spread * (1.0 / 64.0))
    s_lo = spread - s_hi * 64.0
    grp = (lax.broadcasted_iota(i32, (TT // PG, TT), 1) // PG
           == lax.broadcasted_iota(i32, (TT // PG, TT), 0)).astype(bf16)
    folded = 64.0 * _dot(grp, s_hi.astype(bf16)) + _dot(grp, s_lo.astype(bf16))
    pos_ref[...] = (folded * float(ROW)).astype(i32)
    sub = lax.broadcasted_iota(i32, (ROW, LANES), 0)
    tab = jnp.where(sub == 0, lcnt, jnp.where(sub == 1, lbase, jnp.where(sub == 2, gbase, 0.0)))
    tab_ref[...] = tab.astype(i32)
    gate_ref[...] = gate
    cnt = cnt_s[...] + cum[TT - 1:TT, :]
    cnt_s[...] = cnt
    cnt_ref[...] = cnt


def _mixer_out(x, gla, y, mod3, gluw, glub, woa, wob, mpg, fpg, rw_hi, rw_lo, rb):
    nt = T // TT
    full = lambda shape: pl.BlockSpec(shape, lambda b, t: (0,) * len(shape))
    tok = lambda w: pl.BlockSpec((None, TT, w), lambda b, t: (b, t, 0))
    slab = pl.BlockSpec((TT * ROW, LANES), lambda b, t: (b * nt + t, 0))
    slab_shape = jax.ShapeDtypeStruct((N * ROW, LANES), f32)
    yspec = pl.BlockSpec((S5_M, None, TT // S5_L, S5_L * LANES), lambda b, t: (0, b, t, 0))
    modspec = lambda j: pl.BlockSpec((None, 1, D), lambda b, t: (b, 0, j))
    return pl.pallas_call(
        _mixer_out_kernel,
        grid=(B, nt),
        in_specs=[tok(D), tok(512), yspec,
                  modspec(2), modspec(3), modspec(4),
                  full((512, 512)), full((1, 512)), full((512, D)), full((512, D)),
                  full((1, D)), full((1, D)), full((D, LANES)), full((D, LANES)), full((1, LANES))],
        out_specs=[slab, slab, pl.BlockSpec((TT // PG, LANES), lambda b, t: (b * nt + t, 0)),
                   pl.BlockSpec((None, ROW, LANES), lambda b, t: (b * nt + t, 0, 0)), tok(LANES), full((1, LANES))],
        out_shape=[slab_shape, slab_shape,
                   jax.ShapeDtypeStruct((N // PG, LANES), i32), jax.ShapeDtypeStruct((N // TT, ROW, LANES), i32),
                   jax.ShapeDtypeStruct((B, T, LANES), f32),
                   jax.ShapeDtypeStruct((1, LANES), f32)],
        scratch_shapes=[pltpu.VMEM((1, LANES), f32), pltpu.VMEM((S5_M, TT, LANES), f32)],
        compiler_params=pltpu.CompilerParams(
            dimension_semantics=("arbitrary", "arbitrary"), vmem_limit_bytes=VMEM_LIMIT),
        name="mixer_out",
    )(x, gla, y, mod3, mod3, mod3, gluw, glub, woa, wob, mpg, fpg, rw_hi, rw_lo, rb)


STAGE_ROWS = TT * TOPK * ROW


def _strip_copies(tab_ref, stage, region_hbm, sem, to_region):
    def expert(e, carry):
        n = tab_ref[0, e] * ROW

        @pl.when(n > 0)
        def _():
            local = stage.at[pl.ds(pl.multiple_of(tab_ref[1, e] * ROW, ROW), n), :]
            remote = region_hbm.at[pl.ds(pl.multiple_of((e * N + tab_ref[2, e]) * ROW, ROW), n), :]
            src, dst = (local, remote) if to_region else (remote, local)
            pltpu.make_async_copy(src, dst, sem).start()
        return carry
    lax.fori_loop(0, NE, expert, 0)


def _strip_wait(stage, region_hbm, sem):
    pltpu.make_async_copy(stage, region_hbm.at[pl.ds(0, STAGE_ROWS), :], sem).wait()


def _dispatch_kernel(pos_ref, tab_ref, x_ref, xs_hbm, stage, sem):
    i = pl.program_id(0)
    slot = lax.rem(i, 2)

    @pl.when(i >= 2)
    def _():
        _strip_wait(stage.at[slot], xs_hbm, sem.at[slot])

    def scatter(s):
        def group(g, carry):
            for jj in range(PG):
                v = x_ref[pl.ds(pl.multiple_of((g * PG + jj) * ROW, ROW), ROW), :]
                for kk in range(TOPK):
                    stage[s, pl.ds(pl.multiple_of(pos_ref[g, jj * TOPK + kk], ROW), ROW), :] = v
            return carry
        lax.fori_loop(0, TT // PG, group, 0)

    for s in range(2):
        pl.when(slot == s)(functools.partial(scatter, s))
    _strip_copies(tab_ref, stage.at[slot], xs_hbm, sem.at[slot], to_region=True)

    @pl.when(i == pl.num_programs(0) - 1)
    def _():
        _strip_wait(stage.at[1 - slot], xs_hbm, sem.at[1 - slot])
        _strip_wait(stage.at[slot], xs_hbm, sem.at[slot])


def _dispatch(pos, tab, hn2_rows):
    return pl.pallas_call(
        _dispatch_kernel,
        grid=(N // TT,),
        in_specs=[pl.BlockSpec((TT // PG, LANES), lambda i: (i, 0), memory_space=pltpu.SMEM),
                  pl.BlockSpec((None, ROW, LANES), lambda i: (i, 0, 0), memory_space=pltpu.SMEM),
                  pl.BlockSpec((TT * ROW, LANES), lambda i: (i, 0))],
        out_specs=pl.BlockSpec(memory_space=pl.ANY),
        out_shape=jax.ShapeDtypeStruct((NE * N * ROW, LANES), f32),
        scratch_shapes=[pltpu.VMEM((2, STAGE_ROWS, LANES), f32), pltpu.SemaphoreType.DMA((2,))],
        compiler_params=pltpu.CompilerParams(dimension_semantics=("arbitrary",), vmem_limit_bytes=VMEM_LIMIT),
        name="dispatch",
    )(pos, tab, hn2_rows)


def _experts_kernel(tb_ref, te_ref, tv_ref, tf_ref, tn_ref, ts_ref, nu_ref,
                    x_ref, wgu_hbm, bgu_ref, wd_hbm, bd_ref, y_ref, wgu_f, wd_f, wgu_b, wd_b, sem):
    i = pl.program_id(0)

    def fetch(e):
        return (pltpu.make_async_copy(wgu_hbm.at[e], wgu_f, sem.at[0]),
                pltpu.make_async_copy(wd_hbm.at[e], wd_f, sem.at[1]))

    def convert(s):
        wgu_b[s] = wgu_f[...].astype(bf16)
        wd_b[s] = wd_f[...].astype(bf16)

    def tile(s, h):
        base = h * MB * ROW
        xt = jnp.concatenate([x_ref[pl.ds(base + cc, MB, stride=ROW), :] for cc in range(ROW)], axis=1)
        valid = lax.broadcasted_iota(i32, (MB, 1), 0) < tv_ref[i] - h * MB
        xt = jnp.where(valid, xt, 0.0).astype(bf16)
        gu = _dot(xt, wgu_b[s]) + bgu_ref[...]
        g = jnp.minimum(gu[:, :DFF], LIMIT)
        up = jnp.clip(gu[:, DFF:], -LIMIT, LIMIT)
        act = (up + 1.0) * (g * jax.nn.sigmoid(ALPHA * g))
        yv = _dot(act.astype(bf16), wd_b[s]) + bd_ref[...]
        for cc in range(ROW):
            y_ref[pl.ds(base + cc, MB, stride=ROW), :] = yv[:, cc * LANES:(cc + 1) * LANES]

    @pl.when(i < nu_ref[0])
    def _():
        e = te_ref[i]
        s = ts_ref[i]
        nxt = tn_ref[i]
        first = (tf_ref[i] & 1) != 0
        last = (tf_ref[i] & 2) != 0

        @pl.when(i == 0)
        def _():
            for c in fetch(e):
                c.start()
            for c in fetch(e):
                c.wait()
            convert(s)

        @pl.when(jnp.logical_and(first, nxt >= 0))
        def _():
            for c in fetch(nxt):
                c.start()

        handoff = jnp.logical_and(last, nxt >= 0)

        @pl.when(handoff)
        def _():
            for c in fetch(nxt):
                c.wait()
            convert(1 - s)
            tile(s, 0)

        @pl.when(jnp.logical_not(handoff))
        def _():
            tile(s, 0)

        for h in range(1, EB):
            pl.when(tv_ref[i] > h * MB)(functools.partial(tile, s, h))


def _experts(tile_blk, tile_e, tile_nv, tile_flags, tile_next, tile_slot, nused, xs, w_gu, b_gu, w_down, b_down):
    blk = lambda i, tb, te, *_: (tb[i], 0)
    bsel = lambda i, tb, te, *_: (te[i], 0, 0)
    grid_spec = pltpu.PrefetchScalarGridSpec(
        num_scalar_prefetch=7,
        grid=(NB,),
        in_specs=[
            pl.BlockSpec((EB * MB * ROW, LANES), blk),
            pl.BlockSpec(memory_space=pl.ANY),
            pl.BlockSpec((None, 1, 2 * DFF), bsel),
            pl.BlockSpec(memory_space=pl.ANY),
            pl.BlockSpec((None, 1, D), bsel),
        ],
        out_specs=pl.BlockSpec((EB * MB * ROW, LANES), blk),
        scratch_shapes=[pltpu.VMEM((D, 2 * DFF), f32), pltpu.VMEM((DFF, D), f32),
                        pltpu.VMEM((2, D, 2 * DFF), bf16), pltpu.VMEM((2, DFF, D), bf16),
                        pltpu.SemaphoreType.DMA((2,))],
    )
    return pl.pallas_call(
        _experts_kernel,
        grid_spec=grid_spec,
        out_shape=jax.ShapeDtypeStruct((NE * N * ROW, LANES), f32),
        compiler_params=pltpu.CompilerParams(dimension_semantics=("arbitrary",), vmem_limit_bytes=VMEM_LIMIT),
        name="experts",
    )(tile_blk, tile_e, tile_nv, tile_flags, tile_next, tile_slot, nused, xs,
      w_gu, b_gu.reshape(NE, 1, 2 * DFF), w_down, b_down.reshape(NE, 1, D))


def _combine_kernel(pos_ref, tab_ref, tabn_ref, gate_ref, h1_ref, g2_ref, pg_ref, ys_hbm, o_ref,
                    stage, wbuf, rbuf, sem):
    j = pl.program_id(0)
    slot = lax.rem(j, 2)
    GT = PG
    GR = GT * ROW

    @pl.when(j == 0)
    def _():
        _strip_copies(tab_ref, stage.at[0], ys_hbm, sem.at[0], to_region=False)

    @pl.when(j + 1 < pl.num_programs(0))
    def _():
        _strip_copies(tabn_ref, stage.at[1 - slot], ys_hbm, sem.at[1 - slot], to_region=False)

    for kk in range(TOPK):
        wbuf[kk] = jnp.broadcast_to(gate_ref[:, kk:kk + 1], (TT, LANES))
    _strip_wait(stage.at[slot], ys_hbm, sem.at[slot])
    g2 = g2_ref[...]
    pg = pg_ref[...]

    def one_group(s, g):
        rows = pl.ds(pl.multiple_of(g * GR, GR), GR)
        parts = []
        for jj in range(GT):
            t = g * GT + jj
            acc = None
            for kk in range(TOPK):
                w = jnp.broadcast_to(wbuf[kk, pl.ds(t, 1), :], (ROW, LANES))
                term = stage[s, pl.ds(pl.multiple_of(pos_ref[g, jj * TOPK + kk], ROW), ROW), :] * w
                acc = term if acc is None else acc + term
            parts.append(acc)
        ff3 = jnp.concatenate(parts, axis=0).reshape(GT, ROW, LANES)
        ss = jnp.sum(jnp.sum(ff3 * ff3, axis=2, keepdims=True), axis=1, keepdims=True)
        rs = lax.rsqrt(ss * (1.0 / D) + EPS)
        out = h1_ref[rows, :].reshape(GT, ROW, LANES) + g2 * (ff3 * rs * pg)
        rbuf[rows, :] = out.reshape(GR, LANES)

    def gather(s):
        def pair(gg, carry):
            one_group(s, 2 * gg)
            one_group(s, 2 * gg + 1)
            return carry
        lax.fori_loop(0, TT // GT // 2, pair, 0)

    for s in range(2):
        pl.when(slot == s)(functools.partial(gather, s))
    for cc in range(ROW):
        o_ref[:, cc * LANES:(cc + 1) * LANES] = rbuf[pl.ds(cc, TT, stride=ROW), :]


def _combine(pos, tab, gates, h1_rows, mod4, pg, ys):
    nt = N // TT
    per_b = T // TT
    tabspec = lambda imap: pl.BlockSpec((None, ROW, LANES), imap, memory_space=pltpu.SMEM)
    return pl.pallas_call(
        _combine_kernel,
        grid=(nt,),
        in_specs=[pl.BlockSpec((TT // PG, LANES), lambda j: (j, 0), memory_space=pltpu.SMEM),
                  tabspec(lambda j: (j, 0, 0)), tabspec(lambda j: (jnp.minimum(j + 1, nt - 1), 0, 0)),
                  pl.BlockSpec((TT, LANES), lambda j: (j, 0)),
                  pl.BlockSpec((TT * ROW, LANES), lambda j: (j, 0)),
                  pl.BlockSpec((None, None, ROW, LANES), lambda j: (j // per_b, 5, 0, 0)),
                  pl.BlockSpec((ROW, LANES), lambda j: (0, 0)),
                  pl.BlockSpec(memory_space=pl.ANY)],
        out_specs=pl.BlockSpec((TT, D), lambda j: (j, 0)),
        out_shape=jax.ShapeDtypeStruct((N, D), f32),
        scratch_shapes=[pltpu.VMEM((2, STAGE_ROWS, LANES), f32), pltpu.VMEM((TOPK, TT, LANES), f32),
                        pltpu.VMEM((TT * ROW, LANES), f32), pltpu.SemaphoreType.DMA((2,))],
        compiler_params=pltpu.CompilerParams(dimension_semantics=("arbitrary",), vmem_limit_bytes=VMEM_LIMIT),
        name="combine",
    )(pos, tab, tab, gates, h1_rows, mod4, pg, ys)


def _pad_heads(w):
    lead = w.shape[:-1]
    w4 = w.reshape(lead + (GLA_H, GLA_DK))
    w4 = jnp.concatenate([w4, jnp.zeros_like(w4)], axis=-1)
    return w4.reshape(lead + (GLA_H * LANES,))


def kernel(x, c, ada_w, ada_b, mix_pre_g, mix_post_g, ffn_pre_g, ffn_post_g, w_in, w_alpha, b_alpha, gla_norm_g, s5_lambda_re, s5_lambda_im, s5_log_dt, s5_b_re, s5_b_im, s5_c_re, s5_c_im, s5_d, s5_glu_w, s5_glu_b, w_out, router_w, router_b, exp_w_gu, exp_b_gu, exp_w_down, exp_b_down):
    l = 0
    mod = _ada(c, ada_w[l], ada_b[l])
    mod3 = mod.reshape(B, 1, 6 * D)
    mod4 = mod.reshape(B, 6, ROW, LANES)

    w = w_in[l]
    o_q, o_k, o_v, o_g, o_a, o_u = 0, 256, 512, 1024, 1536, 1552
    wq = _pad_heads(w[:, o_q:o_k]).astype(bf16)
    wk = _pad_heads(w[:, o_k:o_v]).astype(bf16)
    wv = w[:, o_v:o_g].astype(bf16)
    wg = w[:, o_g:o_a].astype(bf16)
    wa = jnp.pad(w[:, o_a:o_u], ((0, 0), (0, LANES - LOWRANK))).astype(bf16)
    wu = w[:, o_u:].astype(bf16)
    walpha = jnp.pad(_pad_heads(w_alpha[l]), ((0, LANES - LOWRANK), (0, 0))).astype(bf16)
    balpha = _pad_heads(b_alpha[l]).reshape(1, GLA_H * LANES)

    gla, u = _mixer_in(x, mod3, mix_pre_g[l].reshape(1, D), wq, wk, wv, wg, wa, wu,
                       walpha, balpha, gla_norm_g[l].reshape(1, GLA_DV))

    wt, mi, mo, a16 = _s5_prep(s5_lambda_re[l], s5_lambda_im[l], s5_log_dt[l], s5_b_re[l], s5_b_im[l],
                               s5_c_re[l], s5_c_im[l], s5_d[l])
    y = _s5_scan(u, wt, mi, mo, a16)

    rw = jnp.pad(router_w[l], ((0, 0), (0, LANES - NE)))
    rw_hi = rw.astype(bf16)
    rw_lo = (rw - rw_hi.astype(f32)).astype(bf16)
    rb = jnp.pad(router_b[l], (0, LANES - NE), constant_values=-1e30).reshape(1, LANES)
    wo = w_out[l].astype(bf16)
    h1_rows, hn2_rows, pos, tab, gates, cnt = _mixer_out(
        x, gla, y, mod3, s5_glu_w[l].astype(bf16), s5_glu_b[l].reshape(1, 512), wo[:512], wo[512:],
        mix_post_g[l].reshape(1, D), ffn_pre_g[l].reshape(1, D), rw_hi, rw_lo, rb)

    counts = cnt[0, :NE].astype(i32)
    step_rows = EB * MB
    ntile = (counts + step_rows - 1) // step_rows
    tends = jnp.cumsum(ntile)
    nused = tends[-1]
    gi = jnp.minimum(jnp.arange(NB, dtype=i32), nused - 1)
    tile_e = jnp.sum((tends[None, :] <= gi[:, None]).astype(i32), axis=1)
    sel = tile_e[:, None] == jnp.arange(NE, dtype=i32)[None, :]
    pick = lambda v: jnp.sum(jnp.where(sel, v[None, :], 0), axis=1)
    tile_j = gi - pick(tends - ntile)
    tile_blk = tile_e * TPE + tile_j
    tile_nv = jnp.clip(pick(counts) - tile_j * step_rows, 0, step_rows)
    tile_flags = (tile_j == 0).astype(i32) + 2 * (tile_j == pick(ntile) - 1).astype(i32)
    used = ntile > 0
    eids = jnp.arange(NE, dtype=i32)
    later = jnp.logical_and(used[None, :], eids[None, :] > eids[:, None])
    nxt = jnp.min(jnp.where(later, eids[None, :], NE), axis=1)
    tile_next = pick(jnp.where(nxt < NE, nxt, -1))
    tile_slot = pick((jnp.cumsum(used.astype(i32)) - 1) % 2)

    xs = _dispatch(pos, tab, hn2_rows)
    ys = _experts(tile_blk, tile_e, tile_nv, tile_flags, tile_next, tile_slot, nused.reshape(1), xs,
                  exp_w_gu[l], exp_b_gu[l], exp_w_down[l], exp_b_down[l])
    out = _combine(pos, tab, gates.reshape(N, LANES), h1_rows, mod4, ffn_post_g[l].reshape(ROW, LANES), ys)
    return out.reshape(B, T, D)
```

```python
import functools
import math

import jax
import jax.numpy as jnp
from jax import lax
from jax.experimental import pallas as pl
from jax.experimental.pallas import tpu as pltpu

f32 = jnp.float32
bf16 = jnp.bfloat16
i32 = jnp.int32

D = 1024
B = 8
T = 2048
N = B * T
GLA_H = 4
GLA_DV = 128
GLA_DK = 64
GLA_TAU = 16.0
GLA_CHUNK = 64
LOWRANK = 16
S5_W = 512
S5_CH = 16
S5_G = 32
S5_P = 64
NE = 32
TOPK = 4
DFF = 1024
ALPHA = 1.702
LIMIT = 7.0
EPS = 1e-6

LANES = 128
SUBLANES = 8
VMEM_LIMIT = 56 * 1024 * 1024

TT = 512
S5_L = 16
S5_M = S5_W // LANES
S5_GPT = LANES // S5_CH
S5_ST = S5_GPT * S5_P
S5_NCH = 32
S5_ROWS = B * S5_NCH
ROW = SUBLANES
MB = 256
EB = 4
NB = (N * TOPK) // (EB * MB) + NE
TPE = N // (EB * MB)
PG = 16


def _dot(a, b):
    return jnp.dot(a, b, preferred_element_type=f32)


def _dot_t(a, b, ca, cb):
    return lax.dot_general(a, b, (((ca,), (cb,)), ((), ())), preferred_element_type=f32)


def _rms(x):
    return lax.rsqrt(jnp.mean(x * x, axis=-1, keepdims=True) + EPS)


def _split(a):
    hi = a.astype(bf16)
    return hi, (a - hi.astype(f32)).astype(bf16)


def _dot3(a, b):
    return _dot(a[0], b[0]) + (_dot(a[1], b[0]) + _dot(a[0], b[1]))


def _ada_kernel(c_ref, w_ref, b_ref, o_ref):
    c = c_ref[...]
    s = (c * jax.nn.sigmoid(c)).astype(bf16)
    o_ref[...] = _dot(s, w_ref[...].astype(bf16)) + b_ref[...]


def _ada(c, w, b):
    return pl.pallas_call(
        _ada_kernel,
        grid=(6,),
        in_specs=[
            pl.BlockSpec((B, D), lambda j: (0, 0)),
            pl.BlockSpec((D, D), lambda j: (0, j)),
            pl.BlockSpec((1, D), lambda j: (0, j)),
        ],
        out_specs=pl.BlockSpec((B, D), lambda j: (0, j)),
        out_shape=jax.ShapeDtypeStruct((B, 6 * D), f32),
        name="ada",
    )(c, w, b.reshape(1, 6 * D))


def _mixer_in_kernel(x_ref, sc_ref, sh_ref, g_ref, wq, wk, wv, wg, wa, wu, walpha, balpha, gng,
                     gla_ref, u_ref, st_s, u_s):
    t = pl.program_id(1)

    @pl.when(t == 0)
    def _():
        st_s[...] = jnp.zeros_like(st_s)

    x = x_ref[...]
    hn = (x * _rms(x) * g_ref[...]) * (1.0 + sc_ref[...]) + sh_ref[...]
    hb = hn.astype(bf16)
    u = _dot(hb, wu[...])
    for m in range(S5_M):
        u_s[m] = u[:, m * LANES:(m + 1) * LANES]
        for i in range(S5_L):
            u_ref[m, :, i * LANES:(i + 1) * LANES] = u_s[m, pl.ds(i, TT // S5_L, stride=S5_L), :].astype(bf16)

    C = GLA_CHUNK
    NC = TT // C
    row = lax.broadcasted_iota(i32, (TT, TT), 0)
    col = lax.broadcasted_iota(i32, (TT, TT), 1)
    causal = jnp.logical_and(row >= col, (row // C) == (col // C))
    tri = causal.astype(bf16)
    a_lr = _dot(hb, wa[...]).astype(bf16)
    la = jax.nn.log_sigmoid(_dot(a_lr, walpha[...]) + balpha[...]) * (1.0 / GLA_TAU)
    la_hi = la.astype(bf16)
    la_lo = (la - la_hi.astype(f32)).astype(bf16)
    bc = _dot(tri, la_hi) + _dot(tri, la_lo)
    bl = jnp.broadcast_to(bc.reshape(NC, C, GLA_H * LANES)[:, C - 1:C, :],
                          (NC, C, GLA_H * LANES)).reshape(TT, GLA_H * LANES)
    q = _dot(hb, wq[...]) * (GLA_DK ** -0.5)
    k = _dot(hb, wk[...])
    qd = (q * jnp.exp(bc)).astype(bf16)
    ki = (k * jnp.exp(-bc)).astype(bf16)
    ke = (k * jnp.exp(bl - bc)).astype(bf16)
    dec = jnp.exp(bl)
    vb = _dot(hb, wv[...]).astype(bf16)
    og = _dot(hb, wg[...])
    gn = gng[...]
    for h in range(GLA_H):
        sl = slice(h * LANES, (h + 1) * LANES)
        qh, kih, keh, vh = qd[:, sl], ki[:, sl], ke[:, sl], vb[:, sl]
        sc = jnp.where(causal, _dot_t(qh, kih, 1, 1), 0.0).astype(bf16)
        o_intra = _dot(sc, vh)
        st = st_s[h]
        o_parts = []
        for c in range(NC):
            rs = slice(c * C, (c + 1) * C)
            o_parts.append(_dot_t(qh[rs], st.astype(bf16), 1, 1))
            st = st * dec[c * C:c * C + 1, sl] + _dot_t(vh[rs], keh[rs], 0, 0)
        st_s[h] = st
        o = o_intra + jnp.concatenate(o_parts, axis=0)
        on = o * _rms(o) * gn
        ogh = og[:, sl]
        gla_ref[:, sl] = (on * (ogh * jax.nn.sigmoid(ogh))).astype(bf16)


def _mixer_in(x, mod3, g, wq, wk, wv, wg, wa, wu, walpha, balpha, gng):
    nt = T // TT
    full = lambda shape: pl.BlockSpec(shape, lambda b, t: (0,) * len(shape))
    u_shape = jax.ShapeDtypeStruct((S5_M, B, T // S5_L, S5_L * LANES), bf16)
    u_spec = pl.BlockSpec((S5_M, None, TT // S5_L, S5_L * LANES), lambda b, t: (0, b, t, 0))
    return pl.pallas_call(
        _mixer_in_kernel,
        grid=(B, nt),
        in_specs=[
            pl.BlockSpec((None, TT, D), lambda b, t: (b, t, 0)),
            pl.BlockSpec((None, 1, D), lambda b, t: (b, 0, 1)),
            pl.BlockSpec((None, 1, D), lambda b, t: (b, 0, 0)),
            full((1, D)),
            full((D, 512)), full((D, 512)), full((D, 512)), full((D, 512)),
            full((D, LANES)), full((D, 512)),
            full((LANES, 512)), full((1, 512)), full((1, LANES)),
        ],
        out_specs=[pl.BlockSpec((None, TT, 512), lambda b, t: (b, t, 0)), u_spec],
        out_shape=[jax.ShapeDtypeStruct((B, T, 512), bf16), u_shape],
        scratch_shapes=[pltpu.VMEM((GLA_H, GLA_DV, LANES), f32), pltpu.VMEM((S5_M, TT, LANES), f32)],
        compiler_params=pltpu.CompilerParams(
            dimension_semantics=("arbitrary", "arbitrary"), vmem_limit_bytes=VMEM_LIMIT),
        name="mixer_in",
    )(x, mod3, mod3, g, wq, wk, wv, wg, wa, wu, walpha, balpha, gng)


def _cpow(xr, xi, d):
    mag = jnp.exp(xr * d)
    return mag * jnp.cos(xi * d), mag * jnp.sin(xi * d)


def _s5_prep_kernel(lr_r, li_r, ld_r, lr_c, li_c, ld_c, btr, bti, ctr, cti, d_r,
                    wt_ref, mi_ref, mo_ref, a_ref, kt_s):
    L = S5_L
    lr = lr_r[...]
    li = li_r[...]
    dt = jnp.exp(ld_r[...])
    xr, xi = lr * dt, li * dt
    ar, ai = _cpow(xr, xi, 1.0)
    den = lr * lr + li * li
    fr = ((ar - 1.0) * lr + ai * li) / den
    fi = (ai * lr - (ar - 1.0) * li) / den
    br, bi = btr[...], bti[...]
    bbr = fr * br - fi * bi
    bbi = fr * bi + fi * br
    cr, ci = ctr[...], cti[...]
    cr_s, ci_s = _split(cr), _split(ci)
    pr, pi = jnp.ones_like(ar), jnp.zeros_like(ai)
    for d in range(L):
        xdr = bbr * pr - bbi * pi
        xdi = bbr * pi + bbi * pr
        i = L - 1 - d
        mo_ref[i * LANES:(i + 1) * LANES, :S5_ST] = xdr.astype(bf16)
        mo_ref[i * LANES:(i + 1) * LANES, S5_ST:] = xdi.astype(bf16)
        kt = _dot3(_split(xdr), cr_s) - _dot3(_split(xdi), ci_s)
        if d == 0:
            r = lax.broadcasted_iota(i32, (LANES, LANES), 0)
            c = lax.broadcasted_iota(i32, (LANES, LANES), 1)
            kt = kt + jnp.where(r == c, d_r[...], 0.0)
        kt_s[d] = kt.astype(bf16)
        pr, pi = pr * ar - pi * ai, pr * ai + pi * ar
    a_ref[:, :S5_ST] = pr
    a_ref[:, S5_ST:] = pi
    zero = jnp.zeros((LANES, LANES), bf16)
    for i in range(L):
        for j in range(L):
            wt_ref[i * LANES:(i + 1) * LANES, j * LANES:(j + 1) * LANES] = kt_s[j - i] if j >= i else zero
    lrc = lr_c[...]
    lic = li_c[...]
    dtc = jnp.exp(ld_c[...])
    acr, aci = _cpow(lrc * dtc, lic * dtc, 1.0)
    pr, pi = acr, aci
    for j in range(L):
        mi_ref[:S5_ST, j * LANES:(j + 1) * LANES] = (cr * pr - ci * pi).astype(bf16)
        mi_ref[S5_ST:, j * LANES:(j + 1) * LANES] = (-(cr * pi + ci * pr)).astype(bf16)
        pr, pi = pr * acr - pi * aci, pr * aci + pi * acr


def _s5_prep(lr, li, ld, b_re, b_im, c_re, c_im, dvec):
    G, P, H, M, GPT = S5_G, S5_P, S5_CH, S5_M, S5_GPT
    eye = jnp.eye(GPT, dtype=f32)

    def rows(v):
        return v.reshape(M, 1, S5_ST)

    def cols(v):
        return v.reshape(M, S5_ST, 1)

    ldp = jnp.broadcast_to(ld[:, None], (G, P))

    def bt(b):
        b4 = b.reshape(M, GPT, P, H)
        return jnp.einsum('mgph,gk->mkhgp', b4, eye).reshape(M, LANES, S5_ST)

    def ct(c):
        c4 = c.reshape(M, GPT, H, P)
        return jnp.einsum('mghp,gk->mgpkh', c4, eye).reshape(M, S5_ST, LANES)

    L = S5_L
    mspec = lambda shape: pl.BlockSpec((None,) + shape, lambda m: (m,) + (0,) * len(shape))
    return pl.pallas_call(
        _s5_prep_kernel,
        grid=(M,),
        in_specs=[mspec((1, S5_ST))] * 3 + [mspec((S5_ST, 1))] * 3
        + [mspec((LANES, S5_ST))] * 2 + [mspec((S5_ST, LANES))] * 2 + [mspec((1, LANES))],
        out_specs=[mspec((L * LANES, L * LANES)), mspec((2 * S5_ST, L * LANES)),
                   mspec((L * LANES, 2 * S5_ST)), mspec((1, 2 * S5_ST))],
        out_shape=[jax.ShapeDtypeStruct((M, L * LANES, L * LANES), bf16),
                   jax.ShapeDtypeStruct((M, 2 * S5_ST, L * LANES), bf16),
                   jax.ShapeDtypeStruct((M, L * LANES, 2 * S5_ST), bf16),
                   jax.ShapeDtypeStruct((M, 1, 2 * S5_ST), f32)],
        scratch_shapes=[pltpu.VMEM((L, LANES, LANES), bf16)],
        compiler_params=pltpu.CompilerParams(dimension_semantics=("arbitrary",), vmem_limit_bytes=VMEM_LIMIT),
        name="s5_prep",
    )(rows(lr), rows(li), rows(ldp), cols(lr), cols(li), cols(ldp),
      bt(b_re), bt(b_im), ct(c_re), ct(c_im), dvec.reshape(M, 1, LANES))


def _s5_scan_kernel(u_ref, wt, mi, mo, a_ref, y_ref, xs, s_s, x_s):
    r = pl.program_id(1)

    @pl.when(r == 0)
    def _():
        xs[...] = jnp.zeros_like(xs)

    u = u_ref[...].reshape(S5_ROWS, S5_L * LANES)
    s = _dot(u, mo[...])
    NT = 2 * S5_ST // LANES
    for c in range(NT):
        s_s[c] = s[:, c * LANES:(c + 1) * LANES]
    a = [a_ref[:, c * LANES:(c + 1) * LANES] for c in range(NT)]

    def step(n, x):
        rows = pl.ds(n, B, stride=S5_NCH)
        new = []
        for c in range(NT):
            x_s[c, rows, :] = x[c]
        for c in range(NT // 2):
            ar, ai, xr, xi = a[c], a[NT // 2 + c], x[c], x[NT // 2 + c]
            new.append((ar * xr - ai * xi + s_s[c, rows, :], ar * xi + ai * xr + s_s[NT // 2 + c, rows, :]))
        return tuple(p[0] for p in new) + tuple(p[1] for p in new)

    x = tuple(xs[:, c * LANES:(c + 1) * LANES] for c in range(NT))
    for n in range(S5_NCH):
        x = step(n, x)
    for c in range(NT):
        xs[:, c * LANES:(c + 1) * LANES] = x[c]
    x_in = jnp.concatenate([x_s[c] for c in range(NT)], axis=1)
    y = _dot(u, wt[...]) + _dot(x_in.astype(bf16), mi[...])
    y_ref[...] = jax.nn.gelu(y).astype(bf16).reshape(B, S5_NCH, S5_L * LANES)


def _s5_scan(u, wt, mi, mo, a16):
    L = S5_L
    wspec = lambda shape: pl.BlockSpec((None,) + shape, lambda m, r: (m,) + (0,) * len(shape))
    uspec = pl.BlockSpec((None, B, S5_NCH, L * LANES), lambda m, r: (m, 0, r, 0))
    return pl.pallas_call(
        _s5_scan_kernel,
        grid=(S5_M, T // L // S5_NCH),
        in_specs=[uspec,
                  wspec((L * LANES, L * LANES)), wspec((2 * S5_ST, L * LANES)),
                  wspec((L * LANES, 2 * S5_ST)), wspec((1, 2 * S5_ST))],
        out_specs=uspec,
        out_shape=jax.ShapeDtypeStruct((S5_M, B, T // L, L * LANES), bf16),
        scratch_shapes=[pltpu.VMEM((B, 2 * S5_ST), f32), pltpu.VMEM((2 * S5_ST // LANES, S5_ROWS, LANES), f32),
                        pltpu.VMEM((2 * S5_ST // LANES, S5_ROWS, LANES), f32)],
        compiler_params=pltpu.CompilerParams(
            dimension_semantics=("arbitrary", "arbitrary"), vmem_limit_bytes=VMEM_LIMIT),
        name="s5_scan",
    )(u, wt, mi, mo, a16)


def _mixer_out_kernel(x_ref, gla_ref, y_ref, g1_ref, sh2_ref, sc2_ref, gluw, glub, woa, wob,
                      mpg, fpg, rw_hi, rw_lo, rb, h1_ref, hn2_ref, pos_ref, tab_ref, gate_ref, cnt_ref, cnt_s, y_s):
    first = jnp.logical_and(pl.program_id(0) == 0, pl.program_id(1) == 0)

    @pl.when(first)
    def _():
        cnt_s[...] = jnp.zeros_like(cnt_s)

    for m in range(S5_M):
        for jj in range(S5_L):
            y_s[m, pl.ds(jj, TT // S5_L, stride=S5_L), :] = y_ref[m, :, jj * LANES:(jj + 1) * LANES].astype(f32)
    y = jnp.concatenate([y_s[m] for m in range(S5_M)], axis=1)
    z = _dot(y.astype(bf16), gluw[...]) + glub[...]
    s5 = y * jax.nn.sigmoid(z)
    mix = _dot(gla_ref[...], woa[...]) + _dot(s5.astype(bf16), wob[...])
    h1 = x_ref[...] + g1_ref[...] * (mix * _rms(mix) * mpg[...])
    hn2 = (h1 * _rms(h1) * fpg[...]) * (1.0 + sc2_ref[...]) + sh2_ref[...]
    for cc in range(ROW):
        h1_ref[pl.ds(cc, TT, stride=ROW), :] = h1[:, cc * LANES:(cc + 1) * LANES]
        hn2_ref[pl.ds(cc, TT, stride=ROW), :] = hn2[:, cc * LANES:(cc + 1) * LANES]
    x_hi = hn2.astype(bf16)
    x_lo = (hn2 - x_hi.astype(f32)).astype(bf16)
    logits = _dot(x_hi, rw_hi[...]) + (_dot(x_lo, rw_hi[...]) + _dot(x_hi, rw_lo[...])) + rb[...]

    lane = lax.broadcasted_iota(i32, (TT, LANES), 1).astype(f32)
    l = logits
    vals, idxs = [], []
    for _ in range(TOPK):
        m = jnp.max(l, axis=-1, keepdims=True)
        ix = jnp.min(jnp.where(l == m, lane, float(LANES)), axis=-1, keepdims=True)
        vals.append(m)
        idxs.append(ix)
        l = jnp.where(lane == ix, -jnp.inf, l)
    es = [jnp.exp(v - vals[0]) for v in vals]
    tot = es[0] + es[1] + es[2] + es[3]
    oh = jnp.zeros((TT, LANES), f32)
    for ix in idxs:
        oh = oh + (lane == ix).astype(f32)
    r = lax.broadcasted_iota(i32, (TT, TT), 0)
    c = lax.broadcasted_iota(i32, (TT, TT), 1)
    cum = _dot((r >= c).astype(bf16), oh.astype(bf16))
    lcnt = cum[TT - 1:TT, :]
    gbase = cnt_s[...]
    hi = jnp.floor(lcnt * (1.0 / 64.0))
    lo = lcnt - hi * 64.0
    upper = (lax.broadcasted_iota(i32, (LANES, LANES), 0) < lax.broadcasted_iota(i32, (LANES, LANES), 1)).astype(bf16)
    hi8 = jnp.broadcast_to(hi, (ROW, LANES)).astype(bf16)
    lo8 = jnp.broadcast_to(lo, (ROW, LANES)).astype(bf16)
    lbase = (64.0 * _dot(hi8, upper) + _dot(lo8, upper))[0:1, :]
    loc = lbase + (cum - oh)
    tmod = (lax.broadcasted_iota(i32, (TT, LANES), 0) % PG).astype(f32) * float(TOPK)
    spread = jnp.zeros((TT, LANES), f32)
    gate = jnp.zeros((TT, LANES), f32)
    for kk in range(TOPK):
        p = jnp.sum(jnp.where(lane == idxs[kk], loc, 0.0), axis=-1, keepdims=True)
        spread = jnp.where(lane == tmod + float(kk), p, spread)
        gate = jnp.where(lane == float(kk), es[kk] / tot, gate)
    s_hi = jnp.floor(spread * (1.0 / 64.0))
    s_lo = spread - s_hi * 64.0
    grp = (lax.broadcasted_iota(i32, (TT // PG, TT), 1) // PG
           == lax.broadcasted_iota(i32, (TT // PG, TT), 0)).astype(bf16)
    folded = 64.0 * _dot(grp, s_hi.astype(bf16)) + _dot(grp, s_lo.astype(bf16))
    pos_ref[...] = (folded * float(ROW)).astype(i32)
    sub = lax.broadcasted_iota(i32, (ROW, LANES), 0)
    tab = jnp.where(sub == 0, lcnt, jnp.where(sub == 1, lbase, jnp.where(sub == 2, gbase, 0.0)))
    tab_ref[...] = tab.astype(i32)
    gate_ref[...] = gate
    cnt = cnt_s[...] + cum[TT - 1:TT, :]
    cnt_s[...] = cnt
    cnt_ref[...] = cnt


def _mixer_out(x, gla, y, mod3, gluw, glub, woa, wob, mpg, fpg, rw_hi, rw_lo, rb):
    nt = T // TT
    full = lambda shape: pl.BlockSpec(shape, lambda b, t: (0,) * len(shape))
    tok = lambda w: pl.BlockSpec((None, TT, w), lambda b, t: (b, t, 0))
    slab = pl.BlockSpec((TT * ROW, LANES), lambda b, t: (b * nt + t, 0))
    slab_shape = jax.ShapeDtypeStruct((N * ROW, LANES), f32)
    yspec = pl.BlockSpec((S5_M, None, TT // S5_L, S5_L * LANES), lambda b, t: (0, b, t, 0))
    modspec = lambda j: pl.BlockSpec((None, 1, D), lambda b, t: (b, 0, j))
    return pl.pallas_call(
        _mixer_out_kernel,
        grid=(B, nt),
        in_specs=[tok(D), tok(512), yspec,
                  modspec(2), modspec(3), modspec(4),
                  full((512, 512)), full((1, 512)), full((512, D)), full((512, D)),
                  full((1, D)), full((1, D)), full((D, LANES)), full((D, LANES)), full((1, LANES))],
        out_specs=[slab, slab, pl.BlockSpec((TT // PG, LANES), lambda b, t: (b * nt + t, 0)),
                   pl.BlockSpec((None, ROW, LANES), lambda b, t: (b * nt + t, 0, 0)), tok(LANES), full((1, LANES))],
        out_shape=[slab_shape, slab_shape,
                   jax.ShapeDtypeStruct((N // PG, LANES), i32), jax.ShapeDtypeStruct((N // TT, ROW, LANES), i32),
                   jax.ShapeDtypeStruct((B, T, LANES), f32),
                   jax.ShapeDtypeStruct((1, LANES), f32)],
        scratch_shapes=[pltpu.VMEM((1, LANES), f32), pltpu.VMEM((S5_M, TT, LANES), f32)],
        compiler_params=pltpu.CompilerParams(
            dimension_semantics=("arbitrary", "arbitrary"), vmem_limit_bytes=VMEM_LIMIT),
        name="mixer_out",
    )(x, gla, y, mod3, mod3, mod3, gluw, glub, woa, wob, mpg, fpg, rw_hi, rw_lo, rb)


STAGE_ROWS = TT * TOPK * ROW


def _strip_copies(tab_ref, stage, region_hbm, sem, to_region):
    def expert(e, carry):
        n = tab_ref[0, e] * ROW

        @pl.when(n > 0)
        def _():
            local = stage.at[pl.ds(pl.multiple_of(tab_ref[1, e] * ROW, ROW), n), :]
            remote = region_hbm.at[pl.ds(pl.multiple_of((e * N + tab_ref[2, e]) * ROW, ROW), n), :]
            src, dst = (local, remote) if to_region else (remote, local)
            pltpu.make_async_copy(src, dst, sem).start()
        return carry
    lax.fori_loop(0, NE, expert, 0)


def _strip_wait(stage, region_hbm, sem):
    pltpu.make_async_copy(stage, region_hbm.at[pl.ds(0, STAGE_ROWS), :], sem).wait()


def _dispatch_kernel(pos_ref, tab_ref, x_ref, xs_hbm, stage, sem):
    i = pl.program_id(0)
    slot = lax.rem(i, 2)

    @pl.when(i >= 2)
    def _():
        _strip_wait(stage.at[slot], xs_hbm, sem.at[slot])

    def scatter(s):
        def group(g, carry):
            for jj in range(PG):
                v = x_ref[pl.ds(pl.multiple_of((g * PG + jj) * ROW, ROW), ROW), :]
                for kk in range(TOPK):
                    stage[s, pl.ds(pl.multiple_of(pos_ref[g, jj * TOPK + kk], ROW), ROW), :] = v
            return carry
        lax.fori_loop(0, TT // PG, group, 0)

    for s in range(2):
        pl.when(slot == s)(functools.partial(scatter, s))
    _strip_copies(tab_ref, stage.at[slot], xs_hbm, sem.at[slot], to_region=True)

    @pl.when(i == pl.num_programs(0) - 1)
    def _():
        _strip_wait(stage.at[1 - slot], xs_hbm, sem.at[1 - slot])
        _strip_wait(stage.at[slot], xs_hbm, sem.at[slot])


def _dispatch(pos, tab, hn2_rows):
    return pl.pallas_call(
        _dispatch_kernel,
        grid=(N // TT,),
        in_specs=[pl.BlockSpec((TT // PG, LANES), lambda i: (i, 0), memory_space=pltpu.SMEM),
                  pl.BlockSpec((None, ROW, LANES), lambda i: (i, 0, 0), memory_space=pltpu.SMEM),
                  pl.BlockSpec((TT * ROW, LANES), lambda i: (i, 0))],
        out_specs=pl.BlockSpec(memory_space=pl.ANY),
        out_shape=jax.ShapeDtypeStruct((NE * N * ROW, LANES), f32),
        scratch_shapes=[pltpu.VMEM((2, STAGE_ROWS, LANES), f32), pltpu.SemaphoreType.DMA((2,))],
        compiler_params=pltpu.CompilerParams(dimension_semantics=("arbitrary",), vmem_limit_bytes=VMEM_LIMIT),
        name="dispatch",
    )(pos, tab, hn2_rows)


def _experts_kernel(tb_ref, te_ref, tv_ref, tf_ref, tn_ref, ts_ref, nu_ref,
                    x_ref, wgu_hbm, bgu_ref, wd_hbm, bd_ref, y_ref, wgu_f, wd_f, wgu_b, wd_b, sem):
    i = pl.program_id(0)

    def fetch(e):
        return (pltpu.make_async_copy(wgu_hbm.at[e], wgu_f, sem.at[0]),
                pltpu.make_async_copy(wd_hbm.at[e], wd_f, sem.at[1]))

    def convert(s):
        wgu_b[s] = wgu_f[...].astype(bf16)
        wd_b[s] = wd_f[...].astype(bf16)

    def tile(s, h, nt=1):
        rows = nt * MB
        base = h * MB * ROW
        xt = jnp.concatenate([x_ref[pl.ds(base + cc, rows, stride=ROW), :] for cc in range(ROW)], axis=1)
        valid = lax.broadcasted_iota(i32, (rows, 1), 0) < tv_ref[i] - h * MB
        xt = jnp.where(valid, xt, 0.0).astype(bf16)
        gu = _dot(xt, wgu_b[s]) + bgu_ref[...]
        g = jnp.minimum(gu[:, :DFF], LIMIT)
        up = jnp.clip(gu[:, DFF:], -LIMIT, LIMIT)
        act = (up + 1.0) * (g * jax.nn.sigmoid(ALPHA * g))
        yv = _dot(act.astype(bf16), wd_b[s]) + bd_ref[...]
        for cc in range(ROW):
            y_ref[pl.ds(base + cc, rows, stride=ROW), :] = yv[:, cc * LANES:(cc + 1) * LANES]

    def tiles_from(s, h):
        nv = tv_ref[i]
        pl.when(nv > (h + 1) * MB)(functools.partial(tile, s, h, 2))
        pl.when(jnp.logical_and(nv > h * MB, nv <= (h + 1) * MB))(functools.partial(tile, s, h, 1))

    @pl.when(i < nu_ref[0])
    def _():
        e = te_ref[i]
        s = ts_ref[i]
        nxt = tn_ref[i]
        first = (tf_ref[i] & 1) != 0
        last = (tf_ref[i] & 2) != 0

        @pl.when(i == 0)
        def _():
            for c in fetch(e):
                c.start()
            for c in fetch(e):
                c.wait()
            convert(s)

        @pl.when(jnp.logical_and(first, nxt >= 0))
        def _():
            for c in fetch(nxt):
                c.start()

        handoff = jnp.logical_and(last, nxt >= 0)

        @pl.when(handoff)
        def _():
            for c in fetch(nxt):
                c.wait()
            convert(1 - s)

        for h in range(0, EB, 2):
            tiles_from(s, h)


def _experts(tile_blk, tile_e, tile_nv, tile_flags, tile_next, tile_slot, nused, xs, w_gu, b_gu, w_down, b_down):
    blk = lambda i, tb, te, *_: (tb[i], 0)
    bsel = lambda i, tb, te, *_: (te[i], 0, 0)
    grid_spec = pltpu.PrefetchScalarGridSpec(
        num_scalar_prefetch=7,
        grid=(NB,),
        in_specs=[
            pl.BlockSpec((EB * MB * ROW, LANES), blk),
            pl.BlockSpec(memory_space=pl.ANY),
            pl.BlockSpec((None, 1, 2 * DFF), bsel),
            pl.BlockSpec(memory_space=pl.ANY),
            pl.BlockSpec((None, 1, D), bsel),
        ],
        out_specs=pl.BlockSpec((EB * MB * ROW, LANES), blk),
        scratch_shapes=[pltpu.VMEM((D, 2 * DFF), f32), pltpu.VMEM((DFF, D), f32),
                        pltpu.VMEM((2, D, 2 * DFF), bf16), pltpu.VMEM((2, DFF, D), bf16),
                        pltpu.SemaphoreType.DMA((2,))],
    )
    return pl.pallas_call(
        _experts_kernel,
        grid_spec=grid_spec,
        out_shape=jax.ShapeDtypeStruct((NE * N * ROW, LANES), f32),
        compiler_params=pltpu.CompilerParams(dimension_semantics=("arbitrary",), vmem_limit_bytes=VMEM_LIMIT),
        name="experts",
    )(tile_blk, tile_e, tile_nv, tile_flags, tile_next, tile_slot, nused, xs,
      w_gu, b_gu.reshape(NE, 1, 2 * DFF), w_down, b_down.reshape(NE, 1, D))


def _combine_kernel(pos_ref, tab_ref, tabn_ref, gate_ref, h1_ref, g2_ref, pg_ref, ys_hbm, o_ref,
                    stage, wbuf, rbuf, sem):
    j = pl.program_id(0)
    slot = lax.rem(j, 2)
    GT = PG
    GR = GT * ROW

    @pl.when(j == 0)
    def _():
        _strip_copies(tab_ref, stage.at[0], ys_hbm, sem.at[0], to_region=False)

    @pl.when(j + 1 < pl.num_programs(0))
    def _():
        _strip_copies(tabn_ref, stage.at[1 - slot], ys_hbm, sem.at[1 - slot], to_region=False)

    for kk in range(TOPK):
        wbuf[kk] = jnp.broadcast_to(gate_ref[:, kk:kk + 1], (TT, LANES))
    _strip_wait(stage.at[slot], ys_hbm, sem.at[slot])
    g2 = g2_ref[...]
    pg = pg_ref[...]

    def one_group(s, g):
        rows = pl.ds(pl.multiple_of(g * GR, GR), GR)
        parts = []
        for jj in range(GT):
            t = g * GT + jj
            acc = None
            for kk in range(TOPK):
                w = jnp.broadcast_to(wbuf[kk, pl.ds(t, 1), :], (ROW, LANES))
                term = stage[s, pl.ds(pl.multiple_of(pos_ref[g, jj * TOPK + kk], ROW), ROW), :] * w
                acc = term if acc is None else acc + term
            parts.append(acc)
        ff3 = jnp.concatenate(parts, axis=0).reshape(GT, ROW, LANES)
        ss = jnp.sum(jnp.sum(ff3 * ff3, axis=2, keepdims=True), axis=1, keepdims=True)
        rs = lax.rsqrt(ss * (1.0 / D) + EPS)
        out = h1_ref[rows, :].reshape(GT, ROW, LANES) + g2 * (ff3 * rs * pg)
        rbuf[rows, :] = out.reshape(GR, LANES)

    def gather(s):
        def pair(gg, carry):
            one_group(s, 2 * gg)
            one_group(s, 2 * gg + 1)
            return carry
        lax.fori_loop(0, TT // GT // 2, pair, 0)

    for s in range(2):
        pl.when(slot == s)(functools.partial(gather, s))
    for cc in range(ROW):
        o_ref[:, cc * LANES:(cc + 1) * LANES] = rbuf[pl.ds(cc, TT, stride=ROW), :]


def _combine(pos, tab, gates, h1_rows, mod4, pg, ys):
    nt = N // TT
    per_b = T // TT
    tabspec = lambda imap: pl.BlockSpec((None, ROW, LANES), imap, memory_space=pltpu.SMEM)
    return pl.pallas_call(
        _combine_kernel,
        grid=(nt,),
        in_specs=[pl.BlockSpec((TT // PG, LANES), lambda j: (j, 0), memory_space=pltpu.SMEM),
                  tabspec(lambda j: (j, 0, 0)), tabspec(lambda j: (jnp.minimum(j + 1, nt - 1), 0, 0)),
                  pl.BlockSpec((TT, LANES), lambda j: (j, 0)),
                  pl.BlockSpec((TT * ROW, LANES), lambda j: (j, 0)),
                  pl.BlockSpec((None, None, ROW, LANES), lambda j: (j // per_b, 5, 0, 0)),
                  pl.BlockSpec((ROW, LANES), lambda j: (0, 0)),
                  pl.BlockSpec(memory_space=pl.ANY)],
        out_specs=pl.BlockSpec((TT, D), lambda j: (j, 0)),
        out_shape=jax.ShapeDtypeStruct((N, D), f32),
        scratch_shapes=[pltpu.VMEM((2, STAGE_ROWS, LANES), f32), pltpu.VMEM((TOPK, TT, LANES), f32),
                        pltpu.VMEM((TT * ROW, LANES), f32), pltpu.SemaphoreType.DMA((2,))],
        compiler_params=pltpu.CompilerParams(dimension_semantics=("arbitrary",), vmem_limit_bytes=VMEM_LIMIT),
        name="combine",
    )(pos, tab, tab, gates, h1_rows, mod4, pg, ys)


def _pad_heads(w):
    lead = w.shape[:-1]
    w4 = w.reshape(lead + (GLA_H, GLA_DK))
    w4 = jnp.concatenate([w4, jnp.zeros_like(w4)], axis=-1)
    return w4.reshape(lead + (GLA_H * LANES,))


def kernel(x, c, ada_w, ada_b, mix_pre_g, mix_post_g, ffn_pre_g, ffn_post_g, w_in, w_alpha, b_alpha, gla_norm_g, s5_lambda_re, s5_lambda_im, s5_log_dt, s5_b_re, s5_b_im, s5_c_re, s5_c_im, s5_d, s5_glu_w, s5_glu_b, w_out, router_w, router_b, exp_w_gu, exp_b_gu, exp_w_down, exp_b_down):
    l = 0
    mod = _ada(c, ada_w[l], ada_b[l])
    mod3 = mod.reshape(B, 1, 6 * D)
    mod4 = mod.reshape(B, 6, ROW, LANES)

    w = w_in[l]
    o_q, o_k, o_v, o_g, o_a, o_u = 0, 256, 512, 1024, 1536, 1552
    wq = _pad_heads(w[:, o_q:o_k]).astype(bf16)
    wk = _pad_heads(w[:, o_k:o_v]).astype(bf16)
    wv = w[:, o_v:o_g].astype(bf16)
    wg = w[:, o_g:o_a].astype(bf16)
    wa = jnp.pad(w[:, o_a:o_u], ((0, 0), (0, LANES - LOWRANK))).astype(bf16)
    wu = w[:, o_u:].astype(bf16)
    walpha = jnp.pad(_pad_heads(w_alpha[l]), ((0, LANES - LOWRANK), (0, 0))).astype(bf16)
    balpha = _pad_heads(b_alpha[l]).reshape(1, GLA_H * LANES)

    gla, u = _mixer_in(x, mod3, mix_pre_g[l].reshape(1, D), wq, wk, wv, wg, wa, wu,
                       walpha, balpha, gla_norm_g[l].reshape(1, GLA_DV))

    wt, mi, mo, a16 = _s5_prep(s5_lambda_re[l], s5_lambda_im[l], s5_log_dt[l], s5_b_re[l], s5_b_im[l],
                               s5_c_re[l], s5_c_im[l], s5_d[l])
    y = _s5_scan(u, wt, mi, mo, a16)

    rw = jnp.pad(router_w[l], ((0, 0), (0, LANES - NE)))
    rw_hi = rw.astype(bf16)
    rw_lo = (rw - rw_hi.astype(f32)).astype(bf16)
    rb = jnp.pad(router_b[l], (0, LANES - NE), constant_values=-1e30).reshape(1, LANES)
    wo = w_out[l].astype(bf16)
    h1_rows, hn2_rows, pos, tab, gates, cnt = _mixer_out(
        x, gla, y, mod3, s5_glu_w[l].astype(bf16), s5_glu_b[l].reshape(1, 512), wo[:512], wo[512:],
        mix_post_g[l].reshape(1, D), ffn_pre_g[l].reshape(1, D), rw_hi, rw_lo, rb)

    counts = cnt[0, :NE].astype(i32)
    step_rows = EB * MB
    ntile = (counts + step_rows - 1) // step_rows
    tends = jnp.cumsum(ntile)
    nused = tends[-1]
    gi = jnp.minimum(jnp.arange(NB, dtype=i32), nused - 1)
    tile_e = jnp.sum((tends[None, :] <= gi[:, None]).astype(i32), axis=1)
    sel = tile_e[:, None] == jnp.arange(NE, dtype=i32)[None, :]
    pick = lambda v: jnp.sum(jnp.where(sel, v[None, :], 0), axis=1)
    tile_j = gi - pick(tends - ntile)
    tile_blk = tile_e * TPE + tile_j
    tile_nv = jnp.clip(pick(counts) - tile_j * step_rows, 0, step_rows)
    tile_flags = (tile_j == 0).astype(i32) + 2 * (tile_j == pick(ntile) - 1).astype(i32)
    used = ntile > 0
    eids = jnp.arange(NE, dtype=i32)
    later = jnp.logical_and(used[None, :], eids[None, :] > eids[:, None])
    nxt = jnp.min(jnp.where(later, eids[None, :], NE), axis=1)
    tile_next = pick(jnp.where(nxt < NE, nxt, -1))
    tile_slot = pick((jnp.cumsum(used.astype(i32)) - 1) % 2)

    xs = _dispatch(pos, tab, hn2_rows)
    ys = _experts(tile_blk, tile_e, tile_nv, tile_flags, tile_next, tile_slot, nused.reshape(1), xs,
                  exp_w_gu[l], exp_b_gu[l], exp_w_down[l], exp_b_down[l])
    out = _combine(pos, tab, gates.reshape(N, LANES), h1_rows, mod4, ffn_post_g[l].reshape(ROW, LANES), ys)
    return out.reshape(B, T, D)
```

```python
import functools
import math

import jax
import jax.numpy as jnp
from jax import lax
from jax.experimental import pallas as pl
from jax.experimental.pallas import tpu as pltpu

f32 = jnp.float32
bf16 = jnp.bfloat16
i32 = jnp.int32

D = 1024
B = 8
T = 2048
N = B * T
GLA_H = 4
GLA_DV = 128
GLA_DK = 64
GLA_TAU = 16.0
GLA_CHUNK = 64
LOWRANK = 16
S5_W = 512
S5_CH = 16
S5_G = 32
S5_P = 64
NE = 32
TOPK = 4
DFF = 1024
ALPHA = 1.702
LIMIT = 7.0
EPS = 1e-6

LANES = 128
SUBLANES = 8
VMEM_LIMIT = 56 * 1024 * 1024

TT = 512
S5_L = 16
S5_M = S5_W // LANES
S5_GPT = LANES // S5_CH
S5_ST = S5_GPT * S5_P
S5_NCH = 32
S5_ROWS = B * S5_NCH
ROW = SUBLANES
MB = 256
EB = 4
NB = (N * TOPK) // (EB * MB) + NE
TPE = N // (EB * MB)
PG = 16


def _dot(a, b):
    return jnp.dot(a, b, preferred_element_type=f32)


def _dot_t(a, b, ca, cb):
    return lax.dot_general(a, b, (((ca,), (cb,)), ((), ())), preferred_element_type=f32)


def _rms(x):
    return lax.rsqrt(jnp.mean(x * x, axis=-1, keepdims=True) + EPS)


def _split(a):
    hi = a.astype(bf16)
    return hi, (a - hi.astype(f32)).astype(bf16)


def _dot3(a, b):
    return _dot(a[0], b[0]) + (_dot(a[1], b[0]) + _dot(a[0], b[1]))


def _ada_kernel(c_ref, w_ref, b_ref, o_ref):
    c = c_ref[...]
    s = (c * jax.nn.sigmoid(c)).astype(bf16)
    o_ref[...] = _dot(s, w_ref[...].astype(bf16)) + b_ref[...]


def _ada(c, w, b):
    return pl.pallas_call(
        _ada_kernel,
        grid=(6,),
        in_specs=[
            pl.BlockSpec((B, D), lambda j: (0, 0)),
            pl.BlockSpec((D, D), lambda j: (0, j)),
            pl.BlockSpec((1, D), lambda j: (0, j)),
        ],
        out_specs=pl.BlockSpec((B, D), lambda j: (0, j)),
        out_shape=jax.ShapeDtypeStruct((B, 6 * D), f32),
        name="ada",
    )(c, w, b.reshape(1, 6 * D))


def _mixer_in_kernel(x_ref, sc_ref, sh_ref, g_ref, wq, wk, wv, wg, wa, wu, walpha, balpha, gng, cm_ref,
                     gla_ref, u_ref, st_s, u_s):
    t = pl.program_id(1)

    @pl.when(t == 0)
    def _():
        st_s[...] = jnp.zeros_like(st_s)

    x = x_ref[...]
    hn = (x * _rms(x) * g_ref[...]) * (1.0 + sc_ref[...]) + sh_ref[...]
    hb = hn.astype(bf16)
    u = _dot(hb, wu[...])

    C = GLA_CHUNK
    NC = TT // C
    causal = cm_ref[...] > 0.5
    tri = cm_ref[...].astype(bf16)
    a_lr = _dot(hb, wa[...]).astype(bf16)
    la = jax.nn.log_sigmoid(_dot(a_lr, walpha[...]) + balpha[...]) * (1.0 / GLA_TAU)
    la_hi = la.astype(bf16)
    la_lo = (la - la_hi.astype(f32)).astype(bf16)
    bc = _dot(tri, la_hi) + _dot(tri, la_lo)
    bl = jnp.broadcast_to(bc.reshape(NC, C, GLA_H * LANES)[:, C - 1:C, :],
                          (NC, C, GLA_H * LANES)).reshape(TT, GLA_H * LANES)
    q = _dot(hb, wq[...]) * (GLA_DK ** -0.5)
    k = _dot(hb, wk[...])
    qd = (q * jnp.exp(bc)).astype(bf16)
    ki = (k * jnp.exp(-bc)).astype(bf16)
    ke = k * jnp.exp(bl - bc)
    dec = jnp.exp(bl)
    vb = _dot(hb, wv[...]).astype(bf16)
    og = _dot(hb, wg[...])
    gn = gng[...]
    for h in range(GLA_H):
        sl = slice(h * LANES, (h + 1) * LANES)
        qh, kih, keh, vh = qd[:, sl], ki[:, sl], ke[:, sl], vb[:, sl]
        sc = jnp.where(causal, _dot_t(qh, kih, 1, 1), 0.0).astype(bf16)
        o_intra = _dot(sc, vh)
        kehb = keh.astype(bf16)
        deltas = [_dot_t(vh[c * C:(c + 1) * C], kehb[c * C:(c + 1) * C], 0, 0) for c in range(NC)]
        st = st_s[h]
        sts = []
        for c in range(NC):
            sts.append(st.astype(bf16))
            st = st * dec[c * C:c * C + 1, sl] + deltas[c]
        st_s[h] = st
        o_inter = jnp.concatenate([_dot_t(qh[c * C:(c + 1) * C], sts[c], 1, 1) for c in range(NC)], axis=0)
        o = o_intra + o_inter
        on = o * _rms(o) * gn
        ogh = og[:, sl]
        gla_ref[:, sl] = (on * (ogh * jax.nn.sigmoid(ogh))).astype(bf16)
        u_s[h] = u[:, sl]
        for i in range(S5_L):
            u_ref[h, :, i * LANES:(i + 1) * LANES] = u_s[h, pl.ds(i, TT // S5_L, stride=S5_L), :].astype(bf16)


def _mixer_in(x, mod3, g, wq, wk, wv, wg, wa, wu, walpha, balpha, gng):
    assert S5_M == GLA_H
    nt = T // TT
    full = lambda shape: pl.BlockSpec(shape, lambda b, t: (0,) * len(shape))
    u_shape = jax.ShapeDtypeStruct((S5_M, B, T // S5_L, S5_L * LANES), bf16)
    u_spec = pl.BlockSpec((S5_M, None, TT // S5_L, S5_L * LANES), lambda b, t: (0, b, t, 0))
    tok = jnp.arange(TT, dtype=i32)
    same = (tok[:, None] // GLA_CHUNK) == (tok[None, :] // GLA_CHUNK)
    cmask = jnp.logical_and(tok[:, None] >= tok[None, :], same).astype(f32)
    return pl.pallas_call(
        _mixer_in_kernel,
        grid=(B, nt),
        in_specs=[
            pl.BlockSpec((None, TT, D), lambda b, t: (b, t, 0)),
            pl.BlockSpec((None, 1, D), lambda b, t: (b, 0, 1)),
            pl.BlockSpec((None, 1, D), lambda b, t: (b, 0, 0)),
            full((1, D)),
            full((D, 512)), full((D, 512)), full((D, 512)), full((D, 512)),
            full((D, LANES)), full((D, 512)),
            full((LANES, 512)), full((1, 512)), full((1, LANES)),
            full((TT, TT)),
        ],
        out_specs=[pl.BlockSpec((None, TT, 512), lambda b, t: (b, t, 0)), u_spec],
        out_shape=[jax.ShapeDtypeStruct((B, T, 512), bf16), u_shape],
        scratch_shapes=[pltpu.VMEM((GLA_H, GLA_DV, LANES), f32), pltpu.VMEM((S5_M, TT, LANES), f32)],
        compiler_params=pltpu.CompilerParams(
            dimension_semantics=("arbitrary", "arbitrary"), vmem_limit_bytes=VMEM_LIMIT),
        name="mixer_in",
    )(x, mod3, mod3, g, wq, wk, wv, wg, wa, wu, walpha, balpha, gng, cmask)


def _cpow(xr, xi, d):
    mag = jnp.exp(xr * d)
    return mag * jnp.cos(xi * d), mag * jnp.sin(xi * d)


def _s5_prep_kernel(lr_r, li_r, ld_r, lr_c, li_c, ld_c, btr, bti, ctr, cti, d_r,
                    wt_ref, mi_ref, mo_ref, a_ref, kt_s):
    L = S5_L
    lr = lr_r[...]
    li = li_r[...]
    dt = jnp.exp(ld_r[...])
    xr, xi = lr * dt, li * dt
    ar, ai = _cpow(xr, xi, 1.0)
    den = lr * lr + li * li
    fr = ((ar - 1.0) * lr + ai * li) / den
    fi = (ai * lr - (ar - 1.0) * li) / den
    br, bi = btr[...], bti[...]
    bbr = fr * br - fi * bi
    bbi = fr * bi + fi * br
    cr, ci = ctr[...], cti[...]
    cr_s, ci_s = _split(cr), _split(ci)
    pr, pi = jnp.ones_like(ar), jnp.zeros_like(ai)
    for d in range(L):
        xdr = bbr * pr - bbi * pi
        xdi = bbr * pi + bbi * pr
        i = L - 1 - d
        mo_ref[i * LANES:(i + 1) * LANES, :S5_ST] = xdr.astype(bf16)
        mo_ref[i * LANES:(i + 1) * LANES, S5_ST:] = xdi.astype(bf16)
        kt = _dot3(_split(xdr), cr_s) - _dot3(_split(xdi), ci_s)
        if d == 0:
            r = lax.broadcasted_iota(i32, (LANES, LANES), 0)
            c = lax.broadcasted_iota(i32, (LANES, LANES), 1)
            kt = kt + jnp.where(r == c, d_r[...], 0.0)
        kt_s[d] = kt.astype(bf16)
        pr, pi = pr * ar - pi * ai, pr * ai + pi * ar
    a_ref[:, :S5_ST] = pr
    a_ref[:, S5_ST:] = pi
    zero = jnp.zeros((LANES, LANES), bf16)
    for i in range(L):
        for j in range(L):
            wt_ref[i * LANES:(i + 1) * LANES, j * LANES:(j + 1) * LANES] = kt_s[j - i] if j >= i else zero
    lrc = lr_c[...]
    lic = li_c[...]
    dtc = jnp.exp(ld_c[...])
    acr, aci = _cpow(lrc * dtc, lic * dtc, 1.0)
    pr, pi = acr, aci
    for j in range(L):
        mi_ref[:S5_ST, j * LANES:(j + 1) * LANES] = (cr * pr - ci * pi).astype(bf16)
        mi_ref[S5_ST:, j * LANES:(j + 1) * LANES] = (-(cr * pi + ci * pr)).astype(bf16)
        pr, pi = pr * acr - pi * aci, pr * aci + pi * acr


def _s5_prep(lr, li, ld, b_re, b_im, c_re, c_im, dvec):
    G, P, H, M, GPT = S5_G, S5_P, S5_CH, S5_M, S5_GPT
    eye = jnp.eye(GPT, dtype=f32)

    def rows(v):
        return v.reshape(M, 1, S5_ST)

    def cols(v):
        return v.reshape(M, S5_ST, 1)

    ldp = jnp.broadcast_to(ld[:, None], (G, P))

    def bt(b):
        b4 = b.reshape(M, GPT, P, H)
        return jnp.einsum('mgph,gk->mkhgp', b4, eye).reshape(M, LANES, S5_ST)

    def ct(c):
        c4 = c.reshape(M, GPT, H, P)
        return jnp.einsum('mghp,gk->mgpkh', c4, eye).reshape(M, S5_ST, LANES)

    L = S5_L
    mspec = lambda shape: pl.BlockSpec((None,) + shape, lambda m: (m,) + (0,) * len(shape))
    return pl.pallas_call(
        _s5_prep_kernel,
        grid=(M,),
        in_specs=[mspec((1, S5_ST))] * 3 + [mspec((S5_ST, 1))] * 3
        + [mspec((LANES, S5_ST))] * 2 + [mspec((S5_ST, LANES))] * 2 + [mspec((1, LANES))],
        out_specs=[mspec((L * LANES, L * LANES)), mspec((2 * S5_ST, L * LANES)),
                   mspec((L * LANES, 2 * S5_ST)), mspec((1, 2 * S5_ST))],
        out_shape=[jax.ShapeDtypeStruct((M, L * LANES, L * LANES), bf16),
                   jax.ShapeDtypeStruct((M, 2 * S5_ST, L * LANES), bf16),
                   jax.ShapeDtypeStruct((M, L * LANES, 2 * S5_ST), bf16),
                   jax.ShapeDtypeStruct((M, 1, 2 * S5_ST), f32)],
        scratch_shapes=[pltpu.VMEM((L, LANES, LANES), bf16)],
        compiler_params=pltpu.CompilerParams(dimension_semantics=("arbitrary",), vmem_limit_bytes=VMEM_LIMIT),
        name="s5_prep",
    )(rows(lr), rows(li), rows(ldp), cols(lr), cols(li), cols(ldp),
      bt(b_re), bt(b_im), ct(c_re), ct(c_im), dvec.reshape(M, 1, LANES))


def _s5_scan_kernel(u_ref, wt, mi, mo, a_ref, y_ref, xs, s_s, x_s):
    r = pl.program_id(1)

    @pl.when(r == 0)
    def _():
        xs[...] = jnp.zeros_like(xs)

    u = u_ref[...].reshape(S5_ROWS, S5_L * LANES)
    s = _dot(u, mo[...])
    NT = 2 * S5_ST // LANES
    for c in range(NT):
        s_s[c] = s[:, c * LANES:(c + 1) * LANES]
    a = [a_ref[:, c * LANES:(c + 1) * LANES] for c in range(NT)]

    def step(n, x):
        rows = pl.ds(n, B, stride=S5_NCH)
        new = []
        for c in range(NT):
            x_s[c, rows, :] = x[c]
        for c in range(NT // 2):
            ar, ai, xr, xi = a[c], a[NT // 2 + c], x[c], x[NT // 2 + c]
            new.append((ar * xr - ai * xi + s_s[c, rows, :], ar * xi + ai * xr + s_s[NT // 2 + c, rows, :]))
        return tuple(p[0] for p in new) + tuple(p[1] for p in new)

    x = tuple(xs[:, c * LANES:(c + 1) * LANES] for c in range(NT))
    for n in range(S5_NCH):
        x = step(n, x)
    for c in range(NT):
        xs[:, c * LANES:(c + 1) * LANES] = x[c]
    x_in = jnp.concatenate([x_s[c] for c in range(NT)], axis=1)
    y = _dot(u, wt[...]) + _dot(x_in.astype(bf16), mi[...])
    y_ref[...] = jax.nn.gelu(y).astype(bf16).reshape(B, S5_NCH, S5_L * LANES)


def _s5_scan(u, wt, mi, mo, a16):
    L = S5_L
    wspec = lambda shape: pl.BlockSpec((None,) + shape, lambda m, r: (m,) + (0,) * len(shape))
    uspec = pl.BlockSpec((None, B, S5_NCH, L * LANES), lambda m, r: (m, 0, r, 0))
    return pl.pallas_call(
        _s5_scan_kernel,
        grid=(S5_M, T // L // S5_NCH),
        in_specs=[uspec,
                  wspec((L * LANES, L * LANES)), wspec((2 * S5_ST, L * LANES)),
                  wspec((L * LANES, 2 * S5_ST)), wspec((1, 2 * S5_ST))],
        out_specs=uspec,
        out_shape=jax.ShapeDtypeStruct((S5_M, B, T // L, L * LANES), bf16),
        scratch_shapes=[pltpu.VMEM((B, 2 * S5_ST), f32), pltpu.VMEM((2 * S5_ST // LANES, S5_ROWS, LANES), f32),
                        pltpu.VMEM((2 * S5_ST // LANES, S5_ROWS, LANES), f32)],
        compiler_params=pltpu.CompilerParams(
            dimension_semantics=("arbitrary", "arbitrary"), vmem_limit_bytes=VMEM_LIMIT),
        name="s5_scan",
    )(u, wt, mi, mo, a16)


def _mixer_out_kernel(x_ref, gla_ref, y_ref, g1_ref, sh2_ref, sc2_ref, gluw, glub, woa, wob,
                      mpg, fpg, rw_hi, rw_lo, rb, tri_ref, upper_ref, grp_ref,
                      h1_ref, hn2_ref, pos_ref, tab_ref, gate_ref, cnt_ref, cnt_s, y_s):
    first = jnp.logical_and(pl.program_id(0) == 0, pl.program_id(1) == 0)

    @pl.when(first)
    def _():
        cnt_s[...] = jnp.zeros_like(cnt_s)

    for m in range(S5_M):
        for jj in range(S5_L):
            y_s[m, pl.ds(jj, TT // S5_L, stride=S5_L), :] = y_ref[m, :, jj * LANES:(jj + 1) * LANES].astype(f32)
    y = jnp.concatenate([y_s[m] for m in range(S5_M)], axis=1)
    z = _dot(y.astype(bf16), gluw[...]) + glub[...]
    s5 = y * jax.nn.sigmoid(z)
    mix = _dot(gla_ref[...], woa[...]) + _dot(s5.astype(bf16), wob[...])
    h1 = x_ref[...] + g1_ref[...] * (mix * _rms(mix) * mpg[...])
    hn2 = (h1 * _rms(h1) * fpg[...]) * (1.0 + sc2_ref[...]) + sh2_ref[...]
    for cc in range(ROW):
        h1_ref[pl.ds(cc, TT, stride=ROW), :] = h1[:, cc * LANES:(cc + 1) * LANES]
        hn2_ref[pl.ds(cc, TT, stride=ROW), :] = hn2[:, cc * LANES:(cc + 1) * LANES]
    x_hi = hn2.astype(bf16)
    x_lo = (hn2 - x_hi.astype(f32)).astype(bf16)
    logits = _dot(x_hi, rw_hi[...]) + (_dot(x_lo, rw_hi[...]) + _dot(x_hi, rw_lo[...])) + rb[...]

    lane = lax.broadcasted_iota(i32, (TT, LANES), 1).astype(f32)
    l = logits
    vals, idxs = [], []
    for _ in range(TOPK):
        m = jnp.max(l, axis=-1, keepdims=True)
        ix = jnp.min(jnp.where(l == m, lane, float(LANES)), axis=-1, keepdims=True)
        vals.append(m)
        idxs.append(ix)
        l = jnp.where(lane == ix, -jnp.inf, l)
    es = [jnp.exp(v - vals[0]) for v in vals]
    tot = es[0] + es[1] + es[2] + es[3]
    oh = jnp.zeros((TT, LANES), f32)
    for ix in idxs:
        oh = oh + (lane == ix).astype(f32)
    cum = _dot(tri_ref[...], oh.astype(bf16))
    lcnt = cum[TT - 1:TT, :]
    gbase = cnt_s[...]
    hi = jnp.floor(lcnt * (1.0 / 64.0))
    lo = lcnt - hi * 64.0
    upper = upper_ref[...]
    hi8 = jnp.broadcast_to(hi, (ROW, LANES)).astype(bf16)
    lo8 = jnp.broadcast_to(lo, (ROW, LANES)).astype(bf16)
    lbase = (64.0 * _dot(hi8, upper) + _dot(lo8, upper))[0:1, :]
    loc = lbase + (cum - oh)
    tmod = (lax.broadcasted_iota(i32, (TT, LANES), 0) % PG).astype(f32) * float(TOPK)
    spread = jnp.zeros((TT, LANES), f32)
    gate = jnp.zeros((TT, LANES), f32)
    for kk in range(TOPK):
        p = jnp.sum(jnp.where(lane == idxs[kk], loc, 0.0), axis=-1, keepdims=True)
        spread = jnp.where(lane == tmod + float(kk), p, spread)
        gate = jnp.where(lane == float(kk), es[kk] / tot, gate)
    s_hi = jnp.floor(spread * (1.0 / 64.0))
    s_lo = spread - s_hi * 64.0
    grp = grp_ref[...]
    folded = 64.0 * _dot(grp, s_hi.astype(bf16)) + _dot(grp, s_lo.astype(bf16))
    pos_ref[...] = (folded * float(ROW)).astype(i32)
    sub = lax.broadcasted_iota(i32, (ROW, LANES), 0)
    tab = jnp.where(sub == 0, lcnt, jnp.where(sub == 1, lbase, jnp.where(sub == 2, gbase, 0.0)))
    tab_ref[...] = tab.astype(i32)
    gate_ref[...] = gate
    cnt = cnt_s[...] + cum[TT - 1:TT, :]
    cnt_s[...] = cnt
    cnt_ref[...] = cnt


def _mixer_out(x, gla, y, mod3, gluw, glub, woa, wob, mpg, fpg, rw_hi, rw_lo, rb):
    nt = T // TT
    full = lambda shape: pl.BlockSpec(shape, lambda b, t: (0,) * len(shape))
    tok = lambda w: pl.BlockSpec((None, TT, w), lambda b, t: (b, t, 0))
    slab = pl.BlockSpec((TT * ROW, LANES), lambda b, t: (b * nt + t, 0))
    slab_shape = jax.ShapeDtypeStruct((N * ROW, LANES), f32)
    yspec = pl.BlockSpec((S5_M, None, TT // S5_L, S5_L * LANES), lambda b, t: (0, b, t, 0))
    modspec = lambda j: pl.BlockSpec((None, 1, D), lambda b, t: (b, 0, j))
    tix = jnp.arange(TT, dtype=i32)
    lanes = jnp.arange(LANES, dtype=i32)
    tri = (tix[:, None] >= tix[None, :]).astype(bf16)
    upper = (lanes[:, None] < lanes[None, :]).astype(bf16)
    grp = (tix[None, :] // PG == jnp.arange(TT // PG, dtype=i32)[:, None]).astype(bf16)
    return pl.pallas_call(
        _mixer_out_kernel,
        grid=(B, nt),
        in_specs=[tok(D), tok(512), yspec,
                  modspec(2), modspec(3), modspec(4),
                  full((512, 512)), full((1, 512)), full((512, D)), full((512, D)),
                  full((1, D)), full((1, D)), full((D, LANES)), full((D, LANES)), full((1, LANES)),
                  full((TT, TT)), full((LANES, LANES)), full((TT // PG, TT))],
        out_specs=[slab, slab, pl.BlockSpec((TT // PG, LANES), lambda b, t: (b * nt + t, 0)),
                   pl.BlockSpec((None, ROW, LANES), lambda b, t: (b * nt + t, 0, 0)), tok(LANES), full((1, LANES))],
        out_shape=[slab_shape, slab_shape,
                   jax.ShapeDtypeStruct((N // PG, LANES), i32), jax.ShapeDtypeStruct((N // TT, ROW, LANES), i32),
                   jax.ShapeDtypeStruct((B, T, LANES), f32),
                   jax.ShapeDtypeStruct((1, LANES), f32)],
        scratch_shapes=[pltpu.VMEM((1, LANES), f32), pltpu.VMEM((S5_M, TT, LANES), f32)],
        compiler_params=pltpu.CompilerParams(
            dimension_semantics=("arbitrary", "arbitrary"), vmem_limit_bytes=VMEM_LIMIT),
        name="mixer_out",
    )(x, gla, y, mod3, mod3, mod3, gluw, glub, woa, wob, mpg, fpg, rw_hi, rw_lo, rb, tri, upper, grp)


STAGE_ROWS = TT * TOPK * ROW


def _strip_copies(tab_ref, stage, region_hbm, sem, to_region):
    def expert(e, carry):
        n = tab_ref[0, e] * ROW

        @pl.when(n > 0)
        def _():
            local = stage.at[pl.ds(pl.multiple_of(tab_ref[1, e] * ROW, ROW), n), :]
            remote = region_hbm.at[pl.ds(pl.multiple_of((e * N + tab_ref[2, e]) * ROW, ROW), n), :]
            src, dst = (local, remote) if to_region else (remote, local)
            pltpu.make_async_copy(src, dst, sem).start()
        return carry
    lax.fori_loop(0, NE, expert, 0)


def _strip_wait(stage, region_hbm, sem):
    pltpu.make_async_copy(stage, region_hbm.at[pl.ds(0, STAGE_ROWS), :], sem).wait()


def _dispatch_kernel(pos_ref, tab_ref, x_ref, xs_hbm, stage, sem):
    i = pl.program_id(0)
    slot = lax.rem(i, 2)

    @pl.when(i >= 2)
    def _():
        _strip_wait(stage.at[slot], xs_hbm, sem.at[slot])

    def scatter(s):
        def group(g, carry):
            for jj in range(PG):
                v = x_ref[pl.ds(pl.multiple_of((g * PG + jj) * ROW, ROW), ROW), :]
                for kk in range(TOPK):
                    stage[s, pl.ds(pl.multiple_of(pos_ref[g, jj * TOPK + kk], ROW), ROW), :] = v
            return carry
        lax.fori_loop(0, TT // PG, group, 0)

    for s in range(2):
        pl.when(slot == s)(functools.partial(scatter, s))
    _strip_copies(tab_ref, stage.at[slot], xs_hbm, sem.at[slot], to_region=True)

    @pl.when(i == pl.num_programs(0) - 1)
    def _():
        _strip_wait(stage.at[1 - slot], xs_hbm, sem.at[1 - slot])
        _strip_wait(stage.at[slot], xs_hbm, sem.at[slot])


def _dispatch(pos, tab, hn2_rows):
    return pl.pallas_call(
        _dispatch_kernel,
        grid=(N // TT,),
        in_specs=[pl.BlockSpec((TT // PG, LANES), lambda i: (i, 0), memory_space=pltpu.SMEM),
                  pl.BlockSpec((None, ROW, LANES), lambda i: (i, 0, 0), memory_space=pltpu.SMEM),
                  pl.BlockSpec((TT * ROW, LANES), lambda i: (i, 0))],
        out_specs=pl.BlockSpec(memory_space=pl.ANY),
        out_shape=jax.ShapeDtypeStruct((NE * N * ROW, LANES), f32),
        scratch_shapes=[pltpu.VMEM((2, STAGE_ROWS, LANES), f32), pltpu.SemaphoreType.DMA((2,))],
        compiler_params=pltpu.CompilerParams(dimension_semantics=("arbitrary",), vmem_limit_bytes=VMEM_LIMIT),
        name="dispatch",
    )(pos, tab, hn2_rows)


def _experts_kernel(tb_ref, te_ref, tv_ref, tf_ref, tn_ref, ts_ref, nu_ref,
                    x_ref, wgu_hbm, bgu_ref, wd_hbm, bd_ref, y_ref, wgu_f, wd_f, wgu_b, wd_b, sem):
    i = pl.program_id(0)

    def fetch(e):
        return (pltpu.make_async_copy(wgu_hbm.at[e], wgu_f, sem.at[0]),
                pltpu.make_async_copy(wd_hbm.at[e], wd_f, sem.at[1]))

    def convert(s):
        wgu_b[s] = wgu_f[...].astype(bf16)
        wd_b[s] = wd_f[...].astype(bf16)

    def tile(s, h, nt=1):
        rows = nt * MB
        base = h * MB * ROW
        xt = jnp.concatenate([x_ref[pl.ds(base + cc, rows, stride=ROW), :] for cc in range(ROW)], axis=1)
        valid = lax.broadcasted_iota(i32, (rows, 1), 0) < tv_ref[i] - h * MB
        xt = jnp.where(valid, xt, 0.0).astype(bf16)
        gu = _dot(xt, wgu_b[s]) + bgu_ref[...]
        g = jnp.minimum(gu[:, :DFF], LIMIT)
        up = jnp.clip(gu[:, DFF:], -LIMIT, LIMIT)
        act = (up + 1.0) * (g * jax.nn.sigmoid(ALPHA * g))
        yv = _dot(act.astype(bf16), wd_b[s]) + bd_ref[...]
        for cc in range(ROW):
            y_ref[pl.ds(base + cc, rows, stride=ROW), :] = yv[:, cc * LANES:(cc + 1) * LANES]

    def tiles_from(s, h):
        nv = tv_ref[i]
        pl.when(nv > (h + 1) * MB)(functools.partial(tile, s, h, 2))
        pl.when(jnp.logical_and(nv > h * MB, nv <= (h + 1) * MB))(functools.partial(tile, s, h, 1))

    @pl.when(i < nu_ref[0])
    def _():
        e = te_ref[i]
        s = ts_ref[i]
        nxt = tn_ref[i]
        first = (tf_ref[i] & 1) != 0
        last = (tf_ref[i] & 2) != 0

        @pl.when(i == 0)
        def _():
            for c in fetch(e):
                c.start()
            for c in fetch(e):
                c.wait()
            convert(s)

        @pl.when(jnp.logical_and(first, nxt >= 0))
        def _():
            for c in fetch(nxt):
                c.start()

        handoff = jnp.logical_and(last, nxt >= 0)

        @pl.when(handoff)
        def _():
            for c in fetch(nxt):
                c.wait()
            convert(1 - s)

        for h in range(0, EB, 2):
            tiles_from(s, h)


def _experts(tile_blk, tile_e, tile_nv, tile_flags, tile_next, tile_slot, nused, xs, w_gu, b_gu, w_down, b_down):
    blk = lambda i, tb, te, *_: (tb[i], 0)
    bsel = lambda i, tb, te, *_: (te[i], 0, 0)
    grid_spec = pltpu.PrefetchScalarGridSpec(
        num_scalar_prefetch=7,
        grid=(NB,),
        in_specs=[
            pl.BlockSpec((EB * MB * ROW, LANES), blk),
            pl.BlockSpec(memory_space=pl.ANY),
            pl.BlockSpec((None, 1, 2 * DFF), bsel),
            pl.BlockSpec(memory_space=pl.ANY),
            pl.BlockSpec((None, 1, D), bsel),
        ],
        out_specs=pl.BlockSpec((EB * MB * ROW, LANES), blk),
        scratch_shapes=[pltpu.VMEM((D, 2 * DFF), f32), pltpu.VMEM((DFF, D), f32),
                        pltpu.VMEM((2, D, 2 * DFF), bf16), pltpu.VMEM((2, DFF, D), bf16),
                        pltpu.SemaphoreType.DMA((2,))],
    )
    return pl.pallas_call(
        _experts_kernel,
        grid_spec=grid_spec,
        out_shape=jax.ShapeDtypeStruct((NE * N * ROW, LANES), f32),
        compiler_params=pltpu.CompilerParams(dimension_semantics=("arbitrary",), vmem_limit_bytes=VMEM_LIMIT),
        name="experts",
    )(tile_blk, tile_e, tile_nv, tile_flags, tile_next, tile_slot, nused, xs,
      w_gu, b_gu.reshape(NE, 1, 2 * DFF), w_down, b_down.reshape(NE, 1, D))


def _combine_kernel(pos_ref, tab_ref, tabn_ref, gate_ref, h1_ref, g2_ref, pg_ref, ys_hbm, o_ref,
                    stage, wbuf, rbuf, sem):
    j = pl.program_id(0)
    slot = lax.rem(j, 2)
    GT = PG
    GR = GT * ROW

    @pl.when(j == 0)
    def _():
        _strip_copies(tab_ref, stage.at[0], ys_hbm, sem.at[0], to_region=False)

    @pl.when(j + 1 < pl.num_programs(0))
    def _():
        _strip_copies(tabn_ref, stage.at[1 - slot], ys_hbm, sem.at[1 - slot], to_region=False)

    for kk in range(TOPK):
        wbuf[kk] = jnp.broadcast_to(gate_ref[:, kk:kk + 1], (TT, LANES))
    _strip_wait(stage.at[slot], ys_hbm, sem.at[slot])
    g2 = g2_ref[...]
    pg = pg_ref[...]

    def one_group(s, g):
        rows = pl.ds(pl.multiple_of(g * GR, GR), GR)
        parts = []
        for jj in range(GT):
            t = g * GT + jj
            acc = None
            for kk in range(TOPK):
                w = jnp.broadcast_to(wbuf[kk, pl.ds(t, 1), :], (ROW, LANES))
                term = stage[s, pl.ds(pl.multiple_of(pos_ref[g, jj * TOPK + kk], ROW), ROW), :] * w
                acc = term if acc is None else acc + term
            parts.append(acc)
        ff3 = jnp.concatenate(parts, axis=0).reshape(GT, ROW, LANES)
        ss = jnp.sum(jnp.sum(ff3 * ff3, axis=2, keepdims=True), axis=1, keepdims=True)
        rs = lax.rsqrt(ss * (1.0 / D) + EPS)
        out = h1_ref[rows, :].reshape(GT, ROW, LANES) + g2 * (ff3 * rs * pg)
        rbuf[rows, :] = out.reshape(GR, LANES)

    def gather(s):
        def pair(gg, carry):
            one_group(s, 2 * gg)
            one_group(s, 2 * gg + 1)
            return carry
        lax.fori_loop(0, TT // GT // 2, pair, 0)

    for s in range(2):
        pl.when(slot == s)(functools.partial(gather, s))
    for cc in range(ROW):
        o_ref[:, cc * LANES:(cc + 1) * LANES] = rbuf[pl.ds(cc, TT, stride=ROW), :]


def _combine(pos, tab, gates, h1_rows, mod4, pg, ys):
    nt = N // TT
    per_b = T // TT
    tabspec = lambda imap: pl.BlockSpec((None, ROW, LANES), imap, memory_space=pltpu.SMEM)
    return pl.pallas_call(
        _combine_kernel,
        grid=(nt,),
        in_specs=[pl.BlockSpec((TT // PG, LANES), lambda j: (j, 0), memory_space=pltpu.SMEM),
                  tabspec(lambda j: (j, 0, 0)), tabspec(lambda j: (jnp.minimum(j + 1, nt - 1), 0, 0)),
                  pl.BlockSpec((TT, LANES), lambda j: (j, 0)),
                  pl.BlockSpec((TT * ROW, LANES), lambda j: (j, 0)),
                  pl.BlockSpec((None, None, ROW, LANES), lambda j: (j // per_b, 5, 0, 0)),
                  pl.BlockSpec((ROW, LANES), lambda j: (0, 0)),
                  pl.BlockSpec(memory_space=pl.ANY)],
        out_specs=pl.BlockSpec((TT, D), lambda j: (j, 0)),
        out_shape=jax.ShapeDtypeStruct((N, D), f32),
        scratch_shapes=[pltpu.VMEM((2, STAGE_ROWS, LANES), f32), pltpu.VMEM((TOPK, TT, LANES), f32),
                        pltpu.VMEM((TT * ROW, LANES), f32), pltpu.SemaphoreType.DMA((2,))],
        compiler_params=pltpu.CompilerParams(dimension_semantics=("arbitrary",), vmem_limit_bytes=VMEM_LIMIT),
        name="combine",
    )(pos, tab, tab, gates, h1_rows, mod4, pg, ys)


def _pad_heads(w):
    lead = w.shape[:-1]
    w4 = w.reshape(lead + (GLA_H, GLA_DK))
    w4 = jnp.concatenate([w4, jnp.zeros_like(w4)], axis=-1)
    return w4.reshape(lead + (GLA_H * LANES,))


def kernel(x, c, ada_w, ada_b, mix_pre_g, mix_post_g, ffn_pre_g, ffn_post_g, w_in, w_alpha, b_alpha, gla_norm_g, s5_lambda_re, s5_lambda_im, s5_log_dt, s5_b_re, s5_b_im, s5_c_re, s5_c_im, s5_d, s5_glu_w, s5_glu_b, w_out, router_w, router_b, exp_w_gu, exp_b_gu, exp_w_down, exp_b_down):
    l = 0
    mod = _ada(c, ada_w[l], ada_b[l])
    mod3 = mod.reshape(B, 1, 6 * D)
    mod4 = mod.reshape(B, 6, ROW, LANES)

    w = w_in[l]
    o_q, o_k, o_v, o_g, o_a, o_u = 0, 256, 512, 1024, 1536, 1552
    wq = _pad_heads(w[:, o_q:o_k]).astype(bf16)
    wk = _pad_heads(w[:, o_k:o_v]).astype(bf16)
    wv = w[:, o_v:o_g].astype(bf16)
    wg = w[:, o_g:o_a].astype(bf16)
    wa = jnp.pad(w[:, o_a:o_u], ((0, 0), (0, LANES - LOWRANK))).astype(bf16)
    wu = w[:, o_u:].astype(bf16)
    walpha = jnp.pad(_pad_heads(w_alpha[l]), ((0, LANES - LOWRANK), (0, 0))).astype(bf16)
    balpha = _pad_heads(b_alpha[l]).reshape(1, GLA_H * LANES)

    gla, u = _mixer_in(x, mod3, mix_pre_g[l].reshape(1, D), wq, wk, wv, wg, wa, wu,
                       walpha, balpha, gla_norm_g[l].reshape(1, GLA_DV))

    wt, mi, mo, a16 = _s5_prep(s5_lambda_re[l], s5_lambda_im[l], s5_log_dt[l], s5_b_re[l], s5_b_im[l],
                               s5_c_re[l], s5_c_im[l], s5_d[l])
    y = _s5_scan(u, wt, mi, mo, a16)

    rw = jnp.pad(router_w[l], ((0, 0), (0, LANES - NE)))
    rw_hi = rw.astype(bf16)
    rw_lo = (rw - rw_hi.astype(f32)).astype(bf16)
    rb = jnp.pad(router_b[l], (0, LANES - NE), constant_values=-1e30).reshape(1, LANES)
    wo = w_out[l].astype(bf16)
    h1_rows, hn2_rows, pos, tab, gates, cnt = _mixer_out(
        x, gla, y, mod3, s5_glu_w[l].astype(bf16), s5_glu_b[l].reshape(1, 512), wo[:512], wo[512:],
        mix_post_g[l].reshape(1, D), ffn_pre_g[l].reshape(1, D), rw_hi, rw_lo, rb)

    counts = cnt[0, :NE].astype(i32)
    step_rows = EB * MB
    ntile = (counts + step_rows - 1) // step_rows
    tends = jnp.cumsum(ntile)
    nused = tends[-1]
    gi = jnp.minimum(jnp.arange(NB, dtype=i32), nused - 1)
    tile_e = jnp.sum((tends[None, :] <= gi[:, None]).astype(i32), axis=1)
    sel = tile_e[:, None] == jnp.arange(NE, dtype=i32)[None, :]
    pick = lambda v: jnp.sum(jnp.where(sel, v[None, :], 0), axis=1)
    tile_j = gi - pick(tends - ntile)
    tile_blk = tile_e * TPE + tile_j
    tile_nv = jnp.clip(pick(counts) - tile_j * step_rows, 0, step_rows)
    tile_flags = (tile_j == 0).astype(i32) + 2 * (tile_j == pick(ntile) - 1).astype(i32)
    used = ntile > 0
    eids = jnp.arange(NE, dtype=i32)
    later = jnp.logical_and(used[None, :], eids[None, :] > eids[:, None])
    nxt = jnp.min(jnp.where(later, eids[None, :], NE), axis=1)
    tile_next = pick(jnp.where(nxt < NE, nxt, -1))
    tile_slot = pick((jnp.cumsum(used.astype(i32)) - 1) % 2)

    xs = _dispatch(pos, tab, hn2_rows)
    ys = _experts(tile_blk, tile_e, tile_nv, tile_flags, tile_next, tile_slot, nused.reshape(1), xs,
                  exp_w_gu[l], exp_b_gu[l], exp_w_down[l], exp_b_down[l])
    out = _combine(pos, tab, gates.reshape(N, LANES), h1_rows, mod4, ffn_post_g[l].reshape(ROW, LANES), ys)
    return out.reshape(B, T, D)
```

```python
import functools
import math

import jax
import jax.numpy as jnp
from jax import lax
from jax.experimental import pallas as pl
from jax.experimental.pallas import tpu as pltpu

f32 = jnp.float32
bf16 = jnp.bfloat16
i32 = jnp.int32

D = 1024
B = 8
T = 2048
N = B * T
GLA_H = 4
GLA_DV = 128
GLA_DK = 64
GLA_TAU = 16.0
GLA_CHUNK = 64
LOWRANK = 16
S5_W = 512
S5_CH = 16
S5_G = 32
S5_P = 64
NE = 32
TOPK = 4
DFF = 1024
ALPHA = 1.702
LIMIT = 7.0
EPS = 1e-6

LANES = 128
SUBLANES = 8
VMEM_LIMIT = 56 * 1024 * 1024

TT = 512
S5_L = 16
S5_M = S5_W // LANES
S5_GPT = LANES // S5_CH
S5_ST = S5_GPT * S5_P
S5_NCH = 32
S5_ROWS = B * S5_NCH
S5_PITCH = S5_NCH + SUBLANES
ROW = SUBLANES
MB = 256
EB = 4
NB = (N * TOPK) // (EB * MB) + NE
TPE = N // (EB * MB)
PG = 16


def _dot(a, b):
    return jnp.dot(a, b, preferred_element_type=f32)


def _dot_t(a, b, ca, cb):
    return lax.dot_general(a, b, (((ca,), (cb,)), ((), ())), preferred_element_type=f32)


def _rms(x):
    return lax.rsqrt(jnp.mean(x * x, axis=-1, keepdims=True) + EPS)


def _split(a):
    hi = a.astype(bf16)
    return hi, (a - hi.astype(f32)).astype(bf16)


def _dot3(a, b):
    return _dot(a[0], b[0]) + (_dot(a[1], b[0]) + _dot(a[0], b[1]))


def _ada_kernel(c_ref, w_ref, b_ref, o_ref):
    c = c_ref[...]
    s = (c * jax.nn.sigmoid(c)).astype(bf16)
    o_ref[...] = _dot(s, w_ref[...].astype(bf16)) + b_ref[...]


def _ada(c, w, b):
    return pl.pallas_call(
        _ada_kernel,
        grid=(6,),
        in_specs=[
            pl.BlockSpec((B, D), lambda j: (0, 0)),
            pl.BlockSpec((D, D), lambda j: (0, j)),
            pl.BlockSpec((1, D), lambda j: (0, j)),
        ],
        out_specs=pl.BlockSpec((B, D), lambda j: (0, j)),
        out_shape=jax.ShapeDtypeStruct((B, 6 * D), f32),
        name="ada",
    )(c, w, b.reshape(1, 6 * D))


def _mixer_in_kernel(x_ref, sc_ref, sh_ref, g_ref, wq, wk, wv, wg, wa, wu, walpha, balpha, gng, cm_ref,
                     gla_ref, u_ref, st_s, u_s):
    t = pl.program_id(1)

    @pl.when(t == 0)
    def _():
        st_s[...] = jnp.zeros_like(st_s)

    x = x_ref[...]
    hn = (x * _rms(x) * g_ref[...]) * (1.0 + sc_ref[...]) + sh_ref[...]
    hb = hn.astype(bf16)
    u = _dot(hb, wu[...])

    C = GLA_CHUNK
    NC = TT // C
    causal = cm_ref[...] > 0.5
    tri = cm_ref[...].astype(bf16)
    a_lr = _dot(hb, wa[...]).astype(bf16)
    la = jax.nn.log_sigmoid(_dot(a_lr, walpha[...]) + balpha[...]) * (1.0 / GLA_TAU)
    la_hi = la.astype(bf16)
    la_lo = (la - la_hi.astype(f32)).astype(bf16)
    bc = _dot(tri, la_hi) + _dot(tri, la_lo)
    bl = jnp.broadcast_to(bc.reshape(NC, C, GLA_H * LANES)[:, C - 1:C, :],
                          (NC, C, GLA_H * LANES)).reshape(TT, GLA_H * LANES)
    q = _dot(hb, wq[...]) * (GLA_DK ** -0.5)
    k = _dot(hb, wk[...])
    qd = (q * jnp.exp(bc)).astype(bf16)
    ki = (k * jnp.exp(-bc)).astype(bf16)
    ke = k * jnp.exp(bl - bc)
    dec = jnp.exp(bl)
    vb = _dot(hb, wv[...]).astype(bf16)
    og = _dot(hb, wg[...])
    gn = gng[...]
    for h in range(GLA_H):
        sl = slice(h * LANES, (h + 1) * LANES)
        qh, kih, keh, vh = qd[:, sl], ki[:, sl], ke[:, sl], vb[:, sl]
        sc = jnp.where(causal, _dot_t(qh, kih, 1, 1), 0.0).astype(bf16)
        o_intra = _dot(sc, vh)
        kehb = keh.astype(bf16)
        deltas = [_dot_t(vh[c * C:(c + 1) * C], kehb[c * C:(c + 1) * C], 0, 0) for c in range(NC)]
        st = st_s[h]
        sts = []
        for c in range(NC):
            sts.append(st.astype(bf16))
            st = st * dec[c * C:c * C + 1, sl] + deltas[c]
        st_s[h] = st
        o_inter = jnp.concatenate([_dot_t(qh[c * C:(c + 1) * C], sts[c], 1, 1) for c in range(NC)], axis=0)
        o = o_intra + o_inter
        on = o * _rms(o) * gn
        ogh = og[:, sl]
        gla_ref[:, sl] = (on * (ogh * jax.nn.sigmoid(ogh))).astype(bf16)
        u_s[h] = u[:, sl]
        for i in range(S5_L):
            u_ref[h, :, i * LANES:(i + 1) * LANES] = u_s[h, pl.ds(i, TT // S5_L, stride=S5_L), :].astype(bf16)


def _mixer_in(x, mod3, g, wq, wk, wv, wg, wa, wu, walpha, balpha, gng):
    assert S5_M == GLA_H
    nt = T // TT
    full = lambda shape: pl.BlockSpec(shape, lambda b, t: (0,) * len(shape))
    u_shape = jax.ShapeDtypeStruct((S5_M, B, T // S5_L, S5_L * LANES), bf16)
    u_spec = pl.BlockSpec((S5_M, None, TT // S5_L, S5_L * LANES), lambda b, t: (0, b, t, 0))
    tok = jnp.arange(TT, dtype=i32)
    same = (tok[:, None] // GLA_CHUNK) == (tok[None, :] // GLA_CHUNK)
    cmask = jnp.logical_and(tok[:, None] >= tok[None, :], same).astype(f32)
    return pl.pallas_call(
        _mixer_in_kernel,
        grid=(B, nt),
        in_specs=[
            pl.BlockSpec((None, TT, D), lambda b, t: (b, t, 0)),
            pl.BlockSpec((None, 1, D), lambda b, t: (b, 0, 1)),
            pl.BlockSpec((None, 1, D), lambda b, t: (b, 0, 0)),
            full((1, D)),
            full((D, 512)), full((D, 512)), full((D, 512)), full((D, 512)),
            full((D, LANES)), full((D, 512)),
            full((LANES, 512)), full((1, 512)), full((1, LANES)),
            full((TT, TT)),
        ],
        out_specs=[pl.BlockSpec((None, TT, 512), lambda b, t: (b, t, 0)), u_spec],
        out_shape=[jax.ShapeDtypeStruct((B, T, 512), bf16), u_shape],
        scratch_shapes=[pltpu.VMEM((GLA_H, GLA_DV, LANES), f32), pltpu.VMEM((S5_M, TT, LANES), f32)],
        compiler_params=pltpu.CompilerParams(
            dimension_semantics=("arbitrary", "arbitrary"), vmem_limit_bytes=VMEM_LIMIT),
        name="mixer_in",
    )(x, mod3, mod3, g, wq, wk, wv, wg, wa, wu, walpha, balpha, gng, cmask)


def _cpow(xr, xi, d):
    mag = jnp.exp(xr * d)
    return mag * jnp.cos(xi * d), mag * jnp.sin(xi * d)


def _s5_prep_kernel(lr_r, li_r, ld_r, lr_c, li_c, ld_c, btr, bti, ctr, cti, d_r,
                    wt_ref, mi_ref, mo_ref, a_ref, kt_s):
    L = S5_L
    lr = lr_r[...]
    li = li_r[...]
    dt = jnp.exp(ld_r[...])
    xr, xi = lr * dt, li * dt
    ar, ai = _cpow(xr, xi, 1.0)
    den = lr * lr + li * li
    fr = ((ar - 1.0) * lr + ai * li) / den
    fi = (ai * lr - (ar - 1.0) * li) / den
    br, bi = btr[...], bti[...]
    bbr = fr * br - fi * bi
    bbi = fr * bi + fi * br
    cr, ci = ctr[...], cti[...]
    cr_s, ci_s = _split(cr), _split(ci)
    pr, pi = jnp.ones_like(ar), jnp.zeros_like(ai)
    for d in range(L):
        xdr = bbr * pr - bbi * pi
        xdi = bbr * pi + bbi * pr
        i = L - 1 - d
        mo_ref[i * LANES:(i + 1) * LANES, :S5_ST] = xdr.astype(bf16)
        mo_ref[i * LANES:(i + 1) * LANES, S5_ST:] = xdi.astype(bf16)
        kt = _dot3(_split(xdr), cr_s) - _dot3(_split(xdi), ci_s)
        if d == 0:
            r = lax.broadcasted_iota(i32, (LANES, LANES), 0)
            c = lax.broadcasted_iota(i32, (LANES, LANES), 1)
            kt = kt + jnp.where(r == c, d_r[...], 0.0)
        kt_s[d] = kt.astype(bf16)
        pr, pi = pr * ar - pi * ai, pr * ai + pi * ar
    a_ref[:, :S5_ST] = pr
    a_ref[:, S5_ST:] = pi
    zero = jnp.zeros((LANES, LANES), bf16)
    for i in range(L):
        for j in range(L):
            wt_ref[i * LANES:(i + 1) * LANES, j * LANES:(j + 1) * LANES] = kt_s[j - i] if j >= i else zero
    lrc = lr_c[...]
    lic = li_c[...]
    dtc = jnp.exp(ld_c[...])
    acr, aci = _cpow(lrc * dtc, lic * dtc, 1.0)
    pr, pi = acr, aci
    for j in range(L):
        mi_ref[:S5_ST, j * LANES:(j + 1) * LANES] = (cr * pr - ci * pi).astype(bf16)
        mi_ref[S5_ST:, j * LANES:(j + 1) * LANES] = (-(cr * pi + ci * pr)).astype(bf16)
        pr, pi = pr * acr - pi * aci, pr * aci + pi * acr


def _s5_prep(lr, li, ld, b_re, b_im, c_re, c_im, dvec):
    G, P, H, M, GPT = S5_G, S5_P, S5_CH, S5_M, S5_GPT
    eye = jnp.eye(GPT, dtype=f32)

    def rows(v):
        return v.reshape(M, 1, S5_ST)

    def cols(v):
        return v.reshape(M, S5_ST, 1)

    ldp = jnp.broadcast_to(ld[:, None], (G, P))

    def bt(b):
        b4 = b.reshape(M, GPT, P, H)
        return jnp.einsum('mgph,gk->mkhgp', b4, eye).reshape(M, LANES, S5_ST)

    def ct(c):
        c4 = c.reshape(M, GPT, H, P)
        return jnp.einsum('mghp,gk->mgpkh', c4, eye).reshape(M, S5_ST, LANES)

    L = S5_L
    mspec = lambda shape: pl.BlockSpec((None,) + shape, lambda m: (m,) + (0,) * len(shape))
    return pl.pallas_call(
        _s5_prep_kernel,
        grid=(M,),
        in_specs=[mspec((1, S5_ST))] * 3 + [mspec((S5_ST, 1))] * 3
        + [mspec((LANES, S5_ST))] * 2 + [mspec((S5_ST, LANES))] * 2 + [mspec((1, LANES))],
        out_specs=[mspec((L * LANES, L * LANES)), mspec((2 * S5_ST, L * LANES)),
                   mspec((L * LANES, 2 * S5_ST)), mspec((1, 2 * S5_ST))],
        out_shape=[jax.ShapeDtypeStruct((M, L * LANES, L * LANES), bf16),
                   jax.ShapeDtypeStruct((M, 2 * S5_ST, L * LANES), bf16),
                   jax.ShapeDtypeStruct((M, L * LANES, 2 * S5_ST), bf16),
                   jax.ShapeDtypeStruct((M, 1, 2 * S5_ST), f32)],
        scratch_shapes=[pltpu.VMEM((L, LANES, LANES), bf16)],
        compiler_params=pltpu.CompilerParams(dimension_semantics=("arbitrary",), vmem_limit_bytes=VMEM_LIMIT),
        name="s5_prep",
    )(rows(lr), rows(li), rows(ldp), cols(lr), cols(li), cols(ldp),
      bt(b_re), bt(b_im), ct(c_re), ct(c_im), dvec.reshape(M, 1, LANES))


def _s5_scan_kernel(u_ref, wt, mi, mo, a_ref, y_ref, xs, s_s, x_s):
    r = pl.program_id(1)

    @pl.when(r == 0)
    def _():
        xs[...] = jnp.zeros_like(xs)

    u = u_ref[...].reshape(S5_ROWS, S5_L * LANES)
    s = _dot(u, mo[...])
    y_intra = _dot(u, wt[...])
    NT = 2 * S5_ST // LANES
    for c in range(NT):
        for b in range(B):
            s_s[c, b * S5_PITCH:b * S5_PITCH + S5_NCH, :] = s[b * S5_NCH:(b + 1) * S5_NCH, c * LANES:(c + 1) * LANES]
    a = [a_ref[:, c * LANES:(c + 1) * LANES] for c in range(NT)]

    def step(n, x):
        rows = pl.ds(n, B, stride=S5_PITCH)
        new = []
        for c in range(NT):
            x_s[c, rows, :] = x[c]
        for c in range(NT // 2):
            ar, ai, xr, xi = a[c], a[NT // 2 + c], x[c], x[NT // 2 + c]
            new.append((ar * xr - ai * xi + s_s[c, rows, :], ar * xi + ai * xr + s_s[NT // 2 + c, rows, :]))
        return tuple(p[0] for p in new) + tuple(p[1] for p in new)

    x = tuple(xs[:, c * LANES:(c + 1) * LANES] for c in range(NT))
    for n in range(S5_NCH):
        x = step(n, x)
    for c in range(NT):
        xs[:, c * LANES:(c + 1) * LANES] = x[c]
    x_in = jnp.concatenate(
        [jnp.concatenate([x_s[c, b * S5_PITCH:b * S5_PITCH + S5_NCH, :] for b in range(B)], axis=0) for c in range(NT)],
        axis=1)
    y = y_intra + _dot(x_in.astype(bf16), mi[...])
    y_ref[...] = jax.nn.gelu(y).astype(bf16).reshape(B, S5_NCH, S5_L * LANES)


def _s5_scan(u, wt, mi, mo, a16):
    L = S5_L
    wspec = lambda shape: pl.BlockSpec((None,) + shape, lambda m, r: (m,) + (0,) * len(shape))
    uspec = pl.BlockSpec((None, B, S5_NCH, L * LANES), lambda m, r: (m, 0, r, 0))
    return pl.pallas_call(
        _s5_scan_kernel,
        grid=(S5_M, T // L // S5_NCH),
        in_specs=[uspec,
                  wspec((L * LANES, L * LANES)), wspec((2 * S5_ST, L * LANES)),
                  wspec((L * LANES, 2 * S5_ST)), wspec((1, 2 * S5_ST))],
        out_specs=uspec,
        out_shape=jax.ShapeDtypeStruct((S5_M, B, T // L, L * LANES), bf16),
        scratch_shapes=[pltpu.VMEM((B, 2 * S5_ST), f32), pltpu.VMEM((2 * S5_ST // LANES, B * S5_PITCH, LANES), f32),
                        pltpu.VMEM((2 * S5_ST // LANES, B * S5_PITCH, LANES), f32)],
        compiler_params=pltpu.CompilerParams(
            dimension_semantics=("arbitrary", "arbitrary"), vmem_limit_bytes=VMEM_LIMIT),
        name="s5_scan",
    )(u, wt, mi, mo, a16)


def _mixer_out_kernel(x_ref, gla_ref, y_ref, g1_ref, sh2_ref, sc2_ref, gluw, glub, woa, wob,
                      mpg, fpg, rw_hi, rw_lo, rb, tri_ref, upper_ref, grp_ref,
                      h1_ref, hn2_ref, pos_ref, tab_ref, gate_ref, cnt_ref, cnt_s, y_s):
    first = jnp.logical_and(pl.program_id(0) == 0, pl.program_id(1) == 0)

    @pl.when(first)
    def _():
        cnt_s[...] = jnp.zeros_like(cnt_s)

    for m in range(S5_M):
        for jj in range(S5_L):
            y_s[m, pl.ds(jj, TT // S5_L, stride=S5_L), :] = y_ref[m, :, jj * LANES:(jj + 1) * LANES].astype(f32)
    y = jnp.concatenate([y_s[m] for m in range(S5_M)], axis=1)
    z = _dot(y.astype(bf16), gluw[...]) + glub[...]
    s5 = y * jax.nn.sigmoid(z)
    mix = _dot(gla_ref[...], woa[...]) + _dot(s5.astype(bf16), wob[...])
    h1 = x_ref[...] + g1_ref[...] * (mix * _rms(mix) * mpg[...])
    hn2 = (h1 * _rms(h1) * fpg[...]) * (1.0 + sc2_ref[...]) + sh2_ref[...]
    for cc in range(ROW):
        h1_ref[pl.ds(cc, TT, stride=ROW), :] = h1[:, cc * LANES:(cc + 1) * LANES]
        hn2_ref[pl.ds(cc, TT, stride=ROW), :] = hn2[:, cc * LANES:(cc + 1) * LANES]
    x_hi = hn2.astype(bf16)
    x_lo = (hn2 - x_hi.astype(f32)).astype(bf16)
    logits = _dot(x_hi, rw_hi[...]) + (_dot(x_lo, rw_hi[...]) + _dot(x_hi, rw_lo[...])) + rb[...]

    lane = lax.broadcasted_iota(i32, (TT, LANES), 1).astype(f32)
    l = logits
    vals, idxs = [], []
    for _ in range(TOPK):
        m = jnp.max(l, axis=-1, keepdims=True)
        ix = jnp.min(jnp.where(l == m, lane, float(LANES)), axis=-1, keepdims=True)
        vals.append(m)
        idxs.append(ix)
        l = jnp.where(lane == ix, -jnp.inf, l)
    es = [jnp.exp(v - vals[0]) for v in vals]
    tot = es[0] + es[1] + es[2] + es[3]
    oh = jnp.zeros((TT, LANES), f32)
    for ix in idxs:
        oh = oh + (lane == ix).astype(f32)
    cum = _dot(tri_ref[...], oh.astype(bf16))
    lcnt = cum[TT - 1:TT, :]
    gbase = cnt_s[...]
    hi = jnp.floor(lcnt * (1.0 / 64.0))
    lo = lcnt - hi * 64.0
    upper = upper_ref[...]
    hi8 = jnp.broadcast_to(hi, (ROW, LANES)).astype(bf16)
    lo8 = jnp.broadcast_to(lo, (ROW, LANES)).astype(bf16)
    lbase = (64.0 * _dot(hi8, upper) + _dot(lo8, upper))[0:1, :]
    loc = lbase + (cum - oh)
    tmod = (lax.broadcasted_iota(i32, (TT, LANES), 0) % PG).astype(f32) * float(TOPK)
    spread = jnp.zeros((TT, LANES), f32)
    gate = jnp.zeros((TT, LANES), f32)
    for kk in range(TOPK):
        p = jnp.sum(jnp.where(lane == idxs[kk], loc, 0.0), axis=-1, keepdims=True)
        spread = jnp.where(lane == tmod + float(kk), p, spread)
        gate = jnp.where(lane == float(kk), es[kk] / tot, gate)
    s_hi = jnp.floor(spread * (1.0 / 64.0))
    s_lo = spread - s_hi * 64.0
    grp = grp_ref[...]
    folded = 64.0 * _dot(grp, s_hi.astype(bf16)) + _dot(grp, s_lo.astype(bf16))
    pos_ref[...] = (folded * float(ROW)).astype(i32)
    sub = lax.broadcasted_iota(i32, (ROW, LANES), 0)
    tab = jnp.where(sub == 0, lcnt, jnp.where(sub == 1, lbase, jnp.where(sub == 2, gbase, 0.0)))
    tab_ref[...] = tab.astype(i32)
    gate_ref[...] = gate
    cnt = cnt_s[...] + cum[TT - 1:TT, :]
    cnt_s[...] = cnt
    cnt_ref[...] = cnt


def _mixer_out(x, gla, y, mod3, gluw, glub, woa, wob, mpg, fpg, rw_hi, rw_lo, rb):
    nt = T // TT
    full = lambda shape: pl.BlockSpec(shape, lambda b, t: (0,) * len(shape))
    tok = lambda w: pl.BlockSpec((None, TT, w), lambda b, t: (b, t, 0))
    slab = pl.BlockSpec((TT * ROW, LANES), lambda b, t: (b * nt + t, 0))
    slab_shape = jax.ShapeDtypeStruct((N * ROW, LANES), f32)
    yspec = pl.BlockSpec((S5_M, None, TT // S5_L, S5_L * LANES), lambda b, t: (0, b, t, 0))
    modspec = lambda j: pl.BlockSpec((None, 1, D), lambda b, t: (b, 0, j))
    tix = jnp.arange(TT, dtype=i32)
    lanes = jnp.arange(LANES, dtype=i32)
    tri = (tix[:, None] >= tix[None, :]).astype(bf16)
    upper = (lanes[:, None] < lanes[None, :]).astype(bf16)
    grp = (tix[None, :] // PG == jnp.arange(TT // PG, dtype=i32)[:, None]).astype(bf16)
    return pl.pallas_call(
        _mixer_out_kernel,
        grid=(B, nt),
        in_specs=[tok(D), tok(512), yspec,
                  modspec(2), modspec(3), modspec(4),
                  full((512, 512)), full((1, 512)), full((512, D)), full((512, D)),
                  full((1, D)), full((1, D)), full((D, LANES)), full((D, LANES)), full((1, LANES)),
                  full((TT, TT)), full((LANES, LANES)), full((TT // PG, TT))],
        out_specs=[slab, slab, pl.BlockSpec((TT // PG, LANES), lambda b, t: (b * nt + t, 0)),
                   pl.BlockSpec((None, ROW, LANES), lambda b, t: (b * nt + t, 0, 0)), tok(LANES), full((1, LANES))],
        out_shape=[slab_shape, slab_shape,
                   jax.ShapeDtypeStruct((N // PG, LANES), i32), jax.ShapeDtypeStruct((N // TT, ROW, LANES), i32),
                   jax.ShapeDtypeStruct((B, T, LANES), f32),
                   jax.ShapeDtypeStruct((1, LANES), f32)],
        scratch_shapes=[pltpu.VMEM((1, LANES), f32), pltpu.VMEM((S5_M, TT, LANES), f32)],
        compiler_params=pltpu.CompilerParams(
            dimension_semantics=("arbitrary", "arbitrary"), vmem_limit_bytes=VMEM_LIMIT),
        name="mixer_out",
    )(x, gla, y, mod3, mod3, mod3, gluw, glub, woa, wob, mpg, fpg, rw_hi, rw_lo, rb, tri, upper, grp)


STAGE_ROWS = TT * TOPK * ROW


def _strip_copies(tab_ref, stage, region_hbm, sem, to_region):
    def expert(e, carry):
        n = tab_ref[0, e] * ROW

        @pl.when(n > 0)
        def _():
            local = stage.at[pl.ds(pl.multiple_of(tab_ref[1, e] * ROW, ROW), n), :]
            remote = region_hbm.at[pl.ds(pl.multiple_of((e * N + tab_ref[2, e]) * ROW, ROW), n), :]
            src, dst = (local, remote) if to_region else (remote, local)
            pltpu.make_async_copy(src, dst, sem).start()
        return carry
    lax.fori_loop(0, NE, expert, 0)


def _strip_wait(stage, region_hbm, sem):
    pltpu.make_async_copy(stage, region_hbm.at[pl.ds(0, STAGE_ROWS), :], sem).wait()


def _dispatch_kernel(pos_ref, tab_ref, x_ref, xs_hbm, stage, sem):
    i = pl.program_id(0)
    slot = lax.rem(i, 2)

    @pl.when(i >= 2)
    def _():
        _strip_wait(stage.at[slot], xs_hbm, sem.at[slot])

    def scatter(s):
        def group(g, carry):
            for jj in range(PG):
                v = x_ref[pl.ds(pl.multiple_of((g * PG + jj) * ROW, ROW), ROW), :]
                for kk in range(TOPK):
                    stage[s, pl.ds(pl.multiple_of(pos_ref[g, jj * TOPK + kk], ROW), ROW), :] = v
            return carry
        lax.fori_loop(0, TT // PG, group, 0)

    for s in range(2):
        pl.when(slot == s)(functools.partial(scatter, s))
    _strip_copies(tab_ref, stage.at[slot], xs_hbm, sem.at[slot], to_region=True)

    @pl.when(i == pl.num_programs(0) - 1)
    def _():
        _strip_wait(stage.at[1 - slot], xs_hbm, sem.at[1 - slot])
        _strip_wait(stage.at[slot], xs_hbm, sem.at[slot])


def _dispatch(pos, tab, hn2_rows):
    return pl.pallas_call(
        _dispatch_kernel,
        grid=(N // TT,),
        in_specs=[pl.BlockSpec((TT // PG, LANES), lambda i: (i, 0), memory_space=pltpu.SMEM),
                  pl.BlockSpec((None, ROW, LANES), lambda i: (i, 0, 0), memory_space=pltpu.SMEM),
                  pl.BlockSpec((TT * ROW, LANES), lambda i: (i, 0))],
        out_specs=pl.BlockSpec(memory_space=pl.ANY),
        out_shape=jax.ShapeDtypeStruct((NE * N * ROW, LANES), f32),
        scratch_shapes=[pltpu.VMEM((2, STAGE_ROWS, LANES), f32), pltpu.SemaphoreType.DMA((2,))],
        compiler_params=pltpu.CompilerParams(dimension_semantics=("arbitrary",), vmem_limit_bytes=VMEM_LIMIT),
        name="dispatch",
    )(pos, tab, hn2_rows)


def _experts_kernel(tb_ref, te_ref, tv_ref, tf_ref, tn_ref, ts_ref, nu_ref,
                    x_ref, wgu_hbm, bgu_ref, wd_hbm, bd_ref, y_ref, wgu_f, wd_f, wgu_b, wd_b, sem):
    i = pl.program_id(0)

    def fetch(e):
        return (pltpu.make_async_copy(wgu_hbm.at[e], wgu_f, sem.at[0]),
                pltpu.make_async_copy(wd_hbm.at[e], wd_f, sem.at[1]))

    def convert(s):
        wgu_b[s] = wgu_f[...].astype(bf16)
        wd_b[s] = wd_f[...].astype(bf16)

    def tile(s, h, nt=1):
        rows = nt * MB
        base = h * MB * ROW
        xt = jnp.concatenate([x_ref[pl.ds(base + cc, rows, stride=ROW), :] for cc in range(ROW)], axis=1)
        valid = lax.broadcasted_iota(i32, (rows, 1), 0) < tv_ref[i] - h * MB
        xt = jnp.where(valid, xt, 0.0).astype(bf16)
        gu = _dot(xt, wgu_b[s]) + bgu_ref[...]
        g = jnp.minimum(gu[:, :DFF], LIMIT)
        up = jnp.clip(gu[:, DFF:], -LIMIT, LIMIT)
        act = (up + 1.0) * (g * jax.nn.sigmoid(ALPHA * g))
        yv = _dot(act.astype(bf16), wd_b[s]) + bd_ref[...]
        for cc in range(ROW):
            y_ref[pl.ds(base + cc, rows, stride=ROW), :] = yv[:, cc * LANES:(cc + 1) * LANES]

    def tiles_from(s, h, upto):
        nv = tv_ref[i]
        few = nv <= upto * MB
        pl.when(jnp.logical_and(few, nv > (h + 1) * MB))(functools.partial(tile, s, h, 2))
        pl.when(jnp.logical_and(few, jnp.logical_and(nv > h * MB, nv <= (h + 1) * MB)))(
            functools.partial(tile, s, h, 1))

    @pl.when(i < nu_ref[0])
    def _():
        e = te_ref[i]
        s = ts_ref[i]
        nxt = tn_ref[i]
        first = (tf_ref[i] & 1) != 0
        last = (tf_ref[i] & 2) != 0

        @pl.when(i == 0)
        def _():
            for c in fetch(e):
                c.start()
            for c in fetch(e):
                c.wait()
            convert(s)

        @pl.when(jnp.logical_and(first, nxt >= 0))
        def _():
            for c in fetch(nxt):
                c.start()

        handoff = jnp.logical_and(last, nxt >= 0)

        @pl.when(handoff)
        def _():
            for c in fetch(nxt):
                c.wait()
            convert(1 - s)

        pl.when(tv_ref[i] > (EB - 1) * MB)(functools.partial(tile, s, 0, EB))
        for h in range(0, EB, 2):
            tiles_from(s, h, EB - 1)


def _experts(tile_blk, tile_e, tile_nv, tile_flags, tile_next, tile_slot, nused, xs, w_gu, b_gu, w_down, b_down):
    blk = lambda i, tb, te, *_: (tb[i], 0)
    bsel = lambda i, tb, te, *_: (te[i], 0, 0)
    grid_spec = pltpu.PrefetchScalarGridSpec(
        num_scalar_prefetch=7,
        grid=(NB,),
        in_specs=[
            pl.BlockSpec((EB * MB * ROW, LANES), blk),
            pl.BlockSpec(memory_space=pl.ANY),
            pl.BlockSpec((None, 1, 2 * DFF), bsel),
            pl.BlockSpec(memory_space=pl.ANY),
            pl.BlockSpec((None, 1, D), bsel),
        ],
        out_specs=pl.BlockSpec((EB * MB * ROW, LANES), blk),
        scratch_shapes=[pltpu.VMEM((D, 2 * DFF), f32), pltpu.VMEM((DFF, D), f32),
                        pltpu.VMEM((2, D, 2 * DFF), bf16), pltpu.VMEM((2, DFF, D), bf16),
                        pltpu.SemaphoreType.DMA((2,))],
    )
    return pl.pallas_call(
        _experts_kernel,
        grid_spec=grid_spec,
        out_shape=jax.ShapeDtypeStruct((NE * N * ROW, LANES), f32),
        compiler_params=pltpu.CompilerParams(dimension_semantics=("arbitrary",), vmem_limit_bytes=VMEM_LIMIT),
        name="experts",
    )(tile_blk, tile_e, tile_nv, tile_flags, tile_next, tile_slot, nused, xs,
      w_gu, b_gu.reshape(NE, 1, 2 * DFF), w_down, b_down.reshape(NE, 1, D))


def _combine_kernel(pos_ref, tab_ref, tabn_ref, gate_ref, h1_ref, g2_ref, pg_ref, ys_hbm, o_ref,
                    stage, wbuf, rbuf, sem):
    j = pl.program_id(0)
    slot = lax.rem(j, 2)
    GT = PG
    GR = GT * ROW

    @pl.when(j == 0)
    def _():
        _strip_copies(tab_ref, stage.at[0], ys_hbm, sem.at[0], to_region=False)

    @pl.when(j + 1 < pl.num_programs(0))
    def _():
        _strip_copies(tabn_ref, stage.at[1 - slot], ys_hbm, sem.at[1 - slot], to_region=False)

    for kk in range(TOPK):
        wbuf[kk] = jnp.broadcast_to(gate_ref[:, kk:kk + 1], (TT, LANES))
    _strip_wait(stage.at[slot], ys_hbm, sem.at[slot])
    g2 = g2_ref[...]
    pg = pg_ref[...]

    def one_group(s, g):
        rows = pl.ds(pl.multiple_of(g * GR, GR), GR)
        parts = []
        for jj in range(GT):
            t = g * GT + jj
            acc = None
            for kk in range(TOPK):
                w = jnp.broadcast_to(wbuf[kk, pl.ds(t, 1), :], (ROW, LANES))
                term = stage[s, pl.ds(pl.multiple_of(pos_ref[g, jj * TOPK + kk], ROW), ROW), :] * w
                acc = term if acc is None else acc + term
            parts.append(acc)
        ff3 = jnp.concatenate(parts, axis=0).reshape(GT, ROW, LANES)
        ss = jnp.sum(jnp.sum(ff3 * ff3, axis=2, keepdims=True), axis=1, keepdims=True)
        rs = lax.rsqrt(ss * (1.0 / D) + EPS)
        out = h1_ref[rows, :].reshape(GT, ROW, LANES) + g2 * (ff3 * rs * pg)
        rbuf[rows, :] = out.reshape(GR, LANES)

    def gather(s):
        def pair(gg, carry):
            one_group(s, 2 * gg)
            one_group(s, 2 * gg + 1)
            return carry
        lax.fori_loop(0, TT // GT // 2, pair, 0)

    for s in range(2):
        pl.when(slot == s)(functools.partial(gather, s))
    for cc in range(ROW):
        o_ref[:, cc * LANES:(cc + 1) * LANES] = rbuf[pl.ds(cc, TT, stride=ROW), :]


def _combine(pos, tab, gates, h1_rows, mod4, pg, ys):
    nt = N // TT
    per_b = T // TT
    tabspec = lambda imap: pl.BlockSpec((None, ROW, LANES), imap, memory_space=pltpu.SMEM)
    return pl.pallas_call(
        _combine_kernel,
        grid=(nt,),
        in_specs=[pl.BlockSpec((TT // PG, LANES), lambda j: (j, 0), memory_space=pltpu.SMEM),
                  tabspec(lambda j: (j, 0, 0)), tabspec(lambda j: (jnp.minimum(j + 1, nt - 1), 0, 0)),
                  pl.BlockSpec((TT, LANES), lambda j: (j, 0)),
                  pl.BlockSpec((TT * ROW, LANES), lambda j: (j, 0)),
                  pl.BlockSpec((None, None, ROW, LANES), lambda j: (j // per_b, 5, 0, 0)),
                  pl.BlockSpec((ROW, LANES), lambda j: (0, 0)),
                  pl.BlockSpec(memory_space=pl.ANY)],
        out_specs=pl.BlockSpec((TT, D), lambda j: (j, 0)),
        out_shape=jax.ShapeDtypeStruct((N, D), f32),
        scratch_shapes=[pltpu.VMEM((2, STAGE_ROWS, LANES), f32), pltpu.VMEM((TOPK, TT, LANES), f32),
                        pltpu.VMEM((TT * ROW, LANES), f32), pltpu.SemaphoreType.DMA((2,))],
        compiler_params=pltpu.CompilerParams(dimension_semantics=("arbitrary",), vmem_limit_bytes=VMEM_LIMIT),
        name="combine",
    )(pos, tab, tab, gates, h1_rows, mod4, pg, ys)


def _pad_heads(w):
    lead = w.shape[:-1]
    w4 = w.reshape(lead + (GLA_H, GLA_DK))
    w4 = jnp.concatenate([w4, jnp.zeros_like(w4)], axis=-1)
    return w4.reshape(lead + (GLA_H * LANES,))


def kernel(x, c, ada_w, ada_b, mix_pre_g, mix_post_g, ffn_pre_g, ffn_post_g, w_in, w_alpha, b_alpha, gla_norm_g, s5_lambda_re, s5_lambda_im, s5_log_dt, s5_b_re, s5_b_im, s5_c_re, s5_c_im, s5_d, s5_glu_w, s5_glu_b, w_out, router_w, router_b, exp_w_gu, exp_b_gu, exp_w_down, exp_b_down):
    l = 0
    mod = _ada(c, ada_w[l], ada_b[l])
    mod3 = mod.reshape(B, 1, 6 * D)
    mod4 = mod.reshape(B, 6, ROW, LANES)

    w = w_in[l]
    o_q, o_k, o_v, o_g, o_a, o_u = 0, 256, 512, 1024, 1536, 1552
    wq = _pad_heads(w[:, o_q:o_k]).astype(bf16)
    wk = _pad_heads(w[:, o_k:o_v]).astype(bf16)
    wv = w[:, o_v:o_g].astype(bf16)
    wg = w[:, o_g:o_a].astype(bf16)
    wa = jnp.pad(w[:, o_a:o_u], ((0, 0), (0, LANES - LOWRANK))).astype(bf16)
    wu = w[:, o_u:].astype(bf16)
    walpha = jnp.pad(_pad_heads(w_alpha[l]), ((0, LANES - LOWRANK), (0, 0))).astype(bf16)
    balpha = _pad_heads(b_alpha[l]).reshape(1, GLA_H * LANES)

    gla, u = _mixer_in(x, mod3, mix_pre_g[l].reshape(1, D), wq, wk, wv, wg, wa, wu,
                       walpha, balpha, gla_norm_g[l].reshape(1, GLA_DV))

    wt, mi, mo, a16 = _s5_prep(s5_lambda_re[l], s5_lambda_im[l], s5_log_dt[l], s5_b_re[l], s5_b_im[l],
                               s5_c_re[l], s5_c_im[l], s5_d[l])
    y = _s5_scan(u, wt, mi, mo, a16)

    rw = jnp.pad(router_w[l], ((0, 0), (0, LANES - NE)))
    rw_hi = rw.astype(bf16)
    rw_lo = (rw - rw_hi.astype(f32)).astype(bf16)
    rb = jnp.pad(router_b[l], (0, LANES - NE), constant_values=-1e30).reshape(1, LANES)
    wo = w_out[l].astype(bf16)
    h1_rows, hn2_rows, pos, tab, gates, cnt = _mixer_out(
        x, gla, y, mod3, s5_glu_w[l].astype(bf16), s5_glu_b[l].reshape(1, 512), wo[:512], wo[512:],
        mix_post_g[l].reshape(1, D), ffn_pre_g[l].reshape(1, D), rw_hi, rw_lo, rb)

    counts = cnt[0, :NE].astype(i32)
    step_rows = EB * MB
    ntile = (counts + step_rows - 1) // step_rows
    tends = jnp.cumsum(ntile)
    nused = tends[-1]
    gi = jnp.minimum(jnp.arange(NB, dtype=i32), nused - 1)
    tile_e = jnp.sum((tends[None, :] <= gi[:, None]).astype(i32), axis=1)
    sel = tile_e[:, None] == jnp.arange(NE, dtype=i32)[None, :]
    pick = lambda v: jnp.sum(jnp.where(sel, v[None, :], 0), axis=1)
    tile_j = gi - pick(tends - ntile)
    tile_blk = tile_e * TPE + tile_j
    tile_nv = jnp.clip(pick(counts) - tile_j * step_rows, 0, step_rows)
    tile_flags = (tile_j == 0).astype(i32) + 2 * (tile_j == pick(ntile) - 1).astype(i32)
    used = ntile > 0
    eids = jnp.arange(NE, dtype=i32)
    later = jnp.logical_and(used[None, :], eids[None, :] > eids[:, None])
    nxt = jnp.min(jnp.where(later, eids[None, :], NE), axis=1)
    tile_next = pick(jnp.where(nxt < NE, nxt, -1))
    tile_slot = pick((jnp.cumsum(used.astype(i32)) - 1) % 2)

    xs = _dispatch(pos, tab, hn2_rows)
    ys = _experts(tile_blk, tile_e, tile_nv, tile_flags, tile_next, tile_slot, nused.reshape(1), xs,
                  exp_w_gu[l], exp_b_gu[l], exp_w_down[l], exp_b_down[l])
    out = _combine(pos, tab, gates.reshape(N, LANES), h1_rows, mod4, ffn_post_g[l].reshape(ROW, LANES), ys)
    return out.reshape(B, T, D)
```

```python
import functools
import math

import jax
import jax.numpy as jnp
from jax import lax
from jax.experimental import pallas as pl
from jax.experimental.pallas import tpu as pltpu

f32 = jnp.float32
bf16 = jnp.bfloat16
i32 = jnp.int32

D = 1024
B = 8
T = 2048
N = B * T
GLA_H = 4
GLA_DV = 128
GLA_DK = 64
GLA_TAU = 16.0
GLA_CHUNK = 64
LOWRANK = 16
S5_W = 512
S5_CH = 16
S5_G = 32
S5_P = 64
NE = 32
TOPK = 4
DFF = 1024
ALPHA = 1.702
LIMIT = 7.0
EPS = 1e-6

LANES = 128
SUBLANES = 8
VMEM_LIMIT = 56 * 1024 * 1024

TT = 512
S5_L = 16
S5_M = S5_W // LANES
S5_GPT = LANES // S5_CH
S5_ST = S5_GPT * S5_P
S5_NCH = 32
S5_ROWS = B * S5_NCH
S5_PITCH = S5_NCH + SUBLANES
FOLD_PITCH = S5_L + SUBLANES
ROW = SUBLANES
MB = 256
EB = 4
NB = (N * TOPK) // (EB * MB) + NE
TPE = N // (EB * MB)
PG = 16
COMBINE_GROUPS_PER_TRIP = 8


def _dot(a, b):
    return jnp.dot(a, b, preferred_element_type=f32)


def _dot_t(a, b, ca, cb):
    return lax.dot_general(a, b, (((ca,), (cb,)), ((), ())), preferred_element_type=f32)


def _rms(x):
    return lax.rsqrt(jnp.mean(x * x, axis=-1, keepdims=True) + EPS)


def _ada_kernel(c_ref, w_ref, b_ref, o_ref):
    c = c_ref[...]
    s = (c * jax.nn.sigmoid(c)).astype(bf16)
    o_ref[...] = _dot(s, w_ref[...].astype(bf16)) + b_ref[...]


def _ada(c, w, b):
    return pl.pallas_call(
        _ada_kernel,
        grid=(6,),
        in_specs=[
            pl.BlockSpec((B, D), lambda j: (0, 0)),
            pl.BlockSpec((D, D), lambda j: (0, j)),
            pl.BlockSpec((1, D), lambda j: (0, j)),
        ],
        out_specs=pl.BlockSpec((B, D), lambda j: (0, j)),
        out_shape=jax.ShapeDtypeStruct((B, 6 * D), f32),
        name="ada",
    )(c, w, b)


def _mixer_in_kernel(x_ref, sc_ref, sh_ref, g_ref, wq, wk, wv, wg, wa, wu, walpha, balpha, gng, cm_ref,
                     gla_ref, u_ref, st_s, u_s):
    t = pl.program_id(1)

    @pl.when(t == 0)
    def _():
        st_s[...] = jnp.zeros_like(st_s)

    x = x_ref[...]
    hn = (x * _rms(x) * g_ref[...]) * (1.0 + sc_ref[...]) + sh_ref[...]
    hb = hn.astype(bf16)
    u = _dot(hb, wu[...])

    C = GLA_CHUNK
    NC = TT // C
    causal = cm_ref[...] > 0.5
    tri = cm_ref[...].astype(bf16)
    a_lr = _dot(hb, wa[...]).astype(bf16)
    la = jax.nn.log_sigmoid(_dot(a_lr, walpha[...]) + balpha[...]) * (1.0 / GLA_TAU)
    la_hi = la.astype(bf16)
    la_lo = (la - la_hi.astype(f32)).astype(bf16)
    bc = _dot(tri, la_hi) + _dot(tri, la_lo)
    bl = jnp.broadcast_to(bc.reshape(NC, C, GLA_H * LANES)[:, C - 1:C, :],
                          (NC, C, GLA_H * LANES)).reshape(TT, GLA_H * LANES)
    q = _dot(hb, wq[...]) * (GLA_DK ** -0.5)
    k = _dot(hb, wk[...])
    qd = (q * jnp.exp(bc)).astype(bf16)
    ki = (k * jnp.exp(-bc)).astype(bf16)
    ke = k * jnp.exp(bl - bc)
    dec = jnp.exp(bl)
    vb = _dot(hb, wv[...]).astype(bf16)
    og = _dot(hb, wg[...])
    gn = gng[...]
    heads = range(GLA_H)
    sls = [slice(h * LANES, (h + 1) * LANES) for h in heads]
    kehb = ke.astype(bf16)
    raw = [_dot_t(qd[:, sl], ki[:, sl], 1, 1) for sl in sls]
    deltas = [[_dot_t(vb[c * C:(c + 1) * C, sl], kehb[c * C:(c + 1) * C, sl], 0, 0) for c in range(NC)] for sl in sls]
    scs = [jnp.where(causal, r, 0.0).astype(bf16) for r in raw]
    o_intra = [_dot(scs[h], vb[:, sls[h]]) for h in heads]
    sts = []
    for h in heads:
        st = st_s[h]
        per = []
        for c in range(NC):
            per.append(st.astype(bf16))
            st = st * dec[c * C:c * C + 1, sls[h]] + deltas[h][c]
        st_s[h] = st
        sts.append(per)
    for h in heads:
        sl = sls[h]
        o_inter = jnp.concatenate([_dot_t(qd[c * C:(c + 1) * C, sl], sts[h][c], 1, 1) for c in range(NC)], axis=0)
        o = o_intra[h] + o_inter
        on = o * _rms(o) * gn
        ogh = og[:, sl]
        gla_ref[:, sl] = (on * (ogh * jax.nn.sigmoid(ogh))).astype(bf16)
        for n in range(TT // S5_L):
            u_s[h, n * FOLD_PITCH:n * FOLD_PITCH + S5_L, :] = u[n * S5_L:(n + 1) * S5_L, sl]
        for i in range(S5_L):
            u_ref[h, :, i * LANES:(i + 1) * LANES] = u_s[h, pl.ds(i, TT // S5_L, stride=FOLD_PITCH), :].astype(bf16)


def _mixer_in(x, mod3, g, wq, wk, wv, wg, wa, wu, walpha, balpha, gng):
    assert S5_M == GLA_H
    nt = T // TT
    full = lambda shape: pl.BlockSpec(shape, lambda b, t: (0,) * len(shape))
    u_shape = jax.ShapeDtypeStruct((S5_M, B, T // S5_L, S5_L * LANES), bf16)
    u_spec = pl.BlockSpec((S5_M, None, TT // S5_L, S5_L * LANES), lambda b, t: (0, b, t, 0))
    tok = jnp.arange(TT, dtype=i32)
    same = (tok[:, None] // GLA_CHUNK) == (tok[None, :] // GLA_CHUNK)
    cmask = jnp.logical_and(tok[:, None] >= tok[None, :], same).astype(f32)
    return pl.pallas_call(
        _mixer_in_kernel,
        grid=(B, nt),
        in_specs=[
            pl.BlockSpec((None, TT, D), lambda b, t: (b, t, 0)),
            pl.BlockSpec((None, 1, D), lambda b, t: (b, 0, 1)),
            pl.BlockSpec((None, 1, D), lambda b, t: (b, 0, 0)),
            full((1, D)),
            full((D, 512)), full((D, 512)), full((D, 512)), full((D, 512)),
            full((D, LANES)), full((D, 512)),
            full((LANES, 512)), full((1, 512)), full((1, LANES)),
            full((TT, TT)),
        ],
        out_specs=[pl.BlockSpec((None, TT, 512), lambda b, t: (b, t, 0)), u_spec],
        out_shape=[jax.ShapeDtypeStruct((B, T, 512), bf16), u_shape],
        scratch_shapes=[pltpu.VMEM((GLA_H, GLA_DV, LANES), f32),
                        pltpu.VMEM((S5_M, TT // S5_L * FOLD_PITCH, LANES), f32)],
        compiler_params=pltpu.CompilerParams(
            dimension_semantics=("arbitrary", "arbitrary"), vmem_limit_bytes=VMEM_LIMIT),
        name="mixer_in",
    )(x, mod3, mod3, g, wq, wk, wv, wg, wa, wu, walpha, balpha, gng, cmask)


def _cpow(xr, xi, d):
    mag = jnp.exp(xr * d)
    return mag * jnp.cos(xi * d), mag * jnp.sin(xi * d)


def _s5_prep_kernel(lr_r, li_r, ld_r, lr_c, li_c, ld_c, btr, bti, ctr, cti, d_r,
                    wt_ref, mi_ref, mo_ref, a_ref, kt_s):
    L = S5_L
    lr = lr_r[...]
    li = li_r[...]
    dt = jnp.exp(ld_r[...])
    xr, xi = lr * dt, li * dt
    ar, ai = _cpow(xr, xi, 1.0)
    den = lr * lr + li * li
    fr = ((ar - 1.0) * lr + ai * li) / den
    fi = (ai * lr - (ar - 1.0) * li) / den
    br, bi = btr[...], bti[...]
    bbr = fr * br - fi * bi
    bbi = fr * bi + fi * br
    cr, ci = ctr[...], cti[...]
    cr_b, ci_b = cr.astype(bf16), ci.astype(bf16)
    pr, pi = jnp.ones_like(ar), jnp.zeros_like(ai)
    for d in range(L):
        xdr = bbr * pr - bbi * pi
        xdi = bbr * pi + bbi * pr
        i = L - 1 - d
        mo_ref[i * LANES:(i + 1) * LANES, :S5_ST] = xdr.astype(bf16)
        mo_ref[i * LANES:(i + 1) * LANES, S5_ST:] = xdi.astype(bf16)
        kt = _dot(xdr.astype(bf16), cr_b) - _dot(xdi.astype(bf16), ci_b)
        if d == 0:
            r = lax.broadcasted_iota(i32, (LANES, LANES), 0)
            c = lax.broadcasted_iota(i32, (LANES, LANES), 1)
            kt = kt + jnp.where(r == c, d_r[...], 0.0)
        kt_s[d] = kt.astype(bf16)
        pr, pi = pr * ar - pi * ai, pr * ai + pi * ar
    a_ref[:, :S5_ST] = pr
    a_ref[:, S5_ST:] = pi
    zero = jnp.zeros((LANES, LANES), bf16)
    for i in range(L):
        for j in range(L):
            wt_ref[i * LANES:(i + 1) * LANES, j * LANES:(j + 1) * LANES] = kt_s[j - i] if j >= i else zero
    lrc = lr_c[...]
    lic = li_c[...]
    dtc = jnp.exp(ld_c[...])
    acr, aci = _cpow(lrc * dtc, lic * dtc, 1.0)
    pr, pi = acr, aci
    for j in range(L):
        mi_ref[:S5_ST, j * LANES:(j + 1) * LANES] = (cr * pr - ci * pi).astype(bf16)
        mi_ref[S5_ST:, j * LANES:(j + 1) * LANES] = (-(cr * pi + ci * pr)).astype(bf16)
        pr, pi = pr * acr - pi * aci, pr * aci + pi * acr


def _s5_prep(lr, li, ld, b_re, b_im, c_re, c_im, dvec):
    G, P, H, M, GPT = S5_G, S5_P, S5_CH, S5_M, S5_GPT
    eye = jnp.eye(GPT, dtype=f32)

    def rows(v):
        return v.reshape(M, 1, S5_ST)

    def cols(v):
        return v.reshape(M, S5_ST, 1)

    ldp = jnp.broadcast_to(ld[:, None], (G, P))

    def bt(b):
        b4 = b.reshape(M, GPT, P, H)
        return jnp.einsum('mgph,gk->mkhgp', b4, eye).reshape(M, LANES, S5_ST)

    def ct(c):
        c4 = c.reshape(M, GPT, H, P)
        return jnp.einsum('mghp,gk->mgpkh', c4, eye).reshape(M, S5_ST, LANES)

    L = S5_L
    mspec = lambda shape: pl.BlockSpec((None,) + shape, lambda m: (m,) + (0,) * len(shape))
    return pl.pallas_call(
        _s5_prep_kernel,
        grid=(M,),
        in_specs=[mspec((1, S5_ST))] * 3 + [mspec((S5_ST, 1))] * 3
        + [mspec((LANES, S5_ST))] * 2 + [mspec((S5_ST, LANES))] * 2 + [mspec((1, LANES))],
        out_specs=[mspec((L * LANES, L * LANES)), mspec((2 * S5_ST, L * LANES)),
                   mspec((L * LANES, 2 * S5_ST)), mspec((1, 2 * S5_ST))],
        out_shape=[jax.ShapeDtypeStruct((M, L * LANES, L * LANES), bf16),
                   jax.ShapeDtypeStruct((M, 2 * S5_ST, L * LANES), bf16),
                   jax.ShapeDtypeStruct((M, L * LANES, 2 * S5_ST), bf16),
                   jax.ShapeDtypeStruct((M, 1, 2 * S5_ST), f32)],
        scratch_shapes=[pltpu.VMEM((L, LANES, LANES), bf16)],
        compiler_params=pltpu.CompilerParams(dimension_semantics=("arbitrary",), vmem_limit_bytes=VMEM_LIMIT),
        name="s5_prep",
    )(rows(lr), rows(li), rows(ldp), cols(lr), cols(li), cols(ldp),
      bt(b_re), bt(b_im), ct(c_re), ct(c_im), dvec.reshape(M, 1, LANES))


def _s5_scan_kernel(u_ref, wt, mi, mo, a_ref, y_ref, xs, s_s, x_s):
    r = pl.program_id(1)

    @pl.when(r == 0)
    def _():
        xs[...] = jnp.zeros_like(xs)

    u = u_ref[...].reshape(S5_ROWS, S5_L * LANES)
    s = _dot(u, mo[...])
    y_intra = _dot(u, wt[...])
    NT = 2 * S5_ST // LANES
    for c in range(NT):
        for b in range(B):
            s_s[c, b * S5_PITCH:b * S5_PITCH + S5_NCH, :] = s[b * S5_NCH:(b + 1) * S5_NCH, c * LANES:(c + 1) * LANES]
    a = [a_ref[:, c * LANES:(c + 1) * LANES] for c in range(NT)]

    def step(n, x):
        rows = pl.ds(n, B, stride=S5_PITCH)
        new = []
        for c in range(NT):
            x_s[c, rows, :] = x[c]
        for c in range(NT // 2):
            ar, ai, xr, xi = a[c], a[NT // 2 + c], x[c], x[NT // 2 + c]
            new.append((ar * xr - ai * xi + s_s[c, rows, :], ar * xi + ai * xr + s_s[NT // 2 + c, rows, :]))
        return tuple(p[0] for p in new) + tuple(p[1] for p in new)

    x = tuple(xs[:, c * LANES:(c + 1) * LANES] for c in range(NT))
    for n in range(S5_NCH):
        x = step(n, x)
    for c in range(NT):
        xs[:, c * LANES:(c + 1) * LANES] = x[c]
    x_in = jnp.concatenate(
        [jnp.concatenate([x_s[c, b * S5_PITCH:b * S5_PITCH + S5_NCH, :] for b in range(B)], axis=0) for c in range(NT)],
        axis=1)
    y = y_intra + _dot(x_in.astype(bf16), mi[...])
    y_ref[...] = jax.nn.gelu(y).astype(bf16).reshape(B, S5_NCH, S5_L * LANES)


def _s5_scan(u, wt, mi, mo, a16):
    L = S5_L
    wspec = lambda shape: pl.BlockSpec((None,) + shape, lambda m, r: (m,) + (0,) * len(shape))
    uspec = pl.BlockSpec((None, B, S5_NCH, L * LANES), lambda m, r: (m, 0, r, 0))
    return pl.pallas_call(
        _s5_scan_kernel,
        grid=(S5_M, T // L // S5_NCH),
        in_specs=[uspec,
                  wspec((L * LANES, L * LANES)), wspec((2 * S5_ST, L * LANES)),
                  wspec((L * LANES, 2 * S5_ST)), wspec((1, 2 * S5_ST))],
        out_specs=uspec,
        out_shape=jax.ShapeDtypeStruct((S5_M, B, T // L, L * LANES), bf16),
        scratch_shapes=[pltpu.VMEM((B, 2 * S5_ST), f32), pltpu.VMEM((2 * S5_ST // LANES, B * S5_PITCH, LANES), f32),
                        pltpu.VMEM((2 * S5_ST // LANES, B * S5_PITCH, LANES), f32)],
        compiler_params=pltpu.CompilerParams(
            dimension_semantics=("arbitrary", "arbitrary"), vmem_limit_bytes=VMEM_LIMIT),
        name="s5_scan",
    )(u, wt, mi, mo, a16)


def _mixer_out_kernel(x_ref, gla_ref, y_ref, g1_ref, sh2_ref, sc2_ref, gluw, glub, woa, wob,
                      mpg, fpg, rw_hi, rw_lo, rb, tri_ref, upper_ref, grp_ref,
                      h1_ref, hn2_ref, pos_ref, tab_ref, gate_ref, cnt_ref, cnt_s, y_s):
    first = jnp.logical_and(pl.program_id(0) == 0, pl.program_id(1) == 0)

    @pl.when(first)
    def _():
        cnt_s[...] = jnp.zeros_like(cnt_s)

    for m in range(S5_M):
        for jj in range(S5_L):
            y_s[m, pl.ds(jj, TT // S5_L, stride=FOLD_PITCH), :] = y_ref[m, :, jj * LANES:(jj + 1) * LANES].astype(f32)
    y = jnp.concatenate(
        [jnp.concatenate([y_s[m, n * FOLD_PITCH:n * FOLD_PITCH + S5_L, :] for n in range(TT // S5_L)], axis=0)
         for m in range(S5_M)], axis=1)
    z = _dot(y.astype(bf16), gluw[...]) + glub[...]
    s5 = y * jax.nn.sigmoid(z)
    mix = _dot(gla_ref[...], woa[...]) + _dot(s5.astype(bf16), wob[...])
    h1 = x_ref[...] + g1_ref[...] * (mix * _rms(mix) * mpg[...])
    hn2 = (h1 * _rms(h1) * fpg[...]) * (1.0 + sc2_ref[...]) + sh2_ref[...]
    for cc in range(ROW):
        h1_ref[pl.ds(cc, TT, stride=ROW), :] = h1[:, cc * LANES:(cc + 1) * LANES]
        hn2_ref[pl.ds(cc, TT, stride=ROW), :] = hn2[:, cc * LANES:(cc + 1) * LANES]
    x_hi = hn2.astype(bf16)
    x_lo = (hn2 - x_hi.astype(f32)).astype(bf16)
    logits = _dot(x_hi, rw_hi[...]) + (_dot(x_lo, rw_hi[...]) + _dot(x_hi, rw_lo[...])) + rb[...]

    lane = lax.broadcasted_iota(i32, (TT, LANES), 1).astype(f32)
    l = logits
    vals, idxs = [], []
    for _ in range(TOPK):
        m = jnp.max(l, axis=-1, keepdims=True)
        ix = jnp.min(jnp.where(l == m, lane, float(LANES)), axis=-1, keepdims=True)
        vals.append(m)
        idxs.append(ix)
        l = jnp.where(lane == ix, -jnp.inf, l)
    es = [jnp.exp(v - vals[0]) for v in vals]
    tot = es[0] + es[1] + es[2] + es[3]
    oh = jnp.zeros((TT, LANES), f32)
    for ix in idxs:
        oh = oh + (lane == ix).astype(f32)
    cum = _dot(tri_ref[...], oh.astype(bf16))
    lcnt = cum[TT - 1:TT, :]
    gbase = cnt_s[...]
    hi = jnp.floor(lcnt * (1.0 / 64.0))
    lo = lcnt - hi * 64.0
    upper = upper_ref[...]
    hi8 = jnp.broadcast_to(hi, (ROW, LANES)).astype(bf16)
    lo8 = jnp.broadcast_to(lo, (ROW, LANES)).astype(bf16)
    lbase = (64.0 * _dot(hi8, upper) + _dot(lo8, upper))[0:1, :]
    loc = lbase + (cum - oh)
    tmod = (lax.broadcasted_iota(i32, (TT, LANES), 0) % PG).astype(f32) * float(TOPK)
    spread = jnp.zeros((TT, LANES), f32)
    gate = jnp.zeros((TT, LANES), f32)
    for kk in range(TOPK):
        p = jnp.sum(jnp.where(lane == idxs[kk], loc, 0.0), axis=-1, keepdims=True)
        spread = jnp.where(lane == tmod + float(kk), p, spread)
        gate = jnp.where(lane == float(kk), es[kk] / tot, gate)
    s_hi = jnp.floor(spread * (1.0 / 64.0))
    s_lo = spread - s_hi * 64.0
    grp = grp_ref[...]
    folded = 64.0 * _dot(grp, s_hi.astype(bf16)) + _dot(grp, s_lo.astype(bf16))
    pos_ref[...] = (folded * float(ROW)).astype(i32)
    sub = lax.broadcasted_iota(i32, (ROW, LANES), 0)
    tab = jnp.where(sub == 0, lcnt, jnp.where(sub == 1, lbase, jnp.where(sub == 2, gbase, 0.0)))
    tab_ref[...] = tab.astype(i32)
    gate_ref[...] = gate
    cnt = cnt_s[...] + cum[TT - 1:TT, :]
    cnt_s[...] = cnt
    cnt_ref[...] = cnt


def _mixer_out(x, gla, y, mod3, gluw, glub, woa, wob, mpg, fpg, rw_hi, rw_lo, rb):
    nt = T // TT
    full = lambda shape: pl.BlockSpec(shape, lambda b, t: (0,) * len(shape))
    tok = lambda w: pl.BlockSpec((None, TT, w), lambda b, t: (b, t, 0))
    slab = pl.BlockSpec((TT * ROW, LANES), lambda b, t: (b * nt + t, 0))
    slab_shape = jax.ShapeDtypeStruct((N * ROW, LANES), f32)
    yspec = pl.BlockSpec((S5_M, None, TT // S5_L, S5_L * LANES), lambda b, t: (0, b, t, 0))
    modspec = lambda j: pl.BlockSpec((None, 1, D), lambda b, t: (b, 0, j))
    tix = jnp.arange(TT, dtype=i32)
    lanes = jnp.arange(LANES, dtype=i32)
    tri = (tix[:, None] >= tix[None, :]).astype(bf16)
    upper = (lanes[:, None] < lanes[None, :]).astype(bf16)
    grp = (tix[None, :] // PG == jnp.arange(TT // PG, dtype=i32)[:, None]).astype(bf16)
    return pl.pallas_call(
        _mixer_out_kernel,
        grid=(B, nt),
        in_specs=[tok(D), tok(512), yspec,
                  modspec(2), modspec(3), modspec(4),
                  full((512, 512)), full((1, 512)), full((512, D)), full((512, D)),
                  full((1, D)), full((1, D)), full((D, LANES)), full((D, LANES)), full((1, LANES)),
                  full((TT, TT)), full((LANES, LANES)), full((TT // PG, TT))],
        out_specs=[slab, slab, pl.BlockSpec((TT // PG, LANES), lambda b, t: (b * nt + t, 0)),
                   pl.BlockSpec((None, ROW, LANES), lambda b, t: (b * nt + t, 0, 0)), tok(LANES), full((1, LANES))],
        out_shape=[slab_shape, slab_shape,
                   jax.ShapeDtypeStruct((N // PG, LANES), i32), jax.ShapeDtypeStruct((N // TT, ROW, LANES), i32),
                   jax.ShapeDtypeStruct((B, T, LANES), f32),
                   jax.ShapeDtypeStruct((1, LANES), f32)],
        scratch_shapes=[pltpu.VMEM((1, LANES), f32), pltpu.VMEM((S5_M, TT // S5_L * FOLD_PITCH, LANES), f32)],
        compiler_params=pltpu.CompilerParams(
            dimension_semantics=("arbitrary", "arbitrary"), vmem_limit_bytes=VMEM_LIMIT),
        name="mixer_out",
    )(x, gla, y, mod3, mod3, mod3, gluw, glub, woa, wob, mpg, fpg, rw_hi, rw_lo, rb, tri, upper, grp)


STAGE_ROWS = TT * TOPK * ROW


def _strip_copies(tab_ref, stage, region_hbm, sem, to_region):
    def expert(e, carry):
        n = tab_ref[0, e] * ROW

        @pl.when(n > 0)
        def _():
            local = stage.at[pl.ds(pl.multiple_of(tab_ref[1, e] * ROW, ROW), n), :]
            remote = region_hbm.at[pl.ds(pl.multiple_of((e * N + tab_ref[2, e]) * ROW, ROW), n), :]
            src, dst = (local, remote) if to_region else (remote, local)
            pltpu.make_async_copy(src, dst, sem).start()
        return carry
    lax.fori_loop(0, NE, expert, 0)


def _strip_wait(stage, region_hbm, sem):
    pltpu.make_async_copy(stage, region_hbm.at[pl.ds(0, STAGE_ROWS), :], sem).wait()


def _dispatch_kernel(pos_ref, tab_ref, x_ref, xs_hbm, stage, sem):
    i = pl.program_id(0)
    slot = lax.rem(i, 2)

    @pl.when(i >= 2)
    def _():
        _strip_wait(stage.at[slot], xs_hbm, sem.at[slot])

    def scatter(s):
        def group(g, carry):
            for jj in range(PG):
                v = x_ref[pl.ds(pl.multiple_of((g * PG + jj) * ROW, ROW), ROW), :]
                for kk in range(TOPK):
                    stage[s, pl.ds(pl.multiple_of(pos_ref[g, jj * TOPK + kk], ROW), ROW), :] = v
            return carry
        lax.fori_loop(0, TT // PG, group, 0)

    for s in range(2):
        pl.when(slot == s)(functools.partial(scatter, s))
    _strip_copies(tab_ref, stage.at[slot], xs_hbm, sem.at[slot], to_region=True)

    @pl.when(i == pl.num_programs(0) - 1)
    def _():
        _strip_wait(stage.at[1 - slot], xs_hbm, sem.at[1 - slot])
        _strip_wait(stage.at[slot], xs_hbm, sem.at[slot])


def _dispatch(pos, tab, hn2_rows):
    return pl.pallas_call(
        _dispatch_kernel,
        grid=(N // TT,),
        in_specs=[pl.BlockSpec((TT // PG, LANES), lambda i: (i, 0), memory_space=pltpu.SMEM),
                  pl.BlockSpec((None, ROW, LANES), lambda i: (i, 0, 0), memory_space=pltpu.SMEM),
                  pl.BlockSpec((TT * ROW, LANES), lambda i: (i, 0))],
        out_specs=pl.BlockSpec(memory_space=pl.ANY),
        out_shape=jax.ShapeDtypeStruct((NE * N * ROW, LANES), f32),
        scratch_shapes=[pltpu.VMEM((2, STAGE_ROWS, LANES), f32), pltpu.SemaphoreType.DMA((2,))],
        compiler_params=pltpu.CompilerParams(dimension_semantics=("arbitrary",), vmem_limit_bytes=VMEM_LIMIT),
        name="dispatch",
    )(pos, tab, hn2_rows)


def _experts_kernel(tb_ref, te_ref, tv_ref, tf_ref, tn_ref, ts_ref, nu_ref,
                    x_ref, wgu_hbm, bgu_ref, wd_hbm, bd_ref, y_ref, wgu_f, wd_f, wgu_b, wd_b, sem):
    i = pl.program_id(0)

    def fetch(e):
        return (pltpu.make_async_copy(wgu_hbm.at[e], wgu_f, sem.at[0]),
                pltpu.make_async_copy(wd_hbm.at[e], wd_f, sem.at[1]))

    def convert(s):
        wgu_b[s] = wgu_f[...].astype(bf16)
        wd_b[s] = wd_f[...].astype(bf16)

    def tile(s, h, nt=1):
        rows = nt * MB
        base = h * MB * ROW
        xt = jnp.concatenate([x_ref[pl.ds(base + cc, rows, stride=ROW), :] for cc in range(ROW)], axis=1)
        valid = lax.broadcasted_iota(i32, (rows, 1), 0) < tv_ref[i] - h * MB
        xt = jnp.where(valid, xt, 0.0).astype(bf16)
        gu = _dot(xt, wgu_b[s]) + bgu_ref[...]
        g = jnp.minimum(gu[:, :DFF], LIMIT)
        up = jnp.clip(gu[:, DFF:], -LIMIT, LIMIT)
        act = (up + 1.0) * (g * jax.nn.sigmoid(ALPHA * g))
        yv = _dot(act.astype(bf16), wd_b[s]) + bd_ref[...]
        for cc in range(ROW):
            y_ref[pl.ds(base + cc, rows, stride=ROW), :] = yv[:, cc * LANES:(cc + 1) * LANES]

    def tiles_from(s, h, upto):
        nv = tv_ref[i]
        few = nv <= upto * MB
        pl.when(jnp.logical_and(few, nv > (h + 1) * MB))(functools.partial(tile, s, h, 2))
        pl.when(jnp.logical_and(few, jnp.logical_and(nv > h * MB, nv <= (h + 1) * MB)))(
            functools.partial(tile, s, h, 1))

    @pl.when(i < nu_ref[0])
    def _():
        e = te_ref[i]
        s = ts_ref[i]
        nxt = tn_ref[i]
        first = (tf_ref[i] & 1) != 0
        last = (tf_ref[i] & 2) != 0

        @pl.when(i == 0)
        def _():
            for c in fetch(e):
                c.start()
            for c in fetch(e):
                c.wait()
            convert(s)

        @pl.when(jnp.logical_and(first, nxt >= 0))
        def _():
            for c in fetch(nxt):
                c.start()

        handoff = jnp.logical_and(last, nxt >= 0)

        @pl.when(handoff)
        def _():
            for c in fetch(nxt):
                c.wait()
            convert(1 - s)

        pl.when(tv_ref[i] > (EB - 1) * MB)(functools.partial(tile, s, 0, EB))
        for h in range(0, EB, 2):
            tiles_from(s, h, EB - 1)


def _experts(tile_blk, tile_e, tile_nv, tile_flags, tile_next, tile_slot, nused, xs, w_gu, b_gu, w_down, b_down):
    blk = lambda i, tb, te, *_: (tb[i], 0)
    bsel = lambda i, tb, te, *_: (te[i], 0, 0)
    grid_spec = pltpu.PrefetchScalarGridSpec(
        num_scalar_prefetch=7,
        grid=(NB,),
        in_specs=[
            pl.BlockSpec((EB * MB * ROW, LANES), blk),
            pl.BlockSpec(memory_space=pl.ANY),
            pl.BlockSpec((None, 1, 2 * DFF), bsel),
            pl.BlockSpec(memory_space=pl.ANY),
            pl.BlockSpec((None, 1, D), bsel),
        ],
        out_specs=pl.BlockSpec((EB * MB * ROW, LANES), blk),
        scratch_shapes=[pltpu.VMEM((D, 2 * DFF), f32), pltpu.VMEM((DFF, D), f32),
                        pltpu.VMEM((2, D, 2 * DFF), bf16), pltpu.VMEM((2, DFF, D), bf16),
                        pltpu.SemaphoreType.DMA((2,))],
    )
    return pl.pallas_call(
        _experts_kernel,
        grid_spec=grid_spec,
        out_shape=jax.ShapeDtypeStruct((NE * N * ROW, LANES), f32),
        compiler_params=pltpu.CompilerParams(dimension_semantics=("arbitrary",), vmem_limit_bytes=VMEM_LIMIT),
        name="experts",
    )(tile_blk, tile_e, tile_nv, tile_flags, tile_next, tile_slot, nused, xs,
      w_gu, b_gu.reshape(NE, 1, 2 * DFF), w_down, b_down.reshape(NE, 1, D))


def _combine_kernel(pos_ref, tab_ref, tabn_ref, gate_ref, h1_ref, g2_ref, pg_ref, ys_hbm, o_ref,
                    stage, wbuf, rbuf, sem):
    j = pl.program_id(0)
    slot = lax.rem(j, 2)
    GT = PG
    GR = GT * ROW

    @pl.when(j == 0)
    def _():
        _strip_copies(tab_ref, stage.at[0], ys_hbm, sem.at[0], to_region=False)

    @pl.when(j + 1 < pl.num_programs(0))
    def _():
        _strip_copies(tabn_ref, stage.at[1 - slot], ys_hbm, sem.at[1 - slot], to_region=False)

    for kk in range(TOPK):
        wbuf[kk] = jnp.broadcast_to(gate_ref[:, kk:kk + 1], (TT, LANES))
    _strip_wait(stage.at[slot], ys_hbm, sem.at[slot])
    g2 = g2_ref[...]
    pg = pg_ref[...]

    def one_group(s, g):
        rows = pl.ds(pl.multiple_of(g * GR, GR), GR)
        parts = []
        for jj in range(GT):
            t = g * GT + jj
            acc = None
            for kk in range(TOPK):
                w = jnp.broadcast_to(wbuf[kk, pl.ds(t, 1), :], (ROW, LANES))
                term = stage[s, pl.ds(pl.multiple_of(pos_ref[g, jj * TOPK + kk], ROW), ROW), :] * w
                acc = term if acc is None else acc + term
            parts.append(acc)
        ff3 = jnp.concatenate(parts, axis=0).reshape(GT, ROW, LANES)
        ss = jnp.sum(jnp.sum(ff3 * ff3, axis=2, keepdims=True), axis=1, keepdims=True)
        rs = lax.rsqrt(ss * (1.0 / D) + EPS)
        out = h1_ref[rows, :].reshape(GT, ROW, LANES) + g2 * (ff3 * rs * pg)
        rbuf[rows, :] = out.reshape(GR, LANES)

    def gather(s):
        def trip(gg, carry):
            for k in range(COMBINE_GROUPS_PER_TRIP):
                one_group(s, COMBINE_GROUPS_PER_TRIP * gg + k)
            return carry
        lax.fori_loop(0, TT // GT // COMBINE_GROUPS_PER_TRIP, trip, 0)

    for s in range(2):
        pl.when(slot == s)(functools.partial(gather, s))
    for cc in range(ROW):
        o_ref[:, cc * LANES:(cc + 1) * LANES] = rbuf[pl.ds(cc, TT, stride=ROW), :]


def _combine(pos, tab, gates, h1_rows, mod4, pg, ys):
    nt = N // TT
    per_b = T // TT
    tabspec = lambda imap: pl.BlockSpec((None, ROW, LANES), imap, memory_space=pltpu.SMEM)
    return pl.pallas_call(
        _combine_kernel,
        grid=(nt,),
        in_specs=[pl.BlockSpec((TT // PG, LANES), lambda j: (j, 0), memory_space=pltpu.SMEM),
                  tabspec(lambda j: (j, 0, 0)), tabspec(lambda j: (jnp.minimum(j + 1, nt - 1), 0, 0)),
                  pl.BlockSpec((TT, LANES), lambda j: (j, 0)),
                  pl.BlockSpec((TT * ROW, LANES), lambda j: (j, 0)),
                  pl.BlockSpec((None, None, ROW, LANES), lambda j: (j // per_b, 5, 0, 0)),
                  pl.BlockSpec((ROW, LANES), lambda j: (0, 0)),
                  pl.BlockSpec(memory_space=pl.ANY)],
        out_specs=pl.BlockSpec((TT, D), lambda j: (j, 0)),
        out_shape=jax.ShapeDtypeStruct((N, D), f32),
        scratch_shapes=[pltpu.VMEM((2, STAGE_ROWS, LANES), f32), pltpu.VMEM((TOPK, TT, LANES), f32),
                        pltpu.VMEM((TT * ROW, LANES), f32), pltpu.SemaphoreType.DMA((2,))],
        compiler_params=pltpu.CompilerParams(dimension_semantics=("arbitrary",), vmem_limit_bytes=VMEM_LIMIT),
        name="combine",
    )(pos, tab, tab, gates, h1_rows, mod4, pg, ys)


def _pad_heads(w):
    lead = w.shape[:-1]
    w4 = w.reshape(lead + (GLA_H, GLA_DK))
    w4 = jnp.concatenate([w4, jnp.zeros_like(w4)], axis=-1)
    return w4.reshape(lead + (GLA_H * LANES,))


def kernel(x, c, ada_w, ada_b, mix_pre_g, mix_post_g, ffn_pre_g, ffn_post_g, w_in, w_alpha, b_alpha, gla_norm_g, s5_lambda_re, s5_lambda_im, s5_log_dt, s5_b_re, s5_b_im, s5_c_re, s5_c_im, s5_d, s5_glu_w, s5_glu_b, w_out, router_w, router_b, exp_w_gu, exp_b_gu, exp_w_down, exp_b_down):
    l = 0
    mod = _ada(c, ada_w[l], ada_b[l:l + 1])
    mod3 = mod.reshape(B, 1, 6 * D)
    mod4 = mod.reshape(B, 6, ROW, LANES)

    w = w_in[l]
    o_q, o_k, o_v, o_g, o_a, o_u = 0, 256, 512, 1024, 1536, 1552
    wq = _pad_heads(w[:, o_q:o_k]).astype(bf16)
    wk = _pad_heads(w[:, o_k:o_v]).astype(bf16)
    wv = w[:, o_v:o_g].astype(bf16)
    wg = w[:, o_g:o_a].astype(bf16)
    wa = jnp.pad(w[:, o_a:o_u], ((0, 0), (0, LANES - LOWRANK))).astype(bf16)
    wu = w[:, o_u:].astype(bf16)
    walpha = jnp.pad(_pad_heads(w_alpha[l]), ((0, LANES - LOWRANK), (0, 0))).astype(bf16)
    balpha = _pad_heads(b_alpha[l:l + 1])

    gla, u = _mixer_in(x, mod3, mix_pre_g[l:l + 1], wq, wk, wv, wg, wa, wu,
                       walpha, balpha, gla_norm_g[l:l + 1])

    wt, mi, mo, a16 = _s5_prep(s5_lambda_re[l], s5_lambda_im[l], s5_log_dt[l], s5_b_re[l], s5_b_im[l],
                               s5_c_re[l], s5_c_im[l], s5_d[l])
    y = _s5_scan(u, wt, mi, mo, a16)

    rw = jnp.pad(router_w[l], ((0, 0), (0, LANES - NE)))
    rw_hi = rw.astype(bf16)
    rw_lo = (rw - rw_hi.astype(f32)).astype(bf16)
    rb = jnp.pad(router_b[l:l + 1], ((0, 0), (0, LANES - NE)), constant_values=-1e30)
    wo = w_out[l].astype(bf16)
    h1_rows, hn2_rows, pos, tab, gates, cnt = _mixer_out(
        x, gla, y, mod3, s5_glu_w[l].astype(bf16), s5_glu_b[l:l + 1], wo[:512], wo[512:],
        mix_post_g[l:l + 1], ffn_pre_g[l:l + 1], rw_hi, rw_lo, rb)

    counts = cnt[0, :NE].astype(i32)
    step_rows = EB * MB
    ntile = (counts + step_rows - 1) // step_rows
    tends = jnp.cumsum(ntile)
    nused = tends[-1]
    gi = jnp.minimum(jnp.arange(NB, dtype=i32), nused - 1)
    tile_e = jnp.sum((tends[None, :] <= gi[:, None]).astype(i32), axis=1)
    sel = tile_e[:, None] == jnp.arange(NE, dtype=i32)[None, :]
    pick = lambda v: jnp.sum(jnp.where(sel, v[None, :], 0), axis=1)
    tile_j = gi - pick(tends - ntile)
    tile_blk = tile_e * TPE + tile_j
    tile_nv = jnp.clip(pick(counts) - tile_j * step_rows, 0, step_rows)
    tile_flags = (tile_j == 0).astype(i32) + 2 * (tile_j == pick(ntile) - 1).astype(i32)
    used = ntile > 0
    eids = jnp.arange(NE, dtype=i32)
    later = jnp.logical_and(used[None, :], eids[None, :] > eids[:, None])
    nxt = jnp.min(jnp.where(later, eids[None, :], NE), axis=1)
    tile_next = pick(jnp.where(nxt < NE, nxt, -1))
    tile_slot = pick((jnp.cumsum(used.astype(i32)) - 1) % 2)

    xs = _dispatch(pos, tab, hn2_rows)
    ys = _experts(tile_blk, tile_e, tile_nv, tile_flags, tile_next, tile_slot, nused.reshape(1), xs,
                  exp_w_gu[l], exp_b_gu[l], exp_w_down[l], exp_b_down[l])
    out = _combine(pos, tab, gates.reshape(N, LANES), h1_rows, mod4, ffn_post_g[l].reshape(ROW, LANES), ys)
    return out.reshape(B, T, D)
```

```python
import functools
import math

import jax
import jax.numpy as jnp
from jax import lax
from jax.experimental import pallas as pl
from jax.experimental.pallas import tpu as pltpu

f32 = jnp.float32
bf16 = jnp.bfloat16
i32 = jnp.int32

D = 1024
B = 8
T = 2048
N = B * T
GLA_H = 4
GLA_DV = 128
GLA_DK = 64
GLA_TAU = 16.0
GLA_CHUNK = 64
LOWRANK = 16
S5_W = 512
S5_CH = 16
S5_G = 32
S5_P = 64
NE = 32
TOPK = 4
DFF = 1024
ALPHA = 1.702
LIMIT = 7.0
EPS = 1e-6

LANES = 128
SUBLANES = 8
VMEM_LIMIT = 56 * 1024 * 1024

TT = 512
S5_L = 16
S5_M = S5_W // LANES
S5_GPT = LANES // S5_CH
S5_ST = S5_GPT * S5_P
S5_NCH = 32
S5_ROWS = B * S5_NCH
S5_PITCH = S5_NCH + SUBLANES
FOLD_PITCH = S5_L + SUBLANES
ROW = SUBLANES
MB = 256
EB = 4
NB = (N * TOPK) // (EB * MB) + NE
TPE = N // (EB * MB)
PG = 16
COMBINE_GROUPS_PER_TRIP = 8


def _dot(a, b):
    return jnp.dot(a, b, preferred_element_type=f32)


def _dot_t(a, b, ca, cb):
    return lax.dot_general(a, b, (((ca,), (cb,)), ((), ())), preferred_element_type=f32)


def _rms(x):
    return lax.rsqrt(jnp.mean(x * x, axis=-1, keepdims=True) + EPS)


def _ada_kernel(c_ref, w_ref, b_ref, o_ref):
    c = c_ref[...]
    s = (c * jax.nn.sigmoid(c)).astype(bf16)
    o_ref[...] = _dot(s, w_ref[...].astype(bf16)) + b_ref[...]


def _ada(c, w, b):
    return pl.pallas_call(
        _ada_kernel,
        grid=(6,),
        in_specs=[
            pl.BlockSpec((B, D), lambda j: (0, 0)),
            pl.BlockSpec((D, D), lambda j: (0, j)),
            pl.BlockSpec((1, D), lambda j: (0, j)),
        ],
        out_specs=pl.BlockSpec((B, D), lambda j: (0, j)),
        out_shape=jax.ShapeDtypeStruct((B, 6 * D), f32),
        name="ada",
    )(c, w, b)


def _mixer_in_kernel(x_ref, sc_ref, sh_ref, g_ref, wq, wk, wv, wg, wa, wu, walpha, balpha, gng, cm_ref,
                     gla_ref, u_ref, st_s, u_s):
    t = pl.program_id(1)

    @pl.when(t == 0)
    def _():
        st_s[...] = jnp.zeros_like(st_s)

    x = x_ref[...]
    hn = (x * _rms(x) * g_ref[...]) * (1.0 + sc_ref[...]) + sh_ref[...]
    hb = hn.astype(bf16)
    u = _dot(hb, wu[...])

    C = GLA_CHUNK
    NC = TT // C
    causal = cm_ref[...] > 0.5
    tri = cm_ref[...].astype(bf16)
    a_lr = _dot(hb, wa[...]).astype(bf16)
    la = jax.nn.log_sigmoid(_dot(a_lr, walpha[...]) + balpha[...]) * (1.0 / GLA_TAU)
    la_hi = la.astype(bf16)
    la_lo = (la - la_hi.astype(f32)).astype(bf16)
    bc = _dot(tri, la_hi) + _dot(tri, la_lo)
    bl = jnp.broadcast_to(bc.reshape(NC, C, GLA_H * LANES)[:, C - 1:C, :],
                          (NC, C, GLA_H * LANES)).reshape(TT, GLA_H * LANES)
    q = _dot(hb, wq[...]) * (GLA_DK ** -0.5)
    k = _dot(hb, wk[...])
    qd = (q * jnp.exp(bc)).astype(bf16)
    ki = (k * jnp.exp(-bc)).astype(bf16)
    ke = k * jnp.exp(bl - bc)
    dec = jnp.exp(bl)
    vb = _dot(hb, wv[...]).astype(bf16)
    og = _dot(hb, wg[...])
    gn = gng[...]
    heads = range(GLA_H)
    sls = [slice(h * LANES, (h + 1) * LANES) for h in heads]
    kehb = ke.astype(bf16)
    raw = [_dot_t(qd[:, sl], ki[:, sl], 1, 1) for sl in sls]
    deltas = [[_dot_t(vb[c * C:(c + 1) * C, sl], kehb[c * C:(c + 1) * C, sl], 0, 0) for c in range(NC)] for sl in sls]
    scs = [jnp.where(causal, r, 0.0).astype(bf16) for r in raw]
    o_intra = [_dot(scs[h], vb[:, sls[h]]) for h in heads]
    sts = []
    for h in heads:
        st = st_s[h]
        per = []
        for c in range(NC):
            per.append(st.astype(bf16))
            st = st * dec[c * C:c * C + 1, sls[h]] + deltas[h][c]
        st_s[h] = st
        sts.append(per)
    for h in heads:
        sl = sls[h]
        o_inter = jnp.concatenate([_dot_t(qd[c * C:(c + 1) * C, sl], sts[h][c], 1, 1) for c in range(NC)], axis=0)
        o = o_intra[h] + o_inter
        on = o * _rms(o) * gn
        ogh = og[:, sl]
        gla_ref[:, sl] = (on * (ogh * jax.nn.sigmoid(ogh))).astype(bf16)
        for n in range(TT // S5_L):
            u_s[h, n * FOLD_PITCH:n * FOLD_PITCH + S5_L, :] = u[n * S5_L:(n + 1) * S5_L, sl]
        for i in range(S5_L):
            u_ref[h, :, i * LANES:(i + 1) * LANES] = u_s[h, pl.ds(i, TT // S5_L, stride=FOLD_PITCH), :].astype(bf16)


def _mixer_in(x, mod3, g, wq, wk, wv, wg, wa, wu, walpha, balpha, gng):
    assert S5_M == GLA_H
    nt = T // TT
    full = lambda shape: pl.BlockSpec(shape, lambda b, t: (0,) * len(shape))
    u_shape = jax.ShapeDtypeStruct((S5_M, B, T // S5_L, S5_L * LANES), bf16)
    u_spec = pl.BlockSpec((S5_M, None, TT // S5_L, S5_L * LANES), lambda b, t: (0, b, t, 0))
    tok = jnp.arange(TT, dtype=i32)
    same = (tok[:, None] // GLA_CHUNK) == (tok[None, :] // GLA_CHUNK)
    cmask = jnp.logical_and(tok[:, None] >= tok[None, :], same).astype(f32)
    return pl.pallas_call(
        _mixer_in_kernel,
        grid=(B, nt),
        in_specs=[
            pl.BlockSpec((None, TT, D), lambda b, t: (b, t, 0)),
            pl.BlockSpec((None, 1, D), lambda b, t: (b, 0, 1)),
            pl.BlockSpec((None, 1, D), lambda b, t: (b, 0, 0)),
            full((1, D)),
            full((D, 512)), full((D, 512)), full((D, 512)), full((D, 512)),
            full((D, LANES)), full((D, 512)),
            full((LANES, 512)), full((1, 512)), full((1, LANES)),
            full((TT, TT)),
        ],
        out_specs=[pl.BlockSpec((None, TT, 512), lambda b, t: (b, t, 0)), u_spec],
        out_shape=[jax.ShapeDtypeStruct((B, T, 512), bf16), u_shape],
        scratch_shapes=[pltpu.VMEM((GLA_H, GLA_DV, LANES), f32),
                        pltpu.VMEM((S5_M, TT // S5_L * FOLD_PITCH, LANES), f32)],
        compiler_params=pltpu.CompilerParams(
            dimension_semantics=("arbitrary", "arbitrary"), vmem_limit_bytes=VMEM_LIMIT),
        name="mixer_in",
    )(x, mod3, mod3, g, wq, wk, wv, wg, wa, wu, walpha, balpha, gng, cmask)


def _cpow(xr, xi, d):
    mag = jnp.exp(xr * d)
    return mag * jnp.cos(xi * d), mag * jnp.sin(xi * d)


def _s5_prep_kernel(lr_r, li_r, ld_r, lr_c, li_c, ld_c, btr, bti, ctr, cti, d_r,
                    wt_ref, mi_ref, mo_ref, a_ref, kt_s):
    L = S5_L
    lr = lr_r[...]
    li = li_r[...]
    dt = jnp.exp(ld_r[...])
    xr, xi = lr * dt, li * dt
    ar, ai = _cpow(xr, xi, 1.0)
    den = lr * lr + li * li
    fr = ((ar - 1.0) * lr + ai * li) / den
    fi = (ai * lr - (ar - 1.0) * li) / den
    br, bi = btr[...], bti[...]
    bbr = fr * br - fi * bi
    bbi = fr * bi + fi * br
    cr, ci = ctr[...], cti[...]
    cr_b, ci_b = cr.astype(bf16), ci.astype(bf16)
    pr, pi = jnp.ones_like(ar), jnp.zeros_like(ai)
    for d in range(L):
        xdr = bbr * pr - bbi * pi
        xdi = bbr * pi + bbi * pr
        i = L - 1 - d
        mo_ref[i * LANES:(i + 1) * LANES, :S5_ST] = xdr.astype(bf16)
        mo_ref[i * LANES:(i + 1) * LANES, S5_ST:] = xdi.astype(bf16)
        kt = _dot(xdr.astype(bf16), cr_b) - _dot(xdi.astype(bf16), ci_b)
        if d == 0:
            r = lax.broadcasted_iota(i32, (LANES, LANES), 0)
            c = lax.broadcasted_iota(i32, (LANES, LANES), 1)
            kt = kt + jnp.where(r == c, d_r[...], 0.0)
        kt_s[d] = kt.astype(bf16)
        pr, pi = pr * ar - pi * ai, pr * ai + pi * ar
    a_ref[:, :S5_ST] = pr
    a_ref[:, S5_ST:] = pi
    zero = jnp.zeros((LANES, LANES), bf16)
    for i in range(L):
        for j in range(L):
            wt_ref[i * LANES:(i + 1) * LANES, j * LANES:(j + 1) * LANES] = kt_s[j - i] if j >= i else zero
    lrc = lr_c[...]
    lic = li_c[...]
    dtc = jnp.exp(ld_c[...])
    acr, aci = _cpow(lrc * dtc, lic * dtc, 1.0)
    pr, pi = acr, aci
    for j in range(L):
        mi_ref[:S5_ST, j * LANES:(j + 1) * LANES] = (cr * pr - ci * pi).astype(bf16)
        mi_ref[S5_ST:, j * LANES:(j + 1) * LANES] = (-(cr * pi + ci * pr)).astype(bf16)
        pr, pi = pr * acr - pi * aci, pr * aci + pi * acr


def _s5_prep(lr, li, ld, b_re, b_im, c_re, c_im, dvec):
    G, P, H, M, GPT = S5_G, S5_P, S5_CH, S5_M, S5_GPT
    eye = jnp.eye(GPT, dtype=f32)

    def rows(v):
        return v.reshape(M, 1, S5_ST)

    def cols(v):
        return v.reshape(M, S5_ST, 1)

    ldp = jnp.broadcast_to(ld[:, None], (G, P))

    def bt(b):
        b4 = b.reshape(M, GPT, P, H)
        return jnp.einsum('mgph,gk->mkhgp', b4, eye).reshape(M, LANES, S5_ST)

    def ct(c):
        c4 = c.reshape(M, GPT, H, P)
        return jnp.einsum('mghp,gk->mgpkh', c4, eye).reshape(M, S5_ST, LANES)

    L = S5_L
    mspec = lambda shape: pl.BlockSpec((None,) + shape, lambda m: (m,) + (0,) * len(shape))
    return pl.pallas_call(
        _s5_prep_kernel,
        grid=(M,),
        in_specs=[mspec((1, S5_ST))] * 3 + [mspec((S5_ST, 1))] * 3
        + [mspec((LANES, S5_ST))] * 2 + [mspec((S5_ST, LANES))] * 2 + [mspec((1, LANES))],
        out_specs=[mspec((L * LANES, L * LANES)), mspec((2 * S5_ST, L * LANES)),
                   mspec((L * LANES, 2 * S5_ST)), mspec((1, 2 * S5_ST))],
        out_shape=[jax.ShapeDtypeStruct((M, L * LANES, L * LANES), bf16),
                   jax.ShapeDtypeStruct((M, 2 * S5_ST, L * LANES), bf16),
                   jax.ShapeDtypeStruct((M, L * LANES, 2 * S5_ST), bf16),
                   jax.ShapeDtypeStruct((M, 1, 2 * S5_ST), f32)],
        scratch_shapes=[pltpu.VMEM((L, LANES, LANES), bf16)],
        compiler_params=pltpu.CompilerParams(dimension_semantics=("arbitrary",), vmem_limit_bytes=VMEM_LIMIT),
        name="s5_prep",
    )(rows(lr), rows(li), rows(ldp), cols(lr), cols(li), cols(ldp),
      bt(b_re), bt(b_im), ct(c_re), ct(c_im), dvec.reshape(M, 1, LANES))


def _s5_scan_kernel(u_ref, wt, mi, mo, a_ref, y_ref, xs, s_s, x_s):
    r = pl.program_id(1)

    @pl.when(r == 0)
    def _():
        xs[...] = jnp.zeros_like(xs)

    u = u_ref[...].reshape(S5_ROWS, S5_L * LANES)
    s = _dot(u, mo[...])
    y_intra = _dot(u, wt[...])
    NT = 2 * S5_ST // LANES
    for c in range(NT):
        for b in range(B):
            s_s[c, b * S5_PITCH:b * S5_PITCH + S5_NCH, :] = s[b * S5_NCH:(b + 1) * S5_NCH, c * LANES:(c + 1) * LANES]
    a = [a_ref[:, c * LANES:(c + 1) * LANES] for c in range(NT)]

    def step(n, x):
        rows = pl.ds(n, B, stride=S5_PITCH)
        new = []
        for c in range(NT):
            x_s[c, rows, :] = x[c]
        for c in range(NT // 2):
            ar, ai, xr, xi = a[c], a[NT // 2 + c], x[c], x[NT // 2 + c]
            new.append((ar * xr - ai * xi + s_s[c, rows, :], ar * xi + ai * xr + s_s[NT // 2 + c, rows, :]))
        return tuple(p[0] for p in new) + tuple(p[1] for p in new)

    x = tuple(xs[:, c * LANES:(c + 1) * LANES] for c in range(NT))
    for n in range(S5_NCH):
        x = step(n, x)
    for c in range(NT):
        xs[:, c * LANES:(c + 1) * LANES] = x[c]
    x_in = jnp.concatenate(
        [jnp.concatenate([x_s[c, b * S5_PITCH:b * S5_PITCH + S5_NCH, :] for b in range(B)], axis=0) for c in range(NT)],
        axis=1)
    y = y_intra + _dot(x_in.astype(bf16), mi[...])
    y_ref[...] = jax.nn.gelu(y).astype(bf16).reshape(B, S5_NCH, S5_L * LANES)


def _s5_scan(u, wt, mi, mo, a16):
    L = S5_L
    wspec = lambda shape: pl.BlockSpec((None,) + shape, lambda m, r: (m,) + (0,) * len(shape))
    uspec = pl.BlockSpec((None, B, S5_NCH, L * LANES), lambda m, r: (m, 0, r, 0))
    return pl.pallas_call(
        _s5_scan_kernel,
        grid=(S5_M, T // L // S5_NCH),
        in_specs=[uspec,
                  wspec((L * LANES, L * LANES)), wspec((2 * S5_ST, L * LANES)),
                  wspec((L * LANES, 2 * S5_ST)), wspec((1, 2 * S5_ST))],
        out_specs=uspec,
        out_shape=jax.ShapeDtypeStruct((S5_M, B, T // L, L * LANES), bf16),
        scratch_shapes=[pltpu.VMEM((B, 2 * S5_ST), f32), pltpu.VMEM((2 * S5_ST // LANES, B * S5_PITCH, LANES), f32),
                        pltpu.VMEM((2 * S5_ST // LANES, B * S5_PITCH, LANES), f32)],
        compiler_params=pltpu.CompilerParams(
            dimension_semantics=("arbitrary", "arbitrary"), vmem_limit_bytes=VMEM_LIMIT),
        name="s5_scan",
    )(u, wt, mi, mo, a16)


STAGE_ROWS = TT * TOPK * ROW


def _strip_copies(tab_ref, stage, region_hbm, sem, to_region):
    def expert(e, carry):
        n = tab_ref[0, e] * ROW

        @pl.when(n > 0)
        def _():
            local = stage.at[pl.ds(pl.multiple_of(tab_ref[1, e] * ROW, ROW), n), :]
            remote = region_hbm.at[pl.ds(pl.multiple_of((e * N + tab_ref[2, e]) * ROW, ROW), n), :]
            src, dst = (local, remote) if to_region else (remote, local)
            pltpu.make_async_copy(src, dst, sem).start()
        return carry
    lax.fori_loop(0, NE, expert, 0)


def _strip_wait(stage, region_hbm, sem):
    pltpu.make_async_copy(stage, region_hbm.at[pl.ds(0, STAGE_ROWS), :], sem).wait()


def _mixer_out_kernel(x_ref, gla_ref, y_ref, g1_ref, sh2_ref, sc2_ref, gluw, glub, woa, wob,
                      mpg, fpg, rw_hi, rw_lo, rb, tri_ref, upper_ref, grp_ref,
                      h1_ref, xs_hbm, pos_ref, tab_ref, gate_ref, cnt_ref,
                      cnt_s, y_s, hn2_s, stage, pos_sm, tab_sm, sem):
    step = pl.program_id(0) * pl.num_programs(1) + pl.program_id(1)
    first = step == 0

    @pl.when(first)
    def _():
        cnt_s[...] = jnp.zeros_like(cnt_s)

    for m in range(S5_M):
        for jj in range(S5_L):
            y_s[m, pl.ds(jj, TT // S5_L, stride=FOLD_PITCH), :] = y_ref[m, :, jj * LANES:(jj + 1) * LANES].astype(f32)
    y = jnp.concatenate(
        [jnp.concatenate([y_s[m, n * FOLD_PITCH:n * FOLD_PITCH + S5_L, :] for n in range(TT // S5_L)], axis=0)
         for m in range(S5_M)], axis=1)
    z = _dot(y.astype(bf16), gluw[...]) + glub[...]
    s5 = y * jax.nn.sigmoid(z)
    mix = _dot(gla_ref[...], woa[...]) + _dot(s5.astype(bf16), wob[...])
    h1 = x_ref[...] + g1_ref[...] * (mix * _rms(mix) * mpg[...])
    hn2 = (h1 * _rms(h1) * fpg[...]) * (1.0 + sc2_ref[...]) + sh2_ref[...]
    for cc in range(ROW):
        h1_ref[pl.ds(cc, TT, stride=ROW), :] = h1[:, cc * LANES:(cc + 1) * LANES]
        hn2_s[pl.ds(cc, TT, stride=ROW), :] = hn2[:, cc * LANES:(cc + 1) * LANES]
    x_hi = hn2.astype(bf16)
    x_lo = (hn2 - x_hi.astype(f32)).astype(bf16)
    logits = _dot(x_hi, rw_hi[...]) + (_dot(x_lo, rw_hi[...]) + _dot(x_hi, rw_lo[...])) + rb[...]

    lane = lax.broadcasted_iota(i32, (TT, LANES), 1).astype(f32)
    l = logits
    vals, idxs = [], []
    for _ in range(TOPK):
        m = jnp.max(l, axis=-1, keepdims=True)
        ix = jnp.min(jnp.where(l == m, lane, float(LANES)), axis=-1, keepdims=True)
        vals.append(m)
        idxs.append(ix)
        l = jnp.where(lane == ix, -jnp.inf, l)
    es = [jnp.exp(v - vals[0]) for v in vals]
    tot = es[0] + es[1] + es[2] + es[3]
    oh = jnp.zeros((TT, LANES), f32)
    for ix in idxs:
        oh = oh + (lane == ix).astype(f32)
    cum = _dot(tri_ref[...], oh.astype(bf16))
    lcnt = cum[TT - 1:TT, :]
    gbase = cnt_s[...]
    hi = jnp.floor(lcnt * (1.0 / 64.0))
    lo = lcnt - hi * 64.0
    upper = upper_ref[...]
    hi8 = jnp.broadcast_to(hi, (ROW, LANES)).astype(bf16)
    lo8 = jnp.broadcast_to(lo, (ROW, LANES)).astype(bf16)
    lbase = (64.0 * _dot(hi8, upper) + _dot(lo8, upper))[0:1, :]
    loc = lbase + (cum - oh)
    tmod = (lax.broadcasted_iota(i32, (TT, LANES), 0) % PG).astype(f32) * float(TOPK)
    spread = jnp.zeros((TT, LANES), f32)
    gate = jnp.zeros((TT, LANES), f32)
    for kk in range(TOPK):
        p = jnp.sum(jnp.where(lane == idxs[kk], loc, 0.0), axis=-1, keepdims=True)
        spread = jnp.where(lane == tmod + float(kk), p, spread)
        gate = jnp.where(lane == float(kk), es[kk] / tot, gate)
    s_hi = jnp.floor(spread * (1.0 / 64.0))
    s_lo = spread - s_hi * 64.0
    grp = grp_ref[...]
    folded = 64.0 * _dot(grp, s_hi.astype(bf16)) + _dot(grp, s_lo.astype(bf16))
    pos_ref[...] = (folded * float(ROW)).astype(i32)
    sub = lax.broadcasted_iota(i32, (ROW, LANES), 0)
    tab = jnp.where(sub == 0, lcnt, jnp.where(sub == 1, lbase, jnp.where(sub == 2, gbase, 0.0)))
    tab_ref[...] = tab.astype(i32)
    gate_ref[...] = gate
    cnt = cnt_s[...] + cum[TT - 1:TT, :]
    cnt_s[...] = cnt
    cnt_ref[...] = cnt

    pltpu.sync_copy(pos_ref, pos_sm)
    pltpu.sync_copy(tab_ref, tab_sm)
    slot = lax.rem(step, 2)

    @pl.when(step >= 2)
    def _():
        _strip_wait(stage.at[slot], xs_hbm, sem.at[slot])

    def scatter(s):
        def group(g, carry):
            for jj in range(PG):
                v = hn2_s[pl.ds(pl.multiple_of((g * PG + jj) * ROW, ROW), ROW), :]
                for kk in range(TOPK):
                    stage[s, pl.ds(pl.multiple_of(pos_sm[g, jj * TOPK + kk], ROW), ROW), :] = v
            return carry
        lax.fori_loop(0, TT // PG, group, 0)

    for s in range(2):
        pl.when(slot == s)(functools.partial(scatter, s))
    _strip_copies(tab_sm, stage.at[slot], xs_hbm, sem.at[slot], to_region=True)

    @pl.when(step == pl.num_programs(0) * pl.num_programs(1) - 1)
    def _():
        _strip_wait(stage.at[1 - slot], xs_hbm, sem.at[1 - slot])
        _strip_wait(stage.at[slot], xs_hbm, sem.at[slot])


def _mixer_out(x, gla, y, mod3, gluw, glub, woa, wob, mpg, fpg, rw_hi, rw_lo, rb):
    nt = T // TT
    full = lambda shape: pl.BlockSpec(shape, lambda b, t: (0,) * len(shape))
    tok = lambda w: pl.BlockSpec((None, TT, w), lambda b, t: (b, t, 0))
    slab = pl.BlockSpec((TT * ROW, LANES), lambda b, t: (b * nt + t, 0))
    slab_shape = jax.ShapeDtypeStruct((N * ROW, LANES), f32)
    yspec = pl.BlockSpec((S5_M, None, TT // S5_L, S5_L * LANES), lambda b, t: (0, b, t, 0))
    modspec = lambda j: pl.BlockSpec((None, 1, D), lambda b, t: (b, 0, j))
    tix = jnp.arange(TT, dtype=i32)
    lanes = jnp.arange(LANES, dtype=i32)
    tri = (tix[:, None] >= tix[None, :]).astype(bf16)
    upper = (lanes[:, None] < lanes[None, :]).astype(bf16)
    grp = (tix[None, :] // PG == jnp.arange(TT // PG, dtype=i32)[:, None]).astype(bf16)
    return pl.pallas_call(
        _mixer_out_kernel,
        grid=(B, nt),
        in_specs=[tok(D), tok(512), yspec,
                  modspec(2), modspec(3), modspec(4),
                  full((512, 512)), full((1, 512)), full((512, D)), full((512, D)),
                  full((1, D)), full((1, D)), full((D, LANES)), full((D, LANES)), full((1, LANES)),
                  full((TT, TT)), full((LANES, LANES)), full((TT // PG, TT))],
        out_specs=[slab, pl.BlockSpec(memory_space=pl.ANY),
                   pl.BlockSpec((TT // PG, LANES), lambda b, t: (b * nt + t, 0)),
                   pl.BlockSpec((ROW, LANES), lambda b, t: (b * nt + t, 0)), tok(LANES), full((1, LANES))],
        out_shape=[slab_shape, jax.ShapeDtypeStruct((NE * N * ROW, LANES), f32),
                   jax.ShapeDtypeStruct((N // PG, LANES), i32), jax.ShapeDtypeStruct((N // TT * ROW, LANES), i32),
                   jax.ShapeDtypeStruct((B, T, LANES), f32),
                   jax.ShapeDtypeStruct((1, LANES), f32)],
        scratch_shapes=[pltpu.VMEM((1, LANES), f32), pltpu.VMEM((S5_M, TT // S5_L * FOLD_PITCH, LANES), f32),
                        pltpu.VMEM((TT * ROW, LANES), f32), pltpu.VMEM((2, STAGE_ROWS, LANES), f32),
                        pltpu.SMEM((TT // PG, LANES), i32), pltpu.SMEM((ROW, LANES), i32),
                        pltpu.SemaphoreType.DMA((2,))],
        compiler_params=pltpu.CompilerParams(
            dimension_semantics=("arbitrary", "arbitrary"), vmem_limit_bytes=VMEM_LIMIT),
        name="mixer_out",
    )(x, gla, y, mod3, mod3, mod3, gluw, glub, woa, wob, mpg, fpg, rw_hi, rw_lo, rb, tri, upper, grp)


def _experts_kernel(tb_ref, te_ref, tv_ref, tf_ref, tn_ref, ts_ref, nu_ref,
                    x_ref, wgu_hbm, bgu_ref, wd_hbm, bd_ref, y_ref, wgu_f, wd_f, wgu_b, wd_b, sem):
    i = pl.program_id(0)

    def fetch(e):
        return (pltpu.make_async_copy(wgu_hbm.at[e], wgu_f, sem.at[0]),
                pltpu.make_async_copy(wd_hbm.at[e], wd_f, sem.at[1]))

    def convert(s):
        wgu_b[s] = wgu_f[...].astype(bf16)
        wd_b[s] = wd_f[...].astype(bf16)

    def tile(s, h, nt=1):
        rows = nt * MB
        base = h * MB * ROW
        xt = jnp.concatenate([x_ref[pl.ds(base + cc, rows, stride=ROW), :] for cc in range(ROW)], axis=1)
        valid = lax.broadcasted_iota(i32, (rows, 1), 0) < tv_ref[i] - h * MB
        xt = jnp.where(valid, xt, 0.0).astype(bf16)
        gu = _dot(xt, wgu_b[s]) + bgu_ref[...]
        g = jnp.minimum(gu[:, :DFF], LIMIT)
        up = jnp.clip(gu[:, DFF:], -LIMIT, LIMIT)
        act = (up + 1.0) * (g * jax.nn.sigmoid(ALPHA * g))
        yv = _dot(act.astype(bf16), wd_b[s]) + bd_ref[...]
        for cc in range(ROW):
            y_ref[pl.ds(base + cc, rows, stride=ROW), :] = yv[:, cc * LANES:(cc + 1) * LANES]

    def tiles_from(s, h, upto):
        nv = tv_ref[i]
        few = nv <= upto * MB
        pl.when(jnp.logical_and(few, nv > (h + 1) * MB))(functools.partial(tile, s, h, 2))
        pl.when(jnp.logical_and(few, jnp.logical_and(nv > h * MB, nv <= (h + 1) * MB)))(
            functools.partial(tile, s, h, 1))

    @pl.when(i < nu_ref[0])
    def _():
        e = te_ref[i]
        s = ts_ref[i]
        nxt = tn_ref[i]
        first = (tf_ref[i] & 1) != 0
        last = (tf_ref[i] & 2) != 0

        @pl.when(i == 0)
        def _():
            for c in fetch(e):
                c.start()
            for c in fetch(e):
                c.wait()
            convert(s)

        @pl.when(jnp.logical_and(first, nxt >= 0))
        def _():
            for c in fetch(nxt):
                c.start()

        handoff = jnp.logical_and(last, nxt >= 0)

        @pl.when(handoff)
        def _():
            for c in fetch(nxt):
                c.wait()
            convert(1 - s)

        pl.when(tv_ref[i] > (EB - 1) * MB)(functools.partial(tile, s, 0, EB))
        for h in range(0, EB, 2):
            tiles_from(s, h, EB - 1)


def _experts(tile_blk, tile_e, tile_nv, tile_flags, tile_next, tile_slot, nused, xs, w_gu, b_gu, w_down, b_down):
    blk = lambda i, tb, te, *_: (tb[i], 0)
    bsel = lambda i, tb, te, *_: (te[i], 0, 0)
    grid_spec = pltpu.PrefetchScalarGridSpec(
        num_scalar_prefetch=7,
        grid=(NB,),
        in_specs=[
            pl.BlockSpec((EB * MB * ROW, LANES), blk),
            pl.BlockSpec(memory_space=pl.ANY),
            pl.BlockSpec((None, 1, 2 * DFF), bsel),
            pl.BlockSpec(memory_space=pl.ANY),
            pl.BlockSpec((None, 1, D), bsel),
        ],
        out_specs=pl.BlockSpec((EB * MB * ROW, LANES), blk),
        scratch_shapes=[pltpu.VMEM((D, 2 * DFF), f32), pltpu.VMEM((DFF, D), f32),
                        pltpu.VMEM((2, D, 2 * DFF), bf16), pltpu.VMEM((2, DFF, D), bf16),
                        pltpu.SemaphoreType.DMA((2,))],
    )
    return pl.pallas_call(
        _experts_kernel,
        grid_spec=grid_spec,
        out_shape=jax.ShapeDtypeStruct((NE * N * ROW, LANES), f32),
        compiler_params=pltpu.CompilerParams(dimension_semantics=("arbitrary",), vmem_limit_bytes=VMEM_LIMIT),
        name="experts",
    )(tile_blk, tile_e, tile_nv, tile_flags, tile_next, tile_slot, nused, xs,
      w_gu, b_gu.reshape(NE, 1, 2 * DFF), w_down, b_down.reshape(NE, 1, D))


def _combine_kernel(pos_ref, tab_ref, tabn_ref, gate_ref, h1_ref, g2_ref, pg_ref, ys_hbm, o_ref,
                    stage, wbuf, rbuf, sem):
    j = pl.program_id(0)
    slot = lax.rem(j, 2)
    GT = PG
    GR = GT * ROW

    @pl.when(j == 0)
    def _():
        _strip_copies(tab_ref, stage.at[0], ys_hbm, sem.at[0], to_region=False)

    @pl.when(j + 1 < pl.num_programs(0))
    def _():
        _strip_copies(tabn_ref, stage.at[1 - slot], ys_hbm, sem.at[1 - slot], to_region=False)

    for kk in range(TOPK):
        wbuf[kk] = jnp.broadcast_to(gate_ref[:, kk:kk + 1], (TT, LANES))
    _strip_wait(stage.at[slot], ys_hbm, sem.at[slot])
    g2 = g2_ref[...]
    pg = pg_ref[...]

    def one_group(s, g):
        rows = pl.ds(pl.multiple_of(g * GR, GR), GR)
        parts = []
        for jj in range(GT):
            t = g * GT + jj
            acc = None
            for kk in range(TOPK):
                w = jnp.broadcast_to(wbuf[kk, pl.ds(t, 1), :], (ROW, LANES))
                term = stage[s, pl.ds(pl.multiple_of(pos_ref[g, jj * TOPK + kk], ROW), ROW), :] * w
                acc = term if acc is None else acc + term
            parts.append(acc)
        ff3 = jnp.concatenate(parts, axis=0).reshape(GT, ROW, LANES)
        ss = jnp.sum(jnp.sum(ff3 * ff3, axis=2, keepdims=True), axis=1, keepdims=True)
        rs = lax.rsqrt(ss * (1.0 / D) + EPS)
        out = h1_ref[rows, :].reshape(GT, ROW, LANES) + g2 * (ff3 * rs * pg)
        rbuf[rows, :] = out.reshape(GR, LANES)

    def gather(s):
        def trip(gg, carry):
            for k in range(COMBINE_GROUPS_PER_TRIP):
                one_group(s, COMBINE_GROUPS_PER_TRIP * gg + k)
            return carry
        lax.fori_loop(0, TT // GT // COMBINE_GROUPS_PER_TRIP, trip, 0)

    for s in range(2):
        pl.when(slot == s)(functools.partial(gather, s))
    for cc in range(ROW):
        o_ref[:, cc * LANES:(cc + 1) * LANES] = rbuf[pl.ds(cc, TT, stride=ROW), :]


def _combine(pos, tab, gates, h1_rows, mod4, pg, ys):
    nt = N // TT
    per_b = T // TT
    tabspec = lambda imap: pl.BlockSpec((ROW, LANES), imap, memory_space=pltpu.SMEM)
    return pl.pallas_call(
        _combine_kernel,
        grid=(nt,),
        in_specs=[pl.BlockSpec((TT // PG, LANES), lambda j: (j, 0), memory_space=pltpu.SMEM),
                  tabspec(lambda j: (j, 0)), tabspec(lambda j: (jnp.minimum(j + 1, nt - 1), 0)),
                  pl.BlockSpec((TT, LANES), lambda j: (j, 0)),
                  pl.BlockSpec((TT * ROW, LANES), lambda j: (j, 0)),
                  pl.BlockSpec((None, None, ROW, LANES), lambda j: (j // per_b, 5, 0, 0)),
                  pl.BlockSpec((ROW, LANES), lambda j: (0, 0)),
                  pl.BlockSpec(memory_space=pl.ANY)],
        out_specs=pl.BlockSpec((TT, D), lambda j: (j, 0)),
        out_shape=jax.ShapeDtypeStruct((N, D), f32),
        scratch_shapes=[pltpu.VMEM((2, STAGE_ROWS, LANES), f32), pltpu.VMEM((TOPK, TT, LANES), f32),
                        pltpu.VMEM((TT * ROW, LANES), f32), pltpu.SemaphoreType.DMA((2,))],
        compiler_params=pltpu.CompilerParams(dimension_semantics=("arbitrary",), vmem_limit_bytes=VMEM_LIMIT),
        name="combine",
    )(pos, tab, tab, gates, h1_rows, mod4, pg, ys)


def _pad_heads(w):
    lead = w.shape[:-1]
    w4 = w.reshape(lead + (GLA_H, GLA_DK))
    w4 = jnp.concatenate([w4, jnp.zeros_like(w4)], axis=-1)
    return w4.reshape(lead + (GLA_H * LANES,))


def kernel(x, c, ada_w, ada_b, mix_pre_g, mix_post_g, ffn_pre_g, ffn_post_g, w_in, w_alpha, b_alpha, gla_norm_g, s5_lambda_re, s5_lambda_im, s5_log_dt, s5_b_re, s5_b_im, s5_c_re, s5_c_im, s5_d, s5_glu_w, s5_glu_b, w_out, router_w, router_b, exp_w_gu, exp_b_gu, exp_w_down, exp_b_down):
    l = 0
    mod = _ada(c, ada_w[l], ada_b[l:l + 1])
    mod3 = mod.reshape(B, 1, 6 * D)
    mod4 = mod.reshape(B, 6, ROW, LANES)

    w = w_in[l]
    o_q, o_k, o_v, o_g, o_a, o_u = 0, 256, 512, 1024, 1536, 1552
    wq = _pad_heads(w[:, o_q:o_k]).astype(bf16)
    wk = _pad_heads(w[:, o_k:o_v]).astype(bf16)
    wv = w[:, o_v:o_g].astype(bf16)
    wg = w[:, o_g:o_a].astype(bf16)
    wa = jnp.pad(w[:, o_a:o_u], ((0, 0), (0, LANES - LOWRANK))).astype(bf16)
    wu = w[:, o_u:].astype(bf16)
    walpha = jnp.pad(_pad_heads(w_alpha[l]), ((0, LANES - LOWRANK), (0, 0))).astype(bf16)
    balpha = _pad_heads(b_alpha[l:l + 1])

    gla, u = _mixer_in(x, mod3, mix_pre_g[l:l + 1], wq, wk, wv, wg, wa, wu,
                       walpha, balpha, gla_norm_g[l:l + 1])

    wt, mi, mo, a16 = _s5_prep(s5_lambda_re[l], s5_lambda_im[l], s5_log_dt[l], s5_b_re[l], s5_b_im[l],
                               s5_c_re[l], s5_c_im[l], s5_d[l])
    y = _s5_scan(u, wt, mi, mo, a16)

    rw = jnp.pad(router_w[l], ((0, 0), (0, LANES - NE)))
    rw_hi = rw.astype(bf16)
    rw_lo = (rw - rw_hi.astype(f32)).astype(bf16)
    rb = jnp.pad(router_b[l:l + 1], ((0, 0), (0, LANES - NE)), constant_values=-1e30)
    wo = w_out[l].astype(bf16)
    h1_rows, xs, pos, tab, gates, cnt = _mixer_out(
        x, gla, y, mod3, s5_glu_w[l].astype(bf16), s5_glu_b[l:l + 1], wo[:512], wo[512:],
        mix_post_g[l:l + 1], ffn_pre_g[l:l + 1], rw_hi, rw_lo, rb)

    counts = cnt[0, :NE].astype(i32)
    step_rows = EB * MB
    ntile = (counts + step_rows - 1) // step_rows
    tends = jnp.cumsum(ntile)
    nused = tends[-1]
    gi = jnp.minimum(jnp.arange(NB, dtype=i32), nused - 1)
    tile_e = jnp.sum((tends[None, :] <= gi[:, None]).astype(i32), axis=1)
    sel = tile_e[:, None] == jnp.arange(NE, dtype=i32)[None, :]
    pick = lambda v: jnp.sum(jnp.where(sel, v[None, :], 0), axis=1)
    tile_j = gi - pick(tends - ntile)
    tile_blk = tile_e * TPE + tile_j
    tile_nv = jnp.clip(pick(counts) - tile_j * step_rows, 0, step_rows)
    tile_flags = (tile_j == 0).astype(i32) + 2 * (tile_j == pick(ntile) - 1).astype(i32)
    used = ntile > 0
    eids = jnp.arange(NE, dtype=i32)
    later = jnp.logical_and(used[None, :], eids[None, :] > eids[:, None])
    nxt = jnp.min(jnp.where(later, eids[None, :], NE), axis=1)
    tile_next = pick(jnp.where(nxt < NE, nxt, -1))
    tile_slot = pick((jnp.cumsum(used.astype(i32)) - 1) % 2)

    ys = _experts(tile_blk, tile_e, tile_nv, tile_flags, tile_next, tile_slot, nused.reshape(1), xs,
                  exp_w_gu[l], exp_b_gu[l], exp_w_down[l], exp_b_down[l])
    out = _combine(pos, tab, gates.reshape(N, LANES), h1_rows, mod4, ffn_post_g[l].reshape(ROW, LANES), ys)
    return out.reshape(B, T, D)
```

```python
import functools
import math

import jax
import jax.numpy as jnp
from jax import lax
from jax.experimental import pallas as pl
from jax.experimental.pallas import tpu as pltpu

f32 = jnp.float32
bf16 = jnp.bfloat16
i32 = jnp.int32

D = 1024
B = 8
T = 2048
N = B * T
GLA_H = 4
GLA_DV = 128
GLA_DK = 64
GLA_TAU = 16.0
GLA_CHUNK = 64
LOWRANK = 16
S5_W = 512
S5_CH = 16
S5_G = 32
S5_P = 64
NE = 32
TOPK = 4
DFF = 1024
ALPHA = 1.702
LIMIT = 7.0
EPS = 1e-6

LANES = 128
SUBLANES = 8
VMEM_LIMIT = 56 * 1024 * 1024

TT = 512
S5_L = 16
S5_M = S5_W // LANES
S5_GPT = LANES // S5_CH
S5_ST = S5_GPT * S5_P
S5_NCH = 32
S5_ROWS = B * S5_NCH
S5_PITCH = S5_NCH + SUBLANES
FOLD_PITCH = S5_L + SUBLANES
ROW = SUBLANES
MB = 256
EB = 4
NB = (N * TOPK) // (EB * MB) + NE
TPE = N // (EB * MB)
PG = 16
COMBINE_GROUPS_PER_TRIP = 8


def _dot(a, b):
    return jnp.dot(a, b, preferred_element_type=f32)


def _dot_t(a, b, ca, cb):
    return lax.dot_general(a, b, (((ca,), (cb,)), ((), ())), preferred_element_type=f32)


def _rms(x):
    return lax.rsqrt(jnp.mean(x * x, axis=-1, keepdims=True) + EPS)


def _ada_kernel(c_ref, w_ref, b_ref, o_ref):
    c = c_ref[...]
    s = (c * jax.nn.sigmoid(c)).astype(bf16)
    o_ref[...] = _dot(s, w_ref[...].astype(bf16)) + b_ref[...]


def _ada(c, w, b):
    return pl.pallas_call(
        _ada_kernel,
        grid=(6,),
        in_specs=[
            pl.BlockSpec((B, D), lambda j: (0, 0)),
            pl.BlockSpec((D, D), lambda j: (0, j)),
            pl.BlockSpec((1, D), lambda j: (0, j)),
        ],
        out_specs=pl.BlockSpec((B, D), lambda j: (0, j)),
        out_shape=jax.ShapeDtypeStruct((B, 6 * D), f32),
        name="ada",
    )(c, w, b)


def _mixer_in_kernel(x_ref, sc_ref, sh_ref, g_ref, wq, wk, wv, wg, wa, wu, walpha, balpha, gng, cm_ref,
                     gla_ref, u_ref, st_s, u_s):
    t = pl.program_id(1)

    @pl.when(t == 0)
    def _():
        st_s[...] = jnp.zeros_like(st_s)

    x = x_ref[...]
    hn = (x * _rms(x) * g_ref[...]) * (1.0 + sc_ref[...]) + sh_ref[...]
    hb = hn.astype(bf16)
    u = _dot(hb, wu[...])

    C = GLA_CHUNK
    NC = TT // C
    causal = cm_ref[...] > 0.5
    tri = cm_ref[...].astype(bf16)
    a_lr = _dot(hb, wa[...]).astype(bf16)
    la = jax.nn.log_sigmoid(_dot(a_lr, walpha[...]) + balpha[...]) * (1.0 / GLA_TAU)
    la_hi = la.astype(bf16)
    la_lo = (la - la_hi.astype(f32)).astype(bf16)
    bc = _dot(tri, la_hi) + _dot(tri, la_lo)
    bl = jnp.broadcast_to(bc.reshape(NC, C, GLA_H * LANES)[:, C - 1:C, :],
                          (NC, C, GLA_H * LANES)).reshape(TT, GLA_H * LANES)
    q = _dot(hb, wq[...]) * (GLA_DK ** -0.5)
    k = _dot(hb, wk[...])
    qd = (q * jnp.exp(bc)).astype(bf16)
    ki = (k * jnp.exp(-bc)).astype(bf16)
    ke = k * jnp.exp(bl - bc)
    dec = jnp.exp(bl)
    vb = _dot(hb, wv[...]).astype(bf16)
    og = _dot(hb, wg[...])
    gn = gng[...]
    heads = range(GLA_H)
    sls = [slice(h * LANES, (h + 1) * LANES) for h in heads]
    kehb = ke.astype(bf16)
    raw = [_dot_t(qd[:, sl], ki[:, sl], 1, 1) for sl in sls]
    deltas = [[_dot_t(vb[c * C:(c + 1) * C, sl], kehb[c * C:(c + 1) * C, sl], 0, 0) for c in range(NC)] for sl in sls]
    scs = [jnp.where(causal, r, 0.0).astype(bf16) for r in raw]
    o_intra = [_dot(scs[h], vb[:, sls[h]]) for h in heads]
    sts = []
    for h in heads:
        st = st_s[h]
        per = []
        for c in range(NC):
            per.append(st.astype(bf16))
            st = st * dec[c * C:c * C + 1, sls[h]] + deltas[h][c]
        st_s[h] = st
        sts.append(per)
    for h in heads:
        sl = sls[h]
        o_inter = jnp.concatenate([_dot_t(qd[c * C:(c + 1) * C, sl], sts[h][c], 1, 1) for c in range(NC)], axis=0)
        o = o_intra[h] + o_inter
        on = o * _rms(o) * gn
        ogh = og[:, sl]
        gla_ref[:, sl] = (on * (ogh * jax.nn.sigmoid(ogh))).astype(bf16)
        for n in range(TT // S5_L):
            u_s[h, n * FOLD_PITCH:n * FOLD_PITCH + S5_L, :] = u[n * S5_L:(n + 1) * S5_L, sl]
        for i in range(S5_L):
            u_ref[h, :, i * LANES:(i + 1) * LANES] = u_s[h, pl.ds(i, TT // S5_L, stride=FOLD_PITCH), :].astype(bf16)


def _mixer_in(x, mod3, g, wq, wk, wv, wg, wa, wu, walpha, balpha, gng):
    assert S5_M == GLA_H
    nt = T // TT
    full = lambda shape: pl.BlockSpec(shape, lambda b, t: (0,) * len(shape))
    u_shape = jax.ShapeDtypeStruct((S5_M, B, T // S5_L, S5_L * LANES), bf16)
    u_spec = pl.BlockSpec((S5_M, None, TT // S5_L, S5_L * LANES), lambda b, t: (0, b, t, 0))
    tok = jnp.arange(TT, dtype=i32)
    same = (tok[:, None] // GLA_CHUNK) == (tok[None, :] // GLA_CHUNK)
    cmask = jnp.logical_and(tok[:, None] >= tok[None, :], same).astype(f32)
    return pl.pallas_call(
        _mixer_in_kernel,
        grid=(B, nt),
        in_specs=[
            pl.BlockSpec((None, TT, D), lambda b, t: (b, t, 0)),
            pl.BlockSpec((None, 1, D), lambda b, t: (b, 0, 1)),
            pl.BlockSpec((None, 1, D), lambda b, t: (b, 0, 0)),
            full((1, D)),
            full((D, 512)), full((D, 512)), full((D, 512)), full((D, 512)),
            full((D, LANES)), full((D, 512)),
            full((LANES, 512)), full((1, 512)), full((1, LANES)),
            full((TT, TT)),
        ],
        out_specs=[pl.BlockSpec((None, TT, 512), lambda b, t: (b, t, 0)), u_spec],
        out_shape=[jax.ShapeDtypeStruct((B, T, 512), bf16), u_shape],
        scratch_shapes=[pltpu.VMEM((GLA_H, GLA_DV, LANES), f32),
                        pltpu.VMEM((S5_M, TT // S5_L * FOLD_PITCH, LANES), f32)],
        compiler_params=pltpu.CompilerParams(
            dimension_semantics=("arbitrary", "arbitrary"), vmem_limit_bytes=VMEM_LIMIT),
        name="mixer_in",
    )(x, mod3, mod3, g, wq, wk, wv, wg, wa, wu, walpha, balpha, gng, cmask)


def _cpow(xr, xi, d):
    mag = jnp.exp(xr * d)
    return mag * jnp.cos(xi * d), mag * jnp.sin(xi * d)


def _s5_prep_kernel(lr_r, li_r, ld_r, lr_c, li_c, ld_c, btr, bti, ctr, cti, d_r,
                    wt_ref, mi_ref, mo_ref, a_ref, kt_s):
    L = S5_L
    lr = lr_r[...]
    li = li_r[...]
    dt = jnp.exp(ld_r[...])
    xr, xi = lr * dt, li * dt
    ar, ai = _cpow(xr, xi, 1.0)
    den = lr * lr + li * li
    fr = ((ar - 1.0) * lr + ai * li) / den
    fi = (ai * lr - (ar - 1.0) * li) / den
    br, bi = btr[...], bti[...]
    bbr = fr * br - fi * bi
    bbi = fr * bi + fi * br
    cr, ci = ctr[...], cti[...]
    cr_b, ci_b = cr.astype(bf16), ci.astype(bf16)
    pr, pi = jnp.ones_like(ar), jnp.zeros_like(ai)
    for d in range(L):
        xdr = bbr * pr - bbi * pi
        xdi = bbr * pi + bbi * pr
        i = L - 1 - d
        mo_ref[i * LANES:(i + 1) * LANES, :S5_ST] = xdr.astype(bf16)
        mo_ref[i * LANES:(i + 1) * LANES, S5_ST:] = xdi.astype(bf16)
        kt = _dot(xdr.astype(bf16), cr_b) - _dot(xdi.astype(bf16), ci_b)
        if d == 0:
            r = lax.broadcasted_iota(i32, (LANES, LANES), 0)
            c = lax.broadcasted_iota(i32, (LANES, LANES), 1)
            kt = kt + jnp.where(r == c, d_r[...], 0.0)
        kt_s[d] = kt.astype(bf16)
        pr, pi = pr * ar - pi * ai, pr * ai + pi * ar
    a_ref[:, :S5_ST] = pr
    a_ref[:, S5_ST:] = pi
    zero = jnp.zeros((LANES, LANES), bf16)
    for i in range(L):
        for j in range(L):
            wt_ref[i * LANES:(i + 1) * LANES, j * LANES:(j + 1) * LANES] = kt_s[j - i] if j >= i else zero
    lrc = lr_c[...]
    lic = li_c[...]
    dtc = jnp.exp(ld_c[...])
    acr, aci = _cpow(lrc * dtc, lic * dtc, 1.0)
    pr, pi = acr, aci
    for j in range(L):
        mi_ref[:S5_ST, j * LANES:(j + 1) * LANES] = (cr * pr - ci * pi).astype(bf16)
        mi_ref[S5_ST:, j * LANES:(j + 1) * LANES] = (-(cr * pi + ci * pr)).astype(bf16)
        pr, pi = pr * acr - pi * aci, pr * aci + pi * acr


def _s5_prep(lr, li, ld, b_re, b_im, c_re, c_im, dvec):
    G, P, H, M, GPT = S5_G, S5_P, S5_CH, S5_M, S5_GPT
    eye = jnp.eye(GPT, dtype=f32)

    def rows(v):
        return v.reshape(M, 1, S5_ST)

    def cols(v):
        return v.reshape(M, S5_ST, 1)

    ldp = jnp.broadcast_to(ld[:, None], (G, P))

    def bt(b):
        b4 = b.reshape(M, GPT, P, H)
        return jnp.einsum('mgph,gk->mkhgp', b4, eye).reshape(M, LANES, S5_ST)

    def ct(c):
        c4 = c.reshape(M, GPT, H, P)
        return jnp.einsum('mghp,gk->mgpkh', c4, eye).reshape(M, S5_ST, LANES)

    L = S5_L
    mspec = lambda shape: pl.BlockSpec((None,) + shape, lambda m: (m,) + (0,) * len(shape))
    return pl.pallas_call(
        _s5_prep_kernel,
        grid=(M,),
        in_specs=[mspec((1, S5_ST))] * 3 + [mspec((S5_ST, 1))] * 3
        + [mspec((LANES, S5_ST))] * 2 + [mspec((S5_ST, LANES))] * 2 + [mspec((1, LANES))],
        out_specs=[mspec((L * LANES, L * LANES)), mspec((2 * S5_ST, L * LANES)),
                   mspec((L * LANES, 2 * S5_ST)), mspec((1, 2 * S5_ST))],
        out_shape=[jax.ShapeDtypeStruct((M, L * LANES, L * LANES), bf16),
                   jax.ShapeDtypeStruct((M, 2 * S5_ST, L * LANES), bf16),
                   jax.ShapeDtypeStruct((M, L * LANES, 2 * S5_ST), bf16),
                   jax.ShapeDtypeStruct((M, 1, 2 * S5_ST), f32)],
        scratch_shapes=[pltpu.VMEM((L, LANES, LANES), bf16)],
        compiler_params=pltpu.CompilerParams(dimension_semantics=("arbitrary",), vmem_limit_bytes=VMEM_LIMIT),
        name="s5_prep",
    )(rows(lr), rows(li), rows(ldp), cols(lr), cols(li), cols(ldp),
      bt(b_re), bt(b_im), ct(c_re), ct(c_im), dvec.reshape(M, 1, LANES))


def _s5_scan_kernel(u_ref, wt, mi, mo, a_ref, y_ref, xs, s_s, x_s):
    r = pl.program_id(1)

    @pl.when(r == 0)
    def _():
        xs[...] = jnp.zeros_like(xs)

    u = u_ref[...].reshape(S5_ROWS, S5_L * LANES)
    s = _dot(u, mo[...])
    y_intra = _dot(u, wt[...])
    NT = 2 * S5_ST // LANES
    for c in range(NT):
        for b in range(B):
            s_s[c, b * S5_PITCH:b * S5_PITCH + S5_NCH, :] = s[b * S5_NCH:(b + 1) * S5_NCH, c * LANES:(c + 1) * LANES]
    a = [a_ref[:, c * LANES:(c + 1) * LANES] for c in range(NT)]

    def step(n, x):
        rows = pl.ds(n, B, stride=S5_PITCH)
        new = []
        for c in range(NT):
            x_s[c, rows, :] = x[c]
        for c in range(NT // 2):
            ar, ai, xr, xi = a[c], a[NT // 2 + c], x[c], x[NT // 2 + c]
            new.append((ar * xr - ai * xi + s_s[c, rows, :], ar * xi + ai * xr + s_s[NT // 2 + c, rows, :]))
        return tuple(p[0] for p in new) + tuple(p[1] for p in new)

    x = tuple(xs[:, c * LANES:(c + 1) * LANES] for c in range(NT))
    for n in range(S5_NCH):
        x = step(n, x)
    for c in range(NT):
        xs[:, c * LANES:(c + 1) * LANES] = x[c]
    x_in = jnp.concatenate(
        [jnp.concatenate([x_s[c, b * S5_PITCH:b * S5_PITCH + S5_NCH, :] for b in range(B)], axis=0) for c in range(NT)],
        axis=1)
    y = y_intra + _dot(x_in.astype(bf16), mi[...])
    y_ref[...] = jax.nn.gelu(y).astype(bf16).reshape(B, S5_NCH, S5_L * LANES)


def _s5_scan(u, wt, mi, mo, a16):
    L = S5_L
    wspec = lambda shape: pl.BlockSpec((None,) + shape, lambda m, r: (m,) + (0,) * len(shape))
    uspec = pl.BlockSpec((None, B, S5_NCH, L * LANES), lambda m, r: (m, 0, r, 0))
    return pl.pallas_call(
        _s5_scan_kernel,
        grid=(S5_M, T // L // S5_NCH),
        in_specs=[uspec,
                  wspec((L * LANES, L * LANES)), wspec((2 * S5_ST, L * LANES)),
                  wspec((L * LANES, 2 * S5_ST)), wspec((1, 2 * S5_ST))],
        out_specs=uspec,
        out_shape=jax.ShapeDtypeStruct((S5_M, B, T // L, L * LANES), bf16),
        scratch_shapes=[pltpu.VMEM((B, 2 * S5_ST), f32), pltpu.VMEM((2 * S5_ST // LANES, B * S5_PITCH, LANES), f32),
                        pltpu.VMEM((2 * S5_ST // LANES, B * S5_PITCH, LANES), f32)],
        compiler_params=pltpu.CompilerParams(
            dimension_semantics=("arbitrary", "arbitrary"), vmem_limit_bytes=VMEM_LIMIT),
        name="s5_scan",
    )(u, wt, mi, mo, a16)


STAGE_ROWS = TT * TOPK * ROW


def _strip_copies(tab_ref, stage, region_hbm, sem, to_region):
    def expert(e, carry):
        n = tab_ref[0, e] * ROW

        @pl.when(n > 0)
        def _():
            local = stage.at[pl.ds(pl.multiple_of(tab_ref[1, e] * ROW, ROW), n), :]
            remote = region_hbm.at[pl.ds(pl.multiple_of((e * N + tab_ref[2, e]) * ROW, ROW), n), :]
            src, dst = (local, remote) if to_region else (remote, local)
            pltpu.make_async_copy(src, dst, sem).start()
        return carry
    lax.fori_loop(0, NE, expert, 0)


def _strip_wait(stage, region_hbm, sem):
    pltpu.make_async_copy(stage, region_hbm.at[pl.ds(0, STAGE_ROWS), :], sem).wait()


def _mixer_out_kernel(x_ref, gla_ref, y_ref, g1_ref, sh2_ref, sc2_ref, gluw, glub, woa, wob,
                      mpg, fpg, rw_hi, rw_lo, rb, tri_ref, upper_ref, grp_ref,
                      h1_ref, xs_hbm, pos_ref, tab_ref, gate_ref, cnt_ref,
                      cnt_s, y_s, hn2_s, stage, pos_v, tab_v, pos_sm, tab_sm, sem, tsem):
    step = pl.program_id(0) * pl.num_programs(1) + pl.program_id(1)
    first = step == 0

    @pl.when(first)
    def _():
        cnt_s[...] = jnp.zeros_like(cnt_s)

    for m in range(S5_M):
        for jj in range(S5_L):
            y_s[m, pl.ds(jj, TT // S5_L, stride=FOLD_PITCH), :] = y_ref[m, :, jj * LANES:(jj + 1) * LANES].astype(f32)
    y = jnp.concatenate(
        [jnp.concatenate([y_s[m, n * FOLD_PITCH:n * FOLD_PITCH + S5_L, :] for n in range(TT // S5_L)], axis=0)
         for m in range(S5_M)], axis=1)
    z = _dot(y.astype(bf16), gluw[...]) + glub[...]
    s5 = y * jax.nn.sigmoid(z)
    mix = _dot(gla_ref[...], woa[...]) + _dot(s5.astype(bf16), wob[...])
    h1 = x_ref[...] + g1_ref[...] * (mix * _rms(mix) * mpg[...])
    hn2 = (h1 * _rms(h1) * fpg[...]) * (1.0 + sc2_ref[...]) + sh2_ref[...]
    for cc in range(ROW):
        h1_ref[pl.ds(cc, TT, stride=ROW), :] = h1[:, cc * LANES:(cc + 1) * LANES]
        hn2_s[lax.rem(step, 2), pl.ds(cc, TT, stride=ROW), :] = hn2[:, cc * LANES:(cc + 1) * LANES]
    x_hi = hn2.astype(bf16)
    x_lo = (hn2 - x_hi.astype(f32)).astype(bf16)
    logits = _dot(x_hi, rw_hi[...]) + (_dot(x_lo, rw_hi[...]) + _dot(x_hi, rw_lo[...])) + rb[...]

    lane = lax.broadcasted_iota(i32, (TT, LANES), 1).astype(f32)
    l = logits
    vals, idxs = [], []
    for _ in range(TOPK):
        m = jnp.max(l, axis=-1, keepdims=True)
        ix = jnp.min(jnp.where(l == m, lane, float(LANES)), axis=-1, keepdims=True)
        vals.append(m)
        idxs.append(ix)
        l = jnp.where(lane == ix, -jnp.inf, l)
    es = [jnp.exp(v - vals[0]) for v in vals]
    tot = es[0] + es[1] + es[2] + es[3]
    oh = jnp.zeros((TT, LANES), f32)
    for ix in idxs:
        oh = oh + (lane == ix).astype(f32)
    cum = _dot(tri_ref[...], oh.astype(bf16))
    lcnt = cum[TT - 1:TT, :]
    gbase = cnt_s[...]
    hi = jnp.floor(lcnt * (1.0 / 64.0))
    lo = lcnt - hi * 64.0
    upper = upper_ref[...]
    hi8 = jnp.broadcast_to(hi, (ROW, LANES)).astype(bf16)
    lo8 = jnp.broadcast_to(lo, (ROW, LANES)).astype(bf16)
    lbase = (64.0 * _dot(hi8, upper) + _dot(lo8, upper))[0:1, :]
    loc = lbase + (cum - oh)
    tmod = (lax.broadcasted_iota(i32, (TT, LANES), 0) % PG).astype(f32) * float(TOPK)
    spread = jnp.zeros((TT, LANES), f32)
    gate = jnp.zeros((TT, LANES), f32)
    for kk in range(TOPK):
        p = jnp.sum(jnp.where(lane == idxs[kk], loc, 0.0), axis=-1, keepdims=True)
        spread = jnp.where(lane == tmod + float(kk), p, spread)
        gate = jnp.where(lane == float(kk), es[kk] / tot, gate)
    s_hi = jnp.floor(spread * (1.0 / 64.0))
    s_lo = spread - s_hi * 64.0
    grp = grp_ref[...]
    folded = 64.0 * _dot(grp, s_hi.astype(bf16)) + _dot(grp, s_lo.astype(bf16))
    posi = (folded * float(ROW)).astype(i32)
    pos_ref[...] = posi
    pos_v[lax.rem(step, 2)] = posi
    sub = lax.broadcasted_iota(i32, (ROW, LANES), 0)
    tab = jnp.where(sub == 0, lcnt, jnp.where(sub == 1, lbase, jnp.where(sub == 2, gbase, 0.0)))
    tabi = tab.astype(i32)
    tab_ref[...] = tabi
    tab_v[lax.rem(step, 2)] = tabi
    gate_ref[...] = gate
    cnt = cnt_s[...] + cum[TT - 1:TT, :]
    cnt_s[...] = cnt
    cnt_ref[...] = cnt

    nsteps = pl.num_programs(0) * pl.num_programs(1)
    cur = lax.rem(step, 2)

    def table_copies():
        return (pltpu.make_async_copy(pos_v.at[cur], pos_sm, tsem.at[0]),
                pltpu.make_async_copy(tab_v.at[cur], tab_sm, tsem.at[1]))

    def dispatch(s):
        def group(g, carry):
            for jj in range(PG):
                v = hn2_s[s, pl.ds(pl.multiple_of((g * PG + jj) * ROW, ROW), ROW), :]
                for kk in range(TOPK):
                    stage[s, pl.ds(pl.multiple_of(pos_sm[g, jj * TOPK + kk], ROW), ROW), :] = v
            return carry
        lax.fori_loop(0, TT // PG, group, 0)
        _strip_copies(tab_sm, stage.at[s], xs_hbm, sem.at[s], to_region=True)

    @pl.when(step >= 1)
    def _():
        for c in table_copies():
            c.wait()

        @pl.when(step >= 3)
        def _():
            _strip_wait(stage.at[1 - cur], xs_hbm, sem.at[1 - cur])
        for s in range(2):
            pl.when(1 - cur == s)(functools.partial(dispatch, s))

    for c in table_copies():
        c.start()

    @pl.when(step == nsteps - 1)
    def _():
        for c in table_copies():
            c.wait()

        @pl.when(step >= 2)
        def _():
            _strip_wait(stage.at[cur], xs_hbm, sem.at[cur])
        for s in range(2):
            pl.when(cur == s)(functools.partial(dispatch, s))
        _strip_wait(stage.at[1 - cur], xs_hbm, sem.at[1 - cur])
        _strip_wait(stage.at[cur], xs_hbm, sem.at[cur])


def _mixer_out(x, gla, y, mod3, gluw, glub, woa, wob, mpg, fpg, rw_hi, rw_lo, rb):
    nt = T // TT
    full = lambda shape: pl.BlockSpec(shape, lambda b, t: (0,) * len(shape))
    tok = lambda w: pl.BlockSpec((None, TT, w), lambda b, t: (b, t, 0))
    slab = pl.BlockSpec((TT * ROW, LANES), lambda b, t: (b * nt + t, 0))
    slab_shape = jax.ShapeDtypeStruct((N * ROW, LANES), f32)
    yspec = pl.BlockSpec((S5_M, None, TT // S5_L, S5_L * LANES), lambda b, t: (0, b, t, 0))
    modspec = lambda j: pl.BlockSpec((None, 1, D), lambda b, t: (b, 0, j))
    tix = jnp.arange(TT, dtype=i32)
    lanes = jnp.arange(LANES, dtype=i32)
    tri = (tix[:, None] >= tix[None, :]).astype(bf16)
    upper = (lanes[:, None] < lanes[None, :]).astype(bf16)
    grp = (tix[None, :] // PG == jnp.arange(TT // PG, dtype=i32)[:, None]).astype(bf16)
    return pl.pallas_call(
        _mixer_out_kernel,
        grid=(B, nt),
        in_specs=[tok(D), tok(512), yspec,
                  modspec(2), modspec(3), modspec(4),
                  full((512, 512)), full((1, 512)), full((512, D)), full((512, D)),
                  full((1, D)), full((1, D)), full((D, LANES)), full((D, LANES)), full((1, LANES)),
                  full((TT, TT)), full((LANES, LANES)), full((TT // PG, TT))],
        out_specs=[slab, pl.BlockSpec(memory_space=pl.ANY),
                   pl.BlockSpec((TT // PG, LANES), lambda b, t: (b * nt + t, 0)),
                   pl.BlockSpec((ROW, LANES), lambda b, t: (b * nt + t, 0)), tok(LANES), full((1, LANES))],
        out_shape=[slab_shape, jax.ShapeDtypeStruct((NE * N * ROW, LANES), f32),
                   jax.ShapeDtypeStruct((N // PG, LANES), i32), jax.ShapeDtypeStruct((N // TT * ROW, LANES), i32),
                   jax.ShapeDtypeStruct((B, T, LANES), f32),
                   jax.ShapeDtypeStruct((1, LANES), f32)],
        scratch_shapes=[pltpu.VMEM((1, LANES), f32), pltpu.VMEM((S5_M, TT // S5_L * FOLD_PITCH, LANES), f32),
                        pltpu.VMEM((2, TT * ROW, LANES), f32), pltpu.VMEM((2, STAGE_ROWS, LANES), f32),
                        pltpu.VMEM((2, TT // PG, LANES), i32), pltpu.VMEM((2, ROW, LANES), i32),
                        pltpu.SMEM((TT // PG, LANES), i32), pltpu.SMEM((ROW, LANES), i32),
                        pltpu.SemaphoreType.DMA((2,)), pltpu.SemaphoreType.DMA((2,))],
        compiler_params=pltpu.CompilerParams(
            dimension_semantics=("arbitrary", "arbitrary"), vmem_limit_bytes=VMEM_LIMIT),
        name="mixer_out",
    )(x, gla, y, mod3, mod3, mod3, gluw, glub, woa, wob, mpg, fpg, rw_hi, rw_lo, rb, tri, upper, grp)


def _experts_kernel(tb_ref, te_ref, tv_ref, tf_ref, tn_ref, ts_ref, nu_ref,
                    x_ref, wgu_hbm, bgu_ref, wd_hbm, bd_ref, y_ref, wgu_f, wd_f, wgu_b, wd_b, sem):
    i = pl.program_id(0)

    def fetch(e):
        return (pltpu.make_async_copy(wgu_hbm.at[e], wgu_f, sem.at[0]),
                pltpu.make_async_copy(wd_hbm.at[e], wd_f, sem.at[1]))

    def convert(s):
        wgu_b[s] = wgu_f[...].astype(bf16)
        wd_b[s] = wd_f[...].astype(bf16)

    def tile(s, h, nt=1):
        rows = nt * MB
        base = h * MB * ROW
        xt = jnp.concatenate([x_ref[pl.ds(base + cc, rows, stride=ROW), :] for cc in range(ROW)], axis=1)
        valid = lax.broadcasted_iota(i32, (rows, 1), 0) < tv_ref[i] - h * MB
        xt = jnp.where(valid, xt, 0.0).astype(bf16)
        gu = _dot(xt, wgu_b[s]) + bgu_ref[...]
        g = jnp.minimum(gu[:, :DFF], LIMIT)
        up = jnp.clip(gu[:, DFF:], -LIMIT, LIMIT)
        act = (up + 1.0) * (g * jax.nn.sigmoid(ALPHA * g))
        yv = _dot(act.astype(bf16), wd_b[s]) + bd_ref[...]
        for cc in range(ROW):
            y_ref[pl.ds(base + cc, rows, stride=ROW), :] = yv[:, cc * LANES:(cc + 1) * LANES]

    def tiles_from(s, h, upto):
        nv = tv_ref[i]
        few = nv <= upto * MB
        pl.when(jnp.logical_and(few, nv > (h + 1) * MB))(functools.partial(tile, s, h, 2))
        pl.when(jnp.logical_and(few, jnp.logical_and(nv > h * MB, nv <= (h + 1) * MB)))(
            functools.partial(tile, s, h, 1))

    @pl.when(i < nu_ref[0])
    def _():
        e = te_ref[i]
        s = ts_ref[i]
        nxt = tn_ref[i]
        first = (tf_ref[i] & 1) != 0
        last = (tf_ref[i] & 2) != 0

        @pl.when(i == 0)
        def _():
            for c in fetch(e):
                c.start()
            for c in fetch(e):
                c.wait()
            convert(s)

        @pl.when(jnp.logical_and(first, nxt >= 0))
        def _():
            for c in fetch(nxt):
                c.start()

        handoff = jnp.logical_and(last, nxt >= 0)

        @pl.when(handoff)
        def _():
            for c in fetch(nxt):
                c.wait()
            convert(1 - s)

        pl.when(tv_ref[i] > (EB - 1) * MB)(functools.partial(tile, s, 0, EB))
        for h in range(0, EB, 2):
            tiles_from(s, h, EB - 1)


def _experts(tile_blk, tile_e, tile_nv, tile_flags, tile_next, tile_slot, nused, xs, w_gu, b_gu, w_down, b_down):
    blk = lambda i, tb, te, *_: (tb[i], 0)
    bsel = lambda i, tb, te, *_: (te[i], 0, 0)
    grid_spec = pltpu.PrefetchScalarGridSpec(
        num_scalar_prefetch=7,
        grid=(NB,),
        in_specs=[
            pl.BlockSpec((EB * MB * ROW, LANES), blk),
            pl.BlockSpec(memory_space=pl.ANY),
            pl.BlockSpec((None, 1, 2 * DFF), bsel),
            pl.BlockSpec(memory_space=pl.ANY),
            pl.BlockSpec((None, 1, D), bsel),
        ],
        out_specs=pl.BlockSpec((EB * MB * ROW, LANES), blk),
        scratch_shapes=[pltpu.VMEM((D, 2 * DFF), f32), pltpu.VMEM((DFF, D), f32),
                        pltpu.VMEM((2, D, 2 * DFF), bf16), pltpu.VMEM((2, DFF, D), bf16),
                        pltpu.SemaphoreType.DMA((2,))],
    )
    return pl.pallas_call(
        _experts_kernel,
        grid_spec=grid_spec,
        out_shape=jax.ShapeDtypeStruct((NE * N * ROW, LANES), f32),
        compiler_params=pltpu.CompilerParams(dimension_semantics=("arbitrary",), vmem_limit_bytes=VMEM_LIMIT),
        name="experts",
    )(tile_blk, tile_e, tile_nv, tile_flags, tile_next, tile_slot, nused, xs,
      w_gu, b_gu.reshape(NE, 1, 2 * DFF), w_down, b_down.reshape(NE, 1, D))


def _combine_kernel(pos_ref, tab_ref, tabn_ref, gate_ref, h1_ref, g2_ref, pg_ref, ys_hbm, o_ref,
                    stage, wbuf, rbuf, sem):
    j = pl.program_id(0)
    slot = lax.rem(j, 2)
    GT = PG
    GR = GT * ROW

    @pl.when(j == 0)
    def _():
        _strip_copies(tab_ref, stage.at[0], ys_hbm, sem.at[0], to_region=False)

    @pl.when(j + 1 < pl.num_programs(0))
    def _():
        _strip_copies(tabn_ref, stage.at[1 - slot], ys_hbm, sem.at[1 - slot], to_region=False)

    for kk in range(TOPK):
        wbuf[kk] = jnp.broadcast_to(gate_ref[:, kk:kk + 1], (TT, LANES))
    _strip_wait(stage.at[slot], ys_hbm, sem.at[slot])
    g2 = g2_ref[...]
    pg = pg_ref[...]

    def one_group(s, g):
        rows = pl.ds(pl.multiple_of(g * GR, GR), GR)
        parts = []
        for jj in range(GT):
            t = g * GT + jj
            acc = None
            for kk in range(TOPK):
                w = jnp.broadcast_to(wbuf[kk, pl.ds(t, 1), :], (ROW, LANES))
                term = stage[s, pl.ds(pl.multiple_of(pos_ref[g, jj * TOPK + kk], ROW), ROW), :] * w
                acc = term if acc is None else acc + term
            parts.append(acc)
        ff3 = jnp.concatenate(parts, axis=0).reshape(GT, ROW, LANES)
        ss = jnp.sum(jnp.sum(ff3 * ff3, axis=2, keepdims=True), axis=1, keepdims=True)
        rs = lax.rsqrt(ss * (1.0 / D) + EPS)
        out = h1_ref[rows, :].reshape(GT, ROW, LANES) + g2 * (ff3 * rs * pg)
        rbuf[rows, :] = out.reshape(GR, LANES)

    def gather(s):
        def trip(gg, carry):
            for k in range(COMBINE_GROUPS_PER_TRIP):
                one_group(s, COMBINE_GROUPS_PER_TRIP * gg + k)
            return carry
        lax.fori_loop(0, TT // GT // COMBINE_GROUPS_PER_TRIP, trip, 0)

    for s in range(2):
        pl.when(slot == s)(functools.partial(gather, s))
    for cc in range(ROW):
        o_ref[:, cc * LANES:(cc + 1) * LANES] = rbuf[pl.ds(cc, TT, stride=ROW), :]


def _combine(pos, tab, gates, h1_rows, mod4, pg, ys):
    nt = N // TT
    per_b = T // TT
    tabspec = lambda imap: pl.BlockSpec((ROW, LANES), imap, memory_space=pltpu.SMEM)
    return pl.pallas_call(
        _combine_kernel,
        grid=(nt,),
        in_specs=[pl.BlockSpec((TT // PG, LANES), lambda j: (j, 0), memory_space=pltpu.SMEM),
                  tabspec(lambda j: (j, 0)), tabspec(lambda j: (jnp.minimum(j + 1, nt - 1), 0)),
                  pl.BlockSpec((TT, LANES), lambda j: (j, 0)),
                  pl.BlockSpec((TT * ROW, LANES), lambda j: (j, 0)),
                  pl.BlockSpec((None, None, ROW, LANES), lambda j: (j // per_b, 5, 0, 0)),
                  pl.BlockSpec((ROW, LANES), lambda j: (0, 0)),
                  pl.BlockSpec(memory_space=pl.ANY)],
        out_specs=pl.BlockSpec((TT, D), lambda j: (j, 0)),
        out_shape=jax.ShapeDtypeStruct((N, D), f32),
        scratch_shapes=[pltpu.VMEM((2, STAGE_ROWS, LANES), f32), pltpu.VMEM((TOPK, TT, LANES), f32),
                        pltpu.VMEM((TT * ROW, LANES), f32), pltpu.SemaphoreType.DMA((2,))],
        compiler_params=pltpu.CompilerParams(dimension_semantics=("arbitrary",), vmem_limit_bytes=VMEM_LIMIT),
        name="combine",
    )(pos, tab, tab, gates, h1_rows, mod4, pg, ys)


def _pad_heads(w):
    lead = w.shape[:-1]
    w4 = w.reshape(lead + (GLA_H, GLA_DK))
    w4 = jnp.concatenate([w4, jnp.zeros_like(w4)], axis=-1)
    return w4.reshape(lead + (GLA_H * LANES,))


def kernel(x, c, ada_w, ada_b, mix_pre_g, mix_post_g, ffn_pre_g, ffn_post_g, w_in, w_alpha, b_alpha, gla_norm_g, s5_lambda_re, s5_lambda_im, s5_log_dt, s5_b_re, s5_b_im, s5_c_re, s5_c_im, s5_d, s5_glu_w, s5_glu_b, w_out, router_w, router_b, exp_w_gu, exp_b_gu, exp_w_down, exp_b_down):
    l = 0
    mod = _ada(c, ada_w[l], ada_b[l:l + 1])
    mod3 = mod.reshape(B, 1, 6 * D)
    mod4 = mod.reshape(B, 6, ROW, LANES)

    w = w_in[l]
    o_q, o_k, o_v, o_g, o_a, o_u = 0, 256, 512, 1024, 1536, 1552
    wq = _pad_heads(w[:, o_q:o_k]).astype(bf16)
    wk = _pad_heads(w[:, o_k:o_v]).astype(bf16)
    wv = w[:, o_v:o_g].astype(bf16)
    wg = w[:, o_g:o_a].astype(bf16)
    wa = jnp.pad(w[:, o_a:o_u], ((0, 0), (0, LANES - LOWRANK))).astype(bf16)
    wu = w[:, o_u:].astype(bf16)
    walpha = jnp.pad(_pad_heads(w_alpha[l]), ((0, LANES - LOWRANK), (0, 0))).astype(bf16)
    balpha = _pad_heads(b_alpha[l:l + 1])

    gla, u = _mixer_in(x, mod3, mix_pre_g[l:l + 1], wq, wk, wv, wg, wa, wu,
                       walpha, balpha, gla_norm_g[l:l + 1])

    wt, mi, mo, a16 = _s5_prep(s5_lambda_re[l], s5_lambda_im[l], s5_log_dt[l], s5_b_re[l], s5_b_im[l],
                               s5_c_re[l], s5_c_im[l], s5_d[l])
    y = _s5_scan(u, wt, mi, mo, a16)

    rw = jnp.pad(router_w[l], ((0, 0), (0, LANES - NE)))
    rw_hi = rw.astype(bf16)
    rw_lo = (rw - rw_hi.astype(f32)).astype(bf16)
    rb = jnp.pad(router_b[l:l + 1], ((0, 0), (0, LANES - NE)), constant_values=-1e30)
    wo = w_out[l].astype(bf16)
    h1_rows, xs, pos, tab, gates, cnt = _mixer_out(
        x, gla, y, mod3, s5_glu_w[l].astype(bf16), s5_glu_b[l:l + 1], wo[:512], wo[512:],
        mix_post_g[l:l + 1], ffn_pre_g[l:l + 1], rw_hi, rw_lo, rb)

    counts = cnt[0, :NE].astype(i32)
    step_rows = EB * MB
    ntile = (counts + step_rows - 1) // step_rows
    tends = jnp.cumsum(ntile)
    nused = tends[-1]
    gi = jnp.minimum(jnp.arange(NB, dtype=i32), nused - 1)
    tile_e = jnp.sum((tends[None, :] <= gi[:, None]).astype(i32), axis=1)
    sel = tile_e[:, None] == jnp.arange(NE, dtype=i32)[None, :]
    pick = lambda v: jnp.sum(jnp.where(sel, v[None, :], 0), axis=1)
    tile_j = gi - pick(tends - ntile)
    tile_blk = tile_e * TPE + tile_j
    tile_nv = jnp.clip(pick(counts) - tile_j * step_rows, 0, step_rows)
    tile_flags = (tile_j == 0).astype(i32) + 2 * (tile_j == pick(ntile) - 1).astype(i32)
    used = ntile > 0
    eids = jnp.arange(NE, dtype=i32)
    later = jnp.logical_and(used[None, :], eids[None, :] > eids[:, None])
    nxt = jnp.min(jnp.where(later, eids[None, :], NE), axis=1)
    tile_next = pick(jnp.where(nxt < NE, nxt, -1))
    tile_slot = pick((jnp.cumsum(used.astype(i32)) - 1) % 2)

    ys = _experts(tile_blk, tile_e, tile_nv, tile_flags, tile_next, tile_slot, nused.reshape(1), xs,
                  exp_w_gu[l], exp_b_gu[l], exp_w_down[l], exp_b_down[l])
    out = _combine(pos, tab, gates.reshape(N, LANES), h1_rows, mod4, ffn_post_g[l].reshape(ROW, LANES), ys)
    return out.reshape(B, T, D)
```

```python
import functools
import math

import jax
import jax.numpy as jnp
from jax import lax
from jax.experimental import pallas as pl
from jax.experimental.pallas import tpu as pltpu

f32 = jnp.float32
bf16 = jnp.bfloat16
i32 = jnp.int32

D = 1024
B = 8
T = 2048
N = B * T
GLA_H = 4
GLA_DV = 128
GLA_DK = 64
GLA_TAU = 16.0
GLA_CHUNK = 64
LOWRANK = 16
S5_W = 512
S5_CH = 16
S5_G = 32
S5_P = 64
NE = 32
TOPK = 4
DFF = 1024
ALPHA = 1.702
LIMIT = 7.0
EPS = 1e-6

LANES = 128
SUBLANES = 8
VMEM_LIMIT = 56 * 1024 * 1024

TT = 512
S5_L = 16
S5_M = S5_W // LANES
S5_GPT = LANES // S5_CH
S5_ST = S5_GPT * S5_P
S5_NCH = 32
S5_ROWS = B * S5_NCH
S5_PITCH = S5_NCH + SUBLANES
FOLD_PITCH = S5_L + SUBLANES
ROW = SUBLANES
MB = 256
EB = 4
NB = (N * TOPK) // (EB * MB) + NE
TPE = N // (EB * MB)
PG = 16
COMBINE_GROUPS_PER_TRIP = 8


def _dot(a, b):
    return jnp.dot(a, b, preferred_element_type=f32)


def _dot_t(a, b, ca, cb):
    return lax.dot_general(a, b, (((ca,), (cb,)), ((), ())), preferred_element_type=f32)


def _rms(x):
    return lax.rsqrt(jnp.mean(x * x, axis=-1, keepdims=True) + EPS)


def _ada_kernel(c_ref, w_ref, b_ref, o_ref):
    c = c_ref[...]
    s = (c * jax.nn.sigmoid(c)).astype(bf16)
    o_ref[...] = _dot(s, w_ref[...].astype(bf16)) + b_ref[...]


def _ada(c, w, b):
    return pl.pallas_call(
        _ada_kernel,
        grid=(6,),
        in_specs=[
            pl.BlockSpec((B, D), lambda j: (0, 0)),
            pl.BlockSpec((D, D), lambda j: (0, j)),
            pl.BlockSpec((1, D), lambda j: (0, j)),
        ],
        out_specs=pl.BlockSpec((B, D), lambda j: (0, j)),
        out_shape=jax.ShapeDtypeStruct((B, 6 * D), f32),
        name="ada",
    )(c, w, b)


def _mixer_in_kernel(x_ref, sc_ref, sh_ref, g_ref, wq, wk, wv, wg, wa, wu, walpha, balpha, gng, cm_ref,
                     gla_ref, u_ref, st_s, u_s):
    t = pl.program_id(1)

    @pl.when(t == 0)
    def _():
        st_s[...] = jnp.zeros_like(st_s)

    x = x_ref[...]
    hn = (x * _rms(x) * g_ref[...]) * (1.0 + sc_ref[...]) + sh_ref[...]
    hb = hn.astype(bf16)
    u = _dot(hb, wu[...])

    C = GLA_CHUNK
    NC = TT // C
    causal = cm_ref[...] > 0.5
    tri = cm_ref[...].astype(bf16)
    a_lr = _dot(hb, wa[...]).astype(bf16)
    la = jax.nn.log_sigmoid(_dot(a_lr, walpha[...]) + balpha[...]) * (1.0 / GLA_TAU)
    la_hi = la.astype(bf16)
    la_lo = (la - la_hi.astype(f32)).astype(bf16)
    bc = _dot(tri, la_hi) + _dot(tri, la_lo)
    bl = jnp.broadcast_to(bc.reshape(NC, C, GLA_H * LANES)[:, C - 1:C, :],
                          (NC, C, GLA_H * LANES)).reshape(TT, GLA_H * LANES)
    q = _dot(hb, wq[...]) * (GLA_DK ** -0.5)
    k = _dot(hb, wk[...])
    qd = (q * jnp.exp(bc)).astype(bf16)
    ki = (k * jnp.exp(-bc)).astype(bf16)
    ke = k * jnp.exp(bl - bc)
    dec = jnp.exp(bl)
    vb = _dot(hb, wv[...]).astype(bf16)
    og = _dot(hb, wg[...])
    gn = gng[...]
    heads = range(GLA_H)
    sls = [slice(h * LANES, (h + 1) * LANES) for h in heads]
    kehb = ke.astype(bf16)
    raw = [_dot_t(qd[:, sl], ki[:, sl], 1, 1) for sl in sls]
    deltas = [[_dot_t(vb[c * C:(c + 1) * C, sl], kehb[c * C:(c + 1) * C, sl], 0, 0) for c in range(NC)] for sl in sls]
    scs = [jnp.where(causal, r, 0.0).astype(bf16) for r in raw]
    o_intra = [_dot(scs[h], vb[:, sls[h]]) for h in heads]
    sts = []
    for h in heads:
        st = st_s[h]
        per = []
        for c in range(NC):
            per.append(st.astype(bf16))
            st = st * dec[c * C:c * C + 1, sls[h]] + deltas[h][c]
        st_s[h] = st
        sts.append(per)
    for h in heads:
        sl = sls[h]
        o_inter = jnp.concatenate([_dot_t(qd[c * C:(c + 1) * C, sl], sts[h][c], 1, 1) for c in range(NC)], axis=0)
        o = o_intra[h] + o_inter
        on = o * _rms(o) * gn
        ogh = og[:, sl]
        gla_ref[:, sl] = (on * (ogh * jax.nn.sigmoid(ogh))).astype(bf16)
        for n in range(TT // S5_L):
            u_s[h, n * FOLD_PITCH:n * FOLD_PITCH + S5_L, :] = u[n * S5_L:(n + 1) * S5_L, sl]
        for i in range(S5_L):
            u_ref[h, :, i * LANES:(i + 1) * LANES] = u_s[h, pl.ds(i, TT // S5_L, stride=FOLD_PITCH), :].astype(bf16)


def _mixer_in(x, mod3, g, wq, wk, wv, wg, wa, wu, walpha, balpha, gng):
    assert S5_M == GLA_H
    nt = T // TT
    full = lambda shape: pl.BlockSpec(shape, lambda b, t: (0,) * len(shape))
    u_shape = jax.ShapeDtypeStruct((S5_M, B, T // S5_L, S5_L * LANES), bf16)
    u_spec = pl.BlockSpec((S5_M, None, TT // S5_L, S5_L * LANES), lambda b, t: (0, b, t, 0))
    tok = jnp.arange(TT, dtype=i32)
    same = (tok[:, None] // GLA_CHUNK) == (tok[None, :] // GLA_CHUNK)
    cmask = jnp.logical_and(tok[:, None] >= tok[None, :], same).astype(f32)
    return pl.pallas_call(
        _mixer_in_kernel,
        grid=(B, nt),
        in_specs=[
            pl.BlockSpec((None, TT, D), lambda b, t: (b, t, 0)),
            pl.BlockSpec((None, 1, D), lambda b, t: (b, 0, 1)),
            pl.BlockSpec((None, 1, D), lambda b, t: (b, 0, 0)),
            full((1, D)),
            full((D, 512)), full((D, 512)), full((D, 512)), full((D, 512)),
            full((D, LANES)), full((D, 512)),
            full((LANES, 512)), full((1, 512)), full((1, LANES)),
            full((TT, TT)),
        ],
        out_specs=[pl.BlockSpec((None, TT, 512), lambda b, t: (b, t, 0)), u_spec],
        out_shape=[jax.ShapeDtypeStruct((B, T, 512), bf16), u_shape],
        scratch_shapes=[pltpu.VMEM((GLA_H, GLA_DV, LANES), f32),
                        pltpu.VMEM((S5_M, TT // S5_L * FOLD_PITCH, LANES), f32)],
        compiler_params=pltpu.CompilerParams(
            dimension_semantics=("arbitrary", "arbitrary"), vmem_limit_bytes=VMEM_LIMIT),
        name="mixer_in",
    )(x, mod3, mod3, g, wq, wk, wv, wg, wa, wu, walpha, balpha, gng, cmask)


def _cpow(xr, xi, d):
    mag = jnp.exp(xr * d)
    return mag * jnp.cos(xi * d), mag * jnp.sin(xi * d)


def _s5_prep_kernel(lr_r, li_r, ld_r, lr_c, li_c, ld_c, btr, bti, ctr, cti, d_r,
                    wt_ref, mi_ref, mo_ref, a_ref, kt_s):
    L = S5_L
    lr = lr_r[...]
    li = li_r[...]
    dt = jnp.exp(ld_r[...])
    xr, xi = lr * dt, li * dt
    ar, ai = _cpow(xr, xi, 1.0)
    den = lr * lr + li * li
    fr = ((ar - 1.0) * lr + ai * li) / den
    fi = (ai * lr - (ar - 1.0) * li) / den
    br, bi = btr[...], bti[...]
    bbr = fr * br - fi * bi
    bbi = fr * bi + fi * br
    cr, ci = ctr[...], cti[...]
    cr_b, ci_b = cr.astype(bf16), ci.astype(bf16)
    pr, pi = jnp.ones_like(ar), jnp.zeros_like(ai)
    for d in range(L):
        xdr = bbr * pr - bbi * pi
        xdi = bbr * pi + bbi * pr
        i = L - 1 - d
        mo_ref[i * LANES:(i + 1) * LANES, :S5_ST] = xdr.astype(bf16)
        mo_ref[i * LANES:(i + 1) * LANES, S5_ST:] = xdi.astype(bf16)
        kt = _dot(xdr.astype(bf16), cr_b) - _dot(xdi.astype(bf16), ci_b)
        if d == 0:
            r = lax.broadcasted_iota(i32, (LANES, LANES), 0)
            c = lax.broadcasted_iota(i32, (LANES, LANES), 1)
            kt = kt + jnp.where(r == c, d_r[...], 0.0)
        kt_s[d] = kt.astype(bf16)
        pr, pi = pr * ar - pi * ai, pr * ai + pi * ar
    a_ref[:, :S5_ST] = pr
    a_ref[:, S5_ST:] = pi
    zero = jnp.zeros((LANES, LANES), bf16)
    for i in range(L):
        for j in range(L):
            wt_ref[i * LANES:(i + 1) * LANES, j * LANES:(j + 1) * LANES] = kt_s[j - i] if j >= i else zero
    lrc = lr_c[...]
    lic = li_c[...]
    dtc = jnp.exp(ld_c[...])
    acr, aci = _cpow(lrc * dtc, lic * dtc, 1.0)
    pr, pi = acr, aci
    for j in range(L):
        mi_ref[:S5_ST, j * LANES:(j + 1) * LANES] = (cr * pr - ci * pi).astype(bf16)
        mi_ref[S5_ST:, j * LANES:(j + 1) * LANES] = (-(cr * pi + ci * pr)).astype(bf16)
        pr, pi = pr * acr - pi * aci, pr * aci + pi * acr


def _s5_prep(lr, li, ld, b_re, b_im, c_re, c_im, dvec):
    G, P, H, M, GPT = S5_G, S5_P, S5_CH, S5_M, S5_GPT
    eye = jnp.eye(GPT, dtype=f32)

    def rows(v):
        return v.reshape(M, 1, S5_ST)

    def cols(v):
        return v.reshape(M, S5_ST, 1)

    ldp = jnp.broadcast_to(ld[:, None], (G, P))

    def bt(b):
        b4 = b.reshape(M, GPT, P, H).transpose(0, 3, 1, 2)
        return (eye[None, :, None, :, None] * b4[:, None]).reshape(M, LANES, S5_ST)

    def ct(c):
        c4 = c.reshape(M, GPT, H, P).transpose(0, 1, 3, 2)
        return (c4[:, :, :, None, :] * eye[None, :, None, :, None]).reshape(M, S5_ST, LANES)

    L = S5_L
    mspec = lambda shape: pl.BlockSpec((None,) + shape, lambda m: (m,) + (0,) * len(shape))
    return pl.pallas_call(
        _s5_prep_kernel,
        grid=(M,),
        in_specs=[mspec((1, S5_ST))] * 3 + [mspec((S5_ST, 1))] * 3
        + [mspec((LANES, S5_ST))] * 2 + [mspec((S5_ST, LANES))] * 2 + [mspec((1, LANES))],
        out_specs=[mspec((L * LANES, L * LANES)), mspec((2 * S5_ST, L * LANES)),
                   mspec((L * LANES, 2 * S5_ST)), mspec((1, 2 * S5_ST))],
        out_shape=[jax.ShapeDtypeStruct((M, L * LANES, L * LANES), bf16),
                   jax.ShapeDtypeStruct((M, 2 * S5_ST, L * LANES), bf16),
                   jax.ShapeDtypeStruct((M, L * LANES, 2 * S5_ST), bf16),
                   jax.ShapeDtypeStruct((M, 1, 2 * S5_ST), f32)],
        scratch_shapes=[pltpu.VMEM((L, LANES, LANES), bf16)],
        compiler_params=pltpu.CompilerParams(dimension_semantics=("arbitrary",), vmem_limit_bytes=VMEM_LIMIT),
        name="s5_prep",
    )(rows(lr), rows(li), rows(ldp), cols(lr), cols(li), cols(ldp),
      bt(b_re), bt(b_im), ct(c_re), ct(c_im), dvec.reshape(M, 1, LANES))


def _s5_scan_kernel(u_ref, wt, mi, mo, a_ref, y_ref, xs, s_s, x_s):
    r = pl.program_id(1)

    @pl.when(r == 0)
    def _():
        xs[...] = jnp.zeros_like(xs)

    u = u_ref[...].reshape(S5_ROWS, S5_L * LANES)
    s = _dot(u, mo[...])
    y_intra = _dot(u, wt[...])
    NT = 2 * S5_ST // LANES
    for c in range(NT):
        for b in range(B):
            s_s[c, b * S5_PITCH:b * S5_PITCH + S5_NCH, :] = s[b * S5_NCH:(b + 1) * S5_NCH, c * LANES:(c + 1) * LANES]
    a = [a_ref[:, c * LANES:(c + 1) * LANES] for c in range(NT)]

    def step(n, x):
        rows = pl.ds(n, B, stride=S5_PITCH)
        new = []
        for c in range(NT):
            x_s[c, rows, :] = x[c]
        for c in range(NT // 2):
            ar, ai, xr, xi = a[c], a[NT // 2 + c], x[c], x[NT // 2 + c]
            new.append((ar * xr - ai * xi + s_s[c, rows, :], ar * xi + ai * xr + s_s[NT // 2 + c, rows, :]))
        return tuple(p[0] for p in new) + tuple(p[1] for p in new)

    x = tuple(xs[:, c * LANES:(c + 1) * LANES] for c in range(NT))
    for n in range(S5_NCH):
        x = step(n, x)
    for c in range(NT):
        xs[:, c * LANES:(c + 1) * LANES] = x[c]
    x_in = jnp.concatenate(
        [jnp.concatenate([x_s[c, b * S5_PITCH:b * S5_PITCH + S5_NCH, :] for b in range(B)], axis=0) for c in range(NT)],
        axis=1)
    y = y_intra + _dot(x_in.astype(bf16), mi[...])
    y_ref[...] = jax.nn.gelu(y).astype(bf16).reshape(B, S5_NCH, S5_L * LANES)


def _s5_scan(u, wt, mi, mo, a16):
    L = S5_L
    wspec = lambda shape: pl.BlockSpec((None,) + shape, lambda m, r: (m,) + (0,) * len(shape))
    uspec = pl.BlockSpec((None, B, S5_NCH, L * LANES), lambda m, r: (m, 0, r, 0))
    return pl.pallas_call(
        _s5_scan_kernel,
        grid=(S5_M, T // L // S5_NCH),
        in_specs=[uspec,
                  wspec((L * LANES, L * LANES)), wspec((2 * S5_ST, L * LANES)),
                  wspec((L * LANES, 2 * S5_ST)), wspec((1, 2 * S5_ST))],
        out_specs=uspec,
        out_shape=jax.ShapeDtypeStruct((S5_M, B, T // L, L * LANES), bf16),
        scratch_shapes=[pltpu.VMEM((B, 2 * S5_ST), f32), pltpu.VMEM((2 * S5_ST // LANES, B * S5_PITCH, LANES), f32),
                        pltpu.VMEM((2 * S5_ST // LANES, B * S5_PITCH, LANES), f32)],
        compiler_params=pltpu.CompilerParams(
            dimension_semantics=("arbitrary", "arbitrary"), vmem_limit_bytes=VMEM_LIMIT),
        name="s5_scan",
    )(u, wt, mi, mo, a16)


STAGE_ROWS = TT * TOPK * ROW


def _strip_copies(tab_ref, stage, region_hbm, sem, to_region):
    def expert(e, carry):
        n = tab_ref[0, e] * ROW

        @pl.when(n > 0)
        def _():
            local = stage.at[pl.ds(pl.multiple_of(tab_ref[1, e] * ROW, ROW), n), :]
            remote = region_hbm.at[pl.ds(pl.multiple_of((e * N + tab_ref[2, e]) * ROW, ROW), n), :]
            src, dst = (local, remote) if to_region else (remote, local)
            pltpu.make_async_copy(src, dst, sem).start()
        return carry
    lax.fori_loop(0, NE, expert, 0)


def _strip_wait(stage, region_hbm, sem):
    pltpu.make_async_copy(stage, region_hbm.at[pl.ds(0, STAGE_ROWS), :], sem).wait()


def _mixer_out_kernel(x_ref, gla_ref, y_ref, g1_ref, sh2_ref, sc2_ref, gluw, glub, woa, wob,
                      mpg, fpg, rw_hi, rw_lo, rb, tri_ref, upper_ref, grp_ref,
                      h1_ref, xs_hbm, pos_ref, tab_ref, gate_ref, cnt_ref,
                      cnt_s, y_s, hn2_s, stage, pos_v, tab_v, pos_sm, tab_sm, sem, tsem):
    step = pl.program_id(0) * pl.num_programs(1) + pl.program_id(1)
    first = step == 0

    @pl.when(first)
    def _():
        cnt_s[...] = jnp.zeros_like(cnt_s)

    for m in range(S5_M):
        for jj in range(S5_L):
            y_s[m, pl.ds(jj, TT // S5_L, stride=FOLD_PITCH), :] = y_ref[m, :, jj * LANES:(jj + 1) * LANES].astype(f32)
    y = jnp.concatenate(
        [jnp.concatenate([y_s[m, n * FOLD_PITCH:n * FOLD_PITCH + S5_L, :] for n in range(TT // S5_L)], axis=0)
         for m in range(S5_M)], axis=1)
    z = _dot(y.astype(bf16), gluw[...]) + glub[...]
    s5 = y * jax.nn.sigmoid(z)
    mix = _dot(gla_ref[...], woa[...]) + _dot(s5.astype(bf16), wob[...])
    h1 = x_ref[...] + g1_ref[...] * (mix * _rms(mix) * mpg[...])
    hn2 = (h1 * _rms(h1) * fpg[...]) * (1.0 + sc2_ref[...]) + sh2_ref[...]
    for cc in range(ROW):
        h1_ref[pl.ds(cc, TT, stride=ROW), :] = h1[:, cc * LANES:(cc + 1) * LANES]
        hn2_s[lax.rem(step, 2), pl.ds(cc, TT, stride=ROW), :] = hn2[:, cc * LANES:(cc + 1) * LANES]
    x_hi = hn2.astype(bf16)
    x_lo = (hn2 - x_hi.astype(f32)).astype(bf16)
    logits = _dot(x_hi, rw_hi[...]) + (_dot(x_lo, rw_hi[...]) + _dot(x_hi, rw_lo[...])) + rb[...]

    lane = lax.broadcasted_iota(i32, (TT, LANES), 1).astype(f32)
    l = logits
    vals, idxs = [], []
    for _ in range(TOPK):
        m = jnp.max(l, axis=-1, keepdims=True)
        ix = jnp.min(jnp.where(l == m, lane, float(LANES)), axis=-1, keepdims=True)
        vals.append(m)
        idxs.append(ix)
        l = jnp.where(lane == ix, -jnp.inf, l)
    es = [jnp.exp(v - vals[0]) for v in vals]
    tot = es[0] + es[1] + es[2] + es[3]
    oh = jnp.zeros((TT, LANES), f32)
    for ix in idxs:
        oh = oh + (lane == ix).astype(f32)
    cum = _dot(tri_ref[...], oh.astype(bf16))
    lcnt = cum[TT - 1:TT, :]
    gbase = cnt_s[...]
    hi = jnp.floor(lcnt * (1.0 / 64.0))
    lo = lcnt - hi * 64.0
    upper = upper_ref[...]
    hi8 = jnp.broadcast_to(hi, (ROW, LANES)).astype(bf16)
    lo8 = jnp.broadcast_to(lo, (ROW, LANES)).astype(bf16)
    lbase = (64.0 * _dot(hi8, upper) + _dot(lo8, upper))[0:1, :]
    loc = lbase + (cum - oh)
    tmod = (lax.broadcasted_iota(i32, (TT, LANES), 0) % PG).astype(f32) * float(TOPK)
    spread = jnp.zeros((TT, LANES), f32)
    gate = jnp.zeros((TT, LANES), f32)
    for kk in range(TOPK):
        p = jnp.sum(jnp.where(lane == idxs[kk], loc, 0.0), axis=-1, keepdims=True)
        spread = jnp.where(lane == tmod + float(kk), p, spread)
        gate = jnp.where(lane == float(kk), es[kk] / tot, gate)
    s_hi = jnp.floor(spread * (1.0 / 64.0))
    s_lo = spread - s_hi * 64.0
    grp = grp_ref[...]
    folded = 64.0 * _dot(grp, s_hi.astype(bf16)) + _dot(grp, s_lo.astype(bf16))
    posi = (folded * float(ROW)).astype(i32)
    pos_ref[...] = posi
    pos_v[lax.rem(step, 2)] = posi
    sub = lax.broadcasted_iota(i32, (ROW, LANES), 0)
    tab = jnp.where(sub == 0, lcnt, jnp.where(sub == 1, lbase, jnp.where(sub == 2, gbase, 0.0)))
    tabi = tab.astype(i32)
    tab_ref[...] = tabi
    tab_v[lax.rem(step, 2)] = tabi
    gate_ref[...] = gate
    cnt = cnt_s[...] + cum[TT - 1:TT, :]
    cnt_s[...] = cnt
    cnt_ref[...] = cnt

    nsteps = pl.num_programs(0) * pl.num_programs(1)
    cur = lax.rem(step, 2)

    def table_copies():
        return (pltpu.make_async_copy(pos_v.at[cur], pos_sm, tsem.at[0]),
                pltpu.make_async_copy(tab_v.at[cur], tab_sm, tsem.at[1]))

    def dispatch(s):
        def group(g, carry):
            for jj in range(PG):
                v = hn2_s[s, pl.ds(pl.multiple_of((g * PG + jj) * ROW, ROW), ROW), :]
                for kk in range(TOPK):
                    stage[s, pl.ds(pl.multiple_of(pos_sm[g, jj * TOPK + kk], ROW), ROW), :] = v
            return carry
        lax.fori_loop(0, TT // PG, group, 0)
        _strip_copies(tab_sm, stage.at[s], xs_hbm, sem.at[s], to_region=True)

    @pl.when(step >= 1)
    def _():
        for c in table_copies():
            c.wait()

        @pl.when(step >= 3)
        def _():
            _strip_wait(stage.at[1 - cur], xs_hbm, sem.at[1 - cur])
        for s in range(2):
            pl.when(1 - cur == s)(functools.partial(dispatch, s))

    for c in table_copies():
        c.start()

    @pl.when(step == nsteps - 1)
    def _():
        for c in table_copies():
            c.wait()

        @pl.when(step >= 2)
        def _():
            _strip_wait(stage.at[cur], xs_hbm, sem.at[cur])
        for s in range(2):
            pl.when(cur == s)(functools.partial(dispatch, s))
        _strip_wait(stage.at[1 - cur], xs_hbm, sem.at[1 - cur])
        _strip_wait(stage.at[cur], xs_hbm, sem.at[cur])


def _mixer_out(x, gla, y, mod3, gluw, glub, woa, wob, mpg, fpg, rw_hi, rw_lo, rb):
    nt = T // TT
    full = lambda shape: pl.BlockSpec(shape, lambda b, t: (0,) * len(shape))
    tok = lambda w: pl.BlockSpec((None, TT, w), lambda b, t: (b, t, 0))
    slab = pl.BlockSpec((TT * ROW, LANES), lambda b, t: (b * nt + t, 0))
    slab_shape = jax.ShapeDtypeStruct((N * ROW, LANES), f32)
    yspec = pl.BlockSpec((S5_M, None, TT // S5_L, S5_L * LANES), lambda b, t: (0, b, t, 0))
    modspec = lambda j: pl.BlockSpec((None, 1, D), lambda b, t: (b, 0, j))
    tix = jnp.arange(TT, dtype=i32)
    lanes = jnp.arange(LANES, dtype=i32)
    tri = (tix[:, None] >= tix[None, :]).astype(bf16)
    upper = (lanes[:, None] < lanes[None, :]).astype(bf16)
    grp = (tix[None, :] // PG == jnp.arange(TT // PG, dtype=i32)[:, None]).astype(bf16)
    return pl.pallas_call(
        _mixer_out_kernel,
        grid=(B, nt),
        in_specs=[tok(D), tok(512), yspec,
                  modspec(2), modspec(3), modspec(4),
                  full((512, 512)), full((1, 512)), full((512, D)), full((512, D)),
                  full((1, D)), full((1, D)), full((D, LANES)), full((D, LANES)), full((1, LANES)),
                  full((TT, TT)), full((LANES, LANES)), full((TT // PG, TT))],
        out_specs=[slab, pl.BlockSpec(memory_space=pl.ANY),
                   pl.BlockSpec((TT // PG, LANES), lambda b, t: (b * nt + t, 0)),
                   pl.BlockSpec((ROW, LANES), lambda b, t: (b * nt + t, 0)), tok(LANES), full((1, LANES))],
        out_shape=[slab_shape, jax.ShapeDtypeStruct((NE * N * ROW, LANES), f32),
                   jax.ShapeDtypeStruct((N // PG, LANES), i32), jax.ShapeDtypeStruct((N // TT * ROW, LANES), i32),
                   jax.ShapeDtypeStruct((B, T, LANES), f32),
                   jax.ShapeDtypeStruct((1, LANES), f32)],
        scratch_shapes=[pltpu.VMEM((1, LANES), f32), pltpu.VMEM((S5_M, TT // S5_L * FOLD_PITCH, LANES), f32),
                        pltpu.VMEM((2, TT * ROW, LANES), f32), pltpu.VMEM((2, STAGE_ROWS, LANES), f32),
                        pltpu.VMEM((2, TT // PG, LANES), i32), pltpu.VMEM((2, ROW, LANES), i32),
                        pltpu.SMEM((TT // PG, LANES), i32), pltpu.SMEM((ROW, LANES), i32),
                        pltpu.SemaphoreType.DMA((2,)), pltpu.SemaphoreType.DMA((2,))],
        compiler_params=pltpu.CompilerParams(
            dimension_semantics=("arbitrary", "arbitrary"), vmem_limit_bytes=VMEM_LIMIT),
        name="mixer_out",
    )(x, gla, y, mod3, mod3, mod3, gluw, glub, woa, wob, mpg, fpg, rw_hi, rw_lo, rb, tri, upper, grp)


def _experts_kernel(tb_ref, te_ref, tv_ref, tf_ref, tn_ref, ts_ref, nu_ref,
                    x_ref, wgu_hbm, bgu_ref, wd_hbm, bd_ref, y_ref, wgu_f, wd_f, wgu_b, wd_b, sem):
    i = pl.program_id(0)

    def fetch(e):
        return (pltpu.make_async_copy(wgu_hbm.at[e], wgu_f, sem.at[0]),
                pltpu.make_async_copy(wd_hbm.at[e], wd_f, sem.at[1]))

    def convert(s):
        wgu_b[s] = wgu_f[...].astype(bf16)
        wd_b[s] = wd_f[...].astype(bf16)

    def tile(s, h, nt=1, rows=None):
        rows = nt * MB if rows is None else rows
        base = h * MB * ROW
        xt = jnp.concatenate([x_ref[pl.ds(base + cc, rows, stride=ROW), :] for cc in range(ROW)], axis=1)
        valid = lax.broadcasted_iota(i32, (rows, 1), 0) < tv_ref[i] - h * MB
        xt = jnp.where(valid, xt, 0.0).astype(bf16)
        gu = _dot(xt, wgu_b[s]) + bgu_ref[...]
        g = jnp.minimum(gu[:, :DFF], LIMIT)
        up = jnp.clip(gu[:, DFF:], -LIMIT, LIMIT)
        act = (up + 1.0) * (g * jax.nn.sigmoid(ALPHA * g))
        yv = _dot(act.astype(bf16), wd_b[s]) + bd_ref[...]
        for cc in range(ROW):
            y_ref[pl.ds(base + cc, rows, stride=ROW), :] = yv[:, cc * LANES:(cc + 1) * LANES]

    def run_tiles(s):
        nv = tv_ref[i]
        rem = nv - (nv // MB) * MB
        half = jnp.logical_and(rem > 0, rem <= MB // 2)
        nfull = nv // MB + jnp.where(rem > MB // 2, 1, 0)
        pl.when(nfull == EB)(functools.partial(tile, s, 0, EB))
        few = nfull < EB
        for h in range(0, EB, 2):
            pl.when(jnp.logical_and(few, nfull >= h + 2))(functools.partial(tile, s, h, 2))
            pl.when(jnp.logical_and(few, nfull == h + 1))(functools.partial(tile, s, h, 1))
        for h in range(EB):
            pl.when(jnp.logical_and(half, nfull == h))(functools.partial(tile, s, h, 1, MB // 2))

    @pl.when(i < nu_ref[0])
    def _():
        e = te_ref[i]
        s = ts_ref[i]
        nxt = tn_ref[i]
        first = (tf_ref[i] & 1) != 0
        last = (tf_ref[i] & 2) != 0

        @pl.when(i == 0)
        def _():
            for c in fetch(e):
                c.start()
            for c in fetch(e):
                c.wait()
            convert(s)

        @pl.when(jnp.logical_and(first, nxt >= 0))
        def _():
            for c in fetch(nxt):
                c.start()

        handoff = jnp.logical_and(last, nxt >= 0)

        @pl.when(handoff)
        def _():
            for c in fetch(nxt):
                c.wait()
            convert(1 - s)

        run_tiles(s)


def _experts(tile_blk, tile_e, tile_nv, tile_flags, tile_next, tile_slot, nused, xs, w_gu, b_gu, w_down, b_down):
    blk = lambda i, tb, te, *_: (tb[i], 0)
    bsel = lambda i, tb, te, *_: (te[i], 0, 0)
    grid_spec = pltpu.PrefetchScalarGridSpec(
        num_scalar_prefetch=7,
        grid=(NB,),
        in_specs=[
            pl.BlockSpec((EB * MB * ROW, LANES), blk),
            pl.BlockSpec(memory_space=pl.ANY),
            pl.BlockSpec((None, 1, 2 * DFF), bsel),
            pl.BlockSpec(memory_space=pl.ANY),
            pl.BlockSpec((None, 1, D), bsel),
        ],
        out_specs=pl.BlockSpec((EB * MB * ROW, LANES), blk),
        scratch_shapes=[pltpu.VMEM((D, 2 * DFF), f32), pltpu.VMEM((DFF, D), f32),
                        pltpu.VMEM((2, D, 2 * DFF), bf16), pltpu.VMEM((2, DFF, D), bf16),
                        pltpu.SemaphoreType.DMA((2,))],
    )
    return pl.pallas_call(
        _experts_kernel,
        grid_spec=grid_spec,
        out_shape=jax.ShapeDtypeStruct((NE * N * ROW, LANES), f32),
        compiler_params=pltpu.CompilerParams(dimension_semantics=("arbitrary",), vmem_limit_bytes=VMEM_LIMIT),
        name="experts",
    )(tile_blk, tile_e, tile_nv, tile_flags, tile_next, tile_slot, nused, xs,
      w_gu, b_gu.reshape(NE, 1, 2 * DFF), w_down, b_down.reshape(NE, 1, D))


def _combine_kernel(pos_ref, tab_ref, tabn_ref, gate_ref, h1_ref, g2_ref, pg_ref, ys_hbm, o_ref,
                    stage, wbuf, rbuf, sem):
    j = pl.program_id(0)
    slot = lax.rem(j, 2)
    GT = PG
    GR = GT * ROW

    @pl.when(j == 0)
    def _():
        _strip_copies(tab_ref, stage.at[0], ys_hbm, sem.at[0], to_region=False)

    @pl.when(j + 1 < pl.num_programs(0))
    def _():
        _strip_copies(tabn_ref, stage.at[1 - slot], ys_hbm, sem.at[1 - slot], to_region=False)

    for kk in range(TOPK):
        wbuf[kk] = jnp.broadcast_to(gate_ref[:, kk:kk + 1], (TT, LANES))
    _strip_wait(stage.at[slot], ys_hbm, sem.at[slot])
    g2 = g2_ref[...]
    pg = pg_ref[...]

    def one_group(s, g):
        rows = pl.ds(pl.multiple_of(g * GR, GR), GR)
        parts = []
        for jj in range(GT):
            t = g * GT + jj
            acc = None
            for kk in range(TOPK):
                w = jnp.broadcast_to(wbuf[kk, pl.ds(t, 1), :], (ROW, LANES))
                term = stage[s, pl.ds(pl.multiple_of(pos_ref[g, jj * TOPK + kk], ROW), ROW), :] * w
                acc = term if acc is None else acc + term
            parts.append(acc)
        ff3 = jnp.concatenate(parts, axis=0).reshape(GT, ROW, LANES)
        ss = jnp.sum(jnp.sum(ff3 * ff3, axis=2, keepdims=True), axis=1, keepdims=True)
        rs = lax.rsqrt(ss * (1.0 / D) + EPS)
        out = h1_ref[rows, :].reshape(GT, ROW, LANES) + g2 * (ff3 * rs * pg)
        rbuf[rows, :] = out.reshape(GR, LANES)

    def gather(s):
        def trip(gg, carry):
            for k in range(COMBINE_GROUPS_PER_TRIP):
                one_group(s, COMBINE_GROUPS_PER_TRIP * gg + k)
            return carry
        lax.fori_loop(0, TT // GT // COMBINE_GROUPS_PER_TRIP, trip, 0)

    for s in range(2):
        pl.when(slot == s)(functools.partial(gather, s))
    for cc in range(ROW):
        o_ref[:, cc * LANES:(cc + 1) * LANES] = rbuf[pl.ds(cc, TT, stride=ROW), :]


def _combine(pos, tab, gates, h1_rows, mod4, pg, ys):
    nt = N // TT
    per_b = T // TT
    tabspec = lambda imap: pl.BlockSpec((ROW, LANES), imap, memory_space=pltpu.SMEM)
    return pl.pallas_call(
        _combine_kernel,
        grid=(nt,),
        in_specs=[pl.BlockSpec((TT // PG, LANES), lambda j: (j, 0), memory_space=pltpu.SMEM),
                  tabspec(lambda j: (j, 0)), tabspec(lambda j: (jnp.minimum(j + 1, nt - 1), 0)),
                  pl.BlockSpec((TT, LANES), lambda j: (j, 0)),
                  pl.BlockSpec((TT * ROW, LANES), lambda j: (j, 0)),
                  pl.BlockSpec((None, None, ROW, LANES), lambda j: (j // per_b, 5, 0, 0)),
                  pl.BlockSpec((ROW, LANES), lambda j: (0, 0)),
                  pl.BlockSpec(memory_space=pl.ANY)],
        out_specs=pl.BlockSpec((TT, D), lambda j: (j, 0)),
        out_shape=jax.ShapeDtypeStruct((N, D), f32),
        scratch_shapes=[pltpu.VMEM((2, STAGE_ROWS, LANES), f32), pltpu.VMEM((TOPK, TT, LANES), f32),
                        pltpu.VMEM((TT * ROW, LANES), f32), pltpu.SemaphoreType.DMA((2,))],
        compiler_params=pltpu.CompilerParams(dimension_semantics=("arbitrary",), vmem_limit_bytes=VMEM_LIMIT),
        name="combine",
    )(pos, tab, tab, gates, h1_rows, mod4, pg, ys)


def _pad_heads(w):
    lead = w.shape[:-1]
    w4 = w.reshape(lead + (GLA_H, GLA_DK))
    w4 = jnp.concatenate([w4, jnp.zeros_like(w4)], axis=-1)
    return w4.reshape(lead + (GLA_H * LANES,))


def kernel(x, c, ada_w, ada_b, mix_pre_g, mix_post_g, ffn_pre_g, ffn_post_g, w_in, w_alpha, b_alpha, gla_norm_g, s5_lambda_re, s5_lambda_im, s5_log_dt, s5_b_re, s5_b_im, s5_c_re, s5_c_im, s5_d, s5_glu_w, s5_glu_b, w_out, router_w, router_b, exp_w_gu, exp_b_gu, exp_w_down, exp_b_down):
    l = 0
    mod = _ada(c, ada_w[l], ada_b[l:l + 1])
    mod3 = mod.reshape(B, 1, 6 * D)
    mod4 = mod.reshape(B, 6, ROW, LANES)

    w = w_in[l]
    o_q, o_k, o_v, o_g, o_a, o_u = 0, 256, 512, 1024, 1536, 1552
    wq = _pad_heads(w[:, o_q:o_k]).astype(bf16)
    wk = _pad_heads(w[:, o_k:o_v]).astype(bf16)
    wv = w[:, o_v:o_g].astype(bf16)
    wg = w[:, o_g:o_a].astype(bf16)
    wa = jnp.pad(w[:, o_a:o_u], ((0, 0), (0, LANES - LOWRANK))).astype(bf16)
    wu = w[:, o_u:].astype(bf16)
    walpha = jnp.pad(_pad_heads(w_alpha[l]), ((0, LANES - LOWRANK), (0, 0))).astype(bf16)
    balpha = _pad_heads(b_alpha[l:l + 1])

    gla, u = _mixer_in(x, mod3, mix_pre_g[l:l + 1], wq, wk, wv, wg, wa, wu,
                       walpha, balpha, gla_norm_g[l:l + 1])

    wt, mi, mo, a16 = _s5_prep(s5_lambda_re[l], s5_lambda_im[l], s5_log_dt[l], s5_b_re[l], s5_b_im[l],
                               s5_c_re[l], s5_c_im[l], s5_d[l])
    y = _s5_scan(u, wt, mi, mo, a16)

    rw = jnp.pad(router_w[l], ((0, 0), (0, LANES - NE)))
    rw_hi = rw.astype(bf16)
    rw_lo = (rw - rw_hi.astype(f32)).astype(bf16)
    rb = jnp.pad(router_b[l:l + 1], ((0, 0), (0, LANES - NE)), constant_values=-1e30)
    wo = w_out[l].astype(bf16)
    h1_rows, xs, pos, tab, gates, cnt = _mixer_out(
        x, gla, y, mod3, s5_glu_w[l].astype(bf16), s5_glu_b[l:l + 1], wo[:512], wo[512:],
        mix_post_g[l:l + 1], ffn_pre_g[l:l + 1], rw_hi, rw_lo, rb)

    counts = cnt[0, :NE].astype(i32)
    step_rows = EB * MB
    ntile = (counts + step_rows - 1) // step_rows
    tends = jnp.cumsum(ntile)
    nused = tends[-1]
    gi = jnp.minimum(jnp.arange(NB, dtype=i32), nused - 1)
    tile_e = jnp.sum((tends[None, :] <= gi[:, None]).astype(i32), axis=1)
    sel = tile_e[:, None] == jnp.arange(NE, dtype=i32)[None, :]
    pick = lambda v: jnp.sum(jnp.where(sel, v[None, :], 0), axis=1)
    tile_j = gi - pick(tends - ntile)
    tile_blk = tile_e * TPE + tile_j
    tile_nv = jnp.clip(pick(counts) - tile_j * step_rows, 0, step_rows)
    tile_flags = (tile_j == 0).astype(i32) + 2 * (tile_j == pick(ntile) - 1).astype(i32)
    used = ntile > 0
    eids = jnp.arange(NE, dtype=i32)
    later = jnp.logical_and(used[None, :], eids[None, :] > eids[:, None])
    nxt = jnp.min(jnp.where(later, eids[None, :], NE), axis=1)
    tile_next = pick(jnp.where(nxt < NE, nxt, -1))
    tile_slot = pick((jnp.cumsum(used.astype(i32)) - 1) % 2)

    ys = _experts(tile_blk, tile_e, tile_nv, tile_flags, tile_next, tile_slot, nused.reshape(1), xs,
                  exp_w_gu[l], exp_b_gu[l], exp_w_down[l], exp_b_down[l])
    out = _combine(pos, tab, gates.reshape(N, LANES), h1_rows, mod4, ffn_post_g[l].reshape(ROW, LANES), ys)
    return out.reshape(B, T, D)
```

```python
import functools

import jax
import jax.numpy as jnp
from jax import lax
from jax.experimental import pallas as pl
from jax.experimental.pallas import tpu as pltpu

f32 = jnp.float32
bf16 = jnp.bfloat16
i32 = jnp.int32

D = 1024
B = 8
T = 2048
N = B * T
GLA_H = 4
GLA_DV = 128
GLA_DK = 64
GLA_TAU = 16.0
GLA_CHUNK = 64
LOWRANK = 16
S5_W = 512
S5_CH = 16
S5_G = 32
S5_P = 64
NE = 32
TOPK = 4
DFF = 1024
ALPHA = 1.702
LIMIT = 7.0
EPS = 1e-6

LANES = 128
SUBLANES = 8
VMEM_LIMIT = 56 * 1024 * 1024

TT = 512
S5_L = 16
S5_M = S5_W // LANES
S5_GPT = LANES // S5_CH
S5_ST = S5_GPT * S5_P
S5_NCH = 32
S5_ROWS = B * S5_NCH
S5_PITCH = S5_NCH + SUBLANES
FOLD_PITCH = S5_L + SUBLANES
ROW = SUBLANES
MB = 256
EB = 4
NB = (N * TOPK) // (EB * MB) + NE
TPE = N // (EB * MB)
PG = 16
COMBINE_GROUPS_PER_TRIP = 16


def _dot(a, b):
    return jnp.dot(a, b, preferred_element_type=f32)


def _dot_t(a, b, ca, cb):
    return lax.dot_general(a, b, (((ca,), (cb,)), ((), ())), preferred_element_type=f32)


def _rms(x):
    return lax.rsqrt(jnp.mean(x * x, axis=-1, keepdims=True) + EPS)


def _ada_kernel(c_ref, w_ref, b_ref, o_ref):
    c = c_ref[...]
    s = (c * jax.nn.sigmoid(c)).astype(bf16)
    o_ref[...] = _dot(s, w_ref[...].astype(bf16)) + b_ref[...]


def _ada(c, w, b):
    return pl.pallas_call(
        _ada_kernel,
        grid=(6,),
        in_specs=[
            pl.BlockSpec((B, D), lambda j: (0, 0)),
            pl.BlockSpec((D, D), lambda j: (0, j)),
            pl.BlockSpec((1, D), lambda j: (0, j)),
        ],
        out_specs=pl.BlockSpec((B, D), lambda j: (0, j)),
        out_shape=jax.ShapeDtypeStruct((B, 6 * D), f32),
        name="ada",
    )(c, w, b)


def _mixer_in_kernel(x_ref, sc_ref, sh_ref, g_ref, wq, wk, wv, wg, wa, wu, walpha, balpha, gng, cm_ref,
                     gla_ref, u_ref, st_s, u_s):
    t = pl.program_id(1)

    @pl.when(t == 0)
    def _():
        st_s[...] = jnp.zeros_like(st_s)

    x = x_ref[...]
    hn = (x * _rms(x) * g_ref[...]) * (1.0 + sc_ref[...]) + sh_ref[...]
    hb = hn.astype(bf16)
    u = _dot(hb, wu[...])

    C = GLA_CHUNK
    NC = TT // C
    causal = cm_ref[...] > 0.5
    tri = cm_ref[...].astype(bf16)
    a_lr = _dot(hb, wa[...]).astype(bf16)
    la = jax.nn.log_sigmoid(_dot(a_lr, walpha[...]) + balpha[...]) * (1.0 / GLA_TAU)
    la_hi = la.astype(bf16)
    la_lo = (la - la_hi.astype(f32)).astype(bf16)
    bc = _dot(tri, la_hi) + _dot(tri, la_lo)
    bl = jnp.broadcast_to(bc.reshape(NC, C, GLA_H * LANES)[:, C - 1:C, :],
                          (NC, C, GLA_H * LANES)).reshape(TT, GLA_H * LANES)
    q = _dot(hb, wq[...]) * (GLA_DK ** -0.5)
    k = _dot(hb, wk[...])
    qd = (q * jnp.exp(bc)).astype(bf16)
    ki = (k * jnp.exp(-bc)).astype(bf16)
    ke = k * jnp.exp(bl - bc)
    dec = jnp.exp(bl)
    vb = _dot(hb, wv[...]).astype(bf16)
    og = _dot(hb, wg[...])
    gn = gng[...]
    heads = range(GLA_H)
    sls = [slice(h * LANES, (h + 1) * LANES) for h in heads]
    kehb = ke.astype(bf16)
    raw = [_dot_t(qd[:, sl], ki[:, sl], 1, 1) for sl in sls]
    deltas = [[_dot_t(vb[c * C:(c + 1) * C, sl], kehb[c * C:(c + 1) * C, sl], 0, 0) for c in range(NC)] for sl in sls]
    scs = [jnp.where(causal, r, 0.0).astype(bf16) for r in raw]
    o_intra = [_dot(scs[h], vb[:, sls[h]]) for h in heads]
    sts = []
    for h in heads:
        st = st_s[h]
        per = []
        for c in range(NC):
            per.append(st.astype(bf16))
            st = st * dec[c * C:c * C + 1, sls[h]] + deltas[h][c]
        st_s[h] = st
        sts.append(per)
    for h in heads:
        sl = sls[h]
        o_inter = jnp.concatenate([_dot_t(qd[c * C:(c + 1) * C, sl], sts[h][c], 1, 1) for c in range(NC)], axis=0)
        o = o_intra[h] + o_inter
        on = o * _rms(o) * gn
        ogh = og[:, sl]
        gla_ref[:, sl] = (on * (ogh * jax.nn.sigmoid(ogh))).astype(bf16)
        for n in range(TT // S5_L):
            u_s[h, n * FOLD_PITCH:n * FOLD_PITCH + S5_L, :] = u[n * S5_L:(n + 1) * S5_L, sl]
        for i in range(S5_L):
            u_ref[h, :, i * LANES:(i + 1) * LANES] = u_s[h, pl.ds(i, TT // S5_L, stride=FOLD_PITCH), :].astype(bf16)


def _mixer_in(x, mod3, g, wq, wk, wv, wg, wa, wu, walpha, balpha, gng):
    assert S5_M == GLA_H
    nt = T // TT
    full = lambda shape: pl.BlockSpec(shape, lambda b, t: (0,) * len(shape))
    u_shape = jax.ShapeDtypeStruct((S5_M, B, T // S5_L, S5_L * LANES), bf16)
    u_spec = pl.BlockSpec((S5_M, None, TT // S5_L, S5_L * LANES), lambda b, t: (0, b, t, 0))
    tok = jnp.arange(TT, dtype=i32)
    same = (tok[:, None] // GLA_CHUNK) == (tok[None, :] // GLA_CHUNK)
    cmask = jnp.logical_and(tok[:, None] >= tok[None, :], same).astype(f32)
    return pl.pallas_call(
        _mixer_in_kernel,
        grid=(B, nt),
        in_specs=[
            pl.BlockSpec((None, TT, D), lambda b, t: (b, t, 0)),
            pl.BlockSpec((None, 1, D), lambda b, t: (b, 0, 1)),
            pl.BlockSpec((None, 1, D), lambda b, t: (b, 0, 0)),
            full((1, D)),
            full((D, 512)), full((D, 512)), full((D, 512)), full((D, 512)),
            full((D, LANES)), full((D, 512)),
            full((LANES, 512)), full((1, 512)), full((1, LANES)),
            full((TT, TT)),
        ],
        out_specs=[pl.BlockSpec((None, TT, 512), lambda b, t: (b, t, 0)), u_spec],
        out_shape=[jax.ShapeDtypeStruct((B, T, 512), bf16), u_shape],
        scratch_shapes=[pltpu.VMEM((GLA_H, GLA_DV, LANES), f32),
                        pltpu.VMEM((S5_M, TT // S5_L * FOLD_PITCH, LANES), f32)],
        compiler_params=pltpu.CompilerParams(
            dimension_semantics=("arbitrary", "arbitrary"), vmem_limit_bytes=VMEM_LIMIT),
        name="mixer_in",
    )(x, mod3, mod3, g, wq, wk, wv, wg, wa, wu, walpha, balpha, gng, cmask)


def _cpow(xr, xi, d):
    mag = jnp.exp(xr * d)
    return mag * jnp.cos(xi * d), mag * jnp.sin(xi * d)


def _s5_prep_kernel(lr_r, li_r, ld_r, lr_c, li_c, ld_c, btr, bti, ctr, cti, d_r,
                    wt_ref, mi_ref, mo_ref, a_ref, kt_s):
    L = S5_L
    lr = lr_r[...]
    li = li_r[...]
    dt = jnp.exp(ld_r[...])
    xr, xi = lr * dt, li * dt
    ar, ai = _cpow(xr, xi, 1.0)
    den = lr * lr + li * li
    fr = ((ar - 1.0) * lr + ai * li) / den
    fi = (ai * lr - (ar - 1.0) * li) / den
    br, bi = btr[...], bti[...]
    bbr = fr * br - fi * bi
    bbi = fr * bi + fi * br
    cr, ci = ctr[...], cti[...]
    cr_b, ci_b = cr.astype(bf16), ci.astype(bf16)
    pr, pi = jnp.ones_like(ar), jnp.zeros_like(ai)
    for d in range(L):
        xdr = bbr * pr - bbi * pi
        xdi = bbr * pi + bbi * pr
        i = L - 1 - d
        mo_ref[i * LANES:(i + 1) * LANES, :S5_ST] = xdr.astype(bf16)
        mo_ref[i * LANES:(i + 1) * LANES, S5_ST:] = xdi.astype(bf16)
        kt = _dot(xdr.astype(bf16), cr_b) - _dot(xdi.astype(bf16), ci_b)
        if d == 0:
            r = lax.broadcasted_iota(i32, (LANES, LANES), 0)
            c = lax.broadcasted_iota(i32, (LANES, LANES), 1)
            kt = kt + jnp.where(r == c, d_r[...], 0.0)
        kt_s[d] = kt.astype(bf16)
        pr, pi = pr * ar - pi * ai, pr * ai + pi * ar
    a_ref[:, :S5_ST] = pr
    a_ref[:, S5_ST:] = pi
    zero = jnp.zeros((LANES, LANES), bf16)
    for i in range(L):
        for j in range(L):
            wt_ref[i * LANES:(i + 1) * LANES, j * LANES:(j + 1) * LANES] = kt_s[j - i] if j >= i else zero
    lrc = lr_c[...]
    lic = li_c[...]
    dtc = jnp.exp(ld_c[...])
    acr, aci = _cpow(lrc * dtc, lic * dtc, 1.0)
    pr, pi = acr, aci
    for j in range(L):
        mi_ref[:S5_ST, j * LANES:(j + 1) * LANES] = (cr * pr - ci * pi).astype(bf16)
        mi_ref[S5_ST:, j * LANES:(j + 1) * LANES] = (-(cr * pi + ci * pr)).astype(bf16)
        pr, pi = pr * acr - pi * aci, pr * aci + pi * acr


def _s5_prep(lr, li, ld, b_re, b_im, c_re, c_im, dvec):
    G, P, H, M, GPT = S5_G, S5_P, S5_CH, S5_M, S5_GPT
    eye = jnp.eye(GPT, dtype=f32)

    def rows(v):
        return v.reshape(M, 1, S5_ST)

    def cols(v):
        return v.reshape(M, S5_ST, 1)

    ldp = jnp.broadcast_to(ld[:, None], (G, P))

    def bt(b):
        b4 = b.reshape(M, GPT, P, H).transpose(0, 3, 1, 2)
        return (eye[None, :, None, :, None] * b4[:, None]).reshape(M, LANES, S5_ST)

    def ct(c):
        c4 = c.reshape(M, GPT, H, P).transpose(0, 1, 3, 2)
        return (c4[:, :, :, None, :] * eye[None, :, None, :, None]).reshape(M, S5_ST, LANES)

    L = S5_L
    mspec = lambda shape: pl.BlockSpec((None,) + shape, lambda m: (m,) + (0,) * len(shape))
    return pl.pallas_call(
        _s5_prep_kernel,
        grid=(M,),
        in_specs=[mspec((1, S5_ST))] * 3 + [mspec((S5_ST, 1))] * 3
        + [mspec((LANES, S5_ST))] * 2 + [mspec((S5_ST, LANES))] * 2 + [mspec((1, LANES))],
        out_specs=[mspec((L * LANES, L * LANES)), mspec((2 * S5_ST, L * LANES)),
                   mspec((L * LANES, 2 * S5_ST)), mspec((1, 2 * S5_ST))],
        out_shape=[jax.ShapeDtypeStruct((M, L * LANES, L * LANES), bf16),
                   jax.ShapeDtypeStruct((M, 2 * S5_ST, L * LANES), bf16),
                   jax.ShapeDtypeStruct((M, L * LANES, 2 * S5_ST), bf16),
                   jax.ShapeDtypeStruct((M, 1, 2 * S5_ST), f32)],
        scratch_shapes=[pltpu.VMEM((L, LANES, LANES), bf16)],
        compiler_params=pltpu.CompilerParams(dimension_semantics=("arbitrary",), vmem_limit_bytes=VMEM_LIMIT),
        name="s5_prep",
    )(rows(lr), rows(li), rows(ldp), cols(lr), cols(li), cols(ldp),
      bt(b_re), bt(b_im), ct(c_re), ct(c_im), dvec.reshape(M, 1, LANES))


def _s5_scan_kernel(u_ref, wt, mi, mo, a_ref, y_ref, xs, s_s, x_s):
    r = pl.program_id(1)

    @pl.when(r == 0)
    def _():
        xs[...] = jnp.zeros_like(xs)

    u = u_ref[...].reshape(S5_ROWS, S5_L * LANES)
    s = _dot(u, mo[...])
    y_intra = _dot(u, wt[...])
    NT = 2 * S5_ST // LANES
    for c in range(NT):
        for b in range(B):
            s_s[c, b * S5_PITCH:b * S5_PITCH + S5_NCH, :] = s[b * S5_NCH:(b + 1) * S5_NCH, c * LANES:(c + 1) * LANES]
    a = [a_ref[:, c * LANES:(c + 1) * LANES] for c in range(NT)]

    def step(n, x):
        rows = pl.ds(n, B, stride=S5_PITCH)
        new = []
        for c in range(NT):
            x_s[c, rows, :] = x[c]
        for c in range(NT // 2):
            ar, ai, xr, xi = a[c], a[NT // 2 + c], x[c], x[NT // 2 + c]
            new.append((ar * xr - ai * xi + s_s[c, rows, :], ar * xi + ai * xr + s_s[NT // 2 + c, rows, :]))
        return tuple(p[0] for p in new) + tuple(p[1] for p in new)

    x = tuple(xs[:, c * LANES:(c + 1) * LANES] for c in range(NT))
    for n in range(S5_NCH):
        x = step(n, x)
    for c in range(NT):
        xs[:, c * LANES:(c + 1) * LANES] = x[c]
    x_in = jnp.concatenate(
        [jnp.concatenate([x_s[c, b * S5_PITCH:b * S5_PITCH + S5_NCH, :] for b in range(B)], axis=0) for c in range(NT)],
        axis=1)
    y = y_intra + _dot(x_in.astype(bf16), mi[...])
    y_ref[...] = jax.nn.gelu(y).astype(bf16).reshape(B, S5_NCH, S5_L * LANES)


def _s5_scan(u, wt, mi, mo, a16):
    L = S5_L
    wspec = lambda shape: pl.BlockSpec((None,) + shape, lambda m, r: (m,) + (0,) * len(shape))
    uspec = pl.BlockSpec((None, B, S5_NCH, L * LANES), lambda m, r: (m, 0, r, 0))
    return pl.pallas_call(
        _s5_scan_kernel,
        grid=(S5_M, T // L // S5_NCH),
        in_specs=[uspec,
                  wspec((L * LANES, L * LANES)), wspec((2 * S5_ST, L * LANES)),
                  wspec((L * LANES, 2 * S5_ST)), wspec((1, 2 * S5_ST))],
        out_specs=uspec,
        out_shape=jax.ShapeDtypeStruct((S5_M, B, T // L, L * LANES), bf16),
        scratch_shapes=[pltpu.VMEM((B, 2 * S5_ST), f32), pltpu.VMEM((2 * S5_ST // LANES, B * S5_PITCH, LANES), f32),
                        pltpu.VMEM((2 * S5_ST // LANES, B * S5_PITCH, LANES), f32)],
        compiler_params=pltpu.CompilerParams(
            dimension_semantics=("arbitrary", "arbitrary"), vmem_limit_bytes=VMEM_LIMIT),
        name="s5_scan",
    )(u, wt, mi, mo, a16)


STAGE_ROWS = TT * TOPK * ROW


def _strip_copies(tab_ref, stage, region_hbm, sem, to_region):
    def expert(e, carry):
        n = tab_ref[0, e] * ROW

        @pl.when(n > 0)
        def _():
            local = stage.at[pl.ds(pl.multiple_of(tab_ref[1, e] * ROW, ROW), n), :]
            remote = region_hbm.at[pl.ds(pl.multiple_of((e * N + tab_ref[2, e]) * ROW, ROW), n), :]
            src, dst = (local, remote) if to_region else (remote, local)
            pltpu.make_async_copy(src, dst, sem).start()
        return carry
    lax.fori_loop(0, NE, expert, 0)


def _strip_wait(stage, region_hbm, sem):
    pltpu.make_async_copy(stage, region_hbm.at[pl.ds(0, STAGE_ROWS), :], sem).wait()


def _mixer_out_kernel(x_ref, gla_ref, y_ref, g1_ref, sh2_ref, sc2_ref, gluw, glub, woa, wob,
                      mpg, fpg, rw_hi, rw_lo, rb, tri_ref, upper_ref, grp_ref,
                      h1_ref, xs_hbm, pos_ref, tab_ref, gate_ref, cnt_ref,
                      cnt_s, y_s, hn2_s, stage, pos_v, tab_v, pos_sm, tab_sm, sem, tsem):
    step = pl.program_id(0) * pl.num_programs(1) + pl.program_id(1)
    first = step == 0

    @pl.when(first)
    def _():
        cnt_s[...] = jnp.zeros_like(cnt_s)

    for m in range(S5_M):
        for jj in range(S5_L):
            y_s[m, pl.ds(jj, TT // S5_L, stride=FOLD_PITCH), :] = y_ref[m, :, jj * LANES:(jj + 1) * LANES].astype(f32)
    y = jnp.concatenate(
        [jnp.concatenate([y_s[m, n * FOLD_PITCH:n * FOLD_PITCH + S5_L, :] for n in range(TT // S5_L)], axis=0)
         for m in range(S5_M)], axis=1)
    z = _dot(y.astype(bf16), gluw[...]) + glub[...]
    s5 = y * jax.nn.sigmoid(z)
    mix = _dot(gla_ref[...], woa[...]) + _dot(s5.astype(bf16), wob[...])
    h1 = x_ref[...] + g1_ref[...] * (mix * _rms(mix) * mpg[...])
    hn2 = (h1 * _rms(h1) * fpg[...]) * (1.0 + sc2_ref[...]) + sh2_ref[...]
    for cc in range(ROW):
        h1_ref[pl.ds(cc, TT, stride=ROW), :] = h1[:, cc * LANES:(cc + 1) * LANES]
        hn2_s[lax.rem(step, 2), pl.ds(cc, TT, stride=ROW), :] = hn2[:, cc * LANES:(cc + 1) * LANES]
    x_hi = hn2.astype(bf16)
    x_lo = (hn2 - x_hi.astype(f32)).astype(bf16)
    logits = _dot(x_hi, rw_hi[...]) + (_dot(x_lo, rw_hi[...]) + _dot(x_hi, rw_lo[...])) + rb[...]

    lane = lax.broadcasted_iota(i32, (TT, LANES), 1).astype(f32)
    l = logits
    vals, idxs = [], []
    for _ in range(TOPK):
        m = jnp.max(l, axis=-1, keepdims=True)
        ix = jnp.min(jnp.where(l == m, lane, float(LANES)), axis=-1, keepdims=True)
        vals.append(m)
        idxs.append(ix)
        l = jnp.where(lane == ix, -jnp.inf, l)
    es = [jnp.exp(v - vals[0]) for v in vals]
    tot = es[0] + es[1] + es[2] + es[3]
    oh = jnp.zeros((TT, LANES), f32)
    for ix in idxs:
        oh = oh + (lane == ix).astype(f32)
    cum = _dot(tri_ref[...], oh.astype(bf16))
    lcnt = cum[TT - 1:TT, :]
    gbase = cnt_s[...]
    hi = jnp.floor(lcnt * (1.0 / 64.0))
    lo = lcnt - hi * 64.0
    upper = upper_ref[...]
    hi8 = jnp.broadcast_to(hi, (ROW, LANES)).astype(bf16)
    lo8 = jnp.broadcast_to(lo, (ROW, LANES)).astype(bf16)
    lbase = (64.0 * _dot(hi8, upper) + _dot(lo8, upper))[0:1, :]
    loc = lbase + (cum - oh)
    tmod = (lax.broadcasted_iota(i32, (TT, LANES), 0) % PG).astype(f32) * float(TOPK)
    spread = jnp.zeros((TT, LANES), f32)
    gate = jnp.zeros((TT, LANES), f32)
    for kk in range(TOPK):
        p = jnp.sum(jnp.where(lane == idxs[kk], loc, 0.0), axis=-1, keepdims=True)
        spread = jnp.where(lane == tmod + float(kk), p, spread)
        gate = jnp.where(lane == float(kk), es[kk] / tot, gate)
    s_hi = jnp.floor(spread * (1.0 / 64.0))
    s_lo = spread - s_hi * 64.0
    grp = grp_ref[...]
    folded = 64.0 * _dot(grp, s_hi.astype(bf16)) + _dot(grp, s_lo.astype(bf16))
    posi = (folded * float(ROW)).astype(i32)
    pos_ref[...] = posi
    pos_v[lax.rem(step, 2)] = posi
    sub = lax.broadcasted_iota(i32, (ROW, LANES), 0)
    tab = jnp.where(sub == 0, lcnt, jnp.where(sub == 1, lbase, jnp.where(sub == 2, gbase, 0.0)))
    tabi = tab.astype(i32)
    tab_ref[...] = tabi
    tab_v[lax.rem(step, 2)] = tabi
    gate_ref[...] = gate
    cnt = cnt_s[...] + cum[TT - 1:TT, :]
    cnt_s[...] = cnt
    cnt_ref[...] = cnt

    nsteps = pl.num_programs(0) * pl.num_programs(1)
    cur = lax.rem(step, 2)

    def table_copies():
        return (pltpu.make_async_copy(pos_v.at[cur], pos_sm, tsem.at[0]),
                pltpu.make_async_copy(tab_v.at[cur], tab_sm, tsem.at[1]))

    def dispatch(s):
        def group(g, carry):
            for jj in range(PG):
                v = hn2_s[s, pl.ds(pl.multiple_of((g * PG + jj) * ROW, ROW), ROW), :]
                for kk in range(TOPK):
                    stage[s, pl.ds(pl.multiple_of(pos_sm[g, jj * TOPK + kk], ROW), ROW), :] = v
            return carry
        lax.fori_loop(0, TT // PG, group, 0)
        _strip_copies(tab_sm, stage.at[s], xs_hbm, sem.at[s], to_region=True)

    @pl.when(step >= 1)
    def _():
        for c in table_copies():
            c.wait()

        @pl.when(step >= 3)
        def _():
            _strip_wait(stage.at[1 - cur], xs_hbm, sem.at[1 - cur])
        for s in range(2):
            pl.when(1 - cur == s)(functools.partial(dispatch, s))

    for c in table_copies():
        c.start()

    @pl.when(step == nsteps - 1)
    def _():
        for c in table_copies():
            c.wait()

        @pl.when(step >= 2)
        def _():
            _strip_wait(stage.at[cur], xs_hbm, sem.at[cur])
        for s in range(2):
            pl.when(cur == s)(functools.partial(dispatch, s))
        _strip_wait(stage.at[1 - cur], xs_hbm, sem.at[1 - cur])
        _strip_wait(stage.at[cur], xs_hbm, sem.at[cur])


def _mixer_out(x, gla, y, mod3, gluw, glub, woa, wob, mpg, fpg, rw_hi, rw_lo, rb):
    nt = T // TT
    full = lambda shape: pl.BlockSpec(shape, lambda b, t: (0,) * len(shape))
    tok = lambda w: pl.BlockSpec((None, TT, w), lambda b, t: (b, t, 0))
    slab = pl.BlockSpec((TT * ROW, LANES), lambda b, t: (b * nt + t, 0))
    slab_shape = jax.ShapeDtypeStruct((N * ROW, LANES), f32)
    yspec = pl.BlockSpec((S5_M, None, TT // S5_L, S5_L * LANES), lambda b, t: (0, b, t, 0))
    modspec = lambda j: pl.BlockSpec((None, 1, D), lambda b, t: (b, 0, j))
    tix = jnp.arange(TT, dtype=i32)
    lanes = jnp.arange(LANES, dtype=i32)
    tri = (tix[:, None] >= tix[None, :]).astype(bf16)
    upper = (lanes[:, None] < lanes[None, :]).astype(bf16)
    grp = (tix[None, :] // PG == jnp.arange(TT // PG, dtype=i32)[:, None]).astype(bf16)
    return pl.pallas_call(
        _mixer_out_kernel,
        grid=(B, nt),
        in_specs=[tok(D), tok(512), yspec,
                  modspec(2), modspec(3), modspec(4),
                  full((512, 512)), full((1, 512)), full((512, D)), full((512, D)),
                  full((1, D)), full((1, D)), full((D, LANES)), full((D, LANES)), full((1, LANES)),
                  full((TT, TT)), full((LANES, LANES)), full((TT // PG, TT))],
        out_specs=[slab, pl.BlockSpec(memory_space=pl.ANY),
                   pl.BlockSpec((TT // PG, LANES), lambda b, t: (b * nt + t, 0)),
                   pl.BlockSpec((ROW, LANES), lambda b, t: (b * nt + t, 0)), tok(LANES), full((1, LANES))],
        out_shape=[slab_shape, jax.ShapeDtypeStruct((NE * N * ROW, LANES), f32),
                   jax.ShapeDtypeStruct((N // PG, LANES), i32), jax.ShapeDtypeStruct((N // TT * ROW, LANES), i32),
                   jax.ShapeDtypeStruct((B, T, LANES), f32),
                   jax.ShapeDtypeStruct((1, LANES), f32)],
        scratch_shapes=[pltpu.VMEM((1, LANES), f32), pltpu.VMEM((S5_M, TT // S5_L * FOLD_PITCH, LANES), f32),
                        pltpu.VMEM((2, TT * ROW, LANES), f32), pltpu.VMEM((2, STAGE_ROWS, LANES), f32),
                        pltpu.VMEM((2, TT // PG, LANES), i32), pltpu.VMEM((2, ROW, LANES), i32),
                        pltpu.SMEM((TT // PG, LANES), i32), pltpu.SMEM((ROW, LANES), i32),
                        pltpu.SemaphoreType.DMA((2,)), pltpu.SemaphoreType.DMA((2,))],
        compiler_params=pltpu.CompilerParams(
            dimension_semantics=("arbitrary", "arbitrary"), vmem_limit_bytes=VMEM_LIMIT),
        name="mixer_out",
    )(x, gla, y, mod3, mod3, mod3, gluw, glub, woa, wob, mpg, fpg, rw_hi, rw_lo, rb, tri, upper, grp)


def _experts_kernel(tb_ref, te_ref, tv_ref, tf_ref, tn_ref, ts_ref, nu_ref,
                    x_ref, wgu_hbm, bgu_ref, wd_hbm, bd_ref, y_ref, wgu_f, wd_f, wgu_b, wd_b, sem):
    i = pl.program_id(0)

    def fetch(e):
        return (pltpu.make_async_copy(wgu_hbm.at[e], wgu_f, sem.at[0]),
                pltpu.make_async_copy(wd_hbm.at[e], wd_f, sem.at[1]))

    def convert(s):
        wgu_b[s] = wgu_f[...].astype(bf16)
        wd_b[s] = wd_f[...].astype(bf16)

    def tile(s, h, nt=1, rows=None):
        rows = nt * MB if rows is None else rows
        base = h * MB * ROW
        xt = jnp.concatenate([x_ref[pl.ds(base + cc, rows, stride=ROW), :] for cc in range(ROW)], axis=1)
        valid = lax.broadcasted_iota(i32, (rows, 1), 0) < tv_ref[i] - h * MB
        xt = jnp.where(valid, xt, 0.0).astype(bf16)
        gu = _dot(xt, wgu_b[s]) + bgu_ref[...]
        g = jnp.minimum(gu[:, :DFF], LIMIT)
        up = jnp.clip(gu[:, DFF:], -LIMIT, LIMIT)
        act = (up + 1.0) * (g * jax.nn.sigmoid(ALPHA * g))
        yv = _dot(act.astype(bf16), wd_b[s]) + bd_ref[...]
        for cc in range(ROW):
            y_ref[pl.ds(base + cc, rows, stride=ROW), :] = yv[:, cc * LANES:(cc + 1) * LANES]

    def run_tiles(s):
        nv = tv_ref[i]
        rem = nv - (nv // MB) * MB
        half = jnp.logical_and(rem > 0, rem <= MB // 2)
        nfull = nv // MB + jnp.where(rem > MB // 2, 1, 0)
        pl.when(nfull == EB)(functools.partial(tile, s, 0, EB))
        few = nfull < EB
        for h in range(0, EB, 2):
            pl.when(jnp.logical_and(few, nfull >= h + 2))(functools.partial(tile, s, h, 2))
            pl.when(jnp.logical_and(few, nfull == h + 1))(functools.partial(tile, s, h, 1))
        for h in range(EB):
            pl.when(jnp.logical_and(half, nfull == h))(functools.partial(tile, s, h, 1, MB // 2))

    @pl.when(i < nu_ref[0])
    def _():
        e = te_ref[i]
        s = ts_ref[i]
        nxt = tn_ref[i]
        first = (tf_ref[i] & 1) != 0
        last = (tf_ref[i] & 2) != 0

        @pl.when(i == 0)
        def _():
            for c in fetch(e):
                c.start()
            for c in fetch(e):
                c.wait()
            convert(s)

        @pl.when(jnp.logical_and(first, nxt >= 0))
        def _():
            for c in fetch(nxt):
                c.start()

        handoff = jnp.logical_and(last, nxt >= 0)

        @pl.when(handoff)
        def _():
            for c in fetch(nxt):
                c.wait()
            convert(1 - s)

        run_tiles(s)


def _experts(tile_blk, tile_e, tile_nv, tile_flags, tile_next, tile_slot, nused, xs, w_gu, b_gu, w_down, b_down):
    blk = lambda i, tb, te, *_: (tb[i], 0)
    bsel = lambda i, tb, te, *_: (te[i], 0, 0)
    grid_spec = pltpu.PrefetchScalarGridSpec(
        num_scalar_prefetch=7,
        grid=(NB,),
        in_specs=[
            pl.BlockSpec((EB * MB * ROW, LANES), blk),
            pl.BlockSpec(memory_space=pl.ANY),
            pl.BlockSpec((None, 1, 2 * DFF), bsel),
            pl.BlockSpec(memory_space=pl.ANY),
            pl.BlockSpec((None, 1, D), bsel),
        ],
        out_specs=pl.BlockSpec((EB * MB * ROW, LANES), blk),
        scratch_shapes=[pltpu.VMEM((D, 2 * DFF), f32), pltpu.VMEM((DFF, D), f32),
                        pltpu.VMEM((2, D, 2 * DFF), bf16), pltpu.VMEM((2, DFF, D), bf16),
                        pltpu.SemaphoreType.DMA((2,))],
    )
    return pl.pallas_call(
        _experts_kernel,
        grid_spec=grid_spec,
        out_shape=jax.ShapeDtypeStruct((NE * N * ROW, LANES), f32),
        compiler_params=pltpu.CompilerParams(dimension_semantics=("arbitrary",), vmem_limit_bytes=VMEM_LIMIT),
        name="experts",
    )(tile_blk, tile_e, tile_nv, tile_flags, tile_next, tile_slot, nused, xs,
      w_gu, b_gu.reshape(NE, 1, 2 * DFF), w_down, b_down.reshape(NE, 1, D))


def _combine_kernel(pos_ref, tab_ref, tabn_ref, gate_ref, h1_ref, g2_ref, pg_ref, ys_hbm, o_ref,
                    stage, wbuf, rbuf, sem):
    j = pl.program_id(0)
    slot = lax.rem(j, 2)
    GT = PG
    GR = GT * ROW

    @pl.when(j == 0)
    def _():
        _strip_copies(tab_ref, stage.at[0], ys_hbm, sem.at[0], to_region=False)

    @pl.when(j + 1 < pl.num_programs(0))
    def _():
        _strip_copies(tabn_ref, stage.at[1 - slot], ys_hbm, sem.at[1 - slot], to_region=False)

    for kk in range(TOPK):
        wbuf[kk] = jnp.broadcast_to(gate_ref[:, kk:kk + 1], (TT, LANES))
    _strip_wait(stage.at[slot], ys_hbm, sem.at[slot])
    g2 = g2_ref[...]
    pg = pg_ref[...]

    def one_group(s, g):
        rows = pl.ds(pl.multiple_of(g * GR, GR), GR)
        parts = []
        for jj in range(GT):
            t = g * GT + jj
            acc = None
            for kk in range(TOPK):
                w = jnp.broadcast_to(wbuf[kk, pl.ds(t, 1), :], (ROW, LANES))
                term = stage[s, pl.ds(pl.multiple_of(pos_ref[g, jj * TOPK + kk], ROW), ROW), :] * w
                acc = term if acc is None else acc + term
            parts.append(acc)
        ff3 = jnp.concatenate(parts, axis=0).reshape(GT, ROW, LANES)
        ss = jnp.sum(jnp.sum(ff3 * ff3, axis=2, keepdims=True), axis=1, keepdims=True)
        rs = lax.rsqrt(ss * (1.0 / D) + EPS)
        out = h1_ref[rows, :].reshape(GT, ROW, LANES) + g2 * (ff3 * rs * pg)
        rbuf[rows, :] = out.reshape(GR, LANES)

    def gather(s):
        def trip(gg, carry):
            for k in range(COMBINE_GROUPS_PER_TRIP):
                one_group(s, COMBINE_GROUPS_PER_TRIP * gg + k)
            return carry
        lax.fori_loop(0, TT // GT // COMBINE_GROUPS_PER_TRIP, trip, 0)

    for s in range(2):
        pl.when(slot == s)(functools.partial(gather, s))
    for cc in range(ROW):
        o_ref[:, cc * LANES:(cc + 1) * LANES] = rbuf[pl.ds(cc, TT, stride=ROW), :]


def _combine(pos, tab, gates, h1_rows, mod4, pg, ys):
    nt = N // TT
    per_b = T // TT
    tabspec = lambda imap: pl.BlockSpec((ROW, LANES), imap, memory_space=pltpu.SMEM)
    return pl.pallas_call(
        _combine_kernel,
        grid=(nt,),
        in_specs=[pl.BlockSpec((TT // PG, LANES), lambda j: (j, 0), memory_space=pltpu.SMEM),
                  tabspec(lambda j: (j, 0)), tabspec(lambda j: (jnp.minimum(j + 1, nt - 1), 0)),
                  pl.BlockSpec((TT, LANES), lambda j: (j, 0)),
                  pl.BlockSpec((TT * ROW, LANES), lambda j: (j, 0)),
                  pl.BlockSpec((None, None, ROW, LANES), lambda j: (j // per_b, 5, 0, 0)),
                  pl.BlockSpec((ROW, LANES), lambda j: (0, 0)),
                  pl.BlockSpec(memory_space=pl.ANY)],
        out_specs=pl.BlockSpec((TT, D), lambda j: (j, 0)),
        out_shape=jax.ShapeDtypeStruct((N, D), f32),
        scratch_shapes=[pltpu.VMEM((2, STAGE_ROWS, LANES), f32), pltpu.VMEM((TOPK, TT, LANES), f32),
                        pltpu.VMEM((TT * ROW, LANES), f32), pltpu.SemaphoreType.DMA((2,))],
        compiler_params=pltpu.CompilerParams(dimension_semantics=("arbitrary",), vmem_limit_bytes=VMEM_LIMIT),
        name="combine",
    )(pos, tab, tab, gates, h1_rows, mod4, pg, ys)


def _pad_heads(w):
    lead = w.shape[:-1]
    w4 = w.reshape(lead + (GLA_H, GLA_DK))
    w4 = jnp.concatenate([w4, jnp.zeros_like(w4)], axis=-1)
    return w4.reshape(lead + (GLA_H * LANES,))


def kernel(x, c, ada_w, ada_b, mix_pre_g, mix_post_g, ffn_pre_g, ffn_post_g, w_in, w_alpha, b_alpha, gla_norm_g, s5_lambda_re, s5_lambda_im, s5_log_dt, s5_b_re, s5_b_im, s5_c_re, s5_c_im, s5_d, s5_glu_w, s5_glu_b, w_out, router_w, router_b, exp_w_gu, exp_b_gu, exp_w_down, exp_b_down):
    l = 0
    mod = _ada(c, ada_w[l], ada_b[l:l + 1])
    mod3 = mod.reshape(B, 1, 6 * D)
    mod4 = mod.reshape(B, 6, ROW, LANES)

    w = w_in[l]
    o_q, o_k, o_v, o_g, o_a, o_u = 0, 256, 512, 1024, 1536, 1552
    wq = _pad_heads(w[:, o_q:o_k]).astype(bf16)
    wk = _pad_heads(w[:, o_k:o_v]).astype(bf16)
    wv = w[:, o_v:o_g].astype(bf16)
    wg = w[:, o_g:o_a].astype(bf16)
    wa = jnp.pad(w[:, o_a:o_u], ((0, 0), (0, LANES - LOWRANK))).astype(bf16)
    wu = w[:, o_u:].astype(bf16)
    walpha = jnp.pad(_pad_heads(w_alpha[l]), ((0, LANES - LOWRANK), (0, 0))).astype(bf16)
    balpha = _pad_heads(b_alpha[l:l + 1])

    gla, u = _mixer_in(x, mod3, mix_pre_g[l:l + 1], wq, wk, wv, wg, wa, wu,
                       walpha, balpha, gla_norm_g[l:l + 1])

    wt, mi, mo, a16 = _s5_prep(s5_lambda_re[l], s5_lambda_im[l], s5_log_dt[l], s5_b_re[l], s5_b_im[l],
                               s5_c_re[l], s5_c_im[l], s5_d[l])
    y = _s5_scan(u, wt, mi, mo, a16)

    rw = jnp.pad(router_w[l], ((0, 0), (0, LANES - NE)))
    rw_hi = rw.astype(bf16)
    rw_lo = (rw - rw_hi.astype(f32)).astype(bf16)
    rb = jnp.pad(router_b[l:l + 1], ((0, 0), (0, LANES - NE)), constant_values=-1e30)
    wo = w_out[l].astype(bf16)
    h1_rows, xs, pos, tab, gates, cnt = _mixer_out(
        x, gla, y, mod3, s5_glu_w[l].astype(bf16), s5_glu_b[l:l + 1], wo[:512], wo[512:],
        mix_post_g[l:l + 1], ffn_pre_g[l:l + 1], rw_hi, rw_lo, rb)

    counts = cnt[0, :NE].astype(i32)
    step_rows = EB * MB
    ntile = (counts + step_rows - 1) // step_rows
    tends = jnp.cumsum(ntile)
    nused = tends[-1]
    gi = jnp.minimum(jnp.arange(NB, dtype=i32), nused - 1)
    tile_e = jnp.sum((tends[None, :] <= gi[:, None]).astype(i32), axis=1)
    sel = tile_e[:, None] == jnp.arange(NE, dtype=i32)[None, :]
    pick = lambda v: jnp.sum(jnp.where(sel, v[None, :], 0), axis=1)
    tile_j = gi - pick(tends - ntile)
    tile_blk = tile_e * TPE + tile_j
    tile_nv = jnp.clip(pick(counts) - tile_j * step_rows, 0, step_rows)
    tile_flags = (tile_j == 0).astype(i32) + 2 * (tile_j == pick(ntile) - 1).astype(i32)
    used = ntile > 0
    eids = jnp.arange(NE, dtype=i32)
    later = jnp.logical_and(used[None, :], eids[None, :] > eids[:, None])
    nxt = jnp.min(jnp.where(later, eids[None, :], NE), axis=1)
    tile_next = pick(jnp.where(nxt < NE, nxt, -1))
    tile_slot = pick((jnp.cumsum(used.astype(i32)) - 1) % 2)

    ys = _experts(tile_blk, tile_e, tile_nv, tile_flags, tile_next, tile_slot, nused.reshape(1), xs,
                  exp_w_gu[l], exp_b_gu[l], exp_w_down[l], exp_b_down[l])
    out = _combine(pos, tab, gates.reshape(N, LANES), h1_rows, mod4, ffn_post_g[l].reshape(ROW, LANES), ys)
    return out.reshape(B, T, D)
```

```python
import functools

import jax
import jax.numpy as jnp
from jax import lax
from jax.experimental import pallas as pl
from jax.experimental.pallas import tpu as pltpu

f32 = jnp.float32
bf16 = jnp.bfloat16
i32 = jnp.int32

D = 1024
B = 8
T = 2048
N = B * T
GLA_H = 4
GLA_DV = 128
GLA_DK = 64
GLA_TAU = 16.0
GLA_CHUNK = 64
LOWRANK = 16
S5_W = 512
S5_CH = 16
S5_G = 32
S5_P = 64
NE = 32
TOPK = 4
DFF = 1024
ALPHA = 1.702
LIMIT = 7.0
EPS = 1e-6

LANES = 128
SUBLANES = 8
VMEM_LIMIT = 56 * 1024 * 1024

TT = 512
S5_L = 16
S5_M = S5_W // LANES
S5_GPT = LANES // S5_CH
S5_ST = S5_GPT * S5_P
S5_NCH = 32
S5_ROWS = B * S5_NCH
S5_PITCH = S5_NCH + SUBLANES
FOLD_PITCH = S5_L + SUBLANES
ROW = SUBLANES
MB = 256
EB = 4
NB = (N * TOPK) // (EB * MB) + NE
TPE = N // (EB * MB)
PG = 16
COMBINE_GROUPS_PER_TRIP = 8


def _dot(a, b):
    return jnp.dot(a, b, preferred_element_type=f32)


def _dot_t(a, b, ca, cb):
    return lax.dot_general(a, b, (((ca,), (cb,)), ((), ())), preferred_element_type=f32)


def _rms(x):
    return lax.rsqrt(jnp.mean(x * x, axis=-1, keepdims=True) + EPS)


def _ada_kernel(c_ref, w_ref, b_ref, o_ref):
    c = c_ref[...]
    s = (c * jax.nn.sigmoid(c)).astype(bf16)
    o_ref[...] = _dot(s, w_ref[...].astype(bf16)) + b_ref[...]


def _ada(c, w, b):
    return pl.pallas_call(
        _ada_kernel,
        grid=(6,),
        in_specs=[
            pl.BlockSpec((B, D), lambda j: (0, 0)),
            pl.BlockSpec((D, D), lambda j: (0, j)),
            pl.BlockSpec((1, D), lambda j: (0, j)),
        ],
        out_specs=pl.BlockSpec((B, D), lambda j: (0, j)),
        out_shape=jax.ShapeDtypeStruct((B, 6 * D), f32),
        name="ada",
    )(c, w, b)


def _mixer_in_kernel(x_ref, sc_ref, sh_ref, g_ref, wq, wk, wv, wg, wa, wu, walpha, balpha, gng, cm_ref,
                     gla_ref, u_ref, st_s, u_s):
    t = pl.program_id(1)

    @pl.when(t == 0)
    def _():
        st_s[...] = jnp.zeros_like(st_s)

    x = x_ref[...]
    hn = (x * _rms(x) * g_ref[...]) * (1.0 + sc_ref[...]) + sh_ref[...]
    hb = hn.astype(bf16)
    u = _dot(hb, wu[...])

    C = GLA_CHUNK
    NC = TT // C
    causal = cm_ref[...] > 0.5
    tri = cm_ref[...].astype(bf16)
    a_lr = _dot(hb, wa[...]).astype(bf16)
    la = jax.nn.log_sigmoid(_dot(a_lr, walpha[...]) + balpha[...]) * (1.0 / GLA_TAU)
    la_hi = la.astype(bf16)
    la_lo = (la - la_hi.astype(f32)).astype(bf16)
    bc = _dot(tri, la_hi) + _dot(tri, la_lo)
    bl = jnp.broadcast_to(bc.reshape(NC, C, GLA_H * LANES)[:, C - 1:C, :],
                          (NC, C, GLA_H * LANES)).reshape(TT, GLA_H * LANES)
    q = _dot(hb, wq[...]) * (GLA_DK ** -0.5)
    k = _dot(hb, wk[...])
    qd = (q * jnp.exp(bc)).astype(bf16)
    ki = (k * jnp.exp(-bc)).astype(bf16)
    ke = k * jnp.exp(bl - bc)
    dec = jnp.exp(bl)
    vb = _dot(hb, wv[...]).astype(bf16)
    og = _dot(hb, wg[...])
    gn = gng[...]
    heads = range(GLA_H)
    sls = [slice(h * LANES, (h + 1) * LANES) for h in heads]
    kehb = ke.astype(bf16)
    raw = [_dot_t(qd[:, sl], ki[:, sl], 1, 1) for sl in sls]
    deltas = [[_dot_t(vb[c * C:(c + 1) * C, sl], kehb[c * C:(c + 1) * C, sl], 0, 0) for c in range(NC)] for sl in sls]
    scs = [jnp.where(causal, r, 0.0).astype(bf16) for r in raw]
    o_intra = [_dot(scs[h], vb[:, sls[h]]) for h in heads]
    sts = []
    for h in heads:
        st = st_s[h]
        per = []
        for c in range(NC):
            per.append(st.astype(bf16))
            st = st * dec[c * C:c * C + 1, sls[h]] + deltas[h][c]
        st_s[h] = st
        sts.append(per)
    for h in heads:
        sl = sls[h]
        o_inter = jnp.concatenate([_dot_t(qd[c * C:(c + 1) * C, sl], sts[h][c], 1, 1) for c in range(NC)], axis=0)
        o = o_intra[h] + o_inter
        on = o * _rms(o) * gn
        ogh = og[:, sl]
        gla_ref[:, sl] = (on * (ogh * jax.nn.sigmoid(ogh))).astype(bf16)
        for n in range(TT // S5_L):
            u_s[h, n * FOLD_PITCH:n * FOLD_PITCH + S5_L, :] = u[n * S5_L:(n + 1) * S5_L, sl]
        for i in range(S5_L):
            u_ref[h, :, i * LANES:(i + 1) * LANES] = u_s[h, pl.ds(i, TT // S5_L, stride=FOLD_PITCH), :].astype(bf16)


def _mixer_in(x, mod3, g, wq, wk, wv, wg, wa, wu, walpha, balpha, gng):
    assert S5_M == GLA_H
    nt = T // TT
    full = lambda shape: pl.BlockSpec(shape, lambda b, t: (0,) * len(shape))
    u_shape = jax.ShapeDtypeStruct((S5_M, B, T // S5_L, S5_L * LANES), bf16)
    u_spec = pl.BlockSpec((S5_M, None, TT // S5_L, S5_L * LANES), lambda b, t: (0, b, t, 0))
    tok = jnp.arange(TT, dtype=i32)
    same = (tok[:, None] // GLA_CHUNK) == (tok[None, :] // GLA_CHUNK)
    cmask = jnp.logical_and(tok[:, None] >= tok[None, :], same).astype(f32)
    return pl.pallas_call(
        _mixer_in_kernel,
        grid=(B, nt),
        in_specs=[
            pl.BlockSpec((None, TT, D), lambda b, t: (b, t, 0)),
            pl.BlockSpec((None, 1, D), lambda b, t: (b, 0, 1)),
            pl.BlockSpec((None, 1, D), lambda b, t: (b, 0, 0)),
            full((1, D)),
            full((D, 512)), full((D, 512)), full((D, 512)), full((D, 512)),
            full((D, LANES)), full((D, 512)),
            full((LANES, 512)), full((1, 512)), full((1, LANES)),
            full((TT, TT)),
        ],
        out_specs=[pl.BlockSpec((None, TT, 512), lambda b, t: (b, t, 0)), u_spec],
        out_shape=[jax.ShapeDtypeStruct((B, T, 512), bf16), u_shape],
        scratch_shapes=[pltpu.VMEM((GLA_H, GLA_DV, LANES), f32),
                        pltpu.VMEM((S5_M, TT // S5_L * FOLD_PITCH, LANES), f32)],
        compiler_params=pltpu.CompilerParams(
            dimension_semantics=("arbitrary", "arbitrary"), vmem_limit_bytes=VMEM_LIMIT),
        name="mixer_in",
    )(x, mod3, mod3, g, wq, wk, wv, wg, wa, wu, walpha, balpha, gng, cmask)


def _cpow(xr, xi, d):
    mag = jnp.exp(xr * d)
    return mag * jnp.cos(xi * d), mag * jnp.sin(xi * d)


def _s5_prep_kernel(lr_r, li_r, ld_r, lr_c, li_c, ld_c, btr, bti, ctr, cti, d_r,
                    wt_ref, mi_ref, mo_ref, a_ref, kt_s):
    L = S5_L
    lr = lr_r[...]
    li = li_r[...]
    dt = jnp.exp(ld_r[...])
    xr, xi = lr * dt, li * dt
    ar, ai = _cpow(xr, xi, 1.0)
    den = lr * lr + li * li
    fr = ((ar - 1.0) * lr + ai * li) / den
    fi = (ai * lr - (ar - 1.0) * li) / den
    br, bi = btr[...], bti[...]
    bbr = fr * br - fi * bi
    bbi = fr * bi + fi * br
    cr, ci = ctr[...], cti[...]
    cr_b, ci_b = cr.astype(bf16), ci.astype(bf16)
    pr, pi = jnp.ones_like(ar), jnp.zeros_like(ai)
    for d in range(L):
        xdr = bbr * pr - bbi * pi
        xdi = bbr * pi + bbi * pr
        i = L - 1 - d
        mo_ref[i * LANES:(i + 1) * LANES, :S5_ST] = xdr.astype(bf16)
        mo_ref[i * LANES:(i + 1) * LANES, S5_ST:] = xdi.astype(bf16)
        kt = _dot(xdr.astype(bf16), cr_b) - _dot(xdi.astype(bf16), ci_b)
        if d == 0:
            r = lax.broadcasted_iota(i32, (LANES, LANES), 0)
            c = lax.broadcasted_iota(i32, (LANES, LANES), 1)
            kt = kt + jnp.where(r == c, d_r[...], 0.0)
        kt_s[d] = kt.astype(bf16)
        pr, pi = pr * ar - pi * ai, pr * ai + pi * ar
    a_ref[:, :S5_ST] = pr
    a_ref[:, S5_ST:] = pi
    zero = jnp.zeros((LANES, LANES), bf16)
    for i in range(L):
        for j in range(L):
            wt_ref[i * LANES:(i + 1) * LANES, j * LANES:(j + 1) * LANES] = kt_s[j - i] if j >= i else zero
    lrc = lr_c[...]
    lic = li_c[...]
    dtc = jnp.exp(ld_c[...])
    acr, aci = _cpow(lrc * dtc, lic * dtc, 1.0)
    pr, pi = acr, aci
    for j in range(L):
        mi_ref[:S5_ST, j * LANES:(j + 1) * LANES] = (cr * pr - ci * pi).astype(bf16)
        mi_ref[S5_ST:, j * LANES:(j + 1) * LANES] = (-(cr * pi + ci * pr)).astype(bf16)
        pr, pi = pr * acr - pi * aci, pr * aci + pi * acr


def _s5_prep(lr, li, ld, b_re, b_im, c_re, c_im, dvec):
    G, P, H, M, GPT = S5_G, S5_P, S5_CH, S5_M, S5_GPT
    eye = jnp.eye(GPT, dtype=f32)

    def rows(v):
        return v.reshape(M, 1, S5_ST)

    def cols(v):
        return v.reshape(M, S5_ST, 1)

    ldp = jnp.broadcast_to(ld[:, None], (G, P))

    def bt(b):
        b4 = b.reshape(M, GPT, P, H)
        return jnp.einsum('mgph,gk->mkhgp', b4, eye).reshape(M, LANES, S5_ST)

    def ct(c):
        c4 = c.reshape(M, GPT, H, P)
        return jnp.einsum('mghp,gk->mgpkh', c4, eye).reshape(M, S5_ST, LANES)

    L = S5_L
    mspec = lambda shape: pl.BlockSpec((None,) + shape, lambda m: (m,) + (0,) * len(shape))
    return pl.pallas_call(
        _s5_prep_kernel,
        grid=(M,),
        in_specs=[mspec((1, S5_ST))] * 3 + [mspec((S5_ST, 1))] * 3
        + [mspec((LANES, S5_ST))] * 2 + [mspec((S5_ST, LANES))] * 2 + [mspec((1, LANES))],
        out_specs=[mspec((L * LANES, L * LANES)), mspec((2 * S5_ST, L * LANES)),
                   mspec((L * LANES, 2 * S5_ST)), mspec((1, 2 * S5_ST))],
        out_shape=[jax.ShapeDtypeStruct((M, L * LANES, L * LANES), bf16),
                   jax.ShapeDtypeStruct((M, 2 * S5_ST, L * LANES), bf16),
                   jax.ShapeDtypeStruct((M, L * LANES, 2 * S5_ST), bf16),
                   jax.ShapeDtypeStruct((M, 1, 2 * S5_ST), f32)],
        scratch_shapes=[pltpu.VMEM((L, LANES, LANES), bf16)],
        compiler_params=pltpu.CompilerParams(dimension_semantics=("arbitrary",), vmem_limit_bytes=VMEM_LIMIT),
        name="s5_prep",
    )(rows(lr), rows(li), rows(ldp), cols(lr), cols(li), cols(ldp),
      bt(b_re), bt(b_im), ct(c_re), ct(c_im), dvec.reshape(M, 1, LANES))


def _s5_scan_kernel(u_ref, wt, mi, mo, a_ref, y_ref, xs, s_s, x_s):
    r = pl.program_id(1)

    @pl.when(r == 0)
    def _():
        xs[...] = jnp.zeros_like(xs)

    u = u_ref[...].reshape(S5_ROWS, S5_L * LANES)
    s = _dot(u, mo[...])
    y_intra = _dot(u, wt[...])
    NT = 2 * S5_ST // LANES
    for c in range(NT):
        for b in range(B):
            s_s[c, b * S5_PITCH:b * S5_PITCH + S5_NCH, :] = s[b * S5_NCH:(b + 1) * S5_NCH, c * LANES:(c + 1) * LANES]
    a = [a_ref[:, c * LANES:(c + 1) * LANES] for c in range(NT)]

    def step(n, x):
        rows = pl.ds(n, B, stride=S5_PITCH)
        new = []
        for c in range(NT):
            x_s[c, rows, :] = x[c]
        for c in range(NT // 2):
            ar, ai, xr, xi = a[c], a[NT // 2 + c], x[c], x[NT // 2 + c]
            new.append((ar * xr - ai * xi + s_s[c, rows, :], ar * xi + ai * xr + s_s[NT // 2 + c, rows, :]))
        return tuple(p[0] for p in new) + tuple(p[1] for p in new)

    x = tuple(xs[:, c * LANES:(c + 1) * LANES] for c in range(NT))
    for n in range(S5_NCH):
        x = step(n, x)
    for c in range(NT):
        xs[:, c * LANES:(c + 1) * LANES] = x[c]
    x_in = jnp.concatenate(
        [jnp.concatenate([x_s[c, b * S5_PITCH:b * S5_PITCH + S5_NCH, :] for b in range(B)], axis=0) for c in range(NT)],
        axis=1)
    y = y_intra + _dot(x_in.astype(bf16), mi[...])
    y_ref[...] = jax.nn.gelu(y).astype(bf16).reshape(B, S5_NCH, S5_L * LANES)


def _s5_scan(u, wt, mi, mo, a16):
    L = S5_L
    wspec = lambda shape: pl.BlockSpec((None,) + shape, lambda m, r: (m,) + (0,) * len(shape))
    uspec = pl.BlockSpec((None, B, S5_NCH, L * LANES), lambda m, r: (m, 0, r, 0))
    return pl.pallas_call(
        _s5_scan_kernel,
        grid=(S5_M, T // L // S5_NCH),
        in_specs=[uspec,
                  wspec((L * LANES, L * LANES)), wspec((2 * S5_ST, L * LANES)),
                  wspec((L * LANES, 2 * S5_ST)), wspec((1, 2 * S5_ST))],
        out_specs=uspec,
        out_shape=jax.ShapeDtypeStruct((S5_M, B, T // L, L * LANES), bf16),
        scratch_shapes=[pltpu.VMEM((B, 2 * S5_ST), f32), pltpu.VMEM((2 * S5_ST // LANES, B * S5_PITCH, LANES), f32),
                        pltpu.VMEM((2 * S5_ST // LANES, B * S5_PITCH, LANES), f32)],
        compiler_params=pltpu.CompilerParams(
            dimension_semantics=("arbitrary", "arbitrary"), vmem_limit_bytes=VMEM_LIMIT),
        name="s5_scan",
    )(u, wt, mi, mo, a16)


STAGE_ROWS = TT * TOPK * ROW


def _strip_copies(tab_ref, stage, region_hbm, sem, to_region):
    def expert(e, carry):
        n = tab_ref[0, e] * ROW

        @pl.when(n > 0)
        def _():
            local = stage.at[pl.ds(pl.multiple_of(tab_ref[1, e] * ROW, ROW), n), :]
            remote = region_hbm.at[pl.ds(pl.multiple_of((e * N + tab_ref[2, e]) * ROW, ROW), n), :]
            src, dst = (local, remote) if to_region else (remote, local)
            pltpu.make_async_copy(src, dst, sem).start()
        return carry
    lax.fori_loop(0, NE, expert, 0)


def _strip_wait(stage, region_hbm, sem):
    pltpu.make_async_copy(stage, region_hbm.at[pl.ds(0, STAGE_ROWS), :], sem).wait()


def _mixer_out_kernel(x_ref, gla_ref, y_ref, g1_ref, sh2_ref, sc2_ref, gluw, glub, woa, wob,
                      mpg, fpg, rw_hi, rw_lo, rb, tri_ref, upper_ref, grp_ref,
                      h1_ref, xs_hbm, pos_ref, tab_ref, gate_ref, cnt_ref,
                      cnt_s, y_s, hn2_s, stage, pos_v, tab_v, pos_sm, tab_sm, sem, tsem):
    step = pl.program_id(0) * pl.num_programs(1) + pl.program_id(1)
    first = step == 0

    @pl.when(first)
    def _():
        cnt_s[...] = jnp.zeros_like(cnt_s)

    for m in range(S5_M):
        for jj in range(S5_L):
            y_s[m, pl.ds(jj, TT // S5_L, stride=FOLD_PITCH), :] = y_ref[m, :, jj * LANES:(jj + 1) * LANES].astype(f32)
    y = jnp.concatenate(
        [jnp.concatenate([y_s[m, n * FOLD_PITCH:n * FOLD_PITCH + S5_L, :] for n in range(TT // S5_L)], axis=0)
         for m in range(S5_M)], axis=1)
    z = _dot(y.astype(bf16), gluw[...]) + glub[...]
    s5 = y * jax.nn.sigmoid(z)
    mix = _dot(gla_ref[...], woa[...]) + _dot(s5.astype(bf16), wob[...])
    h1 = x_ref[...] + g1_ref[...] * (mix * _rms(mix) * mpg[...])
    hn2 = (h1 * _rms(h1) * fpg[...]) * (1.0 + sc2_ref[...]) + sh2_ref[...]
    for cc in range(ROW):
        h1_ref[pl.ds(cc, TT, stride=ROW), :] = h1[:, cc * LANES:(cc + 1) * LANES]
        hn2_s[lax.rem(step, 2), pl.ds(cc, TT, stride=ROW), :] = hn2[:, cc * LANES:(cc + 1) * LANES]
    x_hi = hn2.astype(bf16)
    x_lo = (hn2 - x_hi.astype(f32)).astype(bf16)
    logits = _dot(x_hi, rw_hi[...]) + (_dot(x_lo, rw_hi[...]) + _dot(x_hi, rw_lo[...])) + rb[...]

    lane = lax.broadcasted_iota(i32, (TT, LANES), 1).astype(f32)
    l = logits
    vals, idxs = [], []
    for _ in range(TOPK):
        m = jnp.max(l, axis=-1, keepdims=True)
        ix = jnp.min(jnp.where(l == m, lane, float(LANES)), axis=-1, keepdims=True)
        vals.append(m)
        idxs.append(ix)
        l = jnp.where(lane == ix, -jnp.inf, l)
    es = [jnp.exp(v - vals[0]) for v in vals]
    tot = es[0] + es[1] + es[2] + es[3]
    oh = jnp.zeros((TT, LANES), f32)
    for ix in idxs:
        oh = oh + (lane == ix).astype(f32)
    cum = _dot(tri_ref[...], oh.astype(bf16))
    lcnt = cum[TT - 1:TT, :]
    gbase = cnt_s[...]
    hi = jnp.floor(lcnt * (1.0 / 64.0))
    lo = lcnt - hi * 64.0
    upper = upper_ref[...]
    hi8 = jnp.broadcast_to(hi, (ROW, LANES)).astype(bf16)
    lo8 = jnp.broadcast_to(lo, (ROW, LANES)).astype(bf16)
    lbase = (64.0 * _dot(hi8, upper) + _dot(lo8, upper))[0:1, :]
    loc = lbase + (cum - oh)
    tmod = (lax.broadcasted_iota(i32, (TT, LANES), 0) % PG).astype(f32) * float(TOPK)
    spread = jnp.zeros((TT, LANES), f32)
    gate = jnp.zeros((TT, LANES), f32)
    for kk in range(TOPK):
        p = jnp.sum(jnp.where(lane == idxs[kk], loc, 0.0), axis=-1, keepdims=True)
        spread = jnp.where(lane == tmod + float(kk), p, spread)
        gate = jnp.where(lane == float(kk), es[kk] / tot, gate)
    s_hi = jnp.floor(spread * (1.0 / 64.0))
    s_lo = spread - s_hi * 64.0
    grp = grp_ref[...]
    folded = 64.0 * _dot(grp, s_hi.astype(bf16)) + _dot(grp, s_lo.astype(bf16))
    posi = (folded * float(ROW)).astype(i32)
    pos_ref[...] = posi
    pos_v[lax.rem(step, 2)] = posi
    sub = lax.broadcasted_iota(i32, (ROW, LANES), 0)
    tab = jnp.where(sub == 0, lcnt, jnp.where(sub == 1, lbase, jnp.where(sub == 2, gbase, 0.0)))
    tabi = tab.astype(i32)
    tab_ref[...] = tabi
    tab_v[lax.rem(step, 2)] = tabi
    gate_ref[...] = gate
    cnt = cnt_s[...] + cum[TT - 1:TT, :]
    cnt_s[...] = cnt
    cnt_ref[...] = cnt

    nsteps = pl.num_programs(0) * pl.num_programs(1)
    cur = lax.rem(step, 2)

    def table_copies():
        return (pltpu.make_async_copy(pos_v.at[cur], pos_sm, tsem.at[0]),
                pltpu.make_async_copy(tab_v.at[cur], tab_sm, tsem.at[1]))

    def dispatch(s):
        def group(g, carry):
            for jj in range(PG):
                v = hn2_s[s, pl.ds(pl.multiple_of((g * PG + jj) * ROW, ROW), ROW), :]
                for kk in range(TOPK):
                    stage[s, pl.ds(pl.multiple_of(pos_sm[g, jj * TOPK + kk], ROW), ROW), :] = v
            return carry
        lax.fori_loop(0, TT // PG, group, 0)
        _strip_copies(tab_sm, stage.at[s], xs_hbm, sem.at[s], to_region=True)

    @pl.when(step >= 1)
    def _():
        for c in table_copies():
            c.wait()

        @pl.when(step >= 3)
        def _():
            _strip_wait(stage.at[1 - cur], xs_hbm, sem.at[1 - cur])
        for s in range(2):
            pl.when(1 - cur == s)(functools.partial(dispatch, s))

    for c in table_copies():
        c.start()

    @pl.when(step == nsteps - 1)
    def _():
        for c in table_copies():
            c.wait()

        @pl.when(step >= 2)
        def _():
            _strip_wait(stage.at[cur], xs_hbm, sem.at[cur])
        for s in range(2):
            pl.when(cur == s)(functools.partial(dispatch, s))
        _strip_wait(stage.at[1 - cur], xs_hbm, sem.at[1 - cur])
        _strip_wait(stage.at[cur], xs_hbm, sem.at[cur])


def _mixer_out(x, gla, y, mod3, gluw, glub, woa, wob, mpg, fpg, rw_hi, rw_lo, rb):
    nt = T // TT
    full = lambda shape: pl.BlockSpec(shape, lambda b, t: (0,) * len(shape))
    tok = lambda w: pl.BlockSpec((None, TT, w), lambda b, t: (b, t, 0))
    slab = pl.BlockSpec((TT * ROW, LANES), lambda b, t: (b * nt + t, 0))
    slab_shape = jax.ShapeDtypeStruct((N * ROW, LANES), f32)
    yspec = pl.BlockSpec((S5_M, None, TT // S5_L, S5_L * LANES), lambda b, t: (0, b, t, 0))
    modspec = lambda j: pl.BlockSpec((None, 1, D), lambda b, t: (b, 0, j))
    tix = jnp.arange(TT, dtype=i32)
    lanes = jnp.arange(LANES, dtype=i32)
    tri = (tix[:, None] >= tix[None, :]).astype(bf16)
    upper = (lanes[:, None] < lanes[None, :]).astype(bf16)
    grp = (tix[None, :] // PG == jnp.arange(TT // PG, dtype=i32)[:, None]).astype(bf16)
    return pl.pallas_call(
        _mixer_out_kernel,
        grid=(B, nt),
        in_specs=[tok(D), tok(512), yspec,
                  modspec(2), modspec(3), modspec(4),
                  full((512, 512)), full((1, 512)), full((512, D)), full((512, D)),
                  full((1, D)), full((1, D)), full((D, LANES)), full((D, LANES)), full((1, LANES)),
                  full((TT, TT)), full((LANES, LANES)), full((TT // PG, TT))],
        out_specs=[slab, pl.BlockSpec(memory_space=pl.ANY),
                   pl.BlockSpec((TT // PG, LANES), lambda b, t: (b * nt + t, 0)),
                   pl.BlockSpec((ROW, LANES), lambda b, t: (b * nt + t, 0)), tok(LANES), full((1, LANES))],
        out_shape=[slab_shape, jax.ShapeDtypeStruct((NE * N * ROW, LANES), f32),
                   jax.ShapeDtypeStruct((N // PG, LANES), i32), jax.ShapeDtypeStruct((N // TT * ROW, LANES), i32),
                   jax.ShapeDtypeStruct((B, T, LANES), f32),
                   jax.ShapeDtypeStruct((1, LANES), f32)],
        scratch_shapes=[pltpu.VMEM((1, LANES), f32), pltpu.VMEM((S5_M, TT // S5_L * FOLD_PITCH, LANES), f32),
                        pltpu.VMEM((2, TT * ROW, LANES), f32), pltpu.VMEM((2, STAGE_ROWS, LANES), f32),
                        pltpu.VMEM((2, TT // PG, LANES), i32), pltpu.VMEM((2, ROW, LANES), i32),
                        pltpu.SMEM((TT // PG, LANES), i32), pltpu.SMEM((ROW, LANES), i32),
                        pltpu.SemaphoreType.DMA((2,)), pltpu.SemaphoreType.DMA((2,))],
        compiler_params=pltpu.CompilerParams(
            dimension_semantics=("arbitrary", "arbitrary"), vmem_limit_bytes=VMEM_LIMIT),
        name="mixer_out",
    )(x, gla, y, mod3, mod3, mod3, gluw, glub, woa, wob, mpg, fpg, rw_hi, rw_lo, rb, tri, upper, grp)


def _experts_kernel(tb_ref, te_ref, tv_ref, tf_ref, tn_ref, ts_ref, nu_ref,
                    x_ref, wgu_hbm, bgu_ref, wd_hbm, bd_ref, y_ref, wgu_f, wd_f, wgu_b, wd_b, sem):
    i = pl.program_id(0)

    def fetch(e):
        return (pltpu.make_async_copy(wgu_hbm.at[e], wgu_f, sem.at[0]),
                pltpu.make_async_copy(wd_hbm.at[e], wd_f, sem.at[1]))

    def convert(s):
        wgu_b[s] = wgu_f[...].astype(bf16)
        wd_b[s] = wd_f[...].astype(bf16)

    def tile(s, h, nt=1, rows=None):
        rows = nt * MB if rows is None else rows
        base = h * MB * ROW
        xt = jnp.concatenate([x_ref[pl.ds(base + cc, rows, stride=ROW), :] for cc in range(ROW)], axis=1)
        valid = lax.broadcasted_iota(i32, (rows, 1), 0) < tv_ref[i] - h * MB
        xt = jnp.where(valid, xt, 0.0).astype(bf16)
        gu = _dot(xt, wgu_b[s]) + bgu_ref[...]
        g = jnp.minimum(gu[:, :DFF], LIMIT)
        up = jnp.clip(gu[:, DFF:], -LIMIT, LIMIT)
        act = (up + 1.0) * (g * jax.nn.sigmoid(ALPHA * g))
        yv = _dot(act.astype(bf16), wd_b[s]) + bd_ref[...]
        for cc in range(ROW):
            y_ref[pl.ds(base + cc, rows, stride=ROW), :] = yv[:, cc * LANES:(cc + 1) * LANES]

    def run_tiles(s):
        nv = tv_ref[i]
        rem = nv - (nv // MB) * MB
        half = jnp.logical_and(rem > 0, rem <= MB // 2)
        nfull = nv // MB + jnp.where(rem > MB // 2, 1, 0)
        pl.when(nfull == EB)(functools.partial(tile, s, 0, EB))
        few = nfull < EB
        for h in range(0, EB, 2):
            pl.when(jnp.logical_and(few, nfull >= h + 2))(functools.partial(tile, s, h, 2))
            pl.when(jnp.logical_and(few, nfull == h + 1))(functools.partial(tile, s, h, 1))
        for h in range(EB):
            pl.when(jnp.logical_and(half, nfull == h))(functools.partial(tile, s, h, 1, MB // 2))

    @pl.when(i < nu_ref[0])
    def _():
        e = te_ref[i]
        s = ts_ref[i]
        nxt = tn_ref[i]
        first = (tf_ref[i] & 1) != 0
        last = (tf_ref[i] & 2) != 0

        @pl.when(i == 0)
        def _():
            for c in fetch(e):
                c.start()
            for c in fetch(e):
                c.wait()
            convert(s)

        @pl.when(jnp.logical_and(first, nxt >= 0))
        def _():
            for c in fetch(nxt):
                c.start()

        handoff = jnp.logical_and(last, nxt >= 0)

        @pl.when(handoff)
        def _():
            for c in fetch(nxt):
                c.wait()
            convert(1 - s)

        run_tiles(s)


def _experts(tile_blk, tile_e, tile_nv, tile_flags, tile_next, tile_slot, nused, xs, w_gu, b_gu, w_down, b_down):
    blk = lambda i, tb, te, *_: (tb[i], 0)
    bsel = lambda i, tb, te, *_: (te[i], 0, 0)
    grid_spec = pltpu.PrefetchScalarGridSpec(
        num_scalar_prefetch=7,
        grid=(NB,),
        in_specs=[
            pl.BlockSpec((EB * MB * ROW, LANES), blk),
            pl.BlockSpec(memory_space=pl.ANY),
            pl.BlockSpec((None, 1, 2 * DFF), bsel),
            pl.BlockSpec(memory_space=pl.ANY),
            pl.BlockSpec((None, 1, D), bsel),
        ],
        out_specs=pl.BlockSpec((EB * MB * ROW, LANES), blk),
        scratch_shapes=[pltpu.VMEM((D, 2 * DFF), f32), pltpu.VMEM((DFF, D), f32),
                        pltpu.VMEM((2, D, 2 * DFF), bf16), pltpu.VMEM((2, DFF, D), bf16),
                        pltpu.SemaphoreType.DMA((2,))],
    )
    return pl.pallas_call(
        _experts_kernel,
        grid_spec=grid_spec,
        out_shape=jax.ShapeDtypeStruct((NE * N * ROW, LANES), f32),
        compiler_params=pltpu.CompilerParams(dimension_semantics=("arbitrary",), vmem_limit_bytes=VMEM_LIMIT),
        name="experts",
    )(tile_blk, tile_e, tile_nv, tile_flags, tile_next, tile_slot, nused, xs,
      w_gu, b_gu.reshape(NE, 1, 2 * DFF), w_down, b_down.reshape(NE, 1, D))


def _combine_kernel(pos_ref, tab_ref, tabn_ref, gate_ref, h1_ref, g2_ref, pg_ref, ys_hbm, o_ref,
                    stage, wbuf, rbuf, sem):
    j = pl.program_id(0)
    slot = lax.rem(j, 2)
    GT = PG
    GR = GT * ROW

    @pl.when(j == 0)
    def _():
        _strip_copies(tab_ref, stage.at[0], ys_hbm, sem.at[0], to_region=False)

    @pl.when(j + 1 < pl.num_programs(0))
    def _():
        _strip_copies(tabn_ref, stage.at[1 - slot], ys_hbm, sem.at[1 - slot], to_region=False)

    for kk in range(TOPK):
        wbuf[kk] = jnp.broadcast_to(gate_ref[:, kk:kk + 1], (TT, LANES))
    _strip_wait(stage.at[slot], ys_hbm, sem.at[slot])
    g2 = g2_ref[...]
    pg = pg_ref[...]

    def one_group(s, g):
        rows = pl.ds(pl.multiple_of(g * GR, GR), GR)
        parts = []
        for jj in range(GT):
            t = g * GT + jj
            acc = None
            for kk in range(TOPK):
                w = jnp.broadcast_to(wbuf[kk, pl.ds(t, 1), :], (ROW, LANES))
                term = stage[s, pl.ds(pl.multiple_of(pos_ref[g, jj * TOPK + kk], ROW), ROW), :] * w
                acc = term if acc is None else acc + term
            parts.append(acc)
        ff3 = jnp.concatenate(parts, axis=0).reshape(GT, ROW, LANES)
        ss = jnp.sum(jnp.sum(ff3 * ff3, axis=2, keepdims=True), axis=1, keepdims=True)
        rs = lax.rsqrt(ss * (1.0 / D) + EPS)
        out = h1_ref[rows, :].reshape(GT, ROW, LANES) + g2 * (ff3 * rs * pg)
        rbuf[rows, :] = out.reshape(GR, LANES)

    def gather(s):
        def trip(gg, carry):
            for k in range(COMBINE_GROUPS_PER_TRIP):
                one_group(s, COMBINE_GROUPS_PER_TRIP * gg + k)
            return carry
        lax.fori_loop(0, TT // GT // COMBINE_GROUPS_PER_TRIP, trip, 0)

    for s in range(2):
        pl.when(slot == s)(functools.partial(gather, s))
    for cc in range(ROW):
        o_ref[:, cc * LANES:(cc + 1) * LANES] = rbuf[pl.ds(cc, TT, stride=ROW), :]


def _combine(pos, tab, gates, h1_rows, mod4, pg, ys):
    nt = N // TT
    per_b = T // TT
    tabspec = lambda imap: pl.BlockSpec((ROW, LANES), imap, memory_space=pltpu.SMEM)
    return pl.pallas_call(
        _combine_kernel,
        grid=(nt,),
        in_specs=[pl.BlockSpec((TT // PG, LANES), lambda j: (j, 0), memory_space=pltpu.SMEM),
                  tabspec(lambda j: (j, 0)), tabspec(lambda j: (jnp.minimum(j + 1, nt - 1), 0)),
                  pl.BlockSpec((TT, LANES), lambda j: (j, 0)),
                  pl.BlockSpec((TT * ROW, LANES), lambda j: (j, 0)),
                  pl.BlockSpec((None, None, ROW, LANES), lambda j: (j // per_b, 5, 0, 0)),
                  pl.BlockSpec((ROW, LANES), lambda j: (0, 0)),
                  pl.BlockSpec(memory_space=pl.ANY)],
        out_specs=pl.BlockSpec((TT, D), lambda j: (j, 0)),
        out_shape=jax.ShapeDtypeStruct((N, D), f32),
        scratch_shapes=[pltpu.VMEM((2, STAGE_ROWS, LANES), f32), pltpu.VMEM((TOPK, TT, LANES), f32),
                        pltpu.VMEM((TT * ROW, LANES), f32), pltpu.SemaphoreType.DMA((2,))],
        compiler_params=pltpu.CompilerParams(dimension_semantics=("arbitrary",), vmem_limit_bytes=VMEM_LIMIT),
        name="combine",
    )(pos, tab, tab, gates, h1_rows, mod4, pg, ys)


def _pad_heads(w):
    lead = w.shape[:-1]
    w4 = w.reshape(lead + (GLA_H, GLA_DK))
    w4 = jnp.concatenate([w4, jnp.zeros_like(w4)], axis=-1)
    return w4.reshape(lead + (GLA_H * LANES,))


def kernel(x, c, ada_w, ada_b, mix_pre_g, mix_post_g, ffn_pre_g, ffn_post_g, w_in, w_alpha, b_alpha, gla_norm_g, s5_lambda_re, s5_lambda_im, s5_log_dt, s5_b_re, s5_b_im, s5_c_re, s5_c_im, s5_d, s5_glu_w, s5_glu_b, w_out, router_w, router_b, exp_w_gu, exp_b_gu, exp_w_down, exp_b_down):
    l = 0
    mod = _ada(c, ada_w[l], ada_b[l:l + 1])
    mod3 = mod.reshape(B, 1, 6 * D)
    mod4 = mod.reshape(B, 6, ROW, LANES)

    w = w_in[l]
    o_q, o_k, o_v, o_g, o_a, o_u = 0, 256, 512, 1024, 1536, 1552
    wq = _pad_heads(w[:, o_q:o_k]).astype(bf16)
    wk = _pad_heads(w[:, o_k:o_v]).astype(bf16)
    wv = w[:, o_v:o_g].astype(bf16)
    wg = w[:, o_g:o_a].astype(bf16)
    wa = jnp.pad(w[:, o_a:o_u], ((0, 0), (0, LANES - LOWRANK))).astype(bf16)
    wu = w[:, o_u:].astype(bf16)
    walpha = jnp.pad(_pad_heads(w_alpha[l]), ((0, LANES - LOWRANK), (0, 0))).astype(bf16)
    balpha = _pad_heads(b_alpha[l:l + 1])

    gla, u = _mixer_in(x, mod3, mix_pre_g[l:l + 1], wq, wk, wv, wg, wa, wu,
                       walpha, balpha, gla_norm_g[l:l + 1])

    wt, mi, mo, a16 = _s5_prep(s5_lambda_re[l], s5_lambda_im[l], s5_log_dt[l], s5_b_re[l], s5_b_im[l],
                               s5_c_re[l], s5_c_im[l], s5_d[l])
    y = _s5_scan(u, wt, mi, mo, a16)

    rw = jnp.pad(router_w[l], ((0, 0), (0, LANES - NE)))
    rw_hi = rw.astype(bf16)
    rw_lo = (rw - rw_hi.astype(f32)).astype(bf16)
    rb = jnp.pad(router_b[l:l + 1], ((0, 0), (0, LANES - NE)), constant_values=-1e30)
    wo = w_out[l].astype(bf16)
    h1_rows, xs, pos, tab, gates, cnt = _mixer_out(
        x, gla, y, mod3, s5_glu_w[l].astype(bf16), s5_glu_b[l:l + 1], wo[:512], wo[512:],
        mix_post_g[l:l + 1], ffn_pre_g[l:l + 1], rw_hi, rw_lo, rb)

    counts = cnt[0, :NE].astype(i32)
    step_rows = EB * MB
    ntile = (counts + step_rows - 1) // step_rows
    tends = jnp.cumsum(ntile)
    nused = tends[-1]
    gi = jnp.minimum(jnp.arange(NB, dtype=i32), nused - 1)
    tile_e = jnp.sum((tends[None, :] <= gi[:, None]).astype(i32), axis=1)
    sel = tile_e[:, None] == jnp.arange(NE, dtype=i32)[None, :]
    pick = lambda v: jnp.sum(jnp.where(sel, v[None, :], 0), axis=1)
    tile_j = gi - pick(tends - ntile)
    tile_blk = tile_e * TPE + tile_j
    tile_nv = jnp.clip(pick(counts) - tile_j * step_rows, 0, step_rows)
    tile_flags = (tile_j == 0).astype(i32) + 2 * (tile_j == pick(ntile) - 1).astype(i32)
    used = ntile > 0
    eids = jnp.arange(NE, dtype=i32)
    later = jnp.logical_and(used[None, :], eids[None, :] > eids[:, None])
    nxt = jnp.min(jnp.where(later, eids[None, :], NE), axis=1)
    tile_next = pick(jnp.where(nxt < NE, nxt, -1))
    tile_slot = pick((jnp.cumsum(used.astype(i32)) - 1) % 2)

    ys = _experts(tile_blk, tile_e, tile_nv, tile_flags, tile_next, tile_slot, nused.reshape(1), xs,
                  exp_w_gu[l], exp_b_gu[l], exp_w_down[l], exp_b_down[l])
    out = _combine(pos, tab, gates.reshape(N, LANES), h1_rows, mod4, ffn_post_g[l].reshape(ROW, LANES), ys)
    return out.reshape(B, T, D)
```

```python
import functools

import jax
import jax.numpy as jnp
from jax import lax
from jax.experimental import pallas as pl
from jax.experimental.pallas import tpu as pltpu

f32 = jnp.float32
bf16 = jnp.bfloat16
i32 = jnp.int32

D = 1024
B = 8
T = 2048
N = B * T
GLA_H = 4
GLA_DV = 128
GLA_DK = 64
GLA_TAU = 16.0
GLA_CHUNK = 64
LOWRANK = 16
S5_W = 512
S5_CH = 16
S5_G = 32
S5_P = 64
NE = 32
TOPK = 4
DFF = 1024
ALPHA = 1.702
LIMIT = 7.0
EPS = 1e-6

LANES = 128
SUBLANES = 8
VMEM_LIMIT = 56 * 1024 * 1024

TT = 512
S5_L = 16
S5_M = S5_W // LANES
S5_GPT = LANES // S5_CH
S5_ST = S5_GPT * S5_P
S5_NCH = 64
S5_ROWS = B * S5_NCH
S5_PITCH = S5_NCH + SUBLANES
FOLD_PITCH = S5_L + SUBLANES
ROW = SUBLANES
MB = 256
EB = 4
NB = (N * TOPK) // (EB * MB) + NE
TPE = N // (EB * MB)
PG = 16
COMBINE_GROUPS_PER_TRIP = 16


def _dot(a, b):
    return jnp.dot(a, b, preferred_element_type=f32)


def _dot_t(a, b, ca, cb):
    return lax.dot_general(a, b, (((ca,), (cb,)), ((), ())), preferred_element_type=f32)


def _rms(x):
    return lax.rsqrt(jnp.mean(x * x, axis=-1, keepdims=True) + EPS)


def _ada_kernel(c_ref, w_ref, b_ref, o_ref):
    c = c_ref[...]
    s = (c * jax.nn.sigmoid(c)).astype(bf16)
    o_ref[...] = _dot(s, w_ref[...].astype(bf16)) + b_ref[...]


def _ada(c, w, b):
    return pl.pallas_call(
        _ada_kernel,
        grid=(6,),
        in_specs=[
            pl.BlockSpec((B, D), lambda j: (0, 0)),
            pl.BlockSpec((D, D), lambda j: (0, j)),
            pl.BlockSpec((1, D), lambda j: (0, j)),
        ],
        out_specs=pl.BlockSpec((B, D), lambda j: (0, j)),
        out_shape=jax.ShapeDtypeStruct((B, 6 * D), f32),
        name="ada",
    )(c, w, b)


def _mixer_in_kernel(x_ref, sc_ref, sh_ref, g_ref, wq, wk, wv, wg, wa, wu, walpha, balpha, gng, cm_ref,
                     gla_ref, u_ref, st_s, u_s):
    t = pl.program_id(1)

    @pl.when(t == 0)
    def _():
        st_s[...] = jnp.zeros_like(st_s)

    x = x_ref[...]
    hn = (x * _rms(x) * g_ref[...]) * (1.0 + sc_ref[...]) + sh_ref[...]
    hb = hn.astype(bf16)
    u = _dot(hb, wu[...])

    C = GLA_CHUNK
    NC = TT // C
    causal = cm_ref[...] > 0.5
    tri = cm_ref[...].astype(bf16)
    a_lr = _dot(hb, wa[...]).astype(bf16)
    la = jax.nn.log_sigmoid(_dot(a_lr, walpha[...]) + balpha[...]) * (1.0 / GLA_TAU)
    la_hi = la.astype(bf16)
    la_lo = (la - la_hi.astype(f32)).astype(bf16)
    bc = _dot(tri, la_hi) + _dot(tri, la_lo)
    bl = jnp.broadcast_to(bc.reshape(NC, C, GLA_H * LANES)[:, C - 1:C, :],
                          (NC, C, GLA_H * LANES)).reshape(TT, GLA_H * LANES)
    q = _dot(hb, wq[...]) * (GLA_DK ** -0.5)
    k = _dot(hb, wk[...])
    qd = (q * jnp.exp(bc)).astype(bf16)
    ki = (k * jnp.exp(-bc)).astype(bf16)
    ke = k * jnp.exp(bl - bc)
    dec = jnp.exp(bl)
    vb = _dot(hb, wv[...]).astype(bf16)
    og = _dot(hb, wg[...])
    gn = gng[...]
    heads = range(GLA_H)
    sls = [slice(h * LANES, (h + 1) * LANES) for h in heads]
    kehb = ke.astype(bf16)
    raw = [_dot_t(qd[:, sl], ki[:, sl], 1, 1) for sl in sls]
    deltas = [[_dot_t(vb[c * C:(c + 1) * C, sl], kehb[c * C:(c + 1) * C, sl], 0, 0) for c in range(NC)] for sl in sls]
    scs = [jnp.where(causal, r, 0.0).astype(bf16) for r in raw]
    o_intra = [_dot(scs[h], vb[:, sls[h]]) for h in heads]
    sts = []
    for h in heads:
        st = st_s[h]
        per = []
        for c in range(NC):
            per.append(st.astype(bf16))
            st = st * dec[c * C:c * C + 1, sls[h]] + deltas[h][c]
        st_s[h] = st
        sts.append(per)
    for h in heads:
        sl = sls[h]
        o_inter = jnp.concatenate([_dot_t(qd[c * C:(c + 1) * C, sl], sts[h][c], 1, 1) for c in range(NC)], axis=0)
        o = o_intra[h] + o_inter
        on = o * _rms(o) * gn
        ogh = og[:, sl]
        gla_ref[:, sl] = (on * (ogh * jax.nn.sigmoid(ogh))).astype(bf16)
        for n in range(TT // S5_L):
            u_s[h, n * FOLD_PITCH:n * FOLD_PITCH + S5_L, :] = u[n * S5_L:(n + 1) * S5_L, sl]
        for i in range(S5_L):
            u_ref[h, :, i * LANES:(i + 1) * LANES] = u_s[h, pl.ds(i, TT // S5_L, stride=FOLD_PITCH), :].astype(bf16)


def _mixer_in(x, mod3, g, wq, wk, wv, wg, wa, wu, walpha, balpha, gng):
    assert S5_M == GLA_H
    nt = T // TT
    full = lambda shape: pl.BlockSpec(shape, lambda b, t: (0,) * len(shape))
    u_shape = jax.ShapeDtypeStruct((S5_M, B, T // S5_L, S5_L * LANES), bf16)
    u_spec = pl.BlockSpec((S5_M, None, TT // S5_L, S5_L * LANES), lambda b, t: (0, b, t, 0))
    tok = jnp.arange(TT, dtype=i32)
    same = (tok[:, None] // GLA_CHUNK) == (tok[None, :] // GLA_CHUNK)
    cmask = jnp.logical_and(tok[:, None] >= tok[None, :], same).astype(f32)
    return pl.pallas_call(
        _mixer_in_kernel,
        grid=(B, nt),
        in_specs=[
            pl.BlockSpec((None, TT, D), lambda b, t: (b, t, 0)),
            pl.BlockSpec((None, 1, D), lambda b, t: (b, 0, 1)),
            pl.BlockSpec((None, 1, D), lambda b, t: (b, 0, 0)),
            full((1, D)),
            full((D, 512)), full((D, 512)), full((D, 512)), full((D, 512)),
            full((D, LANES)), full((D, 512)),
            full((LANES, 512)), full((1, 512)), full((1, LANES)),
            full((TT, TT)),
        ],
        out_specs=[pl.BlockSpec((None, TT, 512), lambda b, t: (b, t, 0)), u_spec],
        out_shape=[jax.ShapeDtypeStruct((B, T, 512), bf16), u_shape],
        scratch_shapes=[pltpu.VMEM((GLA_H, GLA_DV, LANES), f32),
                        pltpu.VMEM((S5_M, TT // S5_L * FOLD_PITCH, LANES), f32)],
        compiler_params=pltpu.CompilerParams(
            dimension_semantics=("arbitrary", "arbitrary"), vmem_limit_bytes=VMEM_LIMIT),
        name="mixer_in",
    )(x, mod3, mod3, g, wq, wk, wv, wg, wa, wu, walpha, balpha, gng, cmask)


def _cpow(xr, xi, d):
    mag = jnp.exp(xr * d)
    return mag * jnp.cos(xi * d), mag * jnp.sin(xi * d)


def _s5_prep_kernel(lr_r, li_r, ld_r, lr_c, li_c, ld_c, btr, bti, ctr, cti, d_r,
                    wt_ref, mi_ref, mo_ref, a_ref, kt_s):
    L = S5_L
    lr = lr_r[...]
    li = li_r[...]
    dt = jnp.exp(ld_r[...])
    xr, xi = lr * dt, li * dt
    ar, ai = _cpow(xr, xi, 1.0)
    den = lr * lr + li * li
    fr = ((ar - 1.0) * lr + ai * li) / den
    fi = (ai * lr - (ar - 1.0) * li) / den
    br, bi = btr[...], bti[...]
    bbr = fr * br - fi * bi
    bbi = fr * bi + fi * br
    cr, ci = ctr[...], cti[...]
    cr_b, ci_b = cr.astype(bf16), ci.astype(bf16)
    pr, pi = jnp.ones_like(ar), jnp.zeros_like(ai)
    for d in range(L):
        xdr = bbr * pr - bbi * pi
        xdi = bbr * pi + bbi * pr
        i = L - 1 - d
        mo_ref[i * LANES:(i + 1) * LANES, :S5_ST] = xdr.astype(bf16)
        mo_ref[i * LANES:(i + 1) * LANES, S5_ST:] = xdi.astype(bf16)
        kt = _dot(xdr.astype(bf16), cr_b) - _dot(xdi.astype(bf16), ci_b)
        if d == 0:
            r = lax.broadcasted_iota(i32, (LANES, LANES), 0)
            c = lax.broadcasted_iota(i32, (LANES, LANES), 1)
            kt = kt + jnp.where(r == c, d_r[...], 0.0)
        kt_s[d] = kt.astype(bf16)
        pr, pi = pr * ar - pi * ai, pr * ai + pi * ar
    a_ref[:, :S5_ST] = pr
    a_ref[:, S5_ST:] = pi
    zero = jnp.zeros((LANES, LANES), bf16)
    for i in range(L):
        for j in range(L):
            wt_ref[i * LANES:(i + 1) * LANES, j * LANES:(j + 1) * LANES] = kt_s[j - i] if j >= i else zero
    lrc = lr_c[...]
    lic = li_c[...]
    dtc = jnp.exp(ld_c[...])
    acr, aci = _cpow(lrc * dtc, lic * dtc, 1.0)
    pr, pi = acr, aci
    for j in range(L):
        mi_ref[:S5_ST, j * LANES:(j + 1) * LANES] = (cr * pr - ci * pi).astype(bf16)
        mi_ref[S5_ST:, j * LANES:(j + 1) * LANES] = (-(cr * pi + ci * pr)).astype(bf16)
        pr, pi = pr * acr - pi * aci, pr * aci + pi * acr


def _s5_prep(lr, li, ld, b_re, b_im, c_re, c_im, dvec):
    G, P, H, M, GPT = S5_G, S5_P, S5_CH, S5_M, S5_GPT
    eye = jnp.eye(GPT, dtype=f32)

    def rows(v):
        return v.reshape(M, 1, S5_ST)

    def cols(v):
        return v.reshape(M, S5_ST, 1)

    ldp = jnp.broadcast_to(ld[:, None], (G, P))

    def bt(b):
        b4 = b.reshape(M, GPT, P, H)
        return jnp.einsum('mgph,gk->mkhgp', b4, eye).reshape(M, LANES, S5_ST)

    def ct(c):
        c4 = c.reshape(M, GPT, H, P)
        return jnp.einsum('mghp,gk->mgpkh', c4, eye).reshape(M, S5_ST, LANES)

    L = S5_L
    mspec = lambda shape: pl.BlockSpec((None,) + shape, lambda m: (m,) + (0,) * len(shape))
    return pl.pallas_call(
        _s5_prep_kernel,
        grid=(M,),
        in_specs=[mspec((1, S5_ST))] * 3 + [mspec((S5_ST, 1))] * 3
        + [mspec((LANES, S5_ST))] * 2 + [mspec((S5_ST, LANES))] * 2 + [mspec((1, LANES))],
        out_specs=[mspec((L * LANES, L * LANES)), mspec((2 * S5_ST, L * LANES)),
                   mspec((L * LANES, 2 * S5_ST)), mspec((1, 2 * S5_ST))],
        out_shape=[jax.ShapeDtypeStruct((M, L * LANES, L * LANES), bf16),
                   jax.ShapeDtypeStruct((M, 2 * S5_ST, L * LANES), bf16),
                   jax.ShapeDtypeStruct((M, L * LANES, 2 * S5_ST), bf16),
                   jax.ShapeDtypeStruct((M, 1, 2 * S5_ST), f32)],
        scratch_shapes=[pltpu.VMEM((L, LANES, LANES), bf16)],
        compiler_params=pltpu.CompilerParams(dimension_semantics=("arbitrary",), vmem_limit_bytes=VMEM_LIMIT),
        name="s5_prep",
    )(rows(lr), rows(li), rows(ldp), cols(lr), cols(li), cols(ldp),
      bt(b_re), bt(b_im), ct(c_re), ct(c_im), dvec.reshape(M, 1, LANES))


def _s5_scan_kernel(u_ref, wt, mi, mo, a_ref, y_ref, xs, s_s, x_s):
    r = pl.program_id(1)

    @pl.when(r == 0)
    def _():
        xs[...] = jnp.zeros_like(xs)

    u = u_ref[...].reshape(S5_ROWS, S5_L * LANES)
    s = _dot(u, mo[...])
    y_intra = _dot(u, wt[...])
    NT = 2 * S5_ST // LANES
    for c in range(NT):
        for b in range(B):
            s_s[c, b * S5_PITCH:b * S5_PITCH + S5_NCH, :] = s[b * S5_NCH:(b + 1) * S5_NCH, c * LANES:(c + 1) * LANES]
    a = [a_ref[:, c * LANES:(c + 1) * LANES] for c in range(NT)]

    def step(n, x):
        rows = pl.ds(n, B, stride=S5_PITCH)
        new = []
        for c in range(NT):
            x_s[c, rows, :] = x[c]
        for c in range(NT // 2):
            ar, ai, xr, xi = a[c], a[NT // 2 + c], x[c], x[NT // 2 + c]
            new.append((ar * xr - ai * xi + s_s[c, rows, :], ar * xi + ai * xr + s_s[NT // 2 + c, rows, :]))
        return tuple(p[0] for p in new) + tuple(p[1] for p in new)

    x = tuple(xs[:, c * LANES:(c + 1) * LANES] for c in range(NT))
    for n in range(S5_NCH):
        x = step(n, x)
    for c in range(NT):
        xs[:, c * LANES:(c + 1) * LANES] = x[c]
    x_in = jnp.concatenate(
        [jnp.concatenate([x_s[c, b * S5_PITCH:b * S5_PITCH + S5_NCH, :] for b in range(B)], axis=0) for c in range(NT)],
        axis=1)
    y = y_intra + _dot(x_in.astype(bf16), mi[...])
    y_ref[...] = jax.nn.gelu(y).astype(bf16).reshape(B, S5_NCH, S5_L * LANES)


def _s5_scan(u, wt, mi, mo, a16):
    L = S5_L
    wspec = lambda shape: pl.BlockSpec((None,) + shape, lambda m, r: (m,) + (0,) * len(shape))
    uspec = pl.BlockSpec((None, B, S5_NCH, L * LANES), lambda m, r: (m, 0, r, 0))
    return pl.pallas_call(
        _s5_scan_kernel,
        grid=(S5_M, T // L // S5_NCH),
        in_specs=[uspec,
                  wspec((L * LANES, L * LANES)), wspec((2 * S5_ST, L * LANES)),
                  wspec((L * LANES, 2 * S5_ST)), wspec((1, 2 * S5_ST))],
        out_specs=uspec,
        out_shape=jax.ShapeDtypeStruct((S5_M, B, T // L, L * LANES), bf16),
        scratch_shapes=[pltpu.VMEM((B, 2 * S5_ST), f32), pltpu.VMEM((2 * S5_ST // LANES, B * S5_PITCH, LANES), f32),
                        pltpu.VMEM((2 * S5_ST // LANES, B * S5_PITCH, LANES), f32)],
        compiler_params=pltpu.CompilerParams(
            dimension_semantics=("arbitrary", "arbitrary"), vmem_limit_bytes=VMEM_LIMIT),
        name="s5_scan",
    )(u, wt, mi, mo, a16)


STAGE_ROWS = TT * TOPK * ROW


def _strip_copies(tab_ref, stage, region_hbm, sem, to_region):
    def expert(e, carry):
        n = tab_ref[0, e] * ROW

        @pl.when(n > 0)
        def _():
            local = stage.at[pl.ds(pl.multiple_of(tab_ref[1, e] * ROW, ROW), n), :]
            remote = region_hbm.at[pl.ds(pl.multiple_of((e * N + tab_ref[2, e]) * ROW, ROW), n), :]
            src, dst = (local, remote) if to_region else (remote, local)
            pltpu.make_async_copy(src, dst, sem).start()
        return carry
    lax.fori_loop(0, NE, expert, 0)


def _strip_wait(stage, region_hbm, sem):
    pltpu.make_async_copy(stage, region_hbm.at[pl.ds(0, STAGE_ROWS), :], sem).wait()


def _mixer_out_kernel(x_ref, gla_ref, y_ref, g1_ref, sh2_ref, sc2_ref, gluw, glub, woa, wob,
                      mpg, fpg, rw_hi, rw_lo, rb, tri_ref, upper_ref, grp_ref,
                      h1_ref, xs_hbm, pos_ref, tab_ref, gate_ref, cnt_ref,
                      cnt_s, y_s, hn2_s, stage, pos_v, tab_v, pos_sm, tab_sm, sem, tsem):
    step = pl.program_id(0) * pl.num_programs(1) + pl.program_id(1)
    first = step == 0

    @pl.when(first)
    def _():
        cnt_s[...] = jnp.zeros_like(cnt_s)

    for m in range(S5_M):
        for jj in range(S5_L):
            y_s[m, pl.ds(jj, TT // S5_L, stride=FOLD_PITCH), :] = y_ref[m, :, jj * LANES:(jj + 1) * LANES].astype(f32)
    y = jnp.concatenate(
        [jnp.concatenate([y_s[m, n * FOLD_PITCH:n * FOLD_PITCH + S5_L, :] for n in range(TT // S5_L)], axis=0)
         for m in range(S5_M)], axis=1)
    z = _dot(y.astype(bf16), gluw[...]) + glub[...]
    s5 = y * jax.nn.sigmoid(z)
    mix = _dot(gla_ref[...], woa[...]) + _dot(s5.astype(bf16), wob[...])
    h1 = x_ref[...] + g1_ref[...] * (mix * _rms(mix) * mpg[...])
    hn2 = (h1 * _rms(h1) * fpg[...]) * (1.0 + sc2_ref[...]) + sh2_ref[...]
    for cc in range(ROW):
        h1_ref[pl.ds(cc, TT, stride=ROW), :] = h1[:, cc * LANES:(cc + 1) * LANES]
        hn2_s[lax.rem(step, 2), pl.ds(cc, TT, stride=ROW), :] = hn2[:, cc * LANES:(cc + 1) * LANES]
    x_hi = hn2.astype(bf16)
    x_lo = (hn2 - x_hi.astype(f32)).astype(bf16)
    logits = _dot(x_hi, rw_hi[...]) + (_dot(x_lo, rw_hi[...]) + _dot(x_hi, rw_lo[...])) + rb[...]

    lane = lax.broadcasted_iota(i32, (TT, LANES), 1).astype(f32)
    l = logits
    vals, idxs = [], []
    for _ in range(TOPK):
        m = jnp.max(l, axis=-1, keepdims=True)
        ix = jnp.min(jnp.where(l == m, lane, float(LANES)), axis=-1, keepdims=True)
        vals.append(m)
        idxs.append(ix)
        l = jnp.where(lane == ix, -jnp.inf, l)
    es = [jnp.exp(v - vals[0]) for v in vals]
    tot = es[0] + es[1] + es[2] + es[3]
    oh = jnp.zeros((TT, LANES), f32)
    for ix in idxs:
        oh = oh + (lane == ix).astype(f32)
    cum = _dot(tri_ref[...], oh.astype(bf16))
    lcnt = cum[TT - 1:TT, :]
    gbase = cnt_s[...]
    hi = jnp.floor(lcnt * (1.0 / 64.0))
    lo = lcnt - hi * 64.0
    upper = upper_ref[...]
    hi8 = jnp.broadcast_to(hi, (ROW, LANES)).astype(bf16)
    lo8 = jnp.broadcast_to(lo, (ROW, LANES)).astype(bf16)
    lbase = (64.0 * _dot(hi8, upper) + _dot(lo8, upper))[0:1, :]
    loc = lbase + (cum - oh)
    tmod = (lax.broadcasted_iota(i32, (TT, LANES), 0) % PG).astype(f32) * float(TOPK)
    spread = jnp.zeros((TT, LANES), f32)
    gate = jnp.zeros((TT, LANES), f32)
    for kk in range(TOPK):
        p = jnp.sum(jnp.where(lane == idxs[kk], loc, 0.0), axis=-1, keepdims=True)
        spread = jnp.where(lane == tmod + float(kk), p, spread)
        gate = jnp.where(lane == float(kk), es[kk] / tot, gate)
    s_hi = jnp.floor(spread * (1.0 / 64.0))
    s_lo = spread - s_hi * 64.0
    grp = grp_ref[...]
    folded = 64.0 * _dot(grp, s_hi.astype(bf16)) + _dot(grp, s_lo.astype(bf16))
    posi = (folded * float(ROW)).astype(i32)
    pos_ref[...] = posi
    pos_v[lax.rem(step, 2)] = posi
    sub = lax.broadcasted_iota(i32, (ROW, LANES), 0)
    tab = jnp.where(sub == 0, lcnt, jnp.where(sub == 1, lbase, jnp.where(sub == 2, gbase, 0.0)))
    tabi = tab.astype(i32)
    tab_ref[...] = tabi
    tab_v[lax.rem(step, 2)] = tabi
    gate_ref[...] = gate
    cnt = cnt_s[...] + cum[TT - 1:TT, :]
    cnt_s[...] = cnt
    cnt_ref[...] = cnt

    nsteps = pl.num_programs(0) * pl.num_programs(1)
    cur = lax.rem(step, 2)

    def table_copies():
        return (pltpu.make_async_copy(pos_v.at[cur], pos_sm, tsem.at[0]),
                pltpu.make_async_copy(tab_v.at[cur], tab_sm, tsem.at[1]))

    def dispatch(s):
        def group(g, carry):
            for jj in range(PG):
                v = hn2_s[s, pl.ds(pl.multiple_of((g * PG + jj) * ROW, ROW), ROW), :]
                for kk in range(TOPK):
                    stage[s, pl.ds(pl.multiple_of(pos_sm[g, jj * TOPK + kk], ROW), ROW), :] = v
            return carry
        lax.fori_loop(0, TT // PG, group, 0)
        _strip_copies(tab_sm, stage.at[s], xs_hbm, sem.at[s], to_region=True)

    @pl.when(step >= 1)
    def _():
        for c in table_copies():
            c.wait()

        @pl.when(step >= 3)
        def _():
            _strip_wait(stage.at[1 - cur], xs_hbm, sem.at[1 - cur])
        for s in range(2):
            pl.when(1 - cur == s)(functools.partial(dispatch, s))

    for c in table_copies():
        c.start()

    @pl.when(step == nsteps - 1)
    def _():
        for c in table_copies():
            c.wait()

        @pl.when(step >= 2)
        def _():
            _strip_wait(stage.at[cur], xs_hbm, sem.at[cur])
        for s in range(2):
            pl.when(cur == s)(functools.partial(dispatch, s))
        _strip_wait(stage.at[1 - cur], xs_hbm, sem.at[1 - cur])
        _strip_wait(stage.at[cur], xs_hbm, sem.at[cur])


def _mixer_out(x, gla, y, mod3, gluw, glub, woa, wob, mpg, fpg, rw_hi, rw_lo, rb):
    nt = T // TT
    full = lambda shape: pl.BlockSpec(shape, lambda b, t: (0,) * len(shape))
    tok = lambda w: pl.BlockSpec((None, TT, w), lambda b, t: (b, t, 0))
    slab = pl.BlockSpec((TT * ROW, LANES), lambda b, t: (b * nt + t, 0))
    slab_shape = jax.ShapeDtypeStruct((N * ROW, LANES), f32)
    yspec = pl.BlockSpec((S5_M, None, TT // S5_L, S5_L * LANES), lambda b, t: (0, b, t, 0))
    modspec = lambda j: pl.BlockSpec((None, 1, D), lambda b, t: (b, 0, j))
    tix = jnp.arange(TT, dtype=i32)
    lanes = jnp.arange(LANES, dtype=i32)
    tri = (tix[:, None] >= tix[None, :]).astype(bf16)
    upper = (lanes[:, None] < lanes[None, :]).astype(bf16)
    grp = (tix[None, :] // PG == jnp.arange(TT // PG, dtype=i32)[:, None]).astype(bf16)
    return pl.pallas_call(
        _mixer_out_kernel,
        grid=(B, nt),
        in_specs=[tok(D), tok(512), yspec,
                  modspec(2), modspec(3), modspec(4),
                  full((512, 512)), full((1, 512)), full((512, D)), full((512, D)),
                  full((1, D)), full((1, D)), full((D, LANES)), full((D, LANES)), full((1, LANES)),
                  full((TT, TT)), full((LANES, LANES)), full((TT // PG, TT))],
        out_specs=[slab, pl.BlockSpec(memory_space=pl.ANY),
                   pl.BlockSpec((TT // PG, LANES), lambda b, t: (b * nt + t, 0)),
                   pl.BlockSpec((ROW, LANES), lambda b, t: (b * nt + t, 0)), tok(LANES), full((1, LANES))],
        out_shape=[slab_shape, jax.ShapeDtypeStruct((NE * N * ROW, LANES), f32),
                   jax.ShapeDtypeStruct((N // PG, LANES), i32), jax.ShapeDtypeStruct((N // TT * ROW, LANES), i32),
                   jax.ShapeDtypeStruct((B, T, LANES), f32),
                   jax.ShapeDtypeStruct((1, LANES), f32)],
        scratch_shapes=[pltpu.VMEM((1, LANES), f32), pltpu.VMEM((S5_M, TT // S5_L * FOLD_PITCH, LANES), f32),
                        pltpu.VMEM((2, TT * ROW, LANES), f32), pltpu.VMEM((2, STAGE_ROWS, LANES), f32),
                        pltpu.VMEM((2, TT // PG, LANES), i32), pltpu.VMEM((2, ROW, LANES), i32),
                        pltpu.SMEM((TT // PG, LANES), i32), pltpu.SMEM((ROW, LANES), i32),
                        pltpu.SemaphoreType.DMA((2,)), pltpu.SemaphoreType.DMA((2,))],
        compiler_params=pltpu.CompilerParams(
            dimension_semantics=("arbitrary", "arbitrary"), vmem_limit_bytes=VMEM_LIMIT),
        name="mixer_out",
    )(x, gla, y, mod3, mod3, mod3, gluw, glub, woa, wob, mpg, fpg, rw_hi, rw_lo, rb, tri, upper, grp)


def _experts_kernel(tb_ref, te_ref, tv_ref, tf_ref, tn_ref, ts_ref, nu_ref,
                    x_ref, wgu_hbm, bgu_ref, wd_hbm, bd_ref, y_ref, wgu_f, wd_f, wgu_b, wd_b, sem):
    i = pl.program_id(0)

    def fetch(e):
        return (pltpu.make_async_copy(wgu_hbm.at[e], wgu_f, sem.at[0]),
                pltpu.make_async_copy(wd_hbm.at[e], wd_f, sem.at[1]))

    def convert(s):
        wgu_b[s] = wgu_f[...].astype(bf16)
        wd_b[s] = wd_f[...].astype(bf16)

    def tile(s, h, nt=1, rows=None):
        rows = nt * MB if rows is None else rows
        base = h * MB * ROW
        xt = jnp.concatenate([x_ref[pl.ds(base + cc, rows, stride=ROW), :] for cc in range(ROW)], axis=1)
        valid = lax.broadcasted_iota(i32, (rows, 1), 0) < tv_ref[i] - h * MB
        xt = jnp.where(valid, xt, 0.0).astype(bf16)
        gu = _dot(xt, wgu_b[s]) + bgu_ref[...]
        g = jnp.minimum(gu[:, :DFF], LIMIT)
        up = jnp.clip(gu[:, DFF:], -LIMIT, LIMIT)
        act = (up + 1.0) * (g * jax.nn.sigmoid(ALPHA * g))
        yv = _dot(act.astype(bf16), wd_b[s]) + bd_ref[...]
        for cc in range(ROW):
            y_ref[pl.ds(base + cc, rows, stride=ROW), :] = yv[:, cc * LANES:(cc + 1) * LANES]

    def run_tiles(s):
        nv = tv_ref[i]
        rem = nv - (nv // MB) * MB
        half = jnp.logical_and(rem > 0, rem <= MB // 2)
        nfull = nv // MB + jnp.where(rem > MB // 2, 1, 0)
        pl.when(nfull == EB)(functools.partial(tile, s, 0, EB))
        few = nfull < EB
        for h in range(0, EB, 2):
            pl.when(jnp.logical_and(few, nfull >= h + 2))(functools.partial(tile, s, h, 2))
            pl.when(jnp.logical_and(few, nfull == h + 1))(functools.partial(tile, s, h, 1))
        for h in range(EB):
            pl.when(jnp.logical_and(half, nfull == h))(functools.partial(tile, s, h, 1, MB // 2))

    @pl.when(i < nu_ref[0])
    def _():
        e = te_ref[i]
        s = ts_ref[i]
        nxt = tn_ref[i]
        first = (tf_ref[i] & 1) != 0
        last = (tf_ref[i] & 2) != 0

        @pl.when(i == 0)
        def _():
            for c in fetch(e):
                c.start()
            for c in fetch(e):
                c.wait()
            convert(s)

        @pl.when(jnp.logical_and(first, nxt >= 0))
        def _():
            for c in fetch(nxt):
                c.start()

        handoff = jnp.logical_and(last, nxt >= 0)

        @pl.when(handoff)
        def _():
            for c in fetch(nxt):
                c.wait()
            convert(1 - s)

        run_tiles(s)


def _experts(tile_blk, tile_e, tile_nv, tile_flags, tile_next, tile_slot, nused, xs, w_gu, b_gu, w_down, b_down):
    blk = lambda i, tb, te, *_: (tb[i], 0)
    bsel = lambda i, tb, te, *_: (te[i], 0, 0)
    grid_spec = pltpu.PrefetchScalarGridSpec(
        num_scalar_prefetch=7,
        grid=(NB,),
        in_specs=[
            pl.BlockSpec((EB * MB * ROW, LANES), blk),
            pl.BlockSpec(memory_space=pl.ANY),
            pl.BlockSpec((None, 1, 2 * DFF), bsel),
            pl.BlockSpec(memory_space=pl.ANY),
            pl.BlockSpec((None, 1, D), bsel),
        ],
        out_specs=pl.BlockSpec((EB * MB * ROW, LANES), blk),
        scratch_shapes=[pltpu.VMEM((D, 2 * DFF), f32), pltpu.VMEM((DFF, D), f32),
                        pltpu.VMEM((2, D, 2 * DFF), bf16), pltpu.VMEM((2, DFF, D), bf16),
                        pltpu.SemaphoreType.DMA((2,))],
    )
    return pl.pallas_call(
        _experts_kernel,
        grid_spec=grid_spec,
        out_shape=jax.ShapeDtypeStruct((NE * N * ROW, LANES), f32),
        compiler_params=pltpu.CompilerParams(dimension_semantics=("arbitrary",), vmem_limit_bytes=VMEM_LIMIT),
        name="experts",
    )(tile_blk, tile_e, tile_nv, tile_flags, tile_next, tile_slot, nused, xs,
      w_gu, b_gu.reshape(NE, 1, 2 * DFF), w_down, b_down.reshape(NE, 1, D))


def _combine_kernel(pos_ref, tab_ref, tabn_ref, gate_ref, h1_ref, g2_ref, pg_ref, ys_hbm, o_ref,
                    stage, wbuf, rbuf, sem):
    j = pl.program_id(0)
    slot = lax.rem(j, 2)
    GT = PG
    GR = GT * ROW

    @pl.when(j == 0)
    def _():
        _strip_copies(tab_ref, stage.at[0], ys_hbm, sem.at[0], to_region=False)

    @pl.when(j + 1 < pl.num_programs(0))
    def _():
        _strip_copies(tabn_ref, stage.at[1 - slot], ys_hbm, sem.at[1 - slot], to_region=False)

    for kk in range(TOPK):
        wbuf[kk] = jnp.broadcast_to(gate_ref[:, kk:kk + 1], (TT, LANES))
    _strip_wait(stage.at[slot], ys_hbm, sem.at[slot])
    g2 = g2_ref[...]
    pg = pg_ref[...]

    def one_group(s, g):
        rows = pl.ds(pl.multiple_of(g * GR, GR), GR)
        parts = []
        for jj in range(GT):
            t = g * GT + jj
            acc = None
            for kk in range(TOPK):
                w = jnp.broadcast_to(wbuf[kk, pl.ds(t, 1), :], (ROW, LANES))
                term = stage[s, pl.ds(pl.multiple_of(pos_ref[g, jj * TOPK + kk], ROW), ROW), :] * w
                acc = term if acc is None else acc + term
            parts.append(acc)
        ff3 = jnp.concatenate(parts, axis=0).reshape(GT, ROW, LANES)
        ss = jnp.sum(jnp.sum(ff3 * ff3, axis=2, keepdims=True), axis=1, keepdims=True)
        rs = lax.rsqrt(ss * (1.0 / D) + EPS)
        out = h1_ref[rows, :].reshape(GT, ROW, LANES) + g2 * (ff3 * rs * pg)
        rbuf[rows, :] = out.reshape(GR, LANES)

    def gather(s):
        def trip(gg, carry):
            for k in range(COMBINE_GROUPS_PER_TRIP):
                one_group(s, COMBINE_GROUPS_PER_TRIP * gg + k)
            return carry
        lax.fori_loop(0, TT // GT // COMBINE_GROUPS_PER_TRIP, trip, 0)

    for s in range(2):
        pl.when(slot == s)(functools.partial(gather, s))
    for cc in range(ROW):
        o_ref[:, cc * LANES:(cc + 1) * LANES] = rbuf[pl.ds(cc, TT, stride=ROW), :]


def _combine(pos, tab, gates, h1_rows, mod4, pg, ys):
    nt = N // TT
    per_b = T // TT
    tabspec = lambda imap: pl.BlockSpec((ROW, LANES), imap, memory_space=pltpu.SMEM)
    return pl.pallas_call(
        _combine_kernel,
        grid=(nt,),
        in_specs=[pl.BlockSpec((TT // PG, LANES), lambda j: (j, 0), memory_space=pltpu.SMEM),
                  tabspec(lambda j: (j, 0)), tabspec(lambda j: (jnp.minimum(j + 1, nt - 1), 0)),
                  pl.BlockSpec((TT, LANES), lambda j: (j, 0)),
                  pl.BlockSpec((TT * ROW, LANES), lambda j: (j, 0)),
                  pl.BlockSpec((None, None, ROW, LANES), lambda j: (j // per_b, 5, 0, 0)),
                  pl.BlockSpec((ROW, LANES), lambda j: (0, 0)),
                  pl.BlockSpec(memory_space=pl.ANY)],
        out_specs=pl.BlockSpec((TT, D), lambda j: (j, 0)),
        out_shape=jax.ShapeDtypeStruct((N, D), f32),
        scratch_shapes=[pltpu.VMEM((2, STAGE_ROWS, LANES), f32), pltpu.VMEM((TOPK, TT, LANES), f32),
                        pltpu.VMEM((TT * ROW, LANES), f32), pltpu.SemaphoreType.DMA((2,))],
        compiler_params=pltpu.CompilerParams(dimension_semantics=("arbitrary",), vmem_limit_bytes=VMEM_LIMIT),
        name="combine",
    )(pos, tab, tab, gates, h1_rows, mod4, pg, ys)


def _pad_heads(w):
    lead = w.shape[:-1]
    w4 = w.reshape(lead + (GLA_H, GLA_DK))
    w4 = jnp.concatenate([w4, jnp.zeros_like(w4)], axis=-1)
    return w4.reshape(lead + (GLA_H * LANES,))


def kernel(x, c, ada_w, ada_b, mix_pre_g, mix_post_g, ffn_pre_g, ffn_post_g, w_in, w_alpha, b_alpha, gla_norm_g, s5_lambda_re, s5_lambda_im, s5_log_dt, s5_b_re, s5_b_im, s5_c_re, s5_c_im, s5_d, s5_glu_w, s5_glu_b, w_out, router_w, router_b, exp_w_gu, exp_b_gu, exp_w_down, exp_b_down):
    l = 0
    mod = _ada(c, ada_w[l], ada_b[l:l + 1])
    mod3 = mod.reshape(B, 1, 6 * D)
    mod4 = mod.reshape(B, 6, ROW, LANES)

    w = w_in[l]
    o_q, o_k, o_v, o_g, o_a, o_u = 0, 256, 512, 1024, 1536, 1552
    wq = _pad_heads(w[:, o_q:o_k]).astype(bf16)
    wk = _pad_heads(w[:, o_k:o_v]).astype(bf16)
    wv = w[:, o_v:o_g].astype(bf16)
    wg = w[:, o_g:o_a].astype(bf16)
    wa = jnp.pad(w[:, o_a:o_u], ((0, 0), (0, LANES - LOWRANK))).astype(bf16)
    wu = w[:, o_u:].astype(bf16)
    walpha = jnp.pad(_pad_heads(w_alpha[l]), ((0, LANES - LOWRANK), (0, 0))).astype(bf16)
    balpha = _pad_heads(b_alpha[l:l + 1])

    gla, u = _mixer_in(x, mod3, mix_pre_g[l:l + 1], wq, wk, wv, wg, wa, wu,
                       walpha, balpha, gla_norm_g[l:l + 1])

    wt, mi, mo, a16 = _s5_prep(s5_lambda_re[l], s5_lambda_im[l], s5_log_dt[l], s5_b_re[l], s5_b_im[l],
                               s5_c_re[l], s5_c_im[l], s5_d[l])
    y = _s5_scan(u, wt, mi, mo, a16)

    rw = jnp.pad(router_w[l], ((0, 0), (0, LANES - NE)))
    rw_hi = rw.astype(bf16)
    rw_lo = (rw - rw_hi.astype(f32)).astype(bf16)
    rb = jnp.pad(router_b[l:l + 1], ((0, 0), (0, LANES - NE)), constant_values=-1e30)
    wo = w_out[l].astype(bf16)
    h1_rows, xs, pos, tab, gates, cnt = _mixer_out(
        x, gla, y, mod3, s5_glu_w[l].astype(bf16), s5_glu_b[l:l + 1], wo[:512], wo[512:],
        mix_post_g[l:l + 1], ffn_pre_g[l:l + 1], rw_hi, rw_lo, rb)

    counts = cnt[0, :NE].astype(i32)
    step_rows = EB * MB
    ntile = (counts + step_rows - 1) // step_rows
    tends = jnp.cumsum(ntile)
    nused = tends[-1]
    gi = jnp.minimum(jnp.arange(NB, dtype=i32), nused - 1)
    tile_e = jnp.sum((tends[None, :] <= gi[:, None]).astype(i32), axis=1)
    sel = tile_e[:, None] == jnp.arange(NE, dtype=i32)[None, :]
    pick = lambda v: jnp.sum(jnp.where(sel, v[None, :], 0), axis=1)
    tile_j = gi - pick(tends - ntile)
    tile_blk = tile_e * TPE + tile_j
    tile_nv = jnp.clip(pick(counts) - tile_j * step_rows, 0, step_rows)
    tile_flags = (tile_j == 0).astype(i32) + 2 * (tile_j == pick(ntile) - 1).astype(i32)
    used = ntile > 0
    eids = jnp.arange(NE, dtype=i32)
    later = jnp.logical_and(used[None, :], eids[None, :] > eids[:, None])
    nxt = jnp.min(jnp.where(later, eids[None, :], NE), axis=1)
    tile_next = pick(jnp.where(nxt < NE, nxt, -1))
    tile_slot = pick((jnp.cumsum(used.astype(i32)) - 1) % 2)

    ys = _experts(tile_blk, tile_e, tile_nv, tile_flags, tile_next, tile_slot, nused.reshape(1), xs,
                  exp_w_gu[l], exp_b_gu[l], exp_w_down[l], exp_b_down[l])
    out = _combine(pos, tab, gates.reshape(N, LANES), h1_rows, mod4, ffn_post_g[l].reshape(ROW, LANES), ys)
    return out.reshape(B, T, D)
```
